```python
import math
import jax, jax.numpy as jnp
from jax import lax
import numpy as np

D_MODEL = 1024
BATCH = 8
SEQ = 2048
DEPTH = 2

S5_WIDTH = D_MODEL // 2
S5_GROUP_CH = 16
S5_GROUPS = S5_WIDTH // S5_GROUP_CH
S5_STATE = 64
LRU_WIDTH = D_MODEL // 2
LRU_BLOCKS = 8
LRU_BLOCK_DIM = LRU_WIDTH // LRU_BLOCKS
CONV_WIDTH = 4
RG_C = 8.0
EVEN_IN = S5_WIDTH + 2 * LRU_WIDTH
EVEN_OUT = S5_WIDTH + LRU_WIDTH

MLA_HEADS = 8
Q_LORA = 384
KV_LORA = 256
NOPE_DIM = 128
ROPE_DIM = 64
V_DIM = 128
QK_DIM = NOPE_DIM + ROPE_DIM
MLA_IN = Q_LORA + KV_LORA + ROPE_DIM
ROPE_THETA = 10000.0
Q_BLOCK = 128

N_GROUPS = 8
EXPERTS_PER_GROUP = 8
N_EXPERTS = N_GROUPS * EXPERTS_PER_GROUP
TOP_K = 2
D_EXPERT = 256
MOE_BLOCK = 128

N_EVEN = (DEPTH + 1) // 2
N_ODD = DEPTH // 2
RMS_EPS = 1e-6
NEG_INF = -1e30

kernel_name = "hybrid_s5_rglru_mla_hmoe"


def rms_norm(x, g, eps=RMS_EPS):
    xf = x.astype(jnp.float32)
    y = xf * lax.rsqrt(jnp.mean(xf * xf, axis=-1, keepdims=True) + eps)
    return (y * g.astype(jnp.float32)).astype(x.dtype)


def linear_combine(e1, e2):
    a1, b1 = e1
    a2, b2 = e2
    return (a1 * a2, a2 * b1 + b2)


def complex_linear_combine(e1, e2):
    a1r, a1i, b1r, b1i = e1
    a2r, a2i, b2r, b2i = e2
    return (a2r * a1r - a2i * a1i,
            a2r * a1i + a2i * a1r,
            a2r * b1r - a2i * b1i + b2r,
            a2r * b1i + a2i * b1r + b2i)


def s5_mixer(u, lam_re, lam_im, log_dt, b_re, b_im, c_re, c_im, d, glu_w, glu_b):
    bsz, seq, _ = u.shape
    uf = u.astype(jnp.float32).reshape(bsz, seq, S5_GROUPS, S5_GROUP_CH)
    lr = jnp.minimum(lam_re.astype(jnp.float32), -1e-4)
    li = lam_im.astype(jnp.float32)
    dt = jnp.exp(log_dt.astype(jnp.float32))[:, None]
    mag = jnp.exp(lr * dt)
    lb_re = mag * jnp.cos(li * dt)
    lb_im = mag * jnp.sin(li * dt)
    den = lr * lr + li * li
    num_re = lb_re - 1.0
    f_re = (num_re * lr + lb_im * li) / den
    f_im = (lb_im * lr - num_re * li) / den
    br = b_re.astype(jnp.float32)
    bi = b_im.astype(jnp.float32)
    bbar_re = f_re[..., None] * br - f_im[..., None] * bi
    bbar_im = f_re[..., None] * bi + f_im[..., None] * br
    bu_re = jnp.einsum('blgh,gph->blgp', uf, bbar_re)
    bu_im = jnp.einsum('blgh,gph->blgp', uf, bbar_im)
    a_re = jnp.broadcast_to(lb_re, bu_re.shape)
    a_im = jnp.broadcast_to(lb_im, bu_im.shape)
    _, _, s_re, s_im = lax.associative_scan(
        complex_linear_combine, (a_re, a_im, bu_re, bu_im), axis=1)
    y = (jnp.einsum('blgp,ghp->blgh', s_re, c_re.astype(jnp.float32))
         - jnp.einsum('blgp,ghp->blgh', s_im, c_im.astype(jnp.float32))
         + d.astype(jnp.float32) * uf)
    y = jax.nn.gelu(y.reshape(bsz, seq, S5_WIDTH))
    return y * jax.nn.sigmoid(y @ glu_w.astype(jnp.float32) + glu_b.astype(jnp.float32))


def rglru_mixer(xb, gate_in, conv_w, conv_b, wa, ba, wi, bi, lam):
    bsz, seq, _ = xb.shape
    xc = lax.conv_general_dilated(
        xb, conv_w[:, None, :].astype(xb.dtype), window_strides=(1,),
        padding=[(CONV_WIDTH - 1, 0)], dimension_numbers=('NWC', 'WIO', 'NWC'),
        feature_group_count=LRU_WIDTH)
    xf = xc.astype(jnp.float32) + conv_b.astype(jnp.float32)
    xr = xf.reshape(bsz, seq, LRU_BLOCKS, LRU_BLOCK_DIM)
    r = jax.nn.sigmoid(jnp.einsum('blnc,ncd->blnd', xr, wa.astype(jnp.float32))
                       .reshape(bsz, seq, LRU_WIDTH) + ba.astype(jnp.float32))
    i = jax.nn.sigmoid(jnp.einsum('blnc,ncd->blnd', xr, wi.astype(jnp.float32))
                       .reshape(bsz, seq, LRU_WIDTH) + bi.astype(jnp.float32))
    log_a = RG_C * r * jax.nn.log_sigmoid(lam.astype(jnp.float32))
    a = jnp.exp(log_a)
    b = jnp.sqrt(-jnp.expm1(2.0 * log_a)) * (i * xf)
    _, h = lax.associative_scan(linear_combine, (a, b), axis=1)
    return h * jax.nn.gelu(gate_in.astype(jnp.float32))


def even_mixer(h, w_in, lam_re, lam_im, log_dt, b_re, b_im, c_re, c_im, d, glu_w, glu_b,
               conv_w, conv_b, wa, ba, wi, bi, lam, w_out):
    proj = h @ w_in
    u_s5 = proj[..., :S5_WIDTH]
    x_lru = proj[..., S5_WIDTH:S5_WIDTH + LRU_WIDTH]
    g_lru = proj[..., S5_WIDTH + LRU_WIDTH:]
    ya = s5_mixer(u_s5, lam_re, lam_im, log_dt, b_re, b_im, c_re, c_im, d, glu_w, glu_b)
    yb = rglru_mixer(x_lru, g_lru, conv_w, conv_b, wa, ba, wi, bi, lam)
    y = jnp.concatenate([ya, yb], axis=-1).astype(h.dtype)
    return y @ w_out


def apply_rope(t, cos, sin):
    half = ROPE_DIM // 2
    t1, t2 = t[..., :half], t[..., half:]
    c = cos[:, :, None, :]
    s = sin[:, :, None, :]
    return jnp.concatenate([t1 * c - t2 * s, t1 * s + t2 * c], axis=-1)


def causal_block_attention(q, k, v):
    bsz, seq, nh, dh = q.shape
    dv = v.shape[-1]
    nb = seq // Q_BLOCK
    qb = q.transpose(0, 2, 1, 3).reshape(bsz, nh, nb, Q_BLOCK, dh).transpose(2, 0, 1, 3, 4)
    kt = k.transpose(0, 2, 1, 3)
    vt = v.transpose(0, 2, 1, 3)
    key_pos = jnp.arange(seq)
    scale = 1.0 / math.sqrt(dh)

    def one_block(args):
        q_blk, blk = args
        s = jnp.einsum('bhqd,bhkd->bhqk', q_blk, kt) * scale
        q_pos = blk * Q_BLOCK + jnp.arange(Q_BLOCK)
        s = jnp.where(key_pos[None, :] <= q_pos[:, None], s, NEG_INF)
        p = jax.nn.softmax(s, axis=-1)
        return jnp.einsum('bhqk,bhkd->bhqd', p, vt)

    o = lax.map(one_block, (qb, jnp.arange(nb)))
    return o.transpose(1, 0, 3, 2, 4).reshape(bsz, seq, nh * dv)


def mla_mixer(h, positions, w_in, cq_norm, ckv_norm, w_uq, w_ukv, q_norm, k_norm, w_o):
    bsz, seq, _ = h.shape
    proj = h @ w_in
    c_q = rms_norm(proj[..., :Q_LORA], cq_norm)
    c_kv = rms_norm(proj[..., Q_LORA:Q_LORA + KV_LORA], ckv_norm)
    k_rope = proj[..., Q_LORA + KV_LORA:]
    q = (c_q @ w_uq).reshape(bsz, seq, MLA_HEADS, QK_DIM)
    kv = (c_kv @ w_ukv).reshape(bsz, seq, MLA_HEADS, NOPE_DIM + V_DIM)
    k_nope, v = kv[..., :NOPE_DIM], kv[..., NOPE_DIM:]
    k = jnp.concatenate(
        [k_nope, jnp.broadcast_to(k_rope[:, :, None, :], (bsz, seq, MLA_HEADS, ROPE_DIM))], axis=-1)
    q = rms_norm(q, q_norm).astype(jnp.float32)
    k = rms_norm(k, k_norm).astype(jnp.float32)
    inv_freq = 1.0 / (ROPE_THETA ** (jnp.arange(0, ROPE_DIM, 2, dtype=jnp.float32) / ROPE_DIM))
    ang = positions.astype(jnp.float32)[..., None] * inv_freq
    cos, sin = jnp.cos(ang), jnp.sin(ang)
    q = jnp.concatenate([q[..., :NOPE_DIM], apply_rope(q[..., NOPE_DIM:], cos, sin)], axis=-1)
    k = jnp.concatenate([k[..., :NOPE_DIM], apply_rope(k[..., NOPE_DIM:], cos, sin)], axis=-1)
    o = causal_block_attention(q, k, v.astype(jnp.float32))
    return o.astype(h.dtype) @ w_o


def hierarchical_moe(h, w_group, b_group, w_expert, b_expert, w1, w3, w2):
    bsz, seq, dm = h.shape
    n_tok = bsz * seq
    hf = h.reshape(n_tok, dm)
    g_logits = (hf @ w_group).astype(jnp.float32) + b_group.astype(jnp.float32)
    g_prob = jax.nn.softmax(g_logits, axis=-1)
    g_sel = jnp.argmax(g_logits, axis=-1)
    p_g = jnp.take_along_axis(g_prob, g_sel[:, None], axis=1)[:, 0]
    e_logits = ((hf @ w_expert).astype(jnp.float32) + b_expert.astype(jnp.float32)
                ).reshape(n_tok, N_GROUPS, EXPERTS_PER_GROUP)
    e_in_group = jnp.take_along_axis(e_logits, g_sel[:, None, None], axis=1)[:, 0]
    top_vals, top_idx = lax.top_k(e_in_group, TOP_K)
    gates = p_g[:, None] * jax.nn.softmax(top_vals, axis=-1)
    expert_id = g_sel[:, None].astype(jnp.int32) * EXPERTS_PER_GROUP + top_idx.astype(jnp.int32)
    n_assign = n_tok * TOP_K
    e_flat = expert_id.reshape(-1)
    tok_flat = jnp.arange(n_assign, dtype=jnp.int32) // TOP_K
    g_flat = gates.reshape(-1)
    order = jnp.argsort(e_flat)
    e_sorted = e_flat[order]
    tok_sorted = tok_flat[order]
    g_sorted = g_flat[order]
    counts = jnp.bincount(e_flat, length=N_EXPERTS).astype(jnp.int32)
    starts = jnp.cumsum(counts) - counts
    padded = ((counts + MOE_BLOCK - 1) // MOE_BLOCK) * MOE_BLOCK
    pad_ends = jnp.cumsum(padded)
    pad_starts = pad_ends - padded
    dest = pad_starts[e_sorted] + (jnp.arange(n_assign, dtype=jnp.int32) - starts[e_sorted])
    n_blocks = -(-n_assign // MOE_BLOCK) + N_EXPERTS
    n_rows = n_blocks * MOE_BLOCK
    x_disp = jnp.zeros((n_rows, dm), h.dtype).at[dest].set(hf[tok_sorted])
    blk_start = jnp.arange(n_blocks, dtype=jnp.int32) * MOE_BLOCK
    blk_expert = jnp.clip(jnp.searchsorted(pad_ends, blk_start, side='right'), 0, N_EXPERTS - 1)

    def expert_block(args):
        xb, e = args
        return (jax.nn.silu(xb @ w1[e]) * (xb @ w3[e])) @ w2[e]

    y_disp = lax.map(expert_block, (x_disp.reshape(n_blocks, MOE_BLOCK, dm), blk_expert))
    y_disp = y_disp.reshape(n_rows, dm)
    out = jnp.zeros((n_tok, dm), h.dtype).at[tok_sorted].add(
        g_sorted[:, None].astype(h.dtype) * y_disp[dest])
    return out.reshape(bsz, seq, dm)


def setup_inputs(seed: int = 0) -> dict:
    key = jax.random.key(seed)
    keys = iter(jax.random.split(key, 64))

    def nrm(shape, scale):
        return jax.random.normal(next(keys), shape, jnp.float32) * scale

    def gain(shape):
        return 1.0 + nrm(shape, 0.01)

    x = jax.random.normal(next(keys), (BATCH, SEQ, D_MODEL), jnp.float32)
    offset = jax.random.randint(next(keys), (BATCH, 1), 0, 4096, dtype=jnp.int32)
    positions = offset + jnp.arange(SEQ, dtype=jnp.int32)[None, :]

    n_idx = jnp.arange(S5_STATE, dtype=jnp.float32)
    lam_im0 = jnp.broadcast_to(jnp.pi * n_idx, (N_EVEN, S5_GROUPS, S5_STATE))
    u_lru = jax.random.uniform(next(keys), (N_EVEN, LRU_WIDTH), jnp.float32, 0.9, 0.999)

    return {
        'x': x,
        'positions': positions,
        'ev_norm': gain((N_EVEN, D_MODEL)),
        'ev_w_in': nrm((N_EVEN, D_MODEL, EVEN_IN), D_MODEL ** -0.5),
        's5_lambda_re': -0.5 + nrm((N_EVEN, S5_GROUPS, S5_STATE), 0.01),
        's5_lambda_im': lam_im0 + nrm((N_EVEN, S5_GROUPS, S5_STATE), 0.01),
        's5_log_dt': jax.random.uniform(next(keys), (N_EVEN, S5_GROUPS), jnp.float32,
                                        math.log(1e-3), math.log(1e-1)),
        's5_b_re': nrm((N_EVEN, S5_GROUPS, S5_STATE, S5_GROUP_CH), (2 * S5_GROUP_CH) ** -0.5),
        's5_b_im': nrm((N_EVEN, S5_GROUPS, S5_STATE, S5_GROUP_CH), (2 * S5_GROUP_CH) ** -0.5),
        's5_c_re': nrm((N_EVEN, S5_GROUPS, S5_GROUP_CH, S5_STATE), S5_STATE ** -0.5),
        's5_c_im': nrm((N_EVEN, S5_GROUPS, S5_GROUP_CH, S5_STATE), S5_STATE ** -0.5),
        's5_d': nrm((N_EVEN, S5_GROUPS, S5_GROUP_CH), 1.0),
        's5_glu_w': nrm((N_EVEN, S5_WIDTH, S5_WIDTH), S5_WIDTH ** -0.5),
        's5_glu_b': nrm((N_EVEN, S5_WIDTH), 0.01),
        'lru_conv_w': nrm((N_EVEN, CONV_WIDTH, LRU_WIDTH), CONV_WIDTH ** -0.5),
        'lru_conv_b': nrm((N_EVEN, LRU_WIDTH), 0.01),
        'lru_wa': nrm((N_EVEN, LRU_BLOCKS, LRU_BLOCK_DIM, LRU_BLOCK_DIM), LRU_BLOCK_DIM ** -0.5),
        'lru_ba': nrm((N_EVEN, LRU_WIDTH), 0.01),
        'lru_wi': nrm((N_EVEN, LRU_BLOCKS, LRU_BLOCK_DIM, LRU_BLOCK_DIM), LRU_BLOCK_DIM ** -0.5),
        'lru_bi': nrm((N_EVEN, LRU_WIDTH), 0.01),
        'lru_lambda': jnp.log(u_lru) - jnp.log1p(-u_lru),
        'ev_w_out': nrm((N_EVEN, EVEN_OUT, D_MODEL), EVEN_OUT ** -0.5),
        'od_norm': gain((N_ODD, D_MODEL)),
        'mla_w_in': nrm((N_ODD, D_MODEL, MLA_IN), D_MODEL ** -0.5),
        'mla_cq_norm': gain((N_ODD, Q_LORA)),
        'mla_ckv_norm': gain((N_ODD, KV_LORA)),
        'mla_w_uq': nrm((N_ODD, Q_LORA, MLA_HEADS * QK_DIM), Q_LORA ** -0.5),
        'mla_w_ukv': nrm((N_ODD, KV_LORA, MLA_HEADS * (NOPE_DIM + V_DIM)), KV_LORA ** -0.5),
        'mla_q_norm': gain((N_ODD, QK_DIM)),
        'mla_k_norm': gain((N_ODD, QK_DIM)),
        'mla_w_o': nrm((N_ODD, MLA_HEADS * V_DIM, D_MODEL), (MLA_HEADS * V_DIM) ** -0.5),
        'ffn_norm': gain((DEPTH, D_MODEL)),
        'moe_w_group': nrm((DEPTH, D_MODEL, N_GROUPS), D_MODEL ** -0.5),
        'moe_b_group': nrm((DEPTH, N_GROUPS), 0.01),
        'moe_w_expert': nrm((DEPTH, D_MODEL, N_EXPERTS), D_MODEL ** -0.5),
        'moe_b_expert': nrm((DEPTH, N_EXPERTS), 0.01),
        'moe_w1': nrm((DEPTH, N_EXPERTS, D_MODEL, D_EXPERT), D_MODEL ** -0.5),
        'moe_w3': nrm((DEPTH, N_EXPERTS, D_MODEL, D_EXPERT), D_MODEL ** -0.5),
        'moe_w2': nrm((DEPTH, N_EXPERTS, D_EXPERT, D_MODEL), D_EXPERT ** -0.5),
    }


def reference(x, positions, ev_norm, ev_w_in, s5_lambda_re, s5_lambda_im, s5_log_dt,
              s5_b_re, s5_b_im, s5_c_re, s5_c_im, s5_d, s5_glu_w, s5_glu_b,
              lru_conv_w, lru_conv_b, lru_wa, lru_ba, lru_wi, lru_bi, lru_lambda, ev_w_out,
              od_norm, mla_w_in, mla_cq_norm, mla_ckv_norm, mla_w_uq, mla_w_ukv,
              mla_q_norm, mla_k_norm, mla_w_o,
              ffn_norm, moe_w_group, moe_b_group, moe_w_expert, moe_b_expert,
              moe_w1, moe_w3, moe_w2):
    for layer in range(DEPTH):
        j = layer // 2
        if layer % 2 == 0:
            x = x + even_mixer(
                rms_norm(x, ev_norm[j]), ev_w_in[j], s5_lambda_re[j], s5_lambda_im[j],
                s5_log_dt[j], s5_b_re[j], s5_b_im[j], s5_c_re[j], s5_c_im[j], s5_d[j],
                s5_glu_w[j], s5_glu_b[j], lru_conv_w[j], lru_conv_b[j], lru_wa[j], lru_ba[j],
                lru_wi[j], lru_bi[j], lru_lambda[j], ev_w_out[j])
        else:
            x = x + mla_mixer(
                rms_norm(x, od_norm[j]), positions, mla_w_in[j], mla_cq_norm[j],
                mla_ckv_norm[j], mla_w_uq[j], mla_w_ukv[j], mla_q_norm[j], mla_k_norm[j],
                mla_w_o[j])
        x = x + hierarchical_moe(
            rms_norm(x, ffn_norm[layer]), moe_w_group[layer], moe_b_group[layer],
            moe_w_expert[layer], moe_b_expert[layer], moe_w1[layer], moe_w3[layer],
            moe_w2[layer])
    return x
```

```python
import functools
import math

import jax
import jax.numpy as jnp
from jax import lax
from jax.experimental import pallas as pl
from jax.experimental.pallas import tpu as pltpu

F32 = jnp.float32
BF16 = jnp.bfloat16

S5_GROUP_CH = 16
S5_STATE = 64
LRU_BLOCKS = 8
CONV_WIDTH = 4
RG_C = 8.0
MLA_HEADS = 8
Q_LORA = 384
KV_LORA = 256
NOPE_DIM = 128
ROPE_DIM = 64
V_DIM = 128
QK_DIM = NOPE_DIM + ROPE_DIM
ROPE_THETA = 10000.0
N_GROUPS = 8
EXPERTS_PER_GROUP = 8
N_EXPERTS = N_GROUPS * EXPERTS_PER_GROUP
TOP_K = 2
RMS_EPS = 1e-6
NEG_INF = -1e30

SUBLANES = 8
LANES = 128
VMEM_LIMIT = 48 * 1024 * 1024

EVEN_TL = 64
ROW_TILE = 512
MOE_BM = 128
ATT_TQ = 512
ATT_TK = 512
GATHER_TM = 256
DMA_UNROLL = 8


def _cparams(*sem):
    return pltpu.CompilerParams(dimension_semantics=tuple(sem), vmem_limit_bytes=VMEM_LIMIT)


def _rms(x, g):
    ms = jnp.mean(x * x, axis=-1, keepdims=True)
    return x * lax.rsqrt(ms + RMS_EPS) * g


def _dot(a, b):
    return jnp.dot(a, b, preferred_element_type=F32)


def _s5_discretize_kernel(lre_ref, lim_ref, ldt_ref, bre_ref, bim_ref,
                          lbre_ref, lbim_ref, bbre_ref, bbim_ref):
    lr = jnp.minimum(lre_ref[...], -1e-4)
    li = lim_ref[...]
    dt = jnp.exp(ldt_ref[...])
    mag = jnp.exp(lr * dt)
    lb_re = mag * jnp.cos(li * dt)
    lb_im = mag * jnp.sin(li * dt)
    den = lr * lr + li * li
    num_re = lb_re - 1.0
    f_re = (num_re * lr + lb_im * li) / den
    f_im = (lb_im * lr - num_re * li) / den
    lbre_ref[...] = lb_re
    lbim_ref[...] = lb_im
    br = bre_ref[...]
    bi = bim_ref[...]
    bbre_ref[...] = f_re[:, None, :] * br - f_im[:, None, :] * bi
    bbim_ref[...] = f_re[:, None, :] * bi + f_im[:, None, :] * br


def _s5_discretize(lam_re, lam_im, log_dt, b_re, b_im):
    g, p = lam_re.shape
    h = b_re.shape[-1]
    b_re_t = jnp.swapaxes(b_re, 1, 2)
    b_im_t = jnp.swapaxes(b_im, 1, 2)
    return pl.pallas_call(
        _s5_discretize_kernel,
        out_shape=(jax.ShapeDtypeStruct((g, p), F32), jax.ShapeDtypeStruct((g, p), F32),
                   jax.ShapeDtypeStruct((g, h, p), F32), jax.ShapeDtypeStruct((g, h, p), F32)),
        name="s5_discretize",
    )(lam_re, lam_im, log_dt[:, None], b_re_t, b_im_t)


def _even_inproj_kernel(x_ref, g_ref, w_ref, o_ref, h_ref):
    nb, tl, _ = x_ref.shape
    nc = h_ref.shape[0]
    g = g_ref[...]
    for b in range(nb):
        h = _rms(x_ref[b], g)
        for c in range(nc):
            h_ref[c, pl.ds(b, tl, stride=nb), :] = h[:, c * LANES:(c + 1) * LANES]
    h_all = jnp.concatenate([h_ref[c] for c in range(nc)], axis=-1)
    o_ref[...] = _dot(h_all.astype(BF16), w_ref[...]).astype(o_ref.dtype)


def _even_inproj(x, g, w):
    nb, seq, d = x.shape
    n_out = w.shape[1]
    tl = EVEN_TL
    return pl.pallas_call(
        _even_inproj_kernel,
        out_shape=jax.ShapeDtypeStruct((seq * nb, n_out), BF16),
        grid=(seq // tl,),
        in_specs=[pl.BlockSpec((nb, tl, d), lambda i: (0, i, 0)),
                  pl.BlockSpec((1, d), lambda i: (0, 0)),
                  pl.BlockSpec((d, n_out), lambda i: (0, 0))],
        out_specs=pl.BlockSpec((tl * nb, n_out), lambda i: (i, 0)),
        scratch_shapes=[pltpu.VMEM((d // LANES, tl * nb, LANES), F32)],
        compiler_params=_cparams("parallel"),
        name="even_inproj",
    )(x, g, w)


def _s5_kernel(u_ref, bm_ref, lre_ref, lim_ref, cm_ref, d_ref, gw_ref, gb_ref, o_ref,
               bu_ref, sb_ref, st_ref, *, nb):
    rows, width = u_ref.shape
    n_half = bm_ref.shape[0]
    kin = width // n_half
    ncol = bm_ref.shape[2]
    nre = ncol // 2
    chunk = 512
    steps = rows // nb

    @pl.when(pl.program_id(0) == 0)
    def _():
        st_ref[...] = jnp.zeros_like(st_ref)

    u = u_ref[...]
    for j in range(n_half):
        bu_ref[:, j * ncol:(j + 1) * ncol] = _dot(u[:, j * kin:(j + 1) * kin], bm_ref[j])

    for j in range(n_half):
        for c in range(nre // chunk):
            cr = j * ncol + c * chunk
            ci = cr + nre
            lc = j * nre + c * chunk
            lr = lre_ref[:, lc:lc + chunk]
            li = lim_ref[:, lc:lc + chunk]

            def body(i, carry, cr=cr, ci=ci, lr=lr, li=li):
                sr, si = carry
                r0 = pl.multiple_of(i * (2 * nb), 2 * nb)
                sr1 = lr * sr - li * si + bu_ref[pl.ds(r0, nb), cr:cr + chunk]
                si1 = lr * si + li * sr + bu_ref[pl.ds(r0, nb), ci:ci + chunk]
                sr2 = lr * sr1 - li * si1 + bu_ref[pl.ds(r0 + nb, nb), cr:cr + chunk]
                si2 = lr * si1 + li * sr1 + bu_ref[pl.ds(r0 + nb, nb), ci:ci + chunk]
                sb_ref[pl.ds(r0, 2 * nb), cr:cr + chunk] = jnp.concatenate([sr1, sr2], 0).astype(BF16)
                sb_ref[pl.ds(r0, 2 * nb), ci:ci + chunk] = jnp.concatenate([si1, si2], 0).astype(BF16)
                return sr2, si2

            sr, si = lax.fori_loop(0, steps // 2, body,
                                   (st_ref[:, cr:cr + chunk], st_ref[:, ci:ci + chunk]))
            st_ref[:, cr:cr + chunk] = sr
            st_ref[:, ci:ci + chunk] = si

    ys = [_dot(sb_ref[:, j * ncol:(j + 1) * ncol], cm_ref[j]) for j in range(n_half)]
    y = jnp.concatenate(ys, axis=-1) + d_ref[...] * u.astype(F32)
    y = jax.nn.gelu(y)
    z = _dot(y.astype(BF16), gw_ref[...]) + gb_ref[...]
    o_ref[...] = (y * jax.nn.sigmoid(z)).astype(o_ref.dtype)


def _s5_mixer(proj, nb, bm, lre, lim, cm, d, glu_w, glu_b):
    rows_total = proj.shape[0]
    width = d.shape[1]
    rows = EVEN_TL * nb
    n_half, _, ncol = bm.shape
    kern = functools.partial(_s5_kernel, nb=nb)
    return pl.pallas_call(
        kern,
        out_shape=jax.ShapeDtypeStruct((rows_total, width), BF16),
        grid=(rows_total // rows,),
        in_specs=[pl.BlockSpec((rows, width), lambda i: (i, 0)),
                  pl.BlockSpec(bm.shape, lambda i: (0, 0, 0)),
                  pl.BlockSpec(lre.shape, lambda i: (0, 0)),
                  pl.BlockSpec(lim.shape, lambda i: (0, 0)),
                  pl.BlockSpec(cm.shape, lambda i: (0, 0, 0)),
                  pl.BlockSpec((1, width), lambda i: (0, 0)),
                  pl.BlockSpec(glu_w.shape, lambda i: (0, 0)),
                  pl.BlockSpec((1, width), lambda i: (0, 0))],
        out_specs=pl.BlockSpec((rows, width), lambda i: (i, 0)),
        scratch_shapes=[pltpu.VMEM((rows, n_half * ncol), F32),
                        pltpu.VMEM((rows, n_half * ncol), BF16),
                        pltpu.VMEM((nb, n_half * ncol), F32)],
        compiler_params=_cparams("arbitrary"),
        name="s5_mixer",
    )(proj, bm, lre, lim, cm, d, glu_w, glu_b)


def _lru_kernel(x_ref, gate_ref, cw_ref, cb_ref, wa_ref, ba_ref, wi_ref, bi_ref, lam_ref, o_ref,
                xp_ref, a_ref, b_ref, h_ref, *, nb):
    rows, width = x_ref.shape
    halo = (CONV_WIDTH - 1) * nb
    steps = rows // nb

    @pl.when(pl.program_id(0) == 0)
    def _():
        xp_ref[0:halo, :] = jnp.zeros((halo, width), F32)
        h_ref[...] = jnp.zeros_like(h_ref)

    xp_ref[halo:halo + rows, :] = x_ref[...].astype(F32)
    xf = cb_ref[...] + cw_ref[0:1, :] * xp_ref[0:rows, :]
    for k in range(1, CONV_WIDTH):
        xf = xf + cw_ref[k:k + 1, :] * xp_ref[k * nb:k * nb + rows, :]
    xp_ref[0:halo, :] = xp_ref[rows:rows + halo, :]

    xb = xf.astype(BF16)
    nblk = wa_ref.shape[0]
    kb = width // nblk
    ga = jnp.concatenate([_dot(xb[:, j * kb:(j + 1) * kb], wa_ref[j]) for j in range(nblk)], -1)
    gi = jnp.concatenate([_dot(xb[:, j * kb:(j + 1) * kb], wi_ref[j]) for j in range(nblk)], -1)
    r = jax.nn.sigmoid(ga + ba_ref[...])
    ig = jax.nn.sigmoid(gi + bi_ref[...])
    log_a = RG_C * r * jax.nn.log_sigmoid(lam_ref[...])
    a_ref[...] = jnp.exp(log_a)
    th = jnp.tanh(log_a)
    b_ref[...] = jnp.sqrt(-2.0 * th / (1.0 - th)) * (ig * xf)

    def body(t, h):
        r0 = pl.multiple_of(t * nb, nb)
        h = a_ref[pl.ds(r0, nb), :] * h + b_ref[pl.ds(r0, nb), :]
        b_ref[pl.ds(r0, nb), :] = h
        return h

    h_ref[...] = lax.fori_loop(0, steps, body, h_ref[...])
    o_ref[...] = (b_ref[...] * jax.nn.gelu(gate_ref[...].astype(F32))).astype(o_ref.dtype)


def _lru_mixer(proj, nb, conv_w, conv_b, wa, ba, wi, bi, lam):
    rows_total = proj.shape[0]
    width = conv_w.shape[1]
    rows = EVEN_TL * nb
    halo = (CONV_WIDTH - 1) * nb
    kern = functools.partial(_lru_kernel, nb=nb)
    vec = pl.BlockSpec((1, width), lambda i: (0, 0))
    return pl.pallas_call(
        kern,
        out_shape=jax.ShapeDtypeStruct((rows_total, width), BF16),
        grid=(rows_total // rows,),
        in_specs=[pl.BlockSpec((rows, width), lambda i: (i, 1)),
                  pl.BlockSpec((rows, width), lambda i: (i, 2)),
                  pl.BlockSpec(conv_w.shape, lambda i: (0, 0)),
                  vec,
                  pl.BlockSpec(wa.shape, lambda i: (0, 0, 0)),
                  vec,
                  pl.BlockSpec(wi.shape, lambda i: (0, 0, 0)),
                  vec, vec],
        out_specs=pl.BlockSpec((rows, width), lambda i: (i, 0)),
        scratch_shapes=[pltpu.VMEM((rows + halo, width), F32),
                        pltpu.VMEM((rows, width), F32),
                        pltpu.VMEM((rows, width), F32),
                        pltpu.VMEM((nb, width), F32)],
        compiler_params=_cparams("arbitrary"),
        name="lru_mixer",
    )(proj, proj, conv_w, conv_b, wa, ba, wi, bi, lam)


def _even_outproj_kernel(ya_ref, yb_ref, wa_ref, wb_ref, x_ref, o_ref, y_ref):
    nb, tl, _ = x_ref.shape
    nc = y_ref.shape[0]
    y = _dot(ya_ref[...], wa_ref[...]) + _dot(yb_ref[...], wb_ref[...])
    for c in range(nc):
        y_ref[c] = y[:, c * LANES:(c + 1) * LANES]
    for b in range(nb):
        for c in range(nc):
            o_ref[b, :, c * LANES:(c + 1) * LANES] = (x_ref[b, :, c * LANES:(c + 1) * LANES]
                                                     + y_ref[c, pl.ds(b, tl, stride=nb), :])


def _even_outproj(ya, yb, w_a, w_b, x):
    nb, seq, d = x.shape
    tl = EVEN_TL
    rows = tl * nb
    wd = ya.shape[1]
    return pl.pallas_call(
        _even_outproj_kernel,
        out_shape=jax.ShapeDtypeStruct(x.shape, x.dtype),
        grid=(seq // tl,),
        in_specs=[pl.BlockSpec((rows, wd), lambda i: (i, 0)),
                  pl.BlockSpec((rows, wd), lambda i: (i, 0)),
                  pl.BlockSpec(w_a.shape, lambda i: (0, 0)),
                  pl.BlockSpec(w_b.shape, lambda i: (0, 0)),
                  pl.BlockSpec((nb, tl, d), lambda i: (0, i, 0))],
        out_specs=pl.BlockSpec((nb, tl, d), lambda i: (0, i, 0)),
        scratch_shapes=[pltpu.VMEM((d // LANES, rows, LANES), F32)],
        compiler_params=_cparams("parallel"),
        name="even_outproj",
    )(ya, yb, w_a, w_b, x)


def _block_diag(blocks):
    n, r, c = blocks.shape
    eye = jnp.eye(n, dtype=blocks.dtype)
    return jnp.einsum('nrc,nm->nrmc', blocks, eye).reshape(n * r, n * c)


def _even_mixer(x, norm_g, w_in, lam_re, lam_im, log_dt, b_re, b_im, c_re, c_im, d, glu_w, glu_b,
                conv_w, conv_b, wa, ba, wi, bi, lam, w_out):
    nb, seq, dm = x.shape
    g, p = lam_re.shape
    hch = b_re.shape[-1]
    width = g * hch
    n_half = 2
    gh = g // n_half

    proj = _even_inproj(x, norm_g[None, :], w_in.astype(BF16))

    lb_re, lb_im, bb_re, bb_im = _s5_discretize(lam_re, lam_im, log_dt, b_re, b_im)
    bb_re = bb_re.reshape(n_half, gh, hch, p)
    bb_im = bb_im.reshape(n_half, gh, hch, p)
    bm = jnp.stack([jnp.concatenate([_block_diag(bb_re[j]), _block_diag(bb_im[j])], axis=1)
                    for j in range(n_half)]).astype(BF16)
    c_re_t = jnp.swapaxes(c_re, 1, 2).reshape(n_half, gh, p, hch)
    c_im_t = jnp.swapaxes(c_im, 1, 2).reshape(n_half, gh, p, hch)
    cm = jnp.stack([jnp.concatenate([_block_diag(c_re_t[j]), -_block_diag(c_im_t[j])], axis=0)
                    for j in range(n_half)]).astype(BF16)
    lre = jnp.broadcast_to(lb_re.reshape(1, g * p), (nb, g * p))
    lim = jnp.broadcast_to(lb_im.reshape(1, g * p), (nb, g * p))
    ya = _s5_mixer(proj, nb, bm, lre, lim, cm, d.reshape(1, width), glu_w.astype(BF16),
                   glu_b[None, :])

    lw = conv_w.shape[1]
    nblk = 2
    per = LRU_BLOCKS // nblk
    wa_bd = jnp.stack([_block_diag(wa[j * per:(j + 1) * per]) for j in range(nblk)]).astype(BF16)
    wi_bd = jnp.stack([_block_diag(wi[j * per:(j + 1) * per]) for j in range(nblk)]).astype(BF16)
    yb = _lru_mixer(proj, nb, conv_w, conv_b[None, :], wa_bd, ba[None, :], wi_bd, bi[None, :],
                    lam[None, :])

    w_out_b = w_out.astype(BF16)
    return _even_outproj(ya, yb, w_out_b[:width], w_out_b[width:], x)


def _rope_table_kernel(pos_ref, inv_ref, cos_ref, sin_ref):
    ang = pos_ref[...].astype(F32) * inv_ref[...]
    cos_ref[...] = jnp.cos(ang)
    sin_ref[...] = jnp.sin(ang)


def _rope_tables(positions):
    nb, seq = positions.shape
    half = ROPE_DIM // 2
    per_row = LANES // half
    n_tok = nb * seq
    inv_freq = 1.0 / (ROPE_THETA ** (jnp.arange(0, ROPE_DIM, 2, dtype=F32) / ROPE_DIM))
    pos_rep = jnp.repeat(positions.reshape(n_tok // per_row, per_row), half, axis=1)
    inv_rep = jnp.tile(inv_freq, per_row)[None, :]
    cos, sin = pl.pallas_call(
        _rope_table_kernel,
        out_shape=(jax.ShapeDtypeStruct(pos_rep.shape, F32), jax.ShapeDtypeStruct(pos_rep.shape, F32)),
        name="rope_table",
    )(pos_rep, inv_rep)
    cos = cos.reshape(n_tok, half)
    sin = sin.reshape(n_tok, half)
    cc = jnp.concatenate([cos, cos], axis=-1)
    ss = jnp.concatenate([-sin, sin], axis=-1)
    return cc, ss


def _mla_proj_kernel(x_ref, g_ref, win_ref, cqn_ref, ckvn_ref, wq_ref, wkv_ref, qg_ref, kg_ref,
                     cc_ref, ss_ref, qn_ref, qr_ref, kn_ref, kr_ref, v_ref):
    h = _rms(x_ref[...], g_ref[...])
    proj = _dot(h.astype(BF16), win_ref[...])
    c_q = _rms(proj[:, :Q_LORA], cqn_ref[...])
    c_kv = _rms(proj[:, Q_LORA:Q_LORA + KV_LORA], ckvn_ref[...])
    o = Q_LORA + KV_LORA
    k_r = proj[:, o:o + ROPE_DIM]
    k_sw = proj[:, o + ROPE_DIM:o + 2 * ROPE_DIM]
    cc = cc_ref[...]
    ss = ss_ref[...]
    qg = qg_ref[...]
    kg = kg_ref[...]
    q = _dot(c_q.astype(BF16), wq_ref[...])
    kv = _dot(c_kv.astype(BF16), wkv_ref[...])
    nh = MLA_HEADS
    scale = 1.0 / math.sqrt(QK_DIM)
    k_rot = k_r * kg[1:2, :ROPE_DIM] * cc + k_sw * kg[2:3, :ROPE_DIM] * ss
    k_r_ss = jnp.sum(k_r * k_r, axis=-1, keepdims=True)
    for hd in range(nh):
        q_n = q[:, hd * NOPE_DIM:(hd + 1) * NOPE_DIM]
        ro = nh * NOPE_DIM + hd * ROPE_DIM
        q_r = q[:, ro:ro + ROPE_DIM]
        q_sw = q[:, ro + nh * ROPE_DIM:ro + nh * ROPE_DIM + ROPE_DIM]
        ms = (jnp.sum(q_n * q_n, axis=-1, keepdims=True)
              + jnp.sum(q_r * q_r, axis=-1, keepdims=True)) / QK_DIM
        rs = lax.rsqrt(ms + RMS_EPS) * scale
        qn_ref[0, hd] = (q_n * rs * qg[0:1, :]).astype(qn_ref.dtype)
        qr_ref[0, hd] = ((q_r * qg[1:2, :ROPE_DIM] * cc + q_sw * qg[2:3, :ROPE_DIM] * ss) * rs
                         ).astype(qr_ref.dtype)
        ko = hd * (NOPE_DIM + V_DIM)
        k_n = kv[:, ko:ko + NOPE_DIM]
        ms_k = (jnp.sum(k_n * k_n, axis=-1, keepdims=True) + k_r_ss) / QK_DIM
        rs_k = lax.rsqrt(ms_k + RMS_EPS)
        kn_ref[0, hd] = (k_n * rs_k * kg[0:1, :]).astype(kn_ref.dtype)
        kr_ref[0, hd] = (k_rot * rs_k).astype(kr_ref.dtype)
        v_ref[0, hd] = kv[:, ko + NOPE_DIM:ko + NOPE_DIM + V_DIM].astype(v_ref.dtype)


def _swap_halves(a, axis=-1):
    h = a.shape[axis] // 2
    lo = lax.slice_in_dim(a, 0, h, axis=axis)
    hi = lax.slice_in_dim(a, h, 2 * h, axis=axis)
    return jnp.concatenate([hi, lo], axis=axis)


def _mla_proj(x, norm_g, positions, w_in, cq_norm, ckv_norm, w_uq, w_ukv, q_norm, k_norm):
    nb, seq, dm = x.shape
    nh = MLA_HEADS
    tm = ROW_TILE
    n_tok = nb * seq
    cc, ss = _rope_tables(positions)
    o = Q_LORA + KV_LORA
    w_kr = w_in[:, o:o + ROPE_DIM]
    w_in_x = jnp.concatenate([w_in, _swap_halves(w_kr)], axis=1).astype(BF16)
    wq = w_uq.reshape(Q_LORA, nh, QK_DIM)
    wq_n = wq[:, :, :NOPE_DIM].reshape(Q_LORA, nh * NOPE_DIM)
    wq_r = wq[:, :, NOPE_DIM:]
    wq_x = jnp.concatenate([wq_n, wq_r.reshape(Q_LORA, nh * ROPE_DIM),
                            _swap_halves(wq_r).reshape(Q_LORA, nh * ROPE_DIM)], axis=1).astype(BF16)

    def gains(gv):
        pad = jnp.zeros((NOPE_DIM - ROPE_DIM,), F32)
        r = gv[NOPE_DIM:]
        return jnp.stack([gv[:NOPE_DIM], jnp.concatenate([r, pad]),
                          jnp.concatenate([_swap_halves(r), pad])])

    x2 = x.reshape(n_tok, dm)
    per_b = seq // tm
    full = lambda a: pl.BlockSpec(a.shape, lambda i: (0,) * a.ndim)
    args = (x2, norm_g[None, :], w_in_x, cq_norm[None, :], ckv_norm[None, :], wq_x,
            w_ukv.astype(BF16), gains(q_norm), gains(k_norm), cc, ss)
    in_specs = [pl.BlockSpec((tm, dm), lambda i: (i, 0))] + [full(a) for a in args[1:9]] + [
        pl.BlockSpec((tm, ROPE_DIM), lambda i: (i, 0)), pl.BlockSpec((tm, ROPE_DIM), lambda i: (i, 0))]

    def head_spec(dh):
        return pl.BlockSpec((1, nh, tm, dh), lambda i: (i // per_b, 0, i % per_b, 0))

    def head_shape(dh):
        return jax.ShapeDtypeStruct((nb, nh, seq, dh), BF16)

    return pl.pallas_call(
        _mla_proj_kernel,
        out_shape=(head_shape(NOPE_DIM), head_shape(ROPE_DIM), head_shape(NOPE_DIM),
                   head_shape(ROPE_DIM), head_shape(V_DIM)),
        grid=(n_tok // tm,),
        in_specs=in_specs,
        out_specs=(head_spec(NOPE_DIM), head_spec(ROPE_DIM), head_spec(NOPE_DIM),
                   head_spec(ROPE_DIM), head_spec(V_DIM)),
        compiler_params=_cparams("parallel"),
        name="mla_proj",
    )(*args)


def _attention_kernel(qn_ref, qr_ref, kn_ref, kr_ref, v_ref, o_ref):
    tq = qn_ref.shape[2]
    tk = ATT_TK
    qi = pl.program_id(2)
    q = jnp.concatenate([qn_ref[0, 0], qr_ref[0, 0]], axis=-1)
    nt = (((1,), (1,)), ((), ()))

    def scores(kb):
        k0 = pl.multiple_of(kb * tk, tk)
        k = jnp.concatenate([kn_ref[0, 0, pl.ds(k0, tk), :], kr_ref[0, 0, pl.ds(k0, tk), :]], axis=-1)
        return lax.dot_general(q, k, nt, preferred_element_type=F32), k0

    def update(s, k0, carry):
        m, l, acc = carry
        m_new = jnp.maximum(m, jnp.max(s, axis=-1, keepdims=True))
        alpha = jnp.exp(m - m_new)
        p = jnp.exp(s - m_new)
        l = alpha * l + jnp.sum(p, axis=-1, keepdims=True)
        acc = alpha * acc + _dot(p.astype(BF16), v_ref[0, 0, pl.ds(k0, tk), :])
        return m_new, l, acc

    def body(kb, carry):
        s, k0 = scores(kb)
        return update(s, k0, carry)

    init = (jnp.full((tq, 1), NEG_INF, F32), jnp.zeros((tq, 1), F32), jnp.zeros((tq, V_DIM), F32))
    n_full = (qi * tq) // tk
    carry = lax.fori_loop(0, n_full, body, init)
    for d in range(tq // tk):
        s, k0 = scores(n_full + d)
        q_pos = qi * tq + lax.broadcasted_iota(jnp.int32, s.shape, 0)
        k_pos = k0 + lax.broadcasted_iota(jnp.int32, s.shape, 1)
        s = jnp.where(k_pos <= q_pos, s, NEG_INF)
        carry = update(s, k0, carry)
    _, l, acc = carry
    o_ref[0] = (acc / l).astype(o_ref.dtype)


def _attention(qn, qr, kn, kr, v):
    nb, nh, seq, _ = qn.shape
    tq = ATT_TQ

    def qspec(dh):
        return pl.BlockSpec((1, 1, tq, dh), lambda b, h, i: (b, h, i, 0))

    def kspec(dh):
        return pl.BlockSpec((1, 1, seq, dh), lambda b, h, i: (b, h, 0, 0))

    return pl.pallas_call(
        _attention_kernel,
        out_shape=jax.ShapeDtypeStruct((nb, seq, nh * V_DIM), BF16),
        grid=(nb, nh, seq // tq),
        in_specs=[qspec(NOPE_DIM), qspec(ROPE_DIM), kspec(NOPE_DIM), kspec(ROPE_DIM), kspec(V_DIM)],
        out_specs=pl.BlockSpec((1, tq, V_DIM), lambda b, h, i: (b, i, h)),
        compiler_params=_cparams("parallel", "parallel", "arbitrary"),
        name="mla_attention",
    )(qn, qr, kn, kr, v)


def _proj_residual_kernel(a_ref, w_ref, x_ref, o_ref):
    o_ref[...] = x_ref[...] + _dot(a_ref[...], w_ref[...])


def _proj_residual(a, w, x2):
    n_tok, dm = x2.shape
    tm = ROW_TILE
    return pl.pallas_call(
        _proj_residual_kernel,
        out_shape=jax.ShapeDtypeStruct(x2.shape, x2.dtype),
        grid=(n_tok // tm,),
        in_specs=[pl.BlockSpec((tm, a.shape[1]), lambda i: (i, 0)),
                  pl.BlockSpec(w.shape, lambda i: (0, 0)),
                  pl.BlockSpec((tm, dm), lambda i: (i, 0))],
        out_specs=pl.BlockSpec((tm, dm), lambda i: (i, 0)),
        compiler_params=_cparams("parallel"),
        name="proj_residual",
    )(a, w, x2)


def _mla_mixer(x, norm_g, positions, w_in, cq_norm, ckv_norm, w_uq, w_ukv, q_norm, k_norm, w_o):
    nb, seq, dm = x.shape
    qn, qr, kn, kr, v = _mla_proj(x, norm_g, positions, w_in, cq_norm, ckv_norm, w_uq, w_ukv,
                                  q_norm, k_norm)
    o = _attention(qn, qr, kn, kr, v)
    out = _proj_residual(o.reshape(nb * seq, -1), w_o.astype(BF16), x.reshape(nb * seq, dm))
    return out.reshape(nb, seq, dm)


def _router_kernel(x_ref, g_ref, w_ref, b_ref, hn_ref, meta_ref, gate_ref, cnt_ref, base_ref):
    tm, dm = x_ref.shape
    ne = N_EXPERTS

    @pl.when(pl.program_id(0) == 0)
    def _():
        base_ref[...] = jnp.zeros_like(base_ref)

    hn = _rms(x_ref[...], g_ref[...])
    for s in range(dm // LANES):
        hn_ref[:, s, :] = hn[:, s * LANES:(s + 1) * LANES]
    logits = _dot(hn.astype(BF16), w_ref[...]) + b_ref[...]
    lane = lax.broadcasted_iota(jnp.int32, logits.shape, 1)
    big = jnp.int32(1 << 20)

    def first_argmax(v):
        m = jnp.max(v, axis=-1, keepdims=True)
        idx = jnp.min(jnp.where(v == m, lane, big), axis=-1, keepdims=True)
        return m, idx

    is_g = (lane >= ne) & (lane < ne + N_GROUPS)
    gl = jnp.where(is_g, logits, NEG_INF)
    g_max, g_idx = first_argmax(gl)
    g_sel = g_idx - ne
    p_g = 1.0 / jnp.sum(jnp.where(is_g, jnp.exp(gl - g_max), 0.0), axis=-1, keepdims=True)
    in_grp = (lane >= g_sel * EXPERTS_PER_GROUP) & (lane < (g_sel + 1) * EXPERTS_PER_GROUP)
    el = jnp.where(in_grp, logits, NEG_INF)
    v0, e0 = first_argmax(el)
    v1, e1 = first_argmax(jnp.where(lane == e0, NEG_INF, el))
    t = jnp.exp(v1 - v0)
    gate0 = p_g / (1.0 + t)
    gate1 = p_g * t / (1.0 + t)

    oh0 = (lane == e0).astype(F32)
    oh1 = (lane == e1).astype(F32)
    both = oh0 + oh1
    r_i = lax.broadcasted_iota(jnp.int32, (tm, tm), 0)
    c_i = lax.broadcasted_iota(jnp.int32, (tm, tm), 1)
    strict_lower = (c_i < r_i).astype(BF16)
    before = _dot(strict_lower, both.astype(BF16)) + base_ref[0:1, :]
    rank0 = jnp.sum(oh0 * before, axis=-1, keepdims=True)
    rank1 = jnp.sum(oh1 * before, axis=-1, keepdims=True)
    base_ref[0:1, :] = base_ref[0:1, :] + jnp.sum(both, axis=0, keepdims=True)
    cnt_ref[...] = jnp.broadcast_to(base_ref[0:1, :], cnt_ref.shape).astype(jnp.int32)

    meta = jnp.where(lane == 0, e0, jnp.where(lane == 1, e1, jnp.where(
        lane == 2, rank0.astype(jnp.int32), jnp.where(lane == 3, rank1.astype(jnp.int32), 0))))
    meta_ref[...] = meta
    gate_ref[...] = jnp.where(lane == 0, gate0, jnp.where(lane == 1, gate1, 0.0))


def _router(x2, norm_g, w_group, b_group, w_expert, b_expert):
    n_tok, dm = x2.shape
    tm = ROW_TILE
    pad = LANES - N_EXPERTS - N_GROUPS
    w = jnp.concatenate([w_expert, w_group, jnp.zeros((dm, pad), F32)], axis=1).astype(BF16)
    b = jnp.concatenate([b_expert, b_group, jnp.zeros((pad,), F32)])[None, :]
    return pl.pallas_call(
        _router_kernel,
        out_shape=(jax.ShapeDtypeStruct((n_tok, dm // LANES, LANES), F32),
                   jax.ShapeDtypeStruct((n_tok, LANES), jnp.int32),
                   jax.ShapeDtypeStruct((n_tok, LANES), F32),
                   jax.ShapeDtypeStruct((SUBLANES, LANES), jnp.int32)),
        grid=(n_tok // tm,),
        in_specs=[pl.BlockSpec((tm, dm), lambda i: (i, 0)),
                  pl.BlockSpec((1, dm), lambda i: (0, 0)),
                  pl.BlockSpec((dm, LANES), lambda i: (0, 0)),
                  pl.BlockSpec((1, LANES), lambda i: (0, 0))],
        out_specs=(pl.BlockSpec((tm, dm // LANES, LANES), lambda i: (i, 0, 0)),
                   pl.BlockSpec((tm, LANES), lambda i: (i, 0)),
                   pl.BlockSpec((tm, LANES), lambda i: (i, 0)),
                   pl.BlockSpec((SUBLANES, LANES), lambda i: (0, 0))),
        scratch_shapes=[pltpu.VMEM((SUBLANES, LANES), F32)],
        compiler_params=_cparams("arbitrary"),
        name="moe_router",
    )(x2, norm_g[None, :], w, b)


def _dispatch_kernel(dest_ref, hn_ref, xd_in_ref, xd_ref, sem):
    del xd_in_ref
    tm = hn_ref.shape[0]
    n = 2 * tm

    def row_copy(a):
        return pltpu.make_async_copy(hn_ref.at[a // 2], xd_ref.at[dest_ref[0, 0, a]], sem)

    def issue(i, c):
        for u in range(DMA_UNROLL):
            row_copy(i * DMA_UNROLL + u).start()
        return c

    lax.fori_loop(0, n // DMA_UNROLL, issue, 0)

    def drain(i, c):
        for u in range(DMA_UNROLL):
            row_copy(i * DMA_UNROLL + u).wait()
        return c

    lax.fori_loop(0, n // DMA_UNROLL, drain, 0)


def _dispatch(hn3, dest, n_rows):
    n_tok, ns, nl = hn3.shape
    tm = GATHER_TM
    dest3 = dest.reshape(n_tok // tm, 1, 2 * tm)
    xd0 = jnp.zeros((n_rows, ns, nl), hn3.dtype)
    return pl.pallas_call(
        _dispatch_kernel,
        out_shape=jax.ShapeDtypeStruct(xd0.shape, xd0.dtype),
        grid=(n_tok // tm,),
        in_specs=[pl.BlockSpec((1, 1, 2 * tm), lambda i: (i, 0, 0), memory_space=pltpu.SMEM),
                  pl.BlockSpec((tm, ns, nl), lambda i: (i, 0, 0)),
                  pl.BlockSpec(memory_space=pl.ANY)],
        out_specs=pl.BlockSpec(memory_space=pl.ANY),
        scratch_shapes=[pltpu.SemaphoreType.DMA],
        input_output_aliases={2: 0},
        compiler_params=_cparams("arbitrary"),
        name="moe_dispatch",
    )(dest3, hn3, xd0)


def _expert_kernel(be_ref, nu_ref, x_ref, w1_ref, w3_ref, w2_ref, y_ref, w13_s, w2_s):
    b = pl.program_id(0)
    ns = x_ref.shape[1]
    de = w1_ref.shape[2]

    @pl.when(b < nu_ref[0])
    def _():
        changed = jnp.logical_or(b == 0, be_ref[b] != be_ref[jnp.maximum(b - 1, 0)])

        @pl.when(changed)
        def _():
            w13_s[:, :de] = w1_ref[0].astype(BF16)
            w13_s[:, de:] = w3_ref[0].astype(BF16)
            w2_s[...] = w2_ref[0].astype(BF16)

        x = jnp.concatenate([x_ref[:, s, :] for s in range(ns)], axis=-1).astype(BF16)
        h = _dot(x, w13_s[...])
        a = jax.nn.silu(h[:, :de]) * h[:, de:]
        y = _dot(a.astype(BF16), w2_s[...])
        for s in range(ns):
            y_ref[:, s, :] = y[:, s * LANES:(s + 1) * LANES]

    @pl.when(b >= nu_ref[0])
    def _():
        y_ref[...] = jnp.zeros_like(y_ref)


def _experts(xd, blk_expert, n_used, w1, w3, w2):
    n_rows, ns, nl = xd.shape
    bm = MOE_BM
    n_blocks = n_rows // bm
    _, dm, de = w1.shape

    def row_map(b, be, nu):
        return (jnp.minimum(b, nu[0] - 1), 0, 0)

    def w_map(b, be, nu):
        return (be[b], 0, 0)

    grid_spec = pltpu.PrefetchScalarGridSpec(
        num_scalar_prefetch=2,
        grid=(n_blocks,),
        in_specs=[pl.BlockSpec((bm, ns, nl), row_map),
                  pl.BlockSpec((1, dm, de), w_map),
                  pl.BlockSpec((1, dm, de), w_map),
                  pl.BlockSpec((1, de, dm), w_map)],
        out_specs=pl.BlockSpec((bm, ns, nl), lambda b, be, nu: (b, 0, 0)),
        scratch_shapes=[pltpu.VMEM((dm, 2 * de), BF16), pltpu.VMEM((de, dm), BF16)],
    )
    return pl.pallas_call(
        _expert_kernel,
        out_shape=jax.ShapeDtypeStruct(xd.shape, F32),
        grid_spec=grid_spec,
        compiler_params=_cparams("arbitrary"),
        name="moe_experts",
    )(blk_expert, n_used, xd, w1, w3, w2)


def _combine_kernel(dest_ref, yd_ref, gate_ref, x_ref, o_ref, buf_ref, sem):
    tm, dm = x_ref.shape
    ns = buf_ref.shape[1]
    n = 2 * tm

    def row_copy(a):
        return pltpu.make_async_copy(yd_ref.at[dest_ref[0, 0, a]],
                                     buf_ref.at[(a % 2) * tm + a // 2], sem)

    def issue(i, c):
        for u in range(DMA_UNROLL):
            row_copy(i * DMA_UNROLL + u).start()
        return c

    lax.fori_loop(0, n // DMA_UNROLL, issue, 0)

    def drain(i, c):
        for u in range(DMA_UNROLL):
            row_copy(i * DMA_UNROLL + u).wait()
        return c

    lax.fori_loop(0, n // DMA_UNROLL, drain, 0)

    g0 = gate_ref[:, 0:1]
    g1 = gate_ref[:, 1:2]
    for s in range(ns):
        y0 = buf_ref[0:tm, s, :]
        y1 = buf_ref[tm:2 * tm, s, :]
        o_ref[:, s * LANES:(s + 1) * LANES] = x_ref[:, s * LANES:(s + 1) * LANES] + (g0 * y0 + g1 * y1)


def _combine(yd, dest, gates, x2):
    n_tok, dm = x2.shape
    _, ns, nl = yd.shape
    tm = GATHER_TM
    dest3 = dest.reshape(n_tok // tm, 1, 2 * tm)
    return pl.pallas_call(
        _combine_kernel,
        out_shape=jax.ShapeDtypeStruct(x2.shape, x2.dtype),
        grid=(n_tok // tm,),
        in_specs=[pl.BlockSpec((1, 1, 2 * tm), lambda i: (i, 0, 0), memory_space=pltpu.SMEM),
                  pl.BlockSpec(memory_space=pl.ANY),
                  pl.BlockSpec((tm, LANES), lambda i: (i, 0)),
                  pl.BlockSpec((tm, dm), lambda i: (i, 0))],
        out_specs=pl.BlockSpec((tm, dm), lambda i: (i, 0)),
        scratch_shapes=[pltpu.VMEM((2 * tm, ns, nl), F32), pltpu.SemaphoreType.DMA],
        compiler_params=_cparams("arbitrary"),
        name="moe_combine",
    )(dest3, yd, gates, x2)


def _moe(x, norm_g, w_group, b_group, w_expert, b_expert, w1, w3, w2):
    nb, seq, dm = x.shape
    n_tok = nb * seq
    x2 = x.reshape(n_tok, dm)
    hn3, meta, gates, counts = _router(x2, norm_g, w_group, b_group, w_expert, b_expert)
    bm = MOE_BM
    n_blocks = (n_tok * TOP_K) // bm + N_EXPERTS
    counts = counts[0, :N_EXPERTS]
    padded = ((counts + bm - 1) // bm) * bm
    pad_ends = jnp.cumsum(padded)
    pad_starts = pad_ends - padded
    eid = meta[:, 0:TOP_K]
    dest = (pad_starts[eid] + meta[:, TOP_K:2 * TOP_K]).astype(jnp.int32).reshape(-1)
    blk_start = jnp.arange(n_blocks, dtype=jnp.int32) * bm
    blk_expert = jnp.clip(jnp.searchsorted(pad_ends, blk_start, side='right'),
                          0, N_EXPERTS - 1).astype(jnp.int32)
    n_used = (pad_ends[-1] // bm).astype(jnp.int32).reshape(1)

    xd = _dispatch(hn3, dest, n_blocks * bm)
    yd = _experts(xd, blk_expert, n_used, w1, w3, w2)
    out = _combine(yd, dest, gates, x2)
    return out.reshape(nb, seq, dm)


def kernel(x, positions, ev_norm, ev_w_in, s5_lambda_re, s5_lambda_im, s5_log_dt, s5_b_re, s5_b_im,
           s5_c_re, s5_c_im, s5_d, s5_glu_w, s5_glu_b, lru_conv_w, lru_conv_b, lru_wa, lru_ba, lru_wi,
           lru_bi, lru_lambda, ev_w_out, od_norm, mla_w_in, mla_cq_norm, mla_ckv_norm, mla_w_uq,
           mla_w_ukv, mla_q_norm, mla_k_norm, mla_w_o, ffn_norm, moe_w_group, moe_b_group,
           moe_w_expert, moe_b_expert, moe_w1, moe_w3, moe_w2):
    depth = ffn_norm.shape[0]
    for layer in range(depth):
        j = layer // 2
        if layer % 2 == 0:
            x = _even_mixer(x, ev_norm[j], ev_w_in[j], s5_lambda_re[j], s5_lambda_im[j], s5_log_dt[j],
                            s5_b_re[j], s5_b_im[j], s5_c_re[j], s5_c_im[j], s5_d[j], s5_glu_w[j],
                            s5_glu_b[j], lru_conv_w[j], lru_conv_b[j], lru_wa[j], lru_ba[j], lru_wi[j],
                            lru_bi[j], lru_lambda[j], ev_w_out[j])
        else:
            x = _mla_mixer(x, od_norm[j], positions, mla_w_in[j], mla_cq_norm[j], mla_ckv_norm[j],
                           mla_w_uq[j], mla_w_ukv[j], mla_q_norm[j], mla_k_norm[j], mla_w_o[j])
        x = _moe(x, ffn_norm[layer], moe_w_group[layer], moe_b_group[layer], moe_w_expert[layer],
                 moe_b_expert[layer], moe_w1[layer], moe_w3[layer], moe_w2[layer])
    return x
```

```python
import functools
import math

import jax
import jax.numpy as jnp
from jax import lax
from jax.experimental import pallas as pl
from jax.experimental.pallas import tpu as pltpu

F32 = jnp.float32
BF16 = jnp.bfloat16

S5_GROUP_CH = 16
S5_STATE = 64
LRU_BLOCKS = 8
CONV_WIDTH = 4
RG_C = 8.0
MLA_HEADS = 8
Q_LORA = 384
KV_LORA = 256
NOPE_DIM = 128
ROPE_DIM = 64
V_DIM = 128
QK_DIM = NOPE_DIM + ROPE_DIM
ROPE_THETA = 10000.0
N_GROUPS = 8
EXPERTS_PER_GROUP = 8
N_EXPERTS = N_GROUPS * EXPERTS_PER_GROUP
TOP_K = 2
RMS_EPS = 1e-6
NEG_INF = -1e30

SUBLANES = 8
LANES = 128
VMEM_LIMIT = 48 * 1024 * 1024

EVEN_TL = 64
ROW_TILE = 512
MOE_BM = 128
ATT_TQ = 512
ATT_TK = 512
GATHER_TM = 256
DMA_UNROLL = 8


def _cparams(*sem):
    return pltpu.CompilerParams(dimension_semantics=tuple(sem), vmem_limit_bytes=VMEM_LIMIT)


def _rms(x, g):
    ms = jnp.mean(x * x, axis=-1, keepdims=True)
    return x * lax.rsqrt(ms + RMS_EPS) * g


def _dot(a, b):
    return jnp.dot(a, b, preferred_element_type=F32)


def _s5_discretize_kernel(lre_ref, lim_ref, ldt_ref, bre_ref, bim_ref,
                          lbre_ref, lbim_ref, bbre_ref, bbim_ref):
    lr = jnp.minimum(lre_ref[...], -1e-4)
    li = lim_ref[...]
    dt = jnp.exp(ldt_ref[...])
    mag = jnp.exp(lr * dt)
    lb_re = mag * jnp.cos(li * dt)
    lb_im = mag * jnp.sin(li * dt)
    den = lr * lr + li * li
    num_re = lb_re - 1.0
    f_re = (num_re * lr + lb_im * li) / den
    f_im = (lb_im * lr - num_re * li) / den
    lbre_ref[...] = lb_re
    lbim_ref[...] = lb_im
    br = bre_ref[...]
    bi = bim_ref[...]
    bbre_ref[...] = f_re[:, None, :] * br - f_im[:, None, :] * bi
    bbim_ref[...] = f_re[:, None, :] * bi + f_im[:, None, :] * br


def _s5_discretize(lam_re, lam_im, log_dt, b_re, b_im):
    g, p = lam_re.shape
    h = b_re.shape[-1]
    b_re_t = jnp.swapaxes(b_re, 1, 2)
    b_im_t = jnp.swapaxes(b_im, 1, 2)
    return pl.pallas_call(
        _s5_discretize_kernel,
        out_shape=(jax.ShapeDtypeStruct((g, p), F32), jax.ShapeDtypeStruct((g, p), F32),
                   jax.ShapeDtypeStruct((g, h, p), F32), jax.ShapeDtypeStruct((g, h, p), F32)),
        name="s5_discretize",
    )(lam_re, lam_im, log_dt[:, None], b_re_t, b_im_t)


def _even_inproj_kernel(x_ref, g_ref, w_ref, o_ref, h_ref):
    nb, tl, _ = x_ref.shape
    nc = h_ref.shape[0]
    g = g_ref[...]
    for b in range(nb):
        h = _rms(x_ref[b], g)
        for c in range(nc):
            h_ref[c, pl.ds(b, tl, stride=nb), :] = h[:, c * LANES:(c + 1) * LANES]
    h_all = jnp.concatenate([h_ref[c] for c in range(nc)], axis=-1)
    o_ref[...] = _dot(h_all.astype(BF16), w_ref[...]).astype(o_ref.dtype)


def _even_inproj(x, g, w):
    nb, seq, d = x.shape
    n_out = w.shape[1]
    tl = EVEN_TL
    return pl.pallas_call(
        _even_inproj_kernel,
        out_shape=jax.ShapeDtypeStruct((seq * nb, n_out), BF16),
        grid=(seq // tl,),
        in_specs=[pl.BlockSpec((nb, tl, d), lambda i: (0, i, 0)),
                  pl.BlockSpec((1, d), lambda i: (0, 0)),
                  pl.BlockSpec((d, n_out), lambda i: (0, 0))],
        out_specs=pl.BlockSpec((tl * nb, n_out), lambda i: (i, 0)),
        scratch_shapes=[pltpu.VMEM((d // LANES, tl * nb, LANES), F32)],
        compiler_params=_cparams("parallel"),
        name="even_inproj",
    )(x, g, w)


def _s5_kernel(u_ref, bm_ref, lre_ref, lim_ref, cm_ref, d_ref, gw_ref, gb_ref, o_ref,
               bu_ref, sb_ref, st_ref, *, nb):
    rows, width = u_ref.shape
    n_half = bm_ref.shape[0]
    kin = width // n_half
    ncol = bm_ref.shape[2]
    nre = ncol // 2
    chunk = 512
    steps = rows // nb

    @pl.when(pl.program_id(0) == 0)
    def _():
        st_ref[...] = jnp.zeros_like(st_ref)

    u = u_ref[...]
    for j in range(n_half):
        bu_ref[:, j * ncol:(j + 1) * ncol] = _dot(u[:, j * kin:(j + 1) * kin], bm_ref[j])

    for j in range(n_half):
        for c in range(nre // chunk):
            cr = j * ncol + c * chunk
            ci = cr + nre
            lc = j * nre + c * chunk
            lr = lre_ref[:, lc:lc + chunk]
            li = lim_ref[:, lc:lc + chunk]

            def body(i, carry, cr=cr, ci=ci, lr=lr, li=li):
                sr, si = carry
                r0 = pl.multiple_of(i * (2 * nb), 2 * nb)
                sr1 = lr * sr - li * si + bu_ref[pl.ds(r0, nb), cr:cr + chunk]
                si1 = lr * si + li * sr + bu_ref[pl.ds(r0, nb), ci:ci + chunk]
                sr2 = lr * sr1 - li * si1 + bu_ref[pl.ds(r0 + nb, nb), cr:cr + chunk]
                si2 = lr * si1 + li * sr1 + bu_ref[pl.ds(r0 + nb, nb), ci:ci + chunk]
                sb_ref[pl.ds(r0, 2 * nb), cr:cr + chunk] = jnp.concatenate([sr1, sr2], 0).astype(BF16)
                sb_ref[pl.ds(r0, 2 * nb), ci:ci + chunk] = jnp.concatenate([si1, si2], 0).astype(BF16)
                return sr2, si2

            sr, si = lax.fori_loop(0, steps // 2, body,
                                   (st_ref[:, cr:cr + chunk], st_ref[:, ci:ci + chunk]))
            st_ref[:, cr:cr + chunk] = sr
            st_ref[:, ci:ci + chunk] = si

    ys = [_dot(sb_ref[:, j * ncol:(j + 1) * ncol], cm_ref[j]) for j in range(n_half)]
    y = jnp.concatenate(ys, axis=-1) + d_ref[...] * u.astype(F32)
    y = jax.nn.gelu(y)
    z = _dot(y.astype(BF16), gw_ref[...]) + gb_ref[...]
    o_ref[...] = (y * jax.nn.sigmoid(z)).astype(o_ref.dtype)


def _s5_mixer(proj, nb, bm, lre, lim, cm, d, glu_w, glu_b):
    rows_total = proj.shape[0]
    width = d.shape[1]
    rows = EVEN_TL * nb
    n_half, _, ncol = bm.shape
    kern = functools.partial(_s5_kernel, nb=nb)
    return pl.pallas_call(
        kern,
        out_shape=jax.ShapeDtypeStruct((rows_total, width), BF16),
        grid=(rows_total // rows,),
        in_specs=[pl.BlockSpec((rows, width), lambda i: (i, 0)),
                  pl.BlockSpec(bm.shape, lambda i: (0, 0, 0)),
                  pl.BlockSpec(lre.shape, lambda i: (0, 0)),
                  pl.BlockSpec(lim.shape, lambda i: (0, 0)),
                  pl.BlockSpec(cm.shape, lambda i: (0, 0, 0)),
                  pl.BlockSpec((1, width), lambda i: (0, 0)),
                  pl.BlockSpec(glu_w.shape, lambda i: (0, 0)),
                  pl.BlockSpec((1, width), lambda i: (0, 0))],
        out_specs=pl.BlockSpec((rows, width), lambda i: (i, 0)),
        scratch_shapes=[pltpu.VMEM((rows, n_half * ncol), F32),
                        pltpu.VMEM((rows, n_half * ncol), BF16),
                        pltpu.VMEM((nb, n_half * ncol), F32)],
        compiler_params=_cparams("arbitrary"),
        name="s5_mixer",
    )(proj, bm, lre, lim, cm, d, glu_w, glu_b)


def _lru_kernel(x_ref, gate_ref, cw_ref, cb_ref, wa_ref, ba_ref, wi_ref, bi_ref, lam_ref, o_ref,
                xp_ref, a_ref, b_ref, h_ref, *, nb):
    rows, width = x_ref.shape
    halo = (CONV_WIDTH - 1) * nb
    steps = rows // nb

    @pl.when(pl.program_id(0) == 0)
    def _():
        xp_ref[0:halo, :] = jnp.zeros((halo, width), F32)
        h_ref[...] = jnp.zeros_like(h_ref)

    xp_ref[halo:halo + rows, :] = x_ref[...].astype(F32)
    xf = cb_ref[...] + cw_ref[0:1, :] * xp_ref[0:rows, :]
    for k in range(1, CONV_WIDTH):
        xf = xf + cw_ref[k:k + 1, :] * xp_ref[k * nb:k * nb + rows, :]
    xp_ref[0:halo, :] = xp_ref[rows:rows + halo, :]

    xb = xf.astype(BF16)
    nblk = wa_ref.shape[0]
    kb = width // nblk
    ga = jnp.concatenate([_dot(xb[:, j * kb:(j + 1) * kb], wa_ref[j]) for j in range(nblk)], -1)
    gi = jnp.concatenate([_dot(xb[:, j * kb:(j + 1) * kb], wi_ref[j]) for j in range(nblk)], -1)
    r = jax.nn.sigmoid(ga + ba_ref[...])
    ig = jax.nn.sigmoid(gi + bi_ref[...])
    log_a = RG_C * r * jax.nn.log_sigmoid(lam_ref[...])
    a_ref[...] = jnp.exp(log_a)
    th = jnp.tanh(log_a)
    b_ref[...] = jnp.sqrt(-2.0 * th / (1.0 - th)) * (ig * xf)

    def body(t, h):
        r0 = pl.multiple_of(t * nb, nb)
        h = a_ref[pl.ds(r0, nb), :] * h + b_ref[pl.ds(r0, nb), :]
        b_ref[pl.ds(r0, nb), :] = h
        return h

    h_ref[...] = lax.fori_loop(0, steps, body, h_ref[...])
    o_ref[...] = (b_ref[...] * jax.nn.gelu(gate_ref[...].astype(F32))).astype(o_ref.dtype)


def _lru_mixer(proj, nb, conv_w, conv_b, wa, ba, wi, bi, lam):
    rows_total = proj.shape[0]
    width = conv_w.shape[1]
    rows = EVEN_TL * nb
    halo = (CONV_WIDTH - 1) * nb
    kern = functools.partial(_lru_kernel, nb=nb)
    vec = pl.BlockSpec((1, width), lambda i: (0, 0))
    return pl.pallas_call(
        kern,
        out_shape=jax.ShapeDtypeStruct((rows_total, width), BF16),
        grid=(rows_total // rows,),
        in_specs=[pl.BlockSpec((rows, width), lambda i: (i, 1)),
                  pl.BlockSpec((rows, width), lambda i: (i, 2)),
                  pl.BlockSpec(conv_w.shape, lambda i: (0, 0)),
                  vec,
                  pl.BlockSpec(wa.shape, lambda i: (0, 0, 0)),
                  vec,
                  pl.BlockSpec(wi.shape, lambda i: (0, 0, 0)),
                  vec, vec],
        out_specs=pl.BlockSpec((rows, width), lambda i: (i, 0)),
        scratch_shapes=[pltpu.VMEM((rows + halo, width), F32),
                        pltpu.VMEM((rows, width), F32),
                        pltpu.VMEM((rows, width), F32),
                        pltpu.VMEM((nb, width), F32)],
        compiler_params=_cparams("arbitrary"),
        name="lru_mixer",
    )(proj, proj, conv_w, conv_b, wa, ba, wi, bi, lam)


def _even_outproj_kernel(ya_ref, yb_ref, wa_ref, wb_ref, x_ref, o_ref, y_ref):
    nb, tl, _ = x_ref.shape
    nc = y_ref.shape[0]
    y = _dot(ya_ref[...], wa_ref[...]) + _dot(yb_ref[...], wb_ref[...])
    for c in range(nc):
        y_ref[c] = y[:, c * LANES:(c + 1) * LANES]
    for b in range(nb):
        for c in range(nc):
            o_ref[b, :, c * LANES:(c + 1) * LANES] = (x_ref[b, :, c * LANES:(c + 1) * LANES]
                                                     + y_ref[c, pl.ds(b, tl, stride=nb), :])


def _even_outproj(ya, yb, w_a, w_b, x):
    nb, seq, d = x.shape
    tl = EVEN_TL
    rows = tl * nb
    wd = ya.shape[1]
    return pl.pallas_call(
        _even_outproj_kernel,
        out_shape=jax.ShapeDtypeStruct(x.shape, x.dtype),
        grid=(seq // tl,),
        in_specs=[pl.BlockSpec((rows, wd), lambda i: (i, 0)),
                  pl.BlockSpec((rows, wd), lambda i: (i, 0)),
                  pl.BlockSpec(w_a.shape, lambda i: (0, 0)),
                  pl.BlockSpec(w_b.shape, lambda i: (0, 0)),
                  pl.BlockSpec((nb, tl, d), lambda i: (0, i, 0))],
        out_specs=pl.BlockSpec((nb, tl, d), lambda i: (0, i, 0)),
        scratch_shapes=[pltpu.VMEM((d // LANES, rows, LANES), F32)],
        compiler_params=_cparams("parallel"),
        name="even_outproj",
    )(ya, yb, w_a, w_b, x)


def _block_diag(blocks):
    n, r, c = blocks.shape
    eye = jnp.eye(n, dtype=blocks.dtype)
    return jnp.einsum('nrc,nm->nrmc', blocks, eye).reshape(n * r, n * c)


def _even_mixer(x, norm_g, w_in, lam_re, lam_im, log_dt, b_re, b_im, c_re, c_im, d, glu_w, glu_b,
                conv_w, conv_b, wa, ba, wi, bi, lam, w_out):
    nb, seq, dm = x.shape
    g, p = lam_re.shape
    hch = b_re.shape[-1]
    width = g * hch
    n_half = 2
    gh = g // n_half

    proj = _even_inproj(x, norm_g[None, :], w_in.astype(BF16))

    lb_re, lb_im, bb_re, bb_im = _s5_discretize(lam_re, lam_im, log_dt, b_re, b_im)
    bb_re = bb_re.reshape(n_half, gh, hch, p)
    bb_im = bb_im.reshape(n_half, gh, hch, p)
    bm = jnp.stack([jnp.concatenate([_block_diag(bb_re[j]), _block_diag(bb_im[j])], axis=1)
                    for j in range(n_half)]).astype(BF16)
    c_re_t = jnp.swapaxes(c_re, 1, 2).reshape(n_half, gh, p, hch)
    c_im_t = jnp.swapaxes(c_im, 1, 2).reshape(n_half, gh, p, hch)
    cm = jnp.stack([jnp.concatenate([_block_diag(c_re_t[j]), -_block_diag(c_im_t[j])], axis=0)
                    for j in range(n_half)]).astype(BF16)
    lre = jnp.broadcast_to(lb_re.reshape(1, g * p), (nb, g * p))
    lim = jnp.broadcast_to(lb_im.reshape(1, g * p), (nb, g * p))
    ya = _s5_mixer(proj, nb, bm, lre, lim, cm, d.reshape(1, width), glu_w.astype(BF16),
                   glu_b[None, :])

    lw = conv_w.shape[1]
    nblk = 2
    per = LRU_BLOCKS // nblk
    wa_bd = jnp.stack([_block_diag(wa[j * per:(j + 1) * per]) for j in range(nblk)]).astype(BF16)
    wi_bd = jnp.stack([_block_diag(wi[j * per:(j + 1) * per]) for j in range(nblk)]).astype(BF16)
    yb = _lru_mixer(proj, nb, conv_w, conv_b[None, :], wa_bd, ba[None, :], wi_bd, bi[None, :],
                    lam[None, :])

    w_out_b = w_out.astype(BF16)
    return _even_outproj(ya, yb, w_out_b[:width], w_out_b[width:], x)


def _rope_table_kernel(pos_ref, inv_ref, cos_ref, sin_ref):
    ang = pos_ref[...].astype(F32) * inv_ref[...]
    cos_ref[...] = jnp.cos(ang)
    sin_ref[...] = jnp.sin(ang)


def _rope_tables(positions):
    nb, seq = positions.shape
    half = ROPE_DIM // 2
    per_row = LANES // half
    n_tok = nb * seq
    inv_freq = 1.0 / (ROPE_THETA ** (jnp.arange(0, ROPE_DIM, 2, dtype=F32) / ROPE_DIM))
    pos_rep = jnp.repeat(positions.reshape(n_tok // per_row, per_row), half, axis=1)
    inv_rep = jnp.tile(inv_freq, per_row)[None, :]
    cos, sin = pl.pallas_call(
        _rope_table_kernel,
        out_shape=(jax.ShapeDtypeStruct(pos_rep.shape, F32), jax.ShapeDtypeStruct(pos_rep.shape, F32)),
        name="rope_table",
    )(pos_rep, inv_rep)
    cos = cos.reshape(n_tok, half)
    sin = sin.reshape(n_tok, half)
    cc = jnp.concatenate([cos, cos], axis=-1)
    ss = jnp.concatenate([-sin, sin], axis=-1)
    return cc, ss


def _mla_proj_kernel(x_ref, g_ref, win_ref, cqn_ref, ckvn_ref, wq_ref, wkv_ref, qg_ref, kg_ref,
                     cc_ref, ss_ref, qn_ref, qr_ref, kn_ref, kr_ref, v_ref):
    h = _rms(x_ref[...], g_ref[...])
    proj = _dot(h.astype(BF16), win_ref[...])
    c_q = _rms(proj[:, :Q_LORA], cqn_ref[...])
    c_kv = _rms(proj[:, Q_LORA:Q_LORA + KV_LORA], ckvn_ref[...])
    o = Q_LORA + KV_LORA
    k_r = proj[:, o:o + ROPE_DIM]
    k_sw = proj[:, o + ROPE_DIM:o + 2 * ROPE_DIM]
    cc = cc_ref[...]
    ss = ss_ref[...]
    qg = qg_ref[...]
    kg = kg_ref[...]
    q = _dot(c_q.astype(BF16), wq_ref[...])
    kv = _dot(c_kv.astype(BF16), wkv_ref[...])
    nh = MLA_HEADS
    scale = 1.0 / math.sqrt(QK_DIM)
    k_rot = k_r * kg[1:2, :ROPE_DIM] * cc + k_sw * kg[2:3, :ROPE_DIM] * ss
    k_r_ss = jnp.sum(k_r * k_r, axis=-1, keepdims=True)
    for hd in range(nh):
        q_n = q[:, hd * NOPE_DIM:(hd + 1) * NOPE_DIM]
        ro = nh * NOPE_DIM + hd * ROPE_DIM
        q_r = q[:, ro:ro + ROPE_DIM]
        q_sw = q[:, ro + nh * ROPE_DIM:ro + nh * ROPE_DIM + ROPE_DIM]
        ms = (jnp.sum(q_n * q_n, axis=-1, keepdims=True)
              + jnp.sum(q_r * q_r, axis=-1, keepdims=True)) / QK_DIM
        rs = lax.rsqrt(ms + RMS_EPS) * scale
        qn_ref[0, hd] = (q_n * rs * qg[0:1, :]).astype(qn_ref.dtype)
        qr_ref[0, hd] = ((q_r * qg[1:2, :ROPE_DIM] * cc + q_sw * qg[2:3, :ROPE_DIM] * ss) * rs
                         ).astype(qr_ref.dtype)
        ko = hd * (NOPE_DIM + V_DIM)
        k_n = kv[:, ko:ko + NOPE_DIM]
        ms_k = (jnp.sum(k_n * k_n, axis=-1, keepdims=True) + k_r_ss) / QK_DIM
        rs_k = lax.rsqrt(ms_k + RMS_EPS)
        kn_ref[0, hd] = (k_n * rs_k * kg[0:1, :]).astype(kn_ref.dtype)
        kr_ref[0, hd] = (k_rot * rs_k).astype(kr_ref.dtype)
        v_ref[0, hd] = kv[:, ko + NOPE_DIM:ko + NOPE_DIM + V_DIM].astype(v_ref.dtype)


def _swap_halves(a, axis=-1):
    h = a.shape[axis] // 2
    lo = lax.slice_in_dim(a, 0, h, axis=axis)
    hi = lax.slice_in_dim(a, h, 2 * h, axis=axis)
    return jnp.concatenate([hi, lo], axis=axis)


def _mla_proj(x, norm_g, positions, w_in, cq_norm, ckv_norm, w_uq, w_ukv, q_norm, k_norm):
    nb, seq, dm = x.shape
    nh = MLA_HEADS
    tm = ROW_TILE
    n_tok = nb * seq
    cc, ss = _rope_tables(positions)
    o = Q_LORA + KV_LORA
    w_kr = w_in[:, o:o + ROPE_DIM]
    w_in_x = jnp.concatenate([w_in, _swap_halves(w_kr)], axis=1).astype(BF16)
    wq = w_uq.reshape(Q_LORA, nh, QK_DIM)
    wq_n = wq[:, :, :NOPE_DIM].reshape(Q_LORA, nh * NOPE_DIM)
    wq_r = wq[:, :, NOPE_DIM:]
    wq_x = jnp.concatenate([wq_n, wq_r.reshape(Q_LORA, nh * ROPE_DIM),
                            _swap_halves(wq_r).reshape(Q_LORA, nh * ROPE_DIM)], axis=1).astype(BF16)

    def gains(gv):
        pad = jnp.zeros((NOPE_DIM - ROPE_DIM,), F32)
        r = gv[NOPE_DIM:]
        return jnp.stack([gv[:NOPE_DIM], jnp.concatenate([r, pad]),
                          jnp.concatenate([_swap_halves(r), pad])])

    x2 = x.reshape(n_tok, dm)
    per_b = seq // tm
    full = lambda a: pl.BlockSpec(a.shape, lambda i: (0,) * a.ndim)
    args = (x2, norm_g[None, :], w_in_x, cq_norm[None, :], ckv_norm[None, :], wq_x,
            w_ukv.astype(BF16), gains(q_norm), gains(k_norm), cc, ss)
    in_specs = [pl.BlockSpec((tm, dm), lambda i: (i, 0))] + [full(a) for a in args[1:9]] + [
        pl.BlockSpec((tm, ROPE_DIM), lambda i: (i, 0)), pl.BlockSpec((tm, ROPE_DIM), lambda i: (i, 0))]

    def head_spec(dh):
        return pl.BlockSpec((1, nh, tm, dh), lambda i: (i // per_b, 0, i % per_b, 0))

    def head_shape(dh):
        return jax.ShapeDtypeStruct((nb, nh, seq, dh), BF16)

    return pl.pallas_call(
        _mla_proj_kernel,
        out_shape=(head_shape(NOPE_DIM), head_shape(ROPE_DIM), head_shape(NOPE_DIM),
                   head_shape(ROPE_DIM), head_shape(V_DIM)),
        grid=(n_tok // tm,),
        in_specs=in_specs,
        out_specs=(head_spec(NOPE_DIM), head_spec(ROPE_DIM), head_spec(NOPE_DIM),
                   head_spec(ROPE_DIM), head_spec(V_DIM)),
        compiler_params=_cparams("parallel"),
        name="mla_proj",
    )(*args)


def _attention_kernel(qn_ref, qr_ref, kn_ref, kr_ref, v_ref, o_ref):
    tq = qn_ref.shape[2]
    tk = ATT_TK
    qi = pl.program_id(2)
    q = jnp.concatenate([qn_ref[0, 0], qr_ref[0, 0]], axis=-1)
    nt = (((1,), (1,)), ((), ()))

    def scores(kb):
        k0 = pl.multiple_of(kb * tk, tk)
        k = jnp.concatenate([kn_ref[0, 0, pl.ds(k0, tk), :], kr_ref[0, 0, pl.ds(k0, tk), :]], axis=-1)
        return lax.dot_general(q, k, nt, preferred_element_type=F32), k0

    def update(s, k0, carry):
        m, l, acc = carry
        m_new = jnp.maximum(m, jnp.max(s, axis=-1, keepdims=True))
        alpha = jnp.exp(m - m_new)
        p = jnp.exp(s - m_new)
        l = alpha * l + jnp.sum(p, axis=-1, keepdims=True)
        acc = alpha * acc + _dot(p.astype(BF16), v_ref[0, 0, pl.ds(k0, tk), :])
        return m_new, l, acc

    def body(kb, carry):
        s, k0 = scores(kb)
        return update(s, k0, carry)

    init = (jnp.full((tq, 1), NEG_INF, F32), jnp.zeros((tq, 1), F32), jnp.zeros((tq, V_DIM), F32))
    n_full = (qi * tq) // tk
    carry = lax.fori_loop(0, n_full, body, init)
    for d in range(tq // tk):
        s, k0 = scores(n_full + d)
        q_pos = qi * tq + lax.broadcasted_iota(jnp.int32, s.shape, 0)
        k_pos = k0 + lax.broadcasted_iota(jnp.int32, s.shape, 1)
        s = jnp.where(k_pos <= q_pos, s, NEG_INF)
        carry = update(s, k0, carry)
    _, l, acc = carry
    o_ref[0] = (acc / l).astype(o_ref.dtype)


def _attention(qn, qr, kn, kr, v):
    nb, nh, seq, _ = qn.shape
    tq = ATT_TQ

    def qspec(dh):
        return pl.BlockSpec((1, 1, tq, dh), lambda b, h, i: (b, h, i, 0))

    def kspec(dh):
        return pl.BlockSpec((1, 1, seq, dh), lambda b, h, i: (b, h, 0, 0))

    return pl.pallas_call(
        _attention_kernel,
        out_shape=jax.ShapeDtypeStruct((nb, seq, nh * V_DIM), BF16),
        grid=(nb, nh, seq // tq),
        in_specs=[qspec(NOPE_DIM), qspec(ROPE_DIM), kspec(NOPE_DIM), kspec(ROPE_DIM), kspec(V_DIM)],
        out_specs=pl.BlockSpec((1, tq, V_DIM), lambda b, h, i: (b, i, h)),
        compiler_params=_cparams("parallel", "parallel", "arbitrary"),
        name="mla_attention",
    )(qn, qr, kn, kr, v)


def _proj_residual_kernel(a_ref, w_ref, x_ref, o_ref):
    o_ref[...] = x_ref[...] + _dot(a_ref[...], w_ref[...])


def _proj_residual(a, w, x2):
    n_tok, dm = x2.shape
    tm = ROW_TILE
    return pl.pallas_call(
        _proj_residual_kernel,
        out_shape=jax.ShapeDtypeStruct(x2.shape, x2.dtype),
        grid=(n_tok // tm,),
        in_specs=[pl.BlockSpec((tm, a.shape[1]), lambda i: (i, 0)),
                  pl.BlockSpec(w.shape, lambda i: (0, 0)),
                  pl.BlockSpec((tm, dm), lambda i: (i, 0))],
        out_specs=pl.BlockSpec((tm, dm), lambda i: (i, 0)),
        compiler_params=_cparams("parallel"),
        name="proj_residual",
    )(a, w, x2)


def _mla_mixer(x, norm_g, positions, w_in, cq_norm, ckv_norm, w_uq, w_ukv, q_norm, k_norm, w_o):
    nb, seq, dm = x.shape
    qn, qr, kn, kr, v = _mla_proj(x, norm_g, positions, w_in, cq_norm, ckv_norm, w_uq, w_ukv,
                                  q_norm, k_norm)
    o = _attention(qn, qr, kn, kr, v)
    out = _proj_residual(o.reshape(nb * seq, -1), w_o.astype(BF16), x.reshape(nb * seq, dm))
    return out.reshape(nb, seq, dm)


def _router_kernel(x_ref, g_ref, w_ref, b_ref, hn_ref, meta_ref, gate_ref, cnt_ref, base_ref):
    tm, dm = x_ref.shape
    ne = N_EXPERTS

    @pl.when(pl.program_id(0) == 0)
    def _():
        base_ref[...] = jnp.zeros_like(base_ref)

    hn = _rms(x_ref[...], g_ref[...])
    for s in range(dm // LANES):
        hn_ref[:, s, :] = hn[:, s * LANES:(s + 1) * LANES]
    logits = _dot(hn.astype(BF16), w_ref[...]) + b_ref[...]
    lane = lax.broadcasted_iota(jnp.int32, logits.shape, 1)
    big = jnp.int32(1 << 20)

    def first_argmax(v):
        m = jnp.max(v, axis=-1, keepdims=True)
        idx = jnp.min(jnp.where(v == m, lane, big), axis=-1, keepdims=True)
        return m, idx

    is_g = (lane >= ne) & (lane < ne + N_GROUPS)
    gl = jnp.where(is_g, logits, NEG_INF)
    g_max, g_idx = first_argmax(gl)
    g_sel = g_idx - ne
    p_g = 1.0 / jnp.sum(jnp.where(is_g, jnp.exp(gl - g_max), 0.0), axis=-1, keepdims=True)
    in_grp = (lane >= g_sel * EXPERTS_PER_GROUP) & (lane < (g_sel + 1) * EXPERTS_PER_GROUP)
    el = jnp.where(in_grp, logits, NEG_INF)
    v0, e0 = first_argmax(el)
    v1, e1 = first_argmax(jnp.where(lane == e0, NEG_INF, el))
    t = jnp.exp(v1 - v0)
    gate0 = p_g / (1.0 + t)
    gate1 = p_g * t / (1.0 + t)

    oh0 = (lane == e0).astype(F32)
    oh1 = (lane == e1).astype(F32)
    both = oh0 + oh1
    r_i = lax.broadcasted_iota(jnp.int32, (tm, tm), 0)
    c_i = lax.broadcasted_iota(jnp.int32, (tm, tm), 1)
    strict_lower = (c_i < r_i).astype(BF16)
    before = _dot(strict_lower, both.astype(BF16)) + base_ref[0:1, :]
    rank0 = jnp.sum(oh0 * before, axis=-1, keepdims=True)
    rank1 = jnp.sum(oh1 * before, axis=-1, keepdims=True)
    base_ref[0:1, :] = base_ref[0:1, :] + jnp.sum(both, axis=0, keepdims=True)
    cnt_ref[...] = jnp.broadcast_to(base_ref[0:1, :], cnt_ref.shape).astype(jnp.int32)

    meta = jnp.where(lane == 0, e0, jnp.where(lane == 1, e1, jnp.where(
        lane == 2, rank0.astype(jnp.int32), jnp.where(lane == 3, rank1.astype(jnp.int32), 0))))
    meta_ref[...] = meta
    gate_ref[...] = jnp.where(lane == 0, gate0, jnp.where(lane == 1, gate1, 0.0))


def _router(x2, norm_g, w_group, b_group, w_expert, b_expert):
    n_tok, dm = x2.shape
    tm = ROW_TILE
    pad = LANES - N_EXPERTS - N_GROUPS
    w = jnp.concatenate([w_expert, w_group, jnp.zeros((dm, pad), F32)], axis=1).astype(BF16)
    b = jnp.concatenate([b_expert, b_group, jnp.zeros((pad,), F32)])[None, :]
    return pl.pallas_call(
        _router_kernel,
        out_shape=(jax.ShapeDtypeStruct((n_tok, dm // LANES, LANES), F32),
                   jax.ShapeDtypeStruct((n_tok, LANES), jnp.int32),
                   jax.ShapeDtypeStruct((n_tok, LANES), F32),
                   jax.ShapeDtypeStruct((SUBLANES, LANES), jnp.int32)),
        grid=(n_tok // tm,),
        in_specs=[pl.BlockSpec((tm, dm), lambda i: (i, 0)),
                  pl.BlockSpec((1, dm), lambda i: (0, 0)),
                  pl.BlockSpec((dm, LANES), lambda i: (0, 0)),
                  pl.BlockSpec((1, LANES), lambda i: (0, 0))],
        out_specs=(pl.BlockSpec((tm, dm // LANES, LANES), lambda i: (i, 0, 0)),
                   pl.BlockSpec((tm, LANES), lambda i: (i, 0)),
                   pl.BlockSpec((tm, LANES), lambda i: (i, 0)),
                   pl.BlockSpec((SUBLANES, LANES), lambda i: (0, 0))),
        scratch_shapes=[pltpu.VMEM((SUBLANES, LANES), F32)],
        compiler_params=_cparams("arbitrary"),
        name="moe_router",
    )(x2, norm_g[None, :], w, b)


def _dispatch_kernel(dest_ref, hn_ref, xd_in_ref, xd_ref, sem):
    del xd_in_ref
    tm = hn_ref.shape[0]

    def row_copy(t, k):
        return pltpu.make_async_copy(hn_ref.at[t], xd_ref.at[dest_ref[0, 0, 2 * t + k]], sem)

    def issue(i, c):
        for u in range(DMA_UNROLL):
            for k in range(TOP_K):
                row_copy(i * DMA_UNROLL + u, k).start(priority=k)
        return c

    lax.fori_loop(0, tm // DMA_UNROLL, issue, 0)

    def drain(i, c):
        for u in range(DMA_UNROLL):
            for k in range(TOP_K):
                row_copy(i * DMA_UNROLL + u, k).wait()
        return c

    lax.fori_loop(0, tm // DMA_UNROLL, drain, 0)


def _dispatch(hn3, dest, n_rows):
    n_tok, ns, nl = hn3.shape
    tm = GATHER_TM
    dest3 = dest.reshape(n_tok // tm, 1, 2 * tm)
    xd0 = jnp.zeros((n_rows, ns, nl), hn3.dtype)
    return pl.pallas_call(
        _dispatch_kernel,
        out_shape=jax.ShapeDtypeStruct(xd0.shape, xd0.dtype),
        grid=(n_tok // tm,),
        in_specs=[pl.BlockSpec((1, 1, 2 * tm), lambda i: (i, 0, 0), memory_space=pltpu.SMEM),
                  pl.BlockSpec((tm, ns, nl), lambda i: (i, 0, 0)),
                  pl.BlockSpec(memory_space=pl.ANY)],
        out_specs=pl.BlockSpec(memory_space=pl.ANY),
        scratch_shapes=[pltpu.SemaphoreType.DMA],
        input_output_aliases={2: 0},
        compiler_params=_cparams("arbitrary"),
        name="moe_dispatch",
    )(dest3, hn3, xd0)


def _expert_kernel(be_ref, nu_ref, x_ref, w1_ref, w3_ref, w2_ref, y_ref, w13_s, w2_s):
    b = pl.program_id(0)
    ns = x_ref.shape[1]
    de = w1_ref.shape[3]

    @pl.when(b < nu_ref[0])
    def _():
        changed = jnp.logical_or(b == 0, be_ref[b] != be_ref[jnp.maximum(b - 1, 0)])

        @pl.when(changed)
        def _():
            w13_s[:, :de] = w1_ref[0, 0].astype(BF16)
            w13_s[:, de:] = w3_ref[0, 0].astype(BF16)
            w2_s[...] = w2_ref[0, 0].astype(BF16)

        x = jnp.concatenate([x_ref[:, s, :] for s in range(ns)], axis=-1).astype(BF16)
        h = _dot(x, w13_s[...])
        a = jax.nn.silu(h[:, :de]) * h[:, de:]
        y = _dot(a.astype(BF16), w2_s[...])
        for s in range(ns):
            y_ref[:, s, :] = y[:, s * LANES:(s + 1) * LANES]

    @pl.when(b >= nu_ref[0])
    def _():
        y_ref[...] = jnp.zeros_like(y_ref)


def _experts(xd, blk_expert, n_used, w1, w3, w2, layer):
    n_rows, ns, nl = xd.shape
    bm = MOE_BM
    n_blocks = n_rows // bm
    _, _, dm, de = w1.shape

    def row_map(b, be, nu):
        return (jnp.minimum(b, nu[0] - 1), 0, 0)

    def w_map(b, be, nu):
        return (layer, be[b], 0, 0)

    grid_spec = pltpu.PrefetchScalarGridSpec(
        num_scalar_prefetch=2,
        grid=(n_blocks,),
        in_specs=[pl.BlockSpec((bm, ns, nl), row_map),
                  pl.BlockSpec((1, 1, dm, de), w_map),
                  pl.BlockSpec((1, 1, dm, de), w_map),
                  pl.BlockSpec((1, 1, de, dm), w_map)],
        out_specs=pl.BlockSpec((bm, ns, nl), lambda b, be, nu: (b, 0, 0)),
        scratch_shapes=[pltpu.VMEM((dm, 2 * de), BF16), pltpu.VMEM((de, dm), BF16)],
    )
    return pl.pallas_call(
        _expert_kernel,
        out_shape=jax.ShapeDtypeStruct(xd.shape, F32),
        grid_spec=grid_spec,
        compiler_params=_cparams("arbitrary"),
        name="moe_experts",
    )(blk_expert, n_used, xd, w1, w3, w2)


def _combine_kernel(dest_ref, yd_ref, gate_ref, x_ref, o_ref, buf_ref, sem):
    tm, dm = x_ref.shape
    ns = buf_ref.shape[1]

    def row_copy(t, k):
        return pltpu.make_async_copy(yd_ref.at[dest_ref[0, 0, 2 * t + k]], buf_ref.at[k * tm + t], sem)

    def issue(i, c):
        for u in range(DMA_UNROLL):
            for k in range(TOP_K):
                row_copy(i * DMA_UNROLL + u, k).start(priority=k)
        return c

    lax.fori_loop(0, tm // DMA_UNROLL, issue, 0)

    def drain(i, c):
        for u in range(DMA_UNROLL):
            for k in range(TOP_K):
                row_copy(i * DMA_UNROLL + u, k).wait()
        return c

    lax.fori_loop(0, tm // DMA_UNROLL, drain, 0)

    g0 = gate_ref[:, 0:1]
    g1 = gate_ref[:, 1:2]
    for s in range(ns):
        y0 = buf_ref[0:tm, s, :]
        y1 = buf_ref[tm:2 * tm, s, :]
        o_ref[:, s * LANES:(s + 1) * LANES] = x_ref[:, s * LANES:(s + 1) * LANES] + (g0 * y0 + g1 * y1)


def _combine(yd, dest, gates, x2):
    n_tok, dm = x2.shape
    _, ns, nl = yd.shape
    tm = GATHER_TM
    dest3 = dest.reshape(n_tok // tm, 1, 2 * tm)
    return pl.pallas_call(
        _combine_kernel,
        out_shape=jax.ShapeDtypeStruct(x2.shape, x2.dtype),
        grid=(n_tok // tm,),
        in_specs=[pl.BlockSpec((1, 1, 2 * tm), lambda i: (i, 0, 0), memory_space=pltpu.SMEM),
                  pl.BlockSpec(memory_space=pl.ANY),
                  pl.BlockSpec((tm, LANES), lambda i: (i, 0)),
                  pl.BlockSpec((tm, dm), lambda i: (i, 0))],
        out_specs=pl.BlockSpec((tm, dm), lambda i: (i, 0)),
        scratch_shapes=[pltpu.VMEM((2 * tm, ns, nl), F32), pltpu.SemaphoreType.DMA],
        compiler_params=_cparams("arbitrary"),
        name="moe_combine",
    )(dest3, yd, gates, x2)


def _moe(x, norm_g, w_group, b_group, w_expert, b_expert, w1, w3, w2, layer):
    nb, seq, dm = x.shape
    n_tok = nb * seq
    x2 = x.reshape(n_tok, dm)
    hn3, meta, gates, counts = _router(x2, norm_g, w_group, b_group, w_expert, b_expert)
    bm = MOE_BM
    n_blocks = (n_tok * TOP_K) // bm + N_EXPERTS
    counts = counts[0, :N_EXPERTS]
    padded = ((counts + bm - 1) // bm) * bm
    pad_ends = jnp.cumsum(padded)
    pad_starts = pad_ends - padded
    eid = meta[:, 0:TOP_K]
    e_iota = jnp.arange(N_EXPERTS, dtype=jnp.int32)
    seg_start = jnp.sum(jnp.where(eid[:, :, None] == e_iota, pad_starts, 0), axis=-1)
    dest = (seg_start + meta[:, TOP_K:2 * TOP_K]).astype(jnp.int32).reshape(-1)
    blk_start = jnp.arange(n_blocks, dtype=jnp.int32) * bm
    blk_expert = jnp.minimum(jnp.sum((blk_start[:, None] >= pad_ends[None, :]).astype(jnp.int32), axis=1),
                             N_EXPERTS - 1)
    n_used = (pad_ends[-1] // bm).astype(jnp.int32).reshape(1)

    xd = _dispatch(hn3, dest, n_blocks * bm)
    yd = _experts(xd, blk_expert, n_used, w1, w3, w2, layer)
    out = _combine(yd, dest, gates, x2)
    return out.reshape(nb, seq, dm)


def kernel(x, positions, ev_norm, ev_w_in, s5_lambda_re, s5_lambda_im, s5_log_dt, s5_b_re, s5_b_im,
           s5_c_re, s5_c_im, s5_d, s5_glu_w, s5_glu_b, lru_conv_w, lru_conv_b, lru_wa, lru_ba, lru_wi,
           lru_bi, lru_lambda, ev_w_out, od_norm, mla_w_in, mla_cq_norm, mla_ckv_norm, mla_w_uq,
           mla_w_ukv, mla_q_norm, mla_k_norm, mla_w_o, ffn_norm, moe_w_group, moe_b_group,
           moe_w_expert, moe_b_expert, moe_w1, moe_w3, moe_w2):
    depth = ffn_norm.shape[0]
    for layer in range(depth):
        j = layer // 2
        if layer % 2 == 0:
            x = _even_mixer(x, ev_norm[j], ev_w_in[j], s5_lambda_re[j], s5_lambda_im[j], s5_log_dt[j],
                            s5_b_re[j], s5_b_im[j], s5_c_re[j], s5_c_im[j], s5_d[j], s5_glu_w[j],
                            s5_glu_b[j], lru_conv_w[j], lru_conv_b[j], lru_wa[j], lru_ba[j], lru_wi[j],
                            lru_bi[j], lru_lambda[j], ev_w_out[j])
        else:
            x = _mla_mixer(x, od_norm[j], positions, mla_w_in[j], mla_cq_norm[j], mla_ckv_norm[j],
                           mla_w_uq[j], mla_w_ukv[j], mla_q_norm[j], mla_k_norm[j], mla_w_o[j])
        x = _moe(x, ffn_norm[layer], moe_w_group[layer], moe_b_group[layer], moe_w_expert[layer],
                 moe_b_expert[layer], moe_w1, moe_w3, moe_w2, layer)
    return x
```

```python
import functools
import math

import jax
import jax.numpy as jnp
from jax import lax
from jax.experimental import pallas as pl
from jax.experimental.pallas import tpu as pltpu

F32 = jnp.float32
BF16 = jnp.bfloat16

S5_GROUP_CH = 16
S5_STATE = 64
LRU_BLOCKS = 8
CONV_WIDTH = 4
RG_C = 8.0
MLA_HEADS = 8
Q_LORA = 384
KV_LORA = 256
NOPE_DIM = 128
ROPE_DIM = 64
V_DIM = 128
QK_DIM = NOPE_DIM + ROPE_DIM
ROPE_THETA = 10000.0
N_GROUPS = 8
EXPERTS_PER_GROUP = 8
N_EXPERTS = N_GROUPS * EXPERTS_PER_GROUP
TOP_K = 2
RMS_EPS = 1e-6
NEG_INF = -1e30

SUBLANES = 8
LANES = 128
ROW_CHUNKS = 8
VMEM_LIMIT = 48 * 1024 * 1024

EVEN_TL = 64
ROW_TILE = 512
MOE_BM = 256
ATT_TQ = 512
ATT_TK = 512
GATHER_TM = 256
DMA_UNROLL = 8


def _cparams(*sem):
    return pltpu.CompilerParams(dimension_semantics=tuple(sem), vmem_limit_bytes=VMEM_LIMIT)


def _rms(x, g):
    ms = jnp.mean(x * x, axis=-1, keepdims=True)
    return x * lax.rsqrt(ms + RMS_EPS) * g


def _dot(a, b):
    return jnp.dot(a, b, preferred_element_type=F32)


def _s5_discretize_kernel(lre_ref, lim_ref, ldt_ref, bre_ref, bim_ref,
                          lbre_ref, lbim_ref, bbre_ref, bbim_ref):
    lr = jnp.minimum(lre_ref[...], -1e-4)
    li = lim_ref[...]
    dt = jnp.exp(ldt_ref[...])
    mag = jnp.exp(lr * dt)
    lb_re = mag * jnp.cos(li * dt)
    lb_im = mag * jnp.sin(li * dt)
    den = lr * lr + li * li
    num_re = lb_re - 1.0
    f_re = (num_re * lr + lb_im * li) / den
    f_im = (lb_im * lr - num_re * li) / den
    lbre_ref[...] = lb_re
    lbim_ref[...] = lb_im
    br = bre_ref[...]
    bi = bim_ref[...]
    bbre_ref[...] = f_re[:, None, :] * br - f_im[:, None, :] * bi
    bbim_ref[...] = f_re[:, None, :] * bi + f_im[:, None, :] * br


def _s5_discretize(lam_re, lam_im, log_dt, b_re, b_im):
    g, p = lam_re.shape
    h = b_re.shape[-1]
    b_re_t = jnp.swapaxes(b_re, 1, 2)
    b_im_t = jnp.swapaxes(b_im, 1, 2)
    return pl.pallas_call(
        _s5_discretize_kernel,
        out_shape=(jax.ShapeDtypeStruct((g, p), F32), jax.ShapeDtypeStruct((g, p), F32),
                   jax.ShapeDtypeStruct((g, h, p), F32), jax.ShapeDtypeStruct((g, h, p), F32)),
        name="s5_discretize",
    )(lam_re, lam_im, log_dt[:, None], b_re_t, b_im_t)


def _even_inproj_kernel(x_ref, g_ref, w_ref, o_ref, h_ref):
    nb, tl, _ = x_ref.shape
    nc = h_ref.shape[0]
    g = g_ref[...]
    for b in range(nb):
        h = _rms(x_ref[b], g)
        for c in range(nc):
            h_ref[c, pl.ds(b, tl, stride=nb), :] = h[:, c * LANES:(c + 1) * LANES]
    h_all = jnp.concatenate([h_ref[c] for c in range(nc)], axis=-1)
    o_ref[...] = _dot(h_all.astype(BF16), w_ref[...]).astype(o_ref.dtype)


def _even_inproj(x, g, w):
    nb, seq, d = x.shape
    n_out = w.shape[1]
    tl = EVEN_TL
    return pl.pallas_call(
        _even_inproj_kernel,
        out_shape=jax.ShapeDtypeStruct((seq * nb, n_out), BF16),
        grid=(seq // tl,),
        in_specs=[pl.BlockSpec((nb, tl, d), lambda i: (0, i, 0)),
                  pl.BlockSpec((1, d), lambda i: (0, 0)),
                  pl.BlockSpec((d, n_out), lambda i: (0, 0))],
        out_specs=pl.BlockSpec((tl * nb, n_out), lambda i: (i, 0)),
        scratch_shapes=[pltpu.VMEM((d // LANES, tl * nb, LANES), F32)],
        compiler_params=_cparams("parallel"),
        name="even_inproj",
    )(x, g, w)


def _s5_kernel(u_ref, bm_ref, lre_ref, lim_ref, cm_ref, d_ref, gw_ref, gb_ref, o_ref,
               bu_ref, sb_ref, st_ref, *, nb):
    rows, width = u_ref.shape
    n_half = bm_ref.shape[0]
    kin = width // n_half
    ncol = bm_ref.shape[2]
    nre = ncol // 2
    chunk = 512
    steps = rows // nb

    @pl.when(pl.program_id(0) == 0)
    def _():
        st_ref[...] = jnp.zeros_like(st_ref)

    u = u_ref[...]
    for j in range(n_half):
        bu_ref[:, j * ncol:(j + 1) * ncol] = _dot(u[:, j * kin:(j + 1) * kin], bm_ref[j])

    for j in range(n_half):
        for c in range(nre // chunk):
            cr = j * ncol + c * chunk
            ci = cr + nre
            lc = j * nre + c * chunk
            lr = lre_ref[:, lc:lc + chunk]
            li = lim_ref[:, lc:lc + chunk]

            def body(i, carry, cr=cr, ci=ci, lr=lr, li=li):
                sr, si = carry
                r0 = pl.multiple_of(i * (2 * nb), 2 * nb)
                sr1 = lr * sr - li * si + bu_ref[pl.ds(r0, nb), cr:cr + chunk]
                si1 = lr * si + li * sr + bu_ref[pl.ds(r0, nb), ci:ci + chunk]
                sr2 = lr * sr1 - li * si1 + bu_ref[pl.ds(r0 + nb, nb), cr:cr + chunk]
                si2 = lr * si1 + li * sr1 + bu_ref[pl.ds(r0 + nb, nb), ci:ci + chunk]
                sb_ref[pl.ds(r0, 2 * nb), cr:cr + chunk] = jnp.concatenate([sr1, sr2], 0).astype(BF16)
                sb_ref[pl.ds(r0, 2 * nb), ci:ci + chunk] = jnp.concatenate([si1, si2], 0).astype(BF16)
                return sr2, si2

            sr, si = lax.fori_loop(0, steps // 2, body,
                                   (st_ref[:, cr:cr + chunk], st_ref[:, ci:ci + chunk]))
            st_ref[:, cr:cr + chunk] = sr
            st_ref[:, ci:ci + chunk] = si

    ys = [_dot(sb_ref[:, j * ncol:(j + 1) * ncol], cm_ref[j]) for j in range(n_half)]
    y = jnp.concatenate(ys, axis=-1) + d_ref[...] * u.astype(F32)
    y = jax.nn.gelu(y)
    z = _dot(y.astype(BF16), gw_ref[...]) + gb_ref[...]
    o_ref[...] = (y * jax.nn.sigmoid(z)).astype(o_ref.dtype)


def _s5_mixer(proj, nb, bm, lre, lim, cm, d, glu_w, glu_b):
    rows_total = proj.shape[0]
    width = d.shape[1]
    rows = EVEN_TL * nb
    n_half, _, ncol = bm.shape
    kern = functools.partial(_s5_kernel, nb=nb)
    return pl.pallas_call(
        kern,
        out_shape=jax.ShapeDtypeStruct((rows_total, width), BF16),
        grid=(rows_total // rows,),
        in_specs=[pl.BlockSpec((rows, width), lambda i: (i, 0)),
                  pl.BlockSpec(bm.shape, lambda i: (0, 0, 0)),
                  pl.BlockSpec(lre.shape, lambda i: (0, 0)),
                  pl.BlockSpec(lim.shape, lambda i: (0, 0)),
                  pl.BlockSpec(cm.shape, lambda i: (0, 0, 0)),
                  pl.BlockSpec((1, width), lambda i: (0, 0)),
                  pl.BlockSpec(glu_w.shape, lambda i: (0, 0)),
                  pl.BlockSpec((1, width), lambda i: (0, 0))],
        out_specs=pl.BlockSpec((rows, width), lambda i: (i, 0)),
        scratch_shapes=[pltpu.VMEM((rows, n_half * ncol), F32),
                        pltpu.VMEM((rows, n_half * ncol), BF16),
                        pltpu.VMEM((nb, n_half * ncol), F32)],
        compiler_params=_cparams("arbitrary"),
        name="s5_mixer",
    )(proj, bm, lre, lim, cm, d, glu_w, glu_b)


def _lru_kernel(x_ref, gate_ref, cw_ref, cb_ref, wa_ref, ba_ref, wi_ref, bi_ref, lam_ref, o_ref,
                xp_ref, a_ref, b_ref, h_ref, *, nb):
    rows, width = x_ref.shape
    halo = (CONV_WIDTH - 1) * nb
    steps = rows // nb

    @pl.when(pl.program_id(0) == 0)
    def _():
        xp_ref[0:halo, :] = jnp.zeros((halo, width), F32)
        h_ref[...] = jnp.zeros_like(h_ref)

    xp_ref[halo:halo + rows, :] = x_ref[...].astype(F32)
    xf = cb_ref[...] + cw_ref[0:1, :] * xp_ref[0:rows, :]
    for k in range(1, CONV_WIDTH):
        xf = xf + cw_ref[k:k + 1, :] * xp_ref[k * nb:k * nb + rows, :]
    xp_ref[0:halo, :] = xp_ref[rows:rows + halo, :]

    xb = xf.astype(BF16)
    nblk = wa_ref.shape[0]
    kb = width // nblk
    ga = jnp.concatenate([_dot(xb[:, j * kb:(j + 1) * kb], wa_ref[j]) for j in range(nblk)], -1)
    gi = jnp.concatenate([_dot(xb[:, j * kb:(j + 1) * kb], wi_ref[j]) for j in range(nblk)], -1)
    r = jax.nn.sigmoid(ga + ba_ref[...])
    ig = jax.nn.sigmoid(gi + bi_ref[...])
    log_a = RG_C * r * jax.nn.log_sigmoid(lam_ref[...])
    a_ref[...] = jnp.exp(log_a)
    th = jnp.tanh(log_a)
    b_ref[...] = jnp.sqrt(-2.0 * th / (1.0 - th)) * (ig * xf)

    def body(t, h):
        r0 = pl.multiple_of(t * nb, nb)
        h = a_ref[pl.ds(r0, nb), :] * h + b_ref[pl.ds(r0, nb), :]
        b_ref[pl.ds(r0, nb), :] = h
        return h

    h_ref[...] = lax.fori_loop(0, steps, body, h_ref[...])
    o_ref[...] = (b_ref[...] * jax.nn.gelu(gate_ref[...].astype(F32))).astype(o_ref.dtype)


def _lru_mixer(proj, nb, conv_w, conv_b, wa, ba, wi, bi, lam):
    rows_total = proj.shape[0]
    width = conv_w.shape[1]
    rows = EVEN_TL * nb
    halo = (CONV_WIDTH - 1) * nb
    kern = functools.partial(_lru_kernel, nb=nb)
    vec = pl.BlockSpec((1, width), lambda i: (0, 0))
    return pl.pallas_call(
        kern,
        out_shape=jax.ShapeDtypeStruct((rows_total, width), BF16),
        grid=(rows_total // rows,),
        in_specs=[pl.BlockSpec((rows, width), lambda i: (i, 1)),
                  pl.BlockSpec((rows, width), lambda i: (i, 2)),
                  pl.BlockSpec(conv_w.shape, lambda i: (0, 0)),
                  vec,
                  pl.BlockSpec(wa.shape, lambda i: (0, 0, 0)),
                  vec,
                  pl.BlockSpec(wi.shape, lambda i: (0, 0, 0)),
                  vec, vec],
        out_specs=pl.BlockSpec((rows, width), lambda i: (i, 0)),
        scratch_shapes=[pltpu.VMEM((rows + halo, width), F32),
                        pltpu.VMEM((rows, width), F32),
                        pltpu.VMEM((rows, width), F32),
                        pltpu.VMEM((nb, width), F32)],
        compiler_params=_cparams("arbitrary"),
        name="lru_mixer",
    )(proj, proj, conv_w, conv_b, wa, ba, wi, bi, lam)


def _even_outproj_kernel(ya_ref, yb_ref, wa_ref, wb_ref, x_ref, o_ref, y_ref):
    nb, tl, _ = x_ref.shape
    nc = y_ref.shape[0]
    y = _dot(ya_ref[...], wa_ref[...]) + _dot(yb_ref[...], wb_ref[...])
    for c in range(nc):
        y_ref[c] = y[:, c * LANES:(c + 1) * LANES]
    for b in range(nb):
        for c in range(nc):
            o_ref[b, :, c * LANES:(c + 1) * LANES] = (x_ref[b, :, c * LANES:(c + 1) * LANES]
                                                     + y_ref[c, pl.ds(b, tl, stride=nb), :])


def _even_outproj(ya, yb, w_a, w_b, x):
    nb, seq, d = x.shape
    tl = EVEN_TL
    rows = tl * nb
    wd = ya.shape[1]
    return pl.pallas_call(
        _even_outproj_kernel,
        out_shape=jax.ShapeDtypeStruct(x.shape, x.dtype),
        grid=(seq // tl,),
        in_specs=[pl.BlockSpec((rows, wd), lambda i: (i, 0)),
                  pl.BlockSpec((rows, wd), lambda i: (i, 0)),
                  pl.BlockSpec(w_a.shape, lambda i: (0, 0)),
                  pl.BlockSpec(w_b.shape, lambda i: (0, 0)),
                  pl.BlockSpec((nb, tl, d), lambda i: (0, i, 0))],
        out_specs=pl.BlockSpec((nb, tl, d), lambda i: (0, i, 0)),
        scratch_shapes=[pltpu.VMEM((d // LANES, rows, LANES), F32)],
        compiler_params=_cparams("parallel"),
        name="even_outproj",
    )(ya, yb, w_a, w_b, x)


def _block_diag(blocks):
    n, r, c = blocks.shape
    eye = jnp.eye(n, dtype=blocks.dtype)
    return jnp.einsum('nrc,nm->nrmc', blocks, eye).reshape(n * r, n * c)


def _even_mixer(x, norm_g, w_in, lam_re, lam_im, log_dt, b_re, b_im, c_re, c_im, d, glu_w, glu_b,
                conv_w, conv_b, wa, ba, wi, bi, lam, w_out):
    nb, seq, dm = x.shape
    g, p = lam_re.shape
    hch = b_re.shape[-1]
    width = g * hch
    n_half = 2
    gh = g // n_half

    proj = _even_inproj(x, norm_g[None, :], w_in.astype(BF16))

    lb_re, lb_im, bb_re, bb_im = _s5_discretize(lam_re, lam_im, log_dt, b_re, b_im)
    bb_re = bb_re.reshape(n_half, gh, hch, p)
    bb_im = bb_im.reshape(n_half, gh, hch, p)
    bm = jnp.stack([jnp.concatenate([_block_diag(bb_re[j]), _block_diag(bb_im[j])], axis=1)
                    for j in range(n_half)]).astype(BF16)
    c_re_t = jnp.swapaxes(c_re, 1, 2).reshape(n_half, gh, p, hch)
    c_im_t = jnp.swapaxes(c_im, 1, 2).reshape(n_half, gh, p, hch)
    cm = jnp.stack([jnp.concatenate([_block_diag(c_re_t[j]), -_block_diag(c_im_t[j])], axis=0)
                    for j in range(n_half)]).astype(BF16)
    lre = jnp.broadcast_to(lb_re.reshape(1, g * p), (nb, g * p))
    lim = jnp.broadcast_to(lb_im.reshape(1, g * p), (nb, g * p))
    ya = _s5_mixer(proj, nb, bm, lre, lim, cm, d.reshape(1, width), glu_w.astype(BF16),
                   glu_b[None, :])

    lw = conv_w.shape[1]
    nblk = 2
    per = LRU_BLOCKS // nblk
    wa_bd = jnp.stack([_block_diag(wa[j * per:(j + 1) * per]) for j in range(nblk)]).astype(BF16)
    wi_bd = jnp.stack([_block_diag(wi[j * per:(j + 1) * per]) for j in range(nblk)]).astype(BF16)
    yb = _lru_mixer(proj, nb, conv_w, conv_b[None, :], wa_bd, ba[None, :], wi_bd, bi[None, :],
                    lam[None, :])

    w_out_b = w_out.astype(BF16)
    return _even_outproj(ya, yb, w_out_b[:width], w_out_b[width:], x)


def _rope_table_kernel(pos_ref, inv_ref, cos_ref, sin_ref):
    ang = pos_ref[...].astype(F32) * inv_ref[...]
    cos_ref[...] = jnp.cos(ang)
    sin_ref[...] = jnp.sin(ang)


def _rope_tables(positions):
    nb, seq = positions.shape
    half = ROPE_DIM // 2
    per_row = LANES // half
    n_tok = nb * seq
    inv_freq = 1.0 / (ROPE_THETA ** (jnp.arange(0, ROPE_DIM, 2, dtype=F32) / ROPE_DIM))
    pos_rep = jnp.repeat(positions.reshape(n_tok // per_row, per_row), half, axis=1)
    inv_rep = jnp.tile(inv_freq, per_row)[None, :]
    cos, sin = pl.pallas_call(
        _rope_table_kernel,
        out_shape=(jax.ShapeDtypeStruct(pos_rep.shape, F32), jax.ShapeDtypeStruct(pos_rep.shape, F32)),
        name="rope_table",
    )(pos_rep, inv_rep)
    cos = cos.reshape(n_tok, half)
    sin = sin.reshape(n_tok, half)
    cc = jnp.concatenate([cos, cos], axis=-1)
    ss = jnp.concatenate([-sin, sin], axis=-1)
    return cc, ss


def _mla_proj_kernel(x_ref, g_ref, win_ref, cqn_ref, ckvn_ref, wq_ref, wkv_ref, qg_ref, kg_ref,
                     cc_ref, ss_ref, qn_ref, qr_ref, kn_ref, kr_ref, v_ref):
    h = _rms(x_ref[...], g_ref[...])
    proj = _dot(h.astype(BF16), win_ref[...])
    c_q = _rms(proj[:, :Q_LORA], cqn_ref[...])
    c_kv = _rms(proj[:, Q_LORA:Q_LORA + KV_LORA], ckvn_ref[...])
    o = Q_LORA + KV_LORA
    k_r = proj[:, o:o + ROPE_DIM]
    k_sw = proj[:, o + ROPE_DIM:o + 2 * ROPE_DIM]
    cc = cc_ref[...]
    ss = ss_ref[...]
    qg = qg_ref[...]
    kg = kg_ref[...]
    q = _dot(c_q.astype(BF16), wq_ref[...])
    kv = _dot(c_kv.astype(BF16), wkv_ref[...])
    nh = MLA_HEADS
    scale = 1.0 / math.sqrt(QK_DIM)
    k_rot = k_r * kg[1:2, :ROPE_DIM] * cc + k_sw * kg[2:3, :ROPE_DIM] * ss
    k_r_ss = jnp.sum(k_r * k_r, axis=-1, keepdims=True)
    for hd in range(nh):
        q_n = q[:, hd * NOPE_DIM:(hd + 1) * NOPE_DIM]
        ro = nh * NOPE_DIM + hd * ROPE_DIM
        q_r = q[:, ro:ro + ROPE_DIM]
        q_sw = q[:, ro + nh * ROPE_DIM:ro + nh * ROPE_DIM + ROPE_DIM]
        ms = (jnp.sum(q_n * q_n, axis=-1, keepdims=True)
              + jnp.sum(q_r * q_r, axis=-1, keepdims=True)) / QK_DIM
        rs = lax.rsqrt(ms + RMS_EPS) * scale
        qn_ref[0, hd] = (q_n * rs * qg[0:1, :]).astype(qn_ref.dtype)
        qr_ref[0, hd] = ((q_r * qg[1:2, :ROPE_DIM] * cc + q_sw * qg[2:3, :ROPE_DIM] * ss) * rs
                         ).astype(qr_ref.dtype)
        ko = hd * (NOPE_DIM + V_DIM)
        k_n = kv[:, ko:ko + NOPE_DIM]
        ms_k = (jnp.sum(k_n * k_n, axis=-1, keepdims=True) + k_r_ss) / QK_DIM
        rs_k = lax.rsqrt(ms_k + RMS_EPS)
        kn_ref[0, hd] = (k_n * rs_k * kg[0:1, :]).astype(kn_ref.dtype)
        kr_ref[0, hd] = (k_rot * rs_k).astype(kr_ref.dtype)
        v_ref[0, hd] = kv[:, ko + NOPE_DIM:ko + NOPE_DIM + V_DIM].astype(v_ref.dtype)


def _swap_halves(a, axis=-1):
    h = a.shape[axis] // 2
    lo = lax.slice_in_dim(a, 0, h, axis=axis)
    hi = lax.slice_in_dim(a, h, 2 * h, axis=axis)
    return jnp.concatenate([hi, lo], axis=axis)


def _mla_proj(x, norm_g, positions, w_in, cq_norm, ckv_norm, w_uq, w_ukv, q_norm, k_norm):
    nb, seq, dm = x.shape
    nh = MLA_HEADS
    tm = ROW_TILE
    n_tok = nb * seq
    cc, ss = _rope_tables(positions)
    o = Q_LORA + KV_LORA
    w_kr = w_in[:, o:o + ROPE_DIM]
    w_in_x = jnp.concatenate([w_in, _swap_halves(w_kr)], axis=1).astype(BF16)
    wq = w_uq.reshape(Q_LORA, nh, QK_DIM)
    wq_n = wq[:, :, :NOPE_DIM].reshape(Q_LORA, nh * NOPE_DIM)
    wq_r = wq[:, :, NOPE_DIM:]
    wq_x = jnp.concatenate([wq_n, wq_r.reshape(Q_LORA, nh * ROPE_DIM),
                            _swap_halves(wq_r).reshape(Q_LORA, nh * ROPE_DIM)], axis=1).astype(BF16)

    def gains(gv):
        pad = jnp.zeros((NOPE_DIM - ROPE_DIM,), F32)
        r = gv[NOPE_DIM:]
        return jnp.stack([gv[:NOPE_DIM], jnp.concatenate([r, pad]),
                          jnp.concatenate([_swap_halves(r), pad])])

    x2 = x.reshape(n_tok, dm)
    per_b = seq // tm
    full = lambda a: pl.BlockSpec(a.shape, lambda i: (0,) * a.ndim)
    args = (x2, norm_g[None, :], w_in_x, cq_norm[None, :], ckv_norm[None, :], wq_x,
            w_ukv.astype(BF16), gains(q_norm), gains(k_norm), cc, ss)
    in_specs = [pl.BlockSpec((tm, dm), lambda i: (i, 0))] + [full(a) for a in args[1:9]] + [
        pl.BlockSpec((tm, ROPE_DIM), lambda i: (i, 0)), pl.BlockSpec((tm, ROPE_DIM), lambda i: (i, 0))]

    def head_spec(dh):
        return pl.BlockSpec((1, nh, tm, dh), lambda i: (i // per_b, 0, i % per_b, 0))

    def head_shape(dh):
        return jax.ShapeDtypeStruct((nb, nh, seq, dh), BF16)

    return pl.pallas_call(
        _mla_proj_kernel,
        out_shape=(head_shape(NOPE_DIM), head_shape(ROPE_DIM), head_shape(NOPE_DIM),
                   head_shape(ROPE_DIM), head_shape(V_DIM)),
        grid=(n_tok // tm,),
        in_specs=in_specs,
        out_specs=(head_spec(NOPE_DIM), head_spec(ROPE_DIM), head_spec(NOPE_DIM),
                   head_spec(ROPE_DIM), head_spec(V_DIM)),
        compiler_params=_cparams("parallel"),
        name="mla_proj",
    )(*args)


def _attention_kernel(qn_ref, qr_ref, kn_ref, kr_ref, v_ref, o_ref):
    tq = qn_ref.shape[2]
    tk = ATT_TK
    qi = pl.program_id(2)
    q = jnp.concatenate([qn_ref[0, 0], qr_ref[0, 0]], axis=-1)
    nt = (((1,), (1,)), ((), ()))

    def scores(kb):
        k0 = pl.multiple_of(kb * tk, tk)
        k = jnp.concatenate([kn_ref[0, 0, pl.ds(k0, tk), :], kr_ref[0, 0, pl.ds(k0, tk), :]], axis=-1)
        return lax.dot_general(q, k, nt, preferred_element_type=F32), k0

    def update(s, k0, carry):
        m, l, acc = carry
        m_new = jnp.maximum(m, jnp.max(s, axis=-1, keepdims=True))
        alpha = jnp.exp(m - m_new)
        p = jnp.exp(s - m_new)
        l = alpha * l + jnp.sum(p, axis=-1, keepdims=True)
        acc = alpha * acc + _dot(p.astype(BF16), v_ref[0, 0, pl.ds(k0, tk), :])
        return m_new, l, acc

    def body(kb, carry):
        s, k0 = scores(kb)
        return update(s, k0, carry)

    init = (jnp.full((tq, 1), NEG_INF, F32), jnp.zeros((tq, 1), F32), jnp.zeros((tq, V_DIM), F32))
    n_full = (qi * tq) // tk
    carry = lax.fori_loop(0, n_full, body, init)
    for d in range(tq // tk):
        s, k0 = scores(n_full + d)
        q_pos = qi * tq + lax.broadcasted_iota(jnp.int32, s.shape, 0)
        k_pos = k0 + lax.broadcasted_iota(jnp.int32, s.shape, 1)
        s = jnp.where(k_pos <= q_pos, s, NEG_INF)
        carry = update(s, k0, carry)
    _, l, acc = carry
    o_ref[0] = (acc / l).astype(o_ref.dtype)


def _attention(qn, qr, kn, kr, v):
    nb, nh, seq, _ = qn.shape
    tq = ATT_TQ

    def qspec(dh):
        return pl.BlockSpec((1, 1, tq, dh), lambda b, h, i: (b, h, i, 0))

    def kspec(dh):
        return pl.BlockSpec((1, 1, seq, dh), lambda b, h, i: (b, h, 0, 0))

    return pl.pallas_call(
        _attention_kernel,
        out_shape=jax.ShapeDtypeStruct((nb, seq, nh * V_DIM), BF16),
        grid=(nb, nh, seq // tq),
        in_specs=[qspec(NOPE_DIM), qspec(ROPE_DIM), kspec(NOPE_DIM), kspec(ROPE_DIM), kspec(V_DIM)],
        out_specs=pl.BlockSpec((1, tq, V_DIM), lambda b, h, i: (b, i, h)),
        compiler_params=_cparams("parallel", "parallel", "arbitrary"),
        name="mla_attention",
    )(qn, qr, kn, kr, v)


def _proj_residual_kernel(a_ref, w_ref, x_ref, o_ref):
    o_ref[...] = x_ref[...] + _dot(a_ref[...], w_ref[...])


def _proj_residual(a, w, x2):
    n_tok, dm = x2.shape
    tm = ROW_TILE
    return pl.pallas_call(
        _proj_residual_kernel,
        out_shape=jax.ShapeDtypeStruct(x2.shape, x2.dtype),
        grid=(n_tok // tm,),
        in_specs=[pl.BlockSpec((tm, a.shape[1]), lambda i: (i, 0)),
                  pl.BlockSpec(w.shape, lambda i: (0, 0)),
                  pl.BlockSpec((tm, dm), lambda i: (i, 0))],
        out_specs=pl.BlockSpec((tm, dm), lambda i: (i, 0)),
        compiler_params=_cparams("parallel"),
        name="proj_residual",
    )(a, w, x2)


def _mla_mixer(x, norm_g, positions, w_in, cq_norm, ckv_norm, w_uq, w_ukv, q_norm, k_norm, w_o):
    nb, seq, dm = x.shape
    qn, qr, kn, kr, v = _mla_proj(x, norm_g, positions, w_in, cq_norm, ckv_norm, w_uq, w_ukv,
                                  q_norm, k_norm)
    o = _attention(qn, qr, kn, kr, v)
    out = _proj_residual(o.reshape(nb * seq, -1), w_o.astype(BF16), x.reshape(nb * seq, dm))
    return out.reshape(nb, seq, dm)


def _router_kernel(x_ref, g_ref, w_ref, b_ref, hn_ref, meta_ref, gate_ref, cnt_ref, base_ref):
    tm, dm = x_ref.shape
    ne = N_EXPERTS

    @pl.when(pl.program_id(0) == 0)
    def _():
        base_ref[...] = jnp.zeros_like(base_ref)

    hn = _rms(x_ref[...], g_ref[...])
    for c in range(ROW_CHUNKS):
        hn_ref[pl.ds(c, tm, stride=ROW_CHUNKS), :] = hn[:, c * LANES:(c + 1) * LANES]
    logits = _dot(hn.astype(BF16), w_ref[...]) + b_ref[...]
    lane = lax.broadcasted_iota(jnp.int32, logits.shape, 1)
    big = jnp.int32(1 << 20)

    def first_argmax(v):
        m = jnp.max(v, axis=-1, keepdims=True)
        idx = jnp.min(jnp.where(v == m, lane, big), axis=-1, keepdims=True)
        return m, idx

    is_g = (lane >= ne) & (lane < ne + N_GROUPS)
    gl = jnp.where(is_g, logits, NEG_INF)
    g_max, g_idx = first_argmax(gl)
    g_sel = g_idx - ne
    p_g = 1.0 / jnp.sum(jnp.where(is_g, jnp.exp(gl - g_max), 0.0), axis=-1, keepdims=True)
    in_grp = (lane >= g_sel * EXPERTS_PER_GROUP) & (lane < (g_sel + 1) * EXPERTS_PER_GROUP)
    el = jnp.where(in_grp, logits, NEG_INF)
    v0, e0 = first_argmax(el)
    v1, e1 = first_argmax(jnp.where(lane == e0, NEG_INF, el))
    t = jnp.exp(v1 - v0)
    gate0 = p_g / (1.0 + t)
    gate1 = p_g * t / (1.0 + t)

    oh0 = (lane == e0).astype(F32)
    oh1 = (lane == e1).astype(F32)
    both = oh0 + oh1
    r_i = lax.broadcasted_iota(jnp.int32, (tm, tm), 0)
    c_i = lax.broadcasted_iota(jnp.int32, (tm, tm), 1)
    strict_lower = (c_i < r_i).astype(BF16)
    before = _dot(strict_lower, both.astype(BF16)) + base_ref[0:1, :]
    rank0 = jnp.sum(oh0 * before, axis=-1, keepdims=True)
    rank1 = jnp.sum(oh1 * before, axis=-1, keepdims=True)
    base_ref[0:1, :] = base_ref[0:1, :] + jnp.sum(both, axis=0, keepdims=True)
    cnt_ref[...] = jnp.broadcast_to(base_ref[0:1, :], cnt_ref.shape).astype(jnp.int32)

    meta = jnp.where(lane == 0, e0, jnp.where(lane == 1, e1, jnp.where(
        lane == 2, rank0.astype(jnp.int32), jnp.where(lane == 3, rank1.astype(jnp.int32), 0))))
    meta_ref[...] = meta
    gate_ref[...] = jnp.where(lane == 0, gate0, jnp.where(lane == 1, gate1, 0.0))


def _router(x2, norm_g, w_group, b_group, w_expert, b_expert):
    n_tok, dm = x2.shape
    tm = ROW_TILE
    pad = LANES - N_EXPERTS - N_GROUPS
    w = jnp.concatenate([w_expert, w_group, jnp.zeros((dm, pad), F32)], axis=1).astype(BF16)
    b = jnp.concatenate([b_expert, b_group, jnp.zeros((pad,), F32)])[None, :]
    return pl.pallas_call(
        _router_kernel,
        out_shape=(jax.ShapeDtypeStruct((n_tok * ROW_CHUNKS, LANES), F32),
                   jax.ShapeDtypeStruct((n_tok, LANES), jnp.int32),
                   jax.ShapeDtypeStruct((n_tok, LANES), F32),
                   jax.ShapeDtypeStruct((SUBLANES, LANES), jnp.int32)),
        grid=(n_tok // tm,),
        in_specs=[pl.BlockSpec((tm, dm), lambda i: (i, 0)),
                  pl.BlockSpec((1, dm), lambda i: (0, 0)),
                  pl.BlockSpec((dm, LANES), lambda i: (0, 0)),
                  pl.BlockSpec((1, LANES), lambda i: (0, 0))],
        out_specs=(pl.BlockSpec((tm * ROW_CHUNKS, LANES), lambda i: (i, 0)),
                   pl.BlockSpec((tm, LANES), lambda i: (i, 0)),
                   pl.BlockSpec((tm, LANES), lambda i: (i, 0)),
                   pl.BlockSpec((SUBLANES, LANES), lambda i: (0, 0))),
        scratch_shapes=[pltpu.VMEM((SUBLANES, LANES), F32)],
        compiler_params=_cparams("arbitrary"),
        name="moe_router",
    )(x2, norm_g[None, :], w, b)


def _dispatch_kernel(dest_ref, hn_ref, xd_in_ref, xd_ref, sem):
    del xd_in_ref
    tm = hn_ref.shape[0] // ROW_CHUNKS

    def row_copy(t, k):
        src = pl.multiple_of(t * ROW_CHUNKS, ROW_CHUNKS)
        dst = pl.multiple_of(dest_ref[0, 0, 2 * t + k], ROW_CHUNKS)
        return pltpu.make_async_copy(hn_ref.at[pl.ds(src, ROW_CHUNKS)], xd_ref.at[pl.ds(dst, ROW_CHUNKS)], sem)

    def issue(i, c):
        for u in range(DMA_UNROLL):
            for k in range(TOP_K):
                row_copy(i * DMA_UNROLL + u, k).start(priority=k)
        return c

    lax.fori_loop(0, tm // DMA_UNROLL, issue, 0)

    def drain(i, c):
        for u in range(DMA_UNROLL):
            for k in range(TOP_K):
                row_copy(i * DMA_UNROLL + u, k).wait()
        return c

    lax.fori_loop(0, tm // DMA_UNROLL, drain, 0)


def _dispatch(hn2, dest_off, n_rows):
    tm = GATHER_TM
    n_tok = hn2.shape[0] // ROW_CHUNKS
    dest3 = dest_off.reshape(n_tok // tm, 1, 2 * tm)
    xd0 = jnp.zeros((n_rows * ROW_CHUNKS, LANES), hn2.dtype)
    return pl.pallas_call(
        _dispatch_kernel,
        out_shape=jax.ShapeDtypeStruct(xd0.shape, xd0.dtype),
        grid=(n_tok // tm,),
        in_specs=[pl.BlockSpec((1, 1, 2 * tm), lambda i: (i, 0, 0), memory_space=pltpu.SMEM),
                  pl.BlockSpec((tm * ROW_CHUNKS, LANES), lambda i: (i, 0)),
                  pl.BlockSpec(memory_space=pl.ANY)],
        out_specs=pl.BlockSpec(memory_space=pl.ANY),
        scratch_shapes=[pltpu.SemaphoreType.DMA],
        input_output_aliases={2: 0},
        compiler_params=_cparams("arbitrary"),
        name="moe_dispatch",
    )(dest3, hn2, xd0)


def _rows_from_tiles(ref, n, base=0):
    return jnp.concatenate([ref[pl.ds(base + c, n, stride=ROW_CHUNKS), :] for c in range(ROW_CHUNKS)], axis=-1)


def _expert_kernel(be_ref, nu_ref, x_ref, w1_ref, w3_ref, w2_ref, y_ref, w13_s, w2_s):
    b = pl.program_id(0)
    bm = x_ref.shape[0] // ROW_CHUNKS
    de = w1_ref.shape[3]

    @pl.when(b < nu_ref[0])
    def _():
        changed = jnp.logical_or(b == 0, be_ref[b] != be_ref[jnp.maximum(b - 1, 0)])

        @pl.when(changed)
        def _():
            w13_s[:, :de] = w1_ref[0, 0].astype(BF16)
            w13_s[:, de:] = w3_ref[0, 0].astype(BF16)
            w2_s[...] = w2_ref[0, 0].astype(BF16)

        x = _rows_from_tiles(x_ref, bm).astype(BF16)
        h = _dot(x, w13_s[...])
        a = jax.nn.silu(h[:, :de]) * h[:, de:]
        y = _dot(a.astype(BF16), w2_s[...])
        for c in range(ROW_CHUNKS):
            y_ref[pl.ds(c, bm, stride=ROW_CHUNKS), :] = y[:, c * LANES:(c + 1) * LANES]

    @pl.when(b >= nu_ref[0])
    def _():
        y_ref[...] = jnp.zeros_like(y_ref)


def _experts(xd, blk_expert, n_used, w1, w3, w2, layer):
    bm = MOE_BM
    n_blocks = xd.shape[0] // (bm * ROW_CHUNKS)
    _, _, dm, de = w1.shape

    def row_map(b, be, nu):
        return (jnp.minimum(b, nu[0] - 1), 0)

    def w_map(b, be, nu):
        return (layer, be[b], 0, 0)

    grid_spec = pltpu.PrefetchScalarGridSpec(
        num_scalar_prefetch=2,
        grid=(n_blocks,),
        in_specs=[pl.BlockSpec((bm * ROW_CHUNKS, LANES), row_map),
                  pl.BlockSpec((1, 1, dm, de), w_map),
                  pl.BlockSpec((1, 1, dm, de), w_map),
                  pl.BlockSpec((1, 1, de, dm), w_map)],
        out_specs=pl.BlockSpec((bm * ROW_CHUNKS, LANES), lambda b, be, nu: (b, 0)),
        scratch_shapes=[pltpu.VMEM((dm, 2 * de), BF16), pltpu.VMEM((de, dm), BF16)],
    )
    return pl.pallas_call(
        _expert_kernel,
        out_shape=jax.ShapeDtypeStruct(xd.shape, F32),
        grid_spec=grid_spec,
        compiler_params=_cparams("arbitrary"),
        name="moe_experts",
    )(blk_expert, n_used, xd, w1, w3, w2)


def _combine_kernel(dest_ref, dest_next_ref, yd_ref, gate_ref, x_ref, o_ref, buf_ref, sem):
    tm, dm = x_ref.shape
    i = pl.program_id(0)
    n_steps = pl.num_programs(0)
    slot = i % 2
    k_rows = tm * ROW_CHUNKS

    def row_copy(d_ref, sl, t, k):
        src = pl.multiple_of(d_ref[0, 0, 2 * t + k], ROW_CHUNKS)
        dst = pl.multiple_of(k * k_rows + t * ROW_CHUNKS, ROW_CHUNKS)
        return pltpu.make_async_copy(yd_ref.at[pl.ds(src, ROW_CHUNKS)],
                                     buf_ref.at[sl, pl.ds(dst, ROW_CHUNKS)], sem.at[sl])

    def issue_all(d_ref, sl):
        def issue(j, c):
            for u in range(DMA_UNROLL):
                for k in range(TOP_K):
                    row_copy(d_ref, sl, j * DMA_UNROLL + u, k).start(priority=k)
            return c
        lax.fori_loop(0, tm // DMA_UNROLL, issue, 0)

    @pl.when(i == 0)
    def _():
        issue_all(dest_ref, slot)

    @pl.when(i + 1 < n_steps)
    def _():
        issue_all(dest_next_ref, 1 - slot)

    def drain(j, c):
        for u in range(DMA_UNROLL):
            for k in range(TOP_K):
                row_copy(dest_ref, slot, j * DMA_UNROLL + u, k).wait()
        return c

    lax.fori_loop(0, tm // DMA_UNROLL, drain, 0)

    g0 = gate_ref[:, 0:1]
    g1 = gate_ref[:, 1:2]
    buf = buf_ref.at[slot]
    for c in range(ROW_CHUNKS):
        y0 = buf[pl.ds(c, tm, stride=ROW_CHUNKS), :]
        y1 = buf[pl.ds(k_rows + c, tm, stride=ROW_CHUNKS), :]
        o_ref[:, c * LANES:(c + 1) * LANES] = x_ref[:, c * LANES:(c + 1) * LANES] + (g0 * y0 + g1 * y1)


def _combine(yd, dest_off, gates, x2):
    n_tok, dm = x2.shape
    tm = GATHER_TM
    n_steps = n_tok // tm
    dest3 = dest_off.reshape(n_steps, 1, 2 * tm)
    return pl.pallas_call(
        _combine_kernel,
        out_shape=jax.ShapeDtypeStruct(x2.shape, x2.dtype),
        grid=(n_steps,),
        in_specs=[pl.BlockSpec((1, 1, 2 * tm), lambda i: (i, 0, 0), memory_space=pltpu.SMEM),
                  pl.BlockSpec((1, 1, 2 * tm), lambda i: (jnp.minimum(i + 1, n_steps - 1), 0, 0),
                               memory_space=pltpu.SMEM),
                  pl.BlockSpec(memory_space=pl.ANY),
                  pl.BlockSpec((tm, LANES), lambda i: (i, 0)),
                  pl.BlockSpec((tm, dm), lambda i: (i, 0))],
        out_specs=pl.BlockSpec((tm, dm), lambda i: (i, 0)),
        scratch_shapes=[pltpu.VMEM((2, TOP_K * tm * ROW_CHUNKS, LANES), F32),
                        pltpu.SemaphoreType.DMA((2,))],
        compiler_params=_cparams("arbitrary"),
        name="moe_combine",
    )(dest3, dest3, yd, gates, x2)


def _moe(x, norm_g, w_group, b_group, w_expert, b_expert, w1, w3, w2, layer):
    nb, seq, dm = x.shape
    n_tok = nb * seq
    x2 = x.reshape(n_tok, dm)
    hn2, meta, gates, counts = _router(x2, norm_g, w_group, b_group, w_expert, b_expert)
    bm = MOE_BM
    n_blocks = (n_tok * TOP_K) // bm + N_EXPERTS
    counts = counts[0, :N_EXPERTS]
    padded = ((counts + bm - 1) // bm) * bm
    pad_ends = jnp.cumsum(padded)
    pad_starts = pad_ends - padded
    eid = meta[:, 0:TOP_K]
    e_iota = jnp.arange(N_EXPERTS, dtype=jnp.int32)
    seg_start = jnp.sum(jnp.where(eid[:, :, None] == e_iota, pad_starts, 0), axis=-1)
    dest_off = ((seg_start + meta[:, TOP_K:2 * TOP_K]) * ROW_CHUNKS).astype(jnp.int32).reshape(-1)
    blk_start = jnp.arange(n_blocks, dtype=jnp.int32) * bm
    blk_expert = jnp.minimum(jnp.sum((blk_start[:, None] >= pad_ends[None, :]).astype(jnp.int32), axis=1),
                             N_EXPERTS - 1)
    n_used = (pad_ends[-1] // bm).astype(jnp.int32).reshape(1)

    xd = _dispatch(hn2, dest_off, n_blocks * bm)
    yd = _experts(xd, blk_expert, n_used, w1, w3, w2, layer)
    out = _combine(yd, dest_off, gates, x2)
    return out.reshape(nb, seq, dm)


def kernel(x, positions, ev_norm, ev_w_in, s5_lambda_re, s5_lambda_im, s5_log_dt, s5_b_re, s5_b_im,
           s5_c_re, s5_c_im, s5_d, s5_glu_w, s5_glu_b, lru_conv_w, lru_conv_b, lru_wa, lru_ba, lru_wi,
           lru_bi, lru_lambda, ev_w_out, od_norm, mla_w_in, mla_cq_norm, mla_ckv_norm, mla_w_uq,
           mla_w_ukv, mla_q_norm, mla_k_norm, mla_w_o, ffn_norm, moe_w_group, moe_b_group,
           moe_w_expert, moe_b_expert, moe_w1, moe_w3, moe_w2):
    depth = ffn_norm.shape[0]
    for layer in range(depth):
        j = layer // 2
        if layer % 2 == 0:
            x = _even_mixer(x, ev_norm[j], ev_w_in[j], s5_lambda_re[j], s5_lambda_im[j], s5_log_dt[j],
                            s5_b_re[j], s5_b_im[j], s5_c_re[j], s5_c_im[j], s5_d[j], s5_glu_w[j],
                            s5_glu_b[j], lru_conv_w[j], lru_conv_b[j], lru_wa[j], lru_ba[j], lru_wi[j],
                            lru_bi[j], lru_lambda[j], ev_w_out[j])
        else:
            x = _mla_mixer(x, od_norm[j], positions, mla_w_in[j], mla_cq_norm[j], mla_ckv_norm[j],
                           mla_w_uq[j], mla_w_ukv[j], mla_q_norm[j], mla_k_norm[j], mla_w_o[j])
        x = _moe(x, ffn_norm[layer], moe_w_group[layer], moe_b_group[layer], moe_w_expert[layer],
                 moe_b_expert[layer], moe_w1, moe_w3, moe_w2, layer)
    return x
```

```python
import functools
import math

import jax
import jax.numpy as jnp
from jax import lax
from jax.experimental import pallas as pl
from jax.experimental.pallas import tpu as pltpu

F32 = jnp.float32
BF16 = jnp.bfloat16

S5_GROUP_CH = 16
S5_STATE = 64
LRU_BLOCKS = 8
CONV_WIDTH = 4
RG_C = 8.0
MLA_HEADS = 8
Q_LORA = 384
KV_LORA = 256
NOPE_DIM = 128
ROPE_DIM = 64
V_DIM = 128
QK_DIM = NOPE_DIM + ROPE_DIM
ROPE_THETA = 10000.0
N_GROUPS = 8
EXPERTS_PER_GROUP = 8
N_EXPERTS = N_GROUPS * EXPERTS_PER_GROUP
TOP_K = 2
RMS_EPS = 1e-6
NEG_INF = -1e30

SUBLANES = 8
LANES = 128
ROW_CHUNKS = 8
VMEM_LIMIT = 48 * 1024 * 1024

EVEN_TL = 64
ROW_TILE = 512
MOE_BM = 256
ATT_TQ = 512
GATHER_TM = 256
DMA_UNROLL = 8


def _cparams(*sem):
    return pltpu.CompilerParams(dimension_semantics=tuple(sem), vmem_limit_bytes=VMEM_LIMIT)


def _rms(x, g):
    ms = jnp.mean(x * x, axis=-1, keepdims=True)
    return x * lax.rsqrt(ms + RMS_EPS) * g


def _dot(a, b):
    return jnp.dot(a, b, preferred_element_type=F32)


def _s5_discretize_kernel(lre_ref, lim_ref, ldt_ref, bre_ref, bim_ref,
                          lbre_ref, lbim_ref, bbre_ref, bbim_ref):
    lr = jnp.minimum(lre_ref[...], -1e-4)
    li = lim_ref[...]
    dt = jnp.exp(ldt_ref[...])
    mag = jnp.exp(lr * dt)
    lb_re = mag * jnp.cos(li * dt)
    lb_im = mag * jnp.sin(li * dt)
    den = lr * lr + li * li
    num_re = lb_re - 1.0
    f_re = (num_re * lr + lb_im * li) / den
    f_im = (lb_im * lr - num_re * li) / den
    lbre_ref[...] = lb_re
    lbim_ref[...] = lb_im
    br = bre_ref[...]
    bi = bim_ref[...]
    bbre_ref[...] = f_re[:, None, :] * br - f_im[:, None, :] * bi
    bbim_ref[...] = f_re[:, None, :] * bi + f_im[:, None, :] * br


def _s5_discretize(lam_re, lam_im, log_dt, b_re, b_im):
    g, p = lam_re.shape
    h = b_re.shape[-1]
    b_re_t = jnp.swapaxes(b_re, 1, 2)
    b_im_t = jnp.swapaxes(b_im, 1, 2)
    return pl.pallas_call(
        _s5_discretize_kernel,
        out_shape=(jax.ShapeDtypeStruct((g, p), F32), jax.ShapeDtypeStruct((g, p), F32),
                   jax.ShapeDtypeStruct((g, h, p), F32), jax.ShapeDtypeStruct((g, h, p), F32)),
        name="s5_discretize",
    )(lam_re, lam_im, log_dt[:, None], b_re_t, b_im_t)


def _even_inproj_kernel(x_ref, g_ref, w_ref, o_ref, h_ref):
    nb, tl, _ = x_ref.shape
    nc = h_ref.shape[0]
    g = g_ref[...]
    for b in range(nb):
        h = _rms(x_ref[b], g)
        for c in range(nc):
            h_ref[c, pl.ds(b, tl, stride=nb), :] = h[:, c * LANES:(c + 1) * LANES]
    h_all = jnp.concatenate([h_ref[c] for c in range(nc)], axis=-1)
    o_ref[...] = _dot(h_all.astype(BF16), w_ref[...]).astype(o_ref.dtype)


def _even_inproj(x, g, w):
    nb, seq, d = x.shape
    n_out = w.shape[1]
    tl = EVEN_TL
    return pl.pallas_call(
        _even_inproj_kernel,
        out_shape=jax.ShapeDtypeStruct((seq * nb, n_out), BF16),
        grid=(seq // tl,),
        in_specs=[pl.BlockSpec((nb, tl, d), lambda i: (0, i, 0)),
                  pl.BlockSpec((1, d), lambda i: (0, 0)),
                  pl.BlockSpec((d, n_out), lambda i: (0, 0))],
        out_specs=pl.BlockSpec((tl * nb, n_out), lambda i: (i, 0)),
        scratch_shapes=[pltpu.VMEM((d // LANES, tl * nb, LANES), F32)],
        compiler_params=_cparams("parallel"),
        name="even_inproj",
    )(x, g, w)


def _s5_kernel(u_ref, bm_ref, lre_ref, lim_ref, cm_ref, d_ref, gw_ref, gb_ref, o_ref,
               bu_ref, sb_ref, st_ref, *, nb):
    rows, width = u_ref.shape
    n_half = bm_ref.shape[0]
    kin = width // n_half
    ncol = bm_ref.shape[2]
    nre = ncol // 2
    chunk = 512
    steps = rows // nb

    @pl.when(pl.program_id(0) == 0)
    def _():
        st_ref[...] = jnp.zeros_like(st_ref)

    u = u_ref[...]
    for j in range(n_half):
        bu_ref[:, j * ncol:(j + 1) * ncol] = _dot(u[:, j * kin:(j + 1) * kin], bm_ref[j])

    for j in range(n_half):
        for c in range(nre // chunk):
            cr = j * ncol + c * chunk
            ci = cr + nre
            lc = j * nre + c * chunk
            lr = lre_ref[:, lc:lc + chunk]
            li = lim_ref[:, lc:lc + chunk]

            def body(i, carry, cr=cr, ci=ci, lr=lr, li=li):
                sr, si = carry
                r0 = pl.multiple_of(i * (2 * nb), 2 * nb)
                sr1 = lr * sr - li * si + bu_ref[pl.ds(r0, nb), cr:cr + chunk]
                si1 = lr * si + li * sr + bu_ref[pl.ds(r0, nb), ci:ci + chunk]
                sr2 = lr * sr1 - li * si1 + bu_ref[pl.ds(r0 + nb, nb), cr:cr + chunk]
                si2 = lr * si1 + li * sr1 + bu_ref[pl.ds(r0 + nb, nb), ci:ci + chunk]
                sb_ref[pl.ds(r0, 2 * nb), cr:cr + chunk] = jnp.concatenate([sr1, sr2], 0).astype(BF16)
                sb_ref[pl.ds(r0, 2 * nb), ci:ci + chunk] = jnp.concatenate([si1, si2], 0).astype(BF16)
                return sr2, si2

            sr, si = lax.fori_loop(0, steps // 2, body,
                                   (st_ref[:, cr:cr + chunk], st_ref[:, ci:ci + chunk]))
            st_ref[:, cr:cr + chunk] = sr
            st_ref[:, ci:ci + chunk] = si

    ys = [_dot(sb_ref[:, j * ncol:(j + 1) * ncol], cm_ref[j]) for j in range(n_half)]
    y = jnp.concatenate(ys, axis=-1) + d_ref[...] * u.astype(F32)
    y = jax.nn.gelu(y)
    z = _dot(y.astype(BF16), gw_ref[...]) + gb_ref[...]
    o_ref[...] = (y * jax.nn.sigmoid(z)).astype(o_ref.dtype)


def _s5_mixer(proj, nb, bm, lre, lim, cm, d, glu_w, glu_b):
    rows_total = proj.shape[0]
    width = d.shape[1]
    rows = EVEN_TL * nb
    n_half, _, ncol = bm.shape
    kern = functools.partial(_s5_kernel, nb=nb)
    return pl.pallas_call(
        kern,
        out_shape=jax.ShapeDtypeStruct((rows_total, width), BF16),
        grid=(rows_total // rows,),
        in_specs=[pl.BlockSpec((rows, width), lambda i: (i, 0)),
                  pl.BlockSpec(bm.shape, lambda i: (0, 0, 0)),
                  pl.BlockSpec(lre.shape, lambda i: (0, 0)),
                  pl.BlockSpec(lim.shape, lambda i: (0, 0)),
                  pl.BlockSpec(cm.shape, lambda i: (0, 0, 0)),
                  pl.BlockSpec((1, width), lambda i: (0, 0)),
                  pl.BlockSpec(glu_w.shape, lambda i: (0, 0)),
                  pl.BlockSpec((1, width), lambda i: (0, 0))],
        out_specs=pl.BlockSpec((rows, width), lambda i: (i, 0)),
        scratch_shapes=[pltpu.VMEM((rows, n_half * ncol), F32),
                        pltpu.VMEM((rows, n_half * ncol), BF16),
                        pltpu.VMEM((nb, n_half * ncol), F32)],
        compiler_params=_cparams("arbitrary"),
        name="s5_mixer",
    )(proj, bm, lre, lim, cm, d, glu_w, glu_b)


def _lru_kernel(x_ref, gate_ref, cw_ref, cb_ref, wa_ref, ba_ref, wi_ref, bi_ref, lam_ref, o_ref,
                xp_ref, a_ref, b_ref, h_ref, *, nb):
    rows, width = x_ref.shape
    halo = (CONV_WIDTH - 1) * nb
    steps = rows // nb

    @pl.when(pl.program_id(0) == 0)
    def _():
        xp_ref[0:halo, :] = jnp.zeros((halo, width), F32)
        h_ref[...] = jnp.zeros_like(h_ref)

    xp_ref[halo:halo + rows, :] = x_ref[...].astype(F32)
    xf = cb_ref[...] + cw_ref[0:1, :] * xp_ref[0:rows, :]
    for k in range(1, CONV_WIDTH):
        xf = xf + cw_ref[k:k + 1, :] * xp_ref[k * nb:k * nb + rows, :]
    xp_ref[0:halo, :] = xp_ref[rows:rows + halo, :]

    xb = xf.astype(BF16)
    nblk = wa_ref.shape[0]
    kb = width // nblk
    ga = jnp.concatenate([_dot(xb[:, j * kb:(j + 1) * kb], wa_ref[j]) for j in range(nblk)], -1)
    gi = jnp.concatenate([_dot(xb[:, j * kb:(j + 1) * kb], wi_ref[j]) for j in range(nblk)], -1)
    r = jax.nn.sigmoid(ga + ba_ref[...])
    ig = jax.nn.sigmoid(gi + bi_ref[...])
    log_a = RG_C * r * jax.nn.log_sigmoid(lam_ref[...])
    a_ref[...] = jnp.exp(log_a)
    th = jnp.tanh(log_a)
    b_ref[...] = jnp.sqrt(-2.0 * th / (1.0 - th)) * (ig * xf)

    def body(t, h):
        r0 = pl.multiple_of(t * nb, nb)
        h = a_ref[pl.ds(r0, nb), :] * h + b_ref[pl.ds(r0, nb), :]
        b_ref[pl.ds(r0, nb), :] = h
        return h

    h_ref[...] = lax.fori_loop(0, steps, body, h_ref[...])
    o_ref[...] = (b_ref[...] * jax.nn.gelu(gate_ref[...].astype(F32))).astype(o_ref.dtype)


def _lru_mixer(proj, nb, conv_w, conv_b, wa, ba, wi, bi, lam):
    rows_total = proj.shape[0]
    width = conv_w.shape[1]
    rows = EVEN_TL * nb
    halo = (CONV_WIDTH - 1) * nb
    kern = functools.partial(_lru_kernel, nb=nb)
    vec = pl.BlockSpec((1, width), lambda i: (0, 0))
    return pl.pallas_call(
        kern,
        out_shape=jax.ShapeDtypeStruct((rows_total, width), BF16),
        grid=(rows_total // rows,),
        in_specs=[pl.BlockSpec((rows, width), lambda i: (i, 1)),
                  pl.BlockSpec((rows, width), lambda i: (i, 2)),
                  pl.BlockSpec(conv_w.shape, lambda i: (0, 0)),
                  vec,
                  pl.BlockSpec(wa.shape, lambda i: (0, 0, 0)),
                  vec,
                  pl.BlockSpec(wi.shape, lambda i: (0, 0, 0)),
                  vec, vec],
        out_specs=pl.BlockSpec((rows, width), lambda i: (i, 0)),
        scratch_shapes=[pltpu.VMEM((rows + halo, width), F32),
                        pltpu.VMEM((rows, width), F32),
                        pltpu.VMEM((rows, width), F32),
                        pltpu.VMEM((nb, width), F32)],
        compiler_params=_cparams("arbitrary"),
        name="lru_mixer",
    )(proj, proj, conv_w, conv_b, wa, ba, wi, bi, lam)


def _even_outproj_kernel(ya_ref, yb_ref, wa_ref, wb_ref, x_ref, o_ref, y_ref):
    nb, tl, _ = x_ref.shape
    nc = y_ref.shape[0]
    y = _dot(ya_ref[...], wa_ref[...]) + _dot(yb_ref[...], wb_ref[...])
    for c in range(nc):
        y_ref[c] = y[:, c * LANES:(c + 1) * LANES]
    for b in range(nb):
        for c in range(nc):
            o_ref[b, :, c * LANES:(c + 1) * LANES] = (x_ref[b, :, c * LANES:(c + 1) * LANES]
                                                     + y_ref[c, pl.ds(b, tl, stride=nb), :])


def _even_outproj(ya, yb, w_a, w_b, x):
    nb, seq, d = x.shape
    tl = EVEN_TL
    rows = tl * nb
    wd = ya.shape[1]
    return pl.pallas_call(
        _even_outproj_kernel,
        out_shape=jax.ShapeDtypeStruct(x.shape, x.dtype),
        grid=(seq // tl,),
        in_specs=[pl.BlockSpec((rows, wd), lambda i: (i, 0)),
                  pl.BlockSpec((rows, wd), lambda i: (i, 0)),
                  pl.BlockSpec(w_a.shape, lambda i: (0, 0)),
                  pl.BlockSpec(w_b.shape, lambda i: (0, 0)),
                  pl.BlockSpec((nb, tl, d), lambda i: (0, i, 0))],
        out_specs=pl.BlockSpec((nb, tl, d), lambda i: (0, i, 0)),
        scratch_shapes=[pltpu.VMEM((d // LANES, rows, LANES), F32)],
        compiler_params=_cparams("parallel"),
        name="even_outproj",
    )(ya, yb, w_a, w_b, x)


def _block_diag(blocks):
    n, r, c = blocks.shape
    eye = jnp.eye(n, dtype=blocks.dtype)
    return jnp.einsum('nrc,nm->nrmc', blocks, eye).reshape(n * r, n * c)


def _even_mixer(x, norm_g, w_in, lam_re, lam_im, log_dt, b_re, b_im, c_re, c_im, d, glu_w, glu_b,
                conv_w, conv_b, wa, ba, wi, bi, lam, w_out):
    nb, seq, dm = x.shape
    g, p = lam_re.shape
    hch = b_re.shape[-1]
    width = g * hch
    n_half = 2
    gh = g // n_half

    proj = _even_inproj(x, norm_g[None, :], w_in.astype(BF16))

    lb_re, lb_im, bb_re, bb_im = _s5_discretize(lam_re, lam_im, log_dt, b_re, b_im)
    bb_re = bb_re.reshape(n_half, gh, hch, p)
    bb_im = bb_im.reshape(n_half, gh, hch, p)
    bm = jnp.stack([jnp.concatenate([_block_diag(bb_re[j]), _block_diag(bb_im[j])], axis=1)
                    for j in range(n_half)]).astype(BF16)
    c_re_t = jnp.swapaxes(c_re, 1, 2).reshape(n_half, gh, p, hch)
    c_im_t = jnp.swapaxes(c_im, 1, 2).reshape(n_half, gh, p, hch)
    cm = jnp.stack([jnp.concatenate([_block_diag(c_re_t[j]), -_block_diag(c_im_t[j])], axis=0)
                    for j in range(n_half)]).astype(BF16)
    lre = jnp.broadcast_to(lb_re.reshape(1, g * p), (nb, g * p))
    lim = jnp.broadcast_to(lb_im.reshape(1, g * p), (nb, g * p))
    ya = _s5_mixer(proj, nb, bm, lre, lim, cm, d.reshape(1, width), glu_w.astype(BF16),
                   glu_b[None, :])

    lw = conv_w.shape[1]
    nblk = 2
    per = LRU_BLOCKS // nblk
    wa_bd = jnp.stack([_block_diag(wa[j * per:(j + 1) * per]) for j in range(nblk)]).astype(BF16)
    wi_bd = jnp.stack([_block_diag(wi[j * per:(j + 1) * per]) for j in range(nblk)]).astype(BF16)
    yb = _lru_mixer(proj, nb, conv_w, conv_b[None, :], wa_bd, ba[None, :], wi_bd, bi[None, :],
                    lam[None, :])

    w_out_b = w_out.astype(BF16)
    return _even_outproj(ya, yb, w_out_b[:width], w_out_b[width:], x)


def _rope_table_kernel(pos_ref, inv_ref, cos_ref, sin_ref):
    ang = pos_ref[...].astype(F32) * inv_ref[...]
    cos_ref[...] = jnp.cos(ang)
    sin_ref[...] = jnp.sin(ang)


def _rope_tables(positions):
    nb, seq = positions.shape
    half = ROPE_DIM // 2
    per_row = LANES // half
    n_tok = nb * seq
    inv_freq = 1.0 / (ROPE_THETA ** (jnp.arange(0, ROPE_DIM, 2, dtype=F32) / ROPE_DIM))
    pos_rep = jnp.repeat(positions.reshape(n_tok // per_row, per_row), half, axis=1)
    inv_rep = jnp.tile(inv_freq, per_row)[None, :]
    cos, sin = pl.pallas_call(
        _rope_table_kernel,
        out_shape=(jax.ShapeDtypeStruct(pos_rep.shape, F32), jax.ShapeDtypeStruct(pos_rep.shape, F32)),
        name="rope_table",
    )(pos_rep, inv_rep)
    cos = cos.reshape(n_tok, half)
    sin = sin.reshape(n_tok, half)
    cc = jnp.concatenate([cos, cos], axis=-1)
    ss = jnp.concatenate([-sin, sin], axis=-1)
    return cc, ss


def _mla_proj_kernel(x_ref, g_ref, win_ref, cqn_ref, ckvn_ref, wq_ref, wkv_ref, qg_ref, kg_ref,
                     cc_ref, ss_ref, qn_ref, qr_ref, kn_ref, kr_ref, v_ref):
    h = _rms(x_ref[...], g_ref[...])
    proj = _dot(h.astype(BF16), win_ref[...])
    c_q = _rms(proj[:, :Q_LORA], cqn_ref[...])
    c_kv = _rms(proj[:, Q_LORA:Q_LORA + KV_LORA], ckvn_ref[...])
    o = Q_LORA + KV_LORA
    k_r = proj[:, o:o + ROPE_DIM]
    k_sw = proj[:, o + ROPE_DIM:o + 2 * ROPE_DIM]
    cc = cc_ref[...]
    ss = ss_ref[...]
    qg = qg_ref[...]
    kg = kg_ref[...]
    q = _dot(c_q.astype(BF16), wq_ref[...])
    kv = _dot(c_kv.astype(BF16), wkv_ref[...])
    nh = MLA_HEADS
    scale = math.log2(math.e) / math.sqrt(QK_DIM)
    k_rot = k_r * kg[1:2, :ROPE_DIM] * cc + k_sw * kg[2:3, :ROPE_DIM] * ss
    k_r_ss = jnp.sum(k_r * k_r, axis=-1, keepdims=True)
    for hd in range(nh):
        q_n = q[:, hd * NOPE_DIM:(hd + 1) * NOPE_DIM]
        ro = nh * NOPE_DIM + hd * ROPE_DIM
        q_r = q[:, ro:ro + ROPE_DIM]
        q_sw = q[:, ro + nh * ROPE_DIM:ro + nh * ROPE_DIM + ROPE_DIM]
        ms = (jnp.sum(q_n * q_n, axis=-1, keepdims=True)
              + jnp.sum(q_r * q_r, axis=-1, keepdims=True)) / QK_DIM
        rs = lax.rsqrt(ms + RMS_EPS) * scale
        qn_ref[0, hd] = (q_n * rs * qg[0:1, :]).astype(qn_ref.dtype)
        qr_ref[0, hd] = ((q_r * qg[1:2, :ROPE_DIM] * cc + q_sw * qg[2:3, :ROPE_DIM] * ss) * rs
                         ).astype(qr_ref.dtype)
        ko = hd * (NOPE_DIM + V_DIM)
        k_n = kv[:, ko:ko + NOPE_DIM]
        ms_k = (jnp.sum(k_n * k_n, axis=-1, keepdims=True) + k_r_ss) / QK_DIM
        rs_k = lax.rsqrt(ms_k + RMS_EPS)
        kn_ref[0, hd] = (k_n * rs_k * kg[0:1, :]).astype(kn_ref.dtype)
        kr_ref[0, hd] = (k_rot * rs_k).astype(kr_ref.dtype)
        v_ref[0, hd] = kv[:, ko + NOPE_DIM:ko + NOPE_DIM + V_DIM].astype(v_ref.dtype)


def _swap_halves(a, axis=-1):
    h = a.shape[axis] // 2
    lo = lax.slice_in_dim(a, 0, h, axis=axis)
    hi = lax.slice_in_dim(a, h, 2 * h, axis=axis)
    return jnp.concatenate([hi, lo], axis=axis)


def _mla_proj(x, norm_g, positions, w_in, cq_norm, ckv_norm, w_uq, w_ukv, q_norm, k_norm):
    nb, seq, dm = x.shape
    nh = MLA_HEADS
    tm = ROW_TILE
    n_tok = nb * seq
    cc, ss = _rope_tables(positions)
    o = Q_LORA + KV_LORA
    w_kr = w_in[:, o:o + ROPE_DIM]
    w_in_x = jnp.concatenate([w_in, _swap_halves(w_kr)], axis=1).astype(BF16)
    wq = w_uq.reshape(Q_LORA, nh, QK_DIM)
    wq_n = wq[:, :, :NOPE_DIM].reshape(Q_LORA, nh * NOPE_DIM)
    wq_r = wq[:, :, NOPE_DIM:]
    wq_x = jnp.concatenate([wq_n, wq_r.reshape(Q_LORA, nh * ROPE_DIM),
                            _swap_halves(wq_r).reshape(Q_LORA, nh * ROPE_DIM)], axis=1).astype(BF16)

    def gains(gv):
        pad = jnp.zeros((NOPE_DIM - ROPE_DIM,), F32)
        r = gv[NOPE_DIM:]
        return jnp.stack([gv[:NOPE_DIM], jnp.concatenate([r, pad]),
                          jnp.concatenate([_swap_halves(r), pad])])

    x2 = x.reshape(n_tok, dm)
    per_b = seq // tm
    full = lambda a: pl.BlockSpec(a.shape, lambda i: (0,) * a.ndim)
    args = (x2, norm_g[None, :], w_in_x, cq_norm[None, :], ckv_norm[None, :], wq_x,
            w_ukv.astype(BF16), gains(q_norm), gains(k_norm), cc, ss)
    in_specs = [pl.BlockSpec((tm, dm), lambda i: (i, 0))] + [full(a) for a in args[1:9]] + [
        pl.BlockSpec((tm, ROPE_DIM), lambda i: (i, 0)), pl.BlockSpec((tm, ROPE_DIM), lambda i: (i, 0))]

    def head_spec(dh):
        return pl.BlockSpec((1, nh, tm, dh), lambda i: (i // per_b, 0, i % per_b, 0))

    def head_shape(dh):
        return jax.ShapeDtypeStruct((nb, nh, seq, dh), BF16)

    return pl.pallas_call(
        _mla_proj_kernel,
        out_shape=(head_shape(NOPE_DIM), head_shape(ROPE_DIM), head_shape(NOPE_DIM),
                   head_shape(ROPE_DIM), head_shape(V_DIM)),
        grid=(n_tok // tm,),
        in_specs=in_specs,
        out_specs=(head_spec(NOPE_DIM), head_spec(ROPE_DIM), head_spec(NOPE_DIM),
                   head_spec(ROPE_DIM), head_spec(V_DIM)),
        compiler_params=_cparams("parallel"),
        name="mla_proj",
    )(*args)


def _attention_kernel(qn_ref, qr_ref, kn_ref, kr_ref, v_ref, o_ref):
    seq = qn_ref.shape[2]
    t = ATT_TQ
    nt = (((1,), (1,)), ((), ()))
    tri = (lax.broadcasted_iota(jnp.int32, (t, t), 1) <= lax.broadcasted_iota(jnp.int32, (t, t), 0))
    ks = [jnp.concatenate([kn_ref[0, 0, j * t:(j + 1) * t, :], kr_ref[0, 0, j * t:(j + 1) * t, :]], axis=-1)
          for j in range(seq // t)]
    for i in range(seq // t):
        q = jnp.concatenate([qn_ref[0, 0, i * t:(i + 1) * t, :], qr_ref[0, 0, i * t:(i + 1) * t, :]], axis=-1)
        m = jnp.full((t, 1), NEG_INF, F32)
        l = jnp.zeros((t, 1), F32)
        acc = jnp.zeros((t, V_DIM), F32)
        for j in range(i + 1):
            s = lax.dot_general(q, ks[j], nt, preferred_element_type=F32)
            if j == i:
                s = jnp.where(tri, s, NEG_INF)
            m_new = jnp.maximum(m, jnp.max(s, axis=-1, keepdims=True))
            alpha = jnp.exp2(m - m_new)
            p = jnp.exp2(s - m_new)
            l = alpha * l + jnp.sum(p, axis=-1, keepdims=True)
            acc = alpha * acc + _dot(p.astype(BF16), v_ref[0, 0, j * t:(j + 1) * t, :])
            m = m_new
        o_ref[0, i * t:(i + 1) * t, :] = (acc / l).astype(o_ref.dtype)


def _attention(qn, qr, kn, kr, v):
    nb, nh, seq, _ = qn.shape

    def spec(dh):
        return pl.BlockSpec((1, 1, seq, dh), lambda b, h: (b, h, 0, 0))

    return pl.pallas_call(
        _attention_kernel,
        out_shape=jax.ShapeDtypeStruct((nb, seq, nh * V_DIM), BF16),
        grid=(nb, nh),
        in_specs=[spec(NOPE_DIM), spec(ROPE_DIM), spec(NOPE_DIM), spec(ROPE_DIM), spec(V_DIM)],
        out_specs=pl.BlockSpec((1, seq, V_DIM), lambda b, h: (b, 0, h)),
        compiler_params=_cparams("parallel", "parallel"),
        name="mla_attention",
    )(qn, qr, kn, kr, v)


def _proj_residual_kernel(a_ref, w_ref, x_ref, o_ref):
    o_ref[...] = x_ref[...] + _dot(a_ref[...], w_ref[...])


def _proj_residual(a, w, x2):
    n_tok, dm = x2.shape
    tm = ROW_TILE
    return pl.pallas_call(
        _proj_residual_kernel,
        out_shape=jax.ShapeDtypeStruct(x2.shape, x2.dtype),
        grid=(n_tok // tm,),
        in_specs=[pl.BlockSpec((tm, a.shape[1]), lambda i: (i, 0)),
                  pl.BlockSpec(w.shape, lambda i: (0, 0)),
                  pl.BlockSpec((tm, dm), lambda i: (i, 0))],
        out_specs=pl.BlockSpec((tm, dm), lambda i: (i, 0)),
        compiler_params=_cparams("parallel"),
        name="proj_residual",
    )(a, w, x2)


def _mla_mixer(x, norm_g, positions, w_in, cq_norm, ckv_norm, w_uq, w_ukv, q_norm, k_norm, w_o):
    nb, seq, dm = x.shape
    qn, qr, kn, kr, v = _mla_proj(x, norm_g, positions, w_in, cq_norm, ckv_norm, w_uq, w_ukv,
                                  q_norm, k_norm)
    o = _attention(qn, qr, kn, kr, v)
    out = _proj_residual(o.reshape(nb * seq, -1), w_o.astype(BF16), x.reshape(nb * seq, dm))
    return out.reshape(nb, seq, dm)


def _router_kernel(x_ref, g_ref, w_ref, b_ref, hn_ref, meta_ref, gate_ref, cnt_ref, base_ref):
    tm, dm = x_ref.shape
    ne = N_EXPERTS

    @pl.when(pl.program_id(0) == 0)
    def _():
        base_ref[...] = jnp.zeros_like(base_ref)

    hn = _rms(x_ref[...], g_ref[...])
    for c in range(ROW_CHUNKS):
        hn_ref[pl.ds(c, tm, stride=ROW_CHUNKS), :] = hn[:, c * LANES:(c + 1) * LANES]
    logits = _dot(hn.astype(BF16), w_ref[...]) + b_ref[...]
    lane = lax.broadcasted_iota(jnp.int32, logits.shape, 1)
    big = jnp.int32(1 << 20)

    def first_argmax(v):
        m = jnp.max(v, axis=-1, keepdims=True)
        idx = jnp.min(jnp.where(v == m, lane, big), axis=-1, keepdims=True)
        return m, idx

    is_g = (lane >= ne) & (lane < ne + N_GROUPS)
    gl = jnp.where(is_g, logits, NEG_INF)
    g_max, g_idx = first_argmax(gl)
    g_sel = g_idx - ne
    p_g = 1.0 / jnp.sum(jnp.where(is_g, jnp.exp(gl - g_max), 0.0), axis=-1, keepdims=True)
    in_grp = (lane >= g_sel * EXPERTS_PER_GROUP) & (lane < (g_sel + 1) * EXPERTS_PER_GROUP)
    el = jnp.where(in_grp, logits, NEG_INF)
    v0, e0 = first_argmax(el)
    v1, e1 = first_argmax(jnp.where(lane == e0, NEG_INF, el))
    t = jnp.exp(v1 - v0)
    gate0 = p_g / (1.0 + t)
    gate1 = p_g * t / (1.0 + t)

    oh0 = (lane == e0).astype(F32)
    oh1 = (lane == e1).astype(F32)
    both = oh0 + oh1
    r_i = lax.broadcasted_iota(jnp.int32, (tm, tm), 0)
    c_i = lax.broadcasted_iota(jnp.int32, (tm, tm), 1)
    strict_lower = (c_i < r_i).astype(BF16)
    before = _dot(strict_lower, both.astype(BF16)) + base_ref[0:1, :]
    rank0 = jnp.sum(oh0 * before, axis=-1, keepdims=True)
    rank1 = jnp.sum(oh1 * before, axis=-1, keepdims=True)
    base_ref[0:1, :] = base_ref[0:1, :] + jnp.sum(both, axis=0, keepdims=True)
    cnt_ref[...] = jnp.broadcast_to(base_ref[0:1, :], cnt_ref.shape).astype(jnp.int32)

    meta = jnp.where(lane == 0, e0, jnp.where(lane == 1, e1, jnp.where(
        lane == 2, rank0.astype(jnp.int32), jnp.where(lane == 3, rank1.astype(jnp.int32), 0))))
    meta_ref[...] = meta
    gate_ref[...] = jnp.where(lane == 0, gate0, jnp.where(lane == 1, gate1, 0.0))


def _router(x2, norm_g, w_group, b_group, w_expert, b_expert):
    n_tok, dm = x2.shape
    tm = ROW_TILE
    pad = LANES - N_EXPERTS - N_GROUPS
    w = jnp.concatenate([w_expert, w_group, jnp.zeros((dm, pad), F32)], axis=1).astype(BF16)
    b = jnp.concatenate([b_expert, b_group, jnp.zeros((pad,), F32)])[None, :]
    return pl.pallas_call(
        _router_kernel,
        out_shape=(jax.ShapeDtypeStruct((n_tok * ROW_CHUNKS, LANES), F32),
                   jax.ShapeDtypeStruct((n_tok, LANES), jnp.int32),
                   jax.ShapeDtypeStruct((n_tok, LANES), F32),
                   jax.ShapeDtypeStruct((SUBLANES, LANES), jnp.int32)),
        grid=(n_tok // tm,),
        in_specs=[pl.BlockSpec((tm, dm), lambda i: (i, 0)),
                  pl.BlockSpec((1, dm), lambda i: (0, 0)),
                  pl.BlockSpec((dm, LANES), lambda i: (0, 0)),
                  pl.BlockSpec((1, LANES), lambda i: (0, 0))],
        out_specs=(pl.BlockSpec((tm * ROW_CHUNKS, LANES), lambda i: (i, 0)),
                   pl.BlockSpec((tm, LANES), lambda i: (i, 0)),
                   pl.BlockSpec((tm, LANES), lambda i: (i, 0)),
                   pl.BlockSpec((SUBLANES, LANES), lambda i: (0, 0))),
        scratch_shapes=[pltpu.VMEM((SUBLANES, LANES), F32)],
        compiler_params=_cparams("arbitrary"),
        name="moe_router",
    )(x2, norm_g[None, :], w, b)


def _dispatch_kernel(dest_ref, hn_ref, xd_in_ref, xd_ref, sem):
    del xd_in_ref
    tm = hn_ref.shape[0] // ROW_CHUNKS

    def row_copy(t, k):
        src = pl.multiple_of(t * ROW_CHUNKS, ROW_CHUNKS)
        dst = pl.multiple_of(dest_ref[0, 0, 2 * t + k], ROW_CHUNKS)
        return pltpu.make_async_copy(hn_ref.at[pl.ds(src, ROW_CHUNKS)], xd_ref.at[pl.ds(dst, ROW_CHUNKS)], sem)

    def issue(i, c):
        for u in range(DMA_UNROLL):
            for k in range(TOP_K):
                row_copy(i * DMA_UNROLL + u, k).start(priority=k)
        return c

    lax.fori_loop(0, tm // DMA_UNROLL, issue, 0)

    def drain(i, c):
        for u in range(DMA_UNROLL):
            for k in range(TOP_K):
                row_copy(i * DMA_UNROLL + u, k).wait()
        return c

    lax.fori_loop(0, tm // DMA_UNROLL, drain, 0)


def _dispatch(hn2, dest_off, n_rows):
    tm = GATHER_TM
    n_tok = hn2.shape[0] // ROW_CHUNKS
    dest3 = dest_off.reshape(n_tok // tm, 1, 2 * tm)
    xd0 = jnp.zeros((n_rows * ROW_CHUNKS, LANES), hn2.dtype)
    return pl.pallas_call(
        _dispatch_kernel,
        out_shape=jax.ShapeDtypeStruct(xd0.shape, xd0.dtype),
        grid=(n_tok // tm,),
        in_specs=[pl.BlockSpec((1, 1, 2 * tm), lambda i: (i, 0, 0), memory_space=pltpu.SMEM),
                  pl.BlockSpec((tm * ROW_CHUNKS, LANES), lambda i: (i, 0)),
                  pl.BlockSpec(memory_space=pl.ANY)],
        out_specs=pl.BlockSpec(memory_space=pl.ANY),
        scratch_shapes=[pltpu.SemaphoreType.DMA],
        input_output_aliases={2: 0},
        compiler_params=_cparams("arbitrary"),
        name="moe_dispatch",
    )(dest3, hn2, xd0)


def _rows_from_tiles(ref, n, base=0):
    return jnp.concatenate([ref[pl.ds(base + c, n, stride=ROW_CHUNKS), :] for c in range(ROW_CHUNKS)], axis=-1)


def _expert_kernel(be_ref, nu_ref, x_ref, w1_ref, w3_ref, w2_ref, y_ref, w13_s, w2_s):
    b = pl.program_id(0)
    bm = x_ref.shape[0] // ROW_CHUNKS
    de = w1_ref.shape[3]

    @pl.when(b < nu_ref[0])
    def _():
        changed = jnp.logical_or(b == 0, be_ref[b] != be_ref[jnp.maximum(b - 1, 0)])

        @pl.when(changed)
        def _():
            w13_s[:, :de] = w1_ref[0, 0].astype(BF16)
            w13_s[:, de:] = w3_ref[0, 0].astype(BF16)
            w2_s[...] = w2_ref[0, 0].astype(BF16)

        x = _rows_from_tiles(x_ref, bm).astype(BF16)
        h = _dot(x, w13_s[...])
        a = jax.nn.silu(h[:, :de]) * h[:, de:]
        y = _dot(a.astype(BF16), w2_s[...])
        for c in range(ROW_CHUNKS):
            y_ref[pl.ds(c, bm, stride=ROW_CHUNKS), :] = y[:, c * LANES:(c + 1) * LANES]

    @pl.when(b >= nu_ref[0])
    def _():
        y_ref[...] = jnp.zeros_like(y_ref)


def _experts(xd, blk_expert, n_used, w1, w3, w2, layer):
    bm = MOE_BM
    n_blocks = xd.shape[0] // (bm * ROW_CHUNKS)
    _, _, dm, de = w1.shape

    def row_map(b, be, nu):
        return (jnp.minimum(b, nu[0] - 1), 0)

    def w_map(b, be, nu):
        return (layer, be[b], 0, 0)

    grid_spec = pltpu.PrefetchScalarGridSpec(
        num_scalar_prefetch=2,
        grid=(n_blocks,),
        in_specs=[pl.BlockSpec((bm * ROW_CHUNKS, LANES), row_map),
                  pl.BlockSpec((1, 1, dm, de), w_map),
                  pl.BlockSpec((1, 1, dm, de), w_map),
                  pl.BlockSpec((1, 1, de, dm), w_map)],
        out_specs=pl.BlockSpec((bm * ROW_CHUNKS, LANES), lambda b, be, nu: (b, 0)),
        scratch_shapes=[pltpu.VMEM((dm, 2 * de), BF16), pltpu.VMEM((de, dm), BF16)],
    )
    return pl.pallas_call(
        _expert_kernel,
        out_shape=jax.ShapeDtypeStruct(xd.shape, F32),
        grid_spec=grid_spec,
        compiler_params=_cparams("arbitrary"),
        name="moe_experts",
    )(blk_expert, n_used, xd, w1, w3, w2)


def _combine_kernel(dest_ref, dest_next_ref, yd_ref, gate_ref, x_ref, o_ref, buf_ref, sem):
    tm, dm = x_ref.shape
    i = pl.program_id(0)
    n_steps = pl.num_programs(0)
    slot = i % 2
    k_rows = tm * ROW_CHUNKS

    def row_copy(d_ref, sl, t, k):
        src = pl.multiple_of(d_ref[0, 0, 2 * t + k], ROW_CHUNKS)
        dst = pl.multiple_of(k * k_rows + t * ROW_CHUNKS, ROW_CHUNKS)
        return pltpu.make_async_copy(yd_ref.at[pl.ds(src, ROW_CHUNKS)],
                                     buf_ref.at[sl, pl.ds(dst, ROW_CHUNKS)], sem.at[sl])

    def issue_all(d_ref, sl):
        def issue(j, c):
            for u in range(DMA_UNROLL):
                for k in range(TOP_K):
                    row_copy(d_ref, sl, j * DMA_UNROLL + u, k).start(priority=k)
            return c
        lax.fori_loop(0, tm // DMA_UNROLL, issue, 0)

    @pl.when(i == 0)
    def _():
        issue_all(dest_ref, slot)

    @pl.when(i + 1 < n_steps)
    def _():
        issue_all(dest_next_ref, 1 - slot)

    def drain(j, c):
        for u in range(DMA_UNROLL):
            for k in range(TOP_K):
                row_copy(dest_ref, slot, j * DMA_UNROLL + u, k).wait()
        return c

    lax.fori_loop(0, tm // DMA_UNROLL, drain, 0)

    g0 = gate_ref[:, 0:1]
    g1 = gate_ref[:, 1:2]
    buf = buf_ref.at[slot]
    for c in range(ROW_CHUNKS):
        y0 = buf[pl.ds(c, tm, stride=ROW_CHUNKS), :]
        y1 = buf[pl.ds(k_rows + c, tm, stride=ROW_CHUNKS), :]
        o_ref[:, c * LANES:(c + 1) * LANES] = x_ref[:, c * LANES:(c + 1) * LANES] + (g0 * y0 + g1 * y1)


def _combine(yd, dest_off, gates, x2):
    n_tok, dm = x2.shape
    tm = GATHER_TM
    n_steps = n_tok // tm
    dest3 = dest_off.reshape(n_steps, 1, 2 * tm)
    return pl.pallas_call(
        _combine_kernel,
        out_shape=jax.ShapeDtypeStruct(x2.shape, x2.dtype),
        grid=(n_steps,),
        in_specs=[pl.BlockSpec((1, 1, 2 * tm), lambda i: (i, 0, 0), memory_space=pltpu.SMEM),
                  pl.BlockSpec((1, 1, 2 * tm), lambda i: (jnp.minimum(i + 1, n_steps - 1), 0, 0),
                               memory_space=pltpu.SMEM),
                  pl.BlockSpec(memory_space=pl.ANY),
                  pl.BlockSpec((tm, LANES), lambda i: (i, 0)),
                  pl.BlockSpec((tm, dm), lambda i: (i, 0))],
        out_specs=pl.BlockSpec((tm, dm), lambda i: (i, 0)),
        scratch_shapes=[pltpu.VMEM((2, TOP_K * tm * ROW_CHUNKS, LANES), F32),
                        pltpu.SemaphoreType.DMA((2,))],
        compiler_params=_cparams("arbitrary"),
        name="moe_combine",
    )(dest3, dest3, yd, gates, x2)


def _moe(x, norm_g, w_group, b_group, w_expert, b_expert, w1, w3, w2, layer):
    nb, seq, dm = x.shape
    n_tok = nb * seq
    x2 = x.reshape(n_tok, dm)
    hn2, meta, gates, counts = _router(x2, norm_g, w_group, b_group, w_expert, b_expert)
    bm = MOE_BM
    n_blocks = (n_tok * TOP_K) // bm + N_EXPERTS
    counts = counts[0, :N_EXPERTS]
    padded = ((counts + bm - 1) // bm) * bm
    pad_ends = jnp.cumsum(padded)
    pad_starts = pad_ends - padded
    eid = meta[:, 0:TOP_K]
    e_iota = jnp.arange(N_EXPERTS, dtype=jnp.int32)
    seg_start = jnp.sum(jnp.where(eid[:, :, None] == e_iota, pad_starts, 0), axis=-1)
    dest_off = ((seg_start + meta[:, TOP_K:2 * TOP_K]) * ROW_CHUNKS).astype(jnp.int32).reshape(-1)
    blk_start = jnp.arange(n_blocks, dtype=jnp.int32) * bm
    blk_expert = jnp.minimum(jnp.sum((blk_start[:, None] >= pad_ends[None, :]).astype(jnp.int32), axis=1),
                             N_EXPERTS - 1)
    n_used = (pad_ends[-1] // bm).astype(jnp.int32).reshape(1)

    xd = _dispatch(hn2, dest_off, n_blocks * bm)
    yd = _experts(xd, blk_expert, n_used, w1, w3, w2, layer)
    out = _combine(yd, dest_off, gates, x2)
    return out.reshape(nb, seq, dm)


def kernel(x, positions, ev_norm, ev_w_in, s5_lambda_re, s5_lambda_im, s5_log_dt, s5_b_re, s5_b_im,
           s5_c_re, s5_c_im, s5_d, s5_glu_w, s5_glu_b, lru_conv_w, lru_conv_b, lru_wa, lru_ba, lru_wi,
           lru_bi, lru_lambda, ev_w_out, od_norm, mla_w_in, mla_cq_norm, mla_ckv_norm, mla_w_uq,
           mla_w_ukv, mla_q_norm, mla_k_norm, mla_w_o, ffn_norm, moe_w_group, moe_b_group,
           moe_w_expert, moe_b_expert, moe_w1, moe_w3, moe_w2):
    depth = ffn_norm.shape[0]
    for layer in range(depth):
        j = layer // 2
        if layer % 2 == 0:
            x = _even_mixer(x, ev_norm[j], ev_w_in[j], s5_lambda_re[j], s5_lambda_im[j], s5_log_dt[j],
                            s5_b_re[j], s5_b_im[j], s5_c_re[j], s5_c_im[j], s5_d[j], s5_glu_w[j],
                            s5_glu_b[j], lru_conv_w[j], lru_conv_b[j], lru_wa[j], lru_ba[j], lru_wi[j],
                            lru_bi[j], lru_lambda[j], ev_w_out[j])
        else:
            x = _mla_mixer(x, od_norm[j], positions, mla_w_in[j], mla_cq_norm[j], mla_ckv_norm[j],
                           mla_w_uq[j], mla_w_ukv[j], mla_q_norm[j], mla_k_norm[j], mla_w_o[j])
        x = _moe(x, ffn_norm[layer], moe_w_group[layer], moe_b_group[layer], moe_w_expert[layer],
                 moe_b_expert[layer], moe_w1, moe_w3, moe_w2, layer)
    return x
```

```python
import functools
import math

import jax
import jax.numpy as jnp
from jax import lax
from jax.experimental import pallas as pl
from jax.experimental.pallas import tpu as pltpu

F32 = jnp.float32
BF16 = jnp.bfloat16

S5_GROUP_CH = 16
S5_STATE = 64
LRU_BLOCKS = 8
CONV_WIDTH = 4
RG_C = 8.0
MLA_HEADS = 8
Q_LORA = 384
KV_LORA = 256
NOPE_DIM = 128
ROPE_DIM = 64
V_DIM = 128
QK_DIM = NOPE_DIM + ROPE_DIM
ROPE_THETA = 10000.0
N_GROUPS = 8
EXPERTS_PER_GROUP = 8
N_EXPERTS = N_GROUPS * EXPERTS_PER_GROUP
TOP_K = 2
RMS_EPS = 1e-6
NEG_INF = -1e30

SUBLANES = 8
LANES = 128
ROW_CHUNKS = 8
VMEM_LIMIT = 48 * 1024 * 1024

EVEN_TL = 64
ROW_TILE = 512
MOE_BM = 256
ATT_TQ = 512
DMA_UNROLL = 8


def _cparams(*sem):
    return pltpu.CompilerParams(dimension_semantics=tuple(sem), vmem_limit_bytes=VMEM_LIMIT)


def _rms(x, g):
    ms = jnp.mean(x * x, axis=-1, keepdims=True)
    return x * lax.rsqrt(ms + RMS_EPS) * g


def _dot(a, b):
    return jnp.dot(a, b, preferred_element_type=F32)


def _s5_discretize_kernel(lre_ref, lim_ref, ldt_ref, bre_ref, bim_ref,
                          lbre_ref, lbim_ref, bbre_ref, bbim_ref):
    lr = jnp.minimum(lre_ref[...], -1e-4)
    li = lim_ref[...]
    dt = jnp.exp(ldt_ref[...])
    mag = jnp.exp(lr * dt)
    lb_re = mag * jnp.cos(li * dt)
    lb_im = mag * jnp.sin(li * dt)
    den = lr * lr + li * li
    num_re = lb_re - 1.0
    f_re = (num_re * lr + lb_im * li) / den
    f_im = (lb_im * lr - num_re * li) / den
    lbre_ref[...] = lb_re
    lbim_ref[...] = lb_im
    br = bre_ref[...]
    bi = bim_ref[...]
    bbre_ref[...] = f_re[:, None, :] * br - f_im[:, None, :] * bi
    bbim_ref[...] = f_re[:, None, :] * bi + f_im[:, None, :] * br


def _s5_discretize(lam_re, lam_im, log_dt, b_re, b_im):
    g, p = lam_re.shape
    h = b_re.shape[-1]
    b_re_t = jnp.swapaxes(b_re, 1, 2)
    b_im_t = jnp.swapaxes(b_im, 1, 2)
    return pl.pallas_call(
        _s5_discretize_kernel,
        out_shape=(jax.ShapeDtypeStruct((g, p), F32), jax.ShapeDtypeStruct((g, p), F32),
                   jax.ShapeDtypeStruct((g, h, p), F32), jax.ShapeDtypeStruct((g, h, p), F32)),
        name="s5_discretize",
    )(lam_re, lam_im, log_dt[:, None], b_re_t, b_im_t)


def _even_inproj_kernel(x_ref, g_ref, w_ref, o_ref, h_ref):
    nb, tl, _ = x_ref.shape
    nc = h_ref.shape[0]
    g = g_ref[...]
    for b in range(nb):
        h = _rms(x_ref[b], g)
        for c in range(nc):
            h_ref[c, pl.ds(b, tl, stride=nb), :] = h[:, c * LANES:(c + 1) * LANES]
    h_all = jnp.concatenate([h_ref[c] for c in range(nc)], axis=-1)
    o_ref[...] = _dot(h_all.astype(BF16), w_ref[...]).astype(o_ref.dtype)


def _even_inproj(x, g, w):
    nb, seq, d = x.shape
    n_out = w.shape[1]
    tl = EVEN_TL
    return pl.pallas_call(
        _even_inproj_kernel,
        out_shape=jax.ShapeDtypeStruct((seq * nb, n_out), BF16),
        grid=(seq // tl,),
        in_specs=[pl.BlockSpec((nb, tl, d), lambda i: (0, i, 0)),
                  pl.BlockSpec((1, d), lambda i: (0, 0)),
                  pl.BlockSpec((d, n_out), lambda i: (0, 0))],
        out_specs=pl.BlockSpec((tl * nb, n_out), lambda i: (i, 0)),
        scratch_shapes=[pltpu.VMEM((d // LANES, tl * nb, LANES), F32)],
        compiler_params=_cparams("parallel"),
        name="even_inproj",
    )(x, g, w)


def _s5_kernel(u_ref, bm_ref, lre_ref, lim_ref, cm_ref, d_ref, gw_ref, gb_ref, o_ref,
               bu_ref, sb_ref, st_ref, *, nb):
    rows, width = u_ref.shape
    n_half = bm_ref.shape[0]
    kin = width // n_half
    ncol = bm_ref.shape[2]
    nre = ncol // 2
    chunk = 512
    steps = rows // nb

    @pl.when(pl.program_id(0) == 0)
    def _():
        st_ref[...] = jnp.zeros_like(st_ref)

    u = u_ref[...]
    for j in range(n_half):
        bu_ref[:, j * ncol:(j + 1) * ncol] = _dot(u[:, j * kin:(j + 1) * kin], bm_ref[j])

    for j in range(n_half):
        for c in range(nre // chunk):
            cr = j * ncol + c * chunk
            ci = cr + nre
            lc = j * nre + c * chunk
            lr = lre_ref[:, lc:lc + chunk]
            li = lim_ref[:, lc:lc + chunk]

            def body(i, carry, cr=cr, ci=ci, lr=lr, li=li):
                sr, si = carry
                r0 = pl.multiple_of(i * (2 * nb), 2 * nb)
                sr1 = lr * sr - li * si + bu_ref[pl.ds(r0, nb), cr:cr + chunk]
                si1 = lr * si + li * sr + bu_ref[pl.ds(r0, nb), ci:ci + chunk]
                sr2 = lr * sr1 - li * si1 + bu_ref[pl.ds(r0 + nb, nb), cr:cr + chunk]
                si2 = lr * si1 + li * sr1 + bu_ref[pl.ds(r0 + nb, nb), ci:ci + chunk]
                sb_ref[pl.ds(r0, 2 * nb), cr:cr + chunk] = jnp.concatenate([sr1, sr2], 0).astype(BF16)
                sb_ref[pl.ds(r0, 2 * nb), ci:ci + chunk] = jnp.concatenate([si1, si2], 0).astype(BF16)
                return sr2, si2

            sr, si = lax.fori_loop(0, steps // 2, body,
                                   (st_ref[:, cr:cr + chunk], st_ref[:, ci:ci + chunk]))
            st_ref[:, cr:cr + chunk] = sr
            st_ref[:, ci:ci + chunk] = si

    ys = [_dot(sb_ref[:, j * ncol:(j + 1) * ncol], cm_ref[j]) for j in range(n_half)]
    y = jnp.concatenate(ys, axis=-1) + d_ref[...] * u.astype(F32)
    y = jax.nn.gelu(y)
    z = _dot(y.astype(BF16), gw_ref[...]) + gb_ref[...]
    o_ref[...] = (y * jax.nn.sigmoid(z)).astype(o_ref.dtype)


def _s5_mixer(proj, nb, bm, lre, lim, cm, d, glu_w, glu_b):
    rows_total = proj.shape[0]
    width = d.shape[1]
    rows = EVEN_TL * nb
    n_half, _, ncol = bm.shape
    kern = functools.partial(_s5_kernel, nb=nb)
    return pl.pallas_call(
        kern,
        out_shape=jax.ShapeDtypeStruct((rows_total, width), BF16),
        grid=(rows_total // rows,),
        in_specs=[pl.BlockSpec((rows, width), lambda i: (i, 0)),
                  pl.BlockSpec(bm.shape, lambda i: (0, 0, 0)),
                  pl.BlockSpec(lre.shape, lambda i: (0, 0)),
                  pl.BlockSpec(lim.shape, lambda i: (0, 0)),
                  pl.BlockSpec(cm.shape, lambda i: (0, 0, 0)),
                  pl.BlockSpec((1, width), lambda i: (0, 0)),
                  pl.BlockSpec(glu_w.shape, lambda i: (0, 0)),
                  pl.BlockSpec((1, width), lambda i: (0, 0))],
        out_specs=pl.BlockSpec((rows, width), lambda i: (i, 0)),
        scratch_shapes=[pltpu.VMEM((rows, n_half * ncol), F32),
                        pltpu.VMEM((rows, n_half * ncol), BF16),
                        pltpu.VMEM((nb, n_half * ncol), F32)],
        compiler_params=_cparams("arbitrary"),
        name="s5_mixer",
    )(proj, bm, lre, lim, cm, d, glu_w, glu_b)


def _lru_kernel(x_ref, gate_ref, cw_ref, cb_ref, wa_ref, ba_ref, wi_ref, bi_ref, lam_ref, o_ref,
                xp_ref, a_ref, b_ref, h_ref, *, nb):
    rows, width = x_ref.shape
    halo = (CONV_WIDTH - 1) * nb
    steps = rows // nb

    @pl.when(pl.program_id(0) == 0)
    def _():
        xp_ref[0:halo, :] = jnp.zeros((halo, width), F32)
        h_ref[...] = jnp.zeros_like(h_ref)

    xp_ref[halo:halo + rows, :] = x_ref[...].astype(F32)
    xf = cb_ref[...] + cw_ref[0:1, :] * xp_ref[0:rows, :]
    for k in range(1, CONV_WIDTH):
        xf = xf + cw_ref[k:k + 1, :] * xp_ref[k * nb:k * nb + rows, :]
    xp_ref[0:halo, :] = xp_ref[rows:rows + halo, :]

    xb = xf.astype(BF16)
    nblk = wa_ref.shape[0]
    kb = width // nblk
    ga = jnp.concatenate([_dot(xb[:, j * kb:(j + 1) * kb], wa_ref[j]) for j in range(nblk)], -1)
    gi = jnp.concatenate([_dot(xb[:, j * kb:(j + 1) * kb], wi_ref[j]) for j in range(nblk)], -1)
    r = jax.nn.sigmoid(ga + ba_ref[...])
    ig = jax.nn.sigmoid(gi + bi_ref[...])
    log_a = RG_C * r * jax.nn.log_sigmoid(lam_ref[...])
    a_ref[...] = jnp.exp(log_a)
    th = jnp.tanh(log_a)
    b_ref[...] = jnp.sqrt(-2.0 * th / (1.0 - th)) * (ig * xf)

    def body(t, h):
        r0 = pl.multiple_of(t * nb, nb)
        h = a_ref[pl.ds(r0, nb), :] * h + b_ref[pl.ds(r0, nb), :]
        b_ref[pl.ds(r0, nb), :] = h
        return h

    h_ref[...] = lax.fori_loop(0, steps, body, h_ref[...])
    o_ref[...] = (b_ref[...] * jax.nn.gelu(gate_ref[...].astype(F32))).astype(o_ref.dtype)


def _lru_mixer(proj, nb, conv_w, conv_b, wa, ba, wi, bi, lam):
    rows_total = proj.shape[0]
    width = conv_w.shape[1]
    rows = EVEN_TL * nb
    halo = (CONV_WIDTH - 1) * nb
    kern = functools.partial(_lru_kernel, nb=nb)
    vec = pl.BlockSpec((1, width), lambda i: (0, 0))
    return pl.pallas_call(
        kern,
        out_shape=jax.ShapeDtypeStruct((rows_total, width), BF16),
        grid=(rows_total // rows,),
        in_specs=[pl.BlockSpec((rows, width), lambda i: (i, 1)),
                  pl.BlockSpec((rows, width), lambda i: (i, 2)),
                  pl.BlockSpec(conv_w.shape, lambda i: (0, 0)),
                  vec,
                  pl.BlockSpec(wa.shape, lambda i: (0, 0, 0)),
                  vec,
                  pl.BlockSpec(wi.shape, lambda i: (0, 0, 0)),
                  vec, vec],
        out_specs=pl.BlockSpec((rows, width), lambda i: (i, 0)),
        scratch_shapes=[pltpu.VMEM((rows + halo, width), F32),
                        pltpu.VMEM((rows, width), F32),
                        pltpu.VMEM((rows, width), F32),
                        pltpu.VMEM((nb, width), F32)],
        compiler_params=_cparams("arbitrary"),
        name="lru_mixer",
    )(proj, proj, conv_w, conv_b, wa, ba, wi, bi, lam)


def _even_outproj_kernel(ya_ref, yb_ref, wa_ref, wb_ref, x_ref, o_ref, y_ref):
    nb, tl, _ = x_ref.shape
    nc = y_ref.shape[0]
    y = _dot(ya_ref[...], wa_ref[...]) + _dot(yb_ref[...], wb_ref[...])
    for c in range(nc):
        y_ref[c] = y[:, c * LANES:(c + 1) * LANES]
    for b in range(nb):
        for c in range(nc):
            o_ref[b, :, c * LANES:(c + 1) * LANES] = (x_ref[b, :, c * LANES:(c + 1) * LANES]
                                                     + y_ref[c, pl.ds(b, tl, stride=nb), :])


def _even_outproj(ya, yb, w_a, w_b, x):
    nb, seq, d = x.shape
    tl = EVEN_TL
    rows = tl * nb
    wd = ya.shape[1]
    return pl.pallas_call(
        _even_outproj_kernel,
        out_shape=jax.ShapeDtypeStruct(x.shape, x.dtype),
        grid=(seq // tl,),
        in_specs=[pl.BlockSpec((rows, wd), lambda i: (i, 0)),
                  pl.BlockSpec((rows, wd), lambda i: (i, 0)),
                  pl.BlockSpec(w_a.shape, lambda i: (0, 0)),
                  pl.BlockSpec(w_b.shape, lambda i: (0, 0)),
                  pl.BlockSpec((nb, tl, d), lambda i: (0, i, 0))],
        out_specs=pl.BlockSpec((nb, tl, d), lambda i: (0, i, 0)),
        scratch_shapes=[pltpu.VMEM((d // LANES, rows, LANES), F32)],
        compiler_params=_cparams("parallel"),
        name="even_outproj",
    )(ya, yb, w_a, w_b, x)


def _block_diag(blocks):
    n, r, c = blocks.shape
    eye = jnp.eye(n, dtype=blocks.dtype)
    return jnp.einsum('nrc,nm->nrmc', blocks, eye).reshape(n * r, n * c)


def _even_mixer(x, norm_g, w_in, lam_re, lam_im, log_dt, b_re, b_im, c_re, c_im, d, glu_w, glu_b,
                conv_w, conv_b, wa, ba, wi, bi, lam, w_out):
    nb, seq, dm = x.shape
    g, p = lam_re.shape
    hch = b_re.shape[-1]
    width = g * hch
    n_half = 2
    gh = g // n_half

    proj = _even_inproj(x, norm_g[None, :], w_in.astype(BF16))

    lb_re, lb_im, bb_re, bb_im = _s5_discretize(lam_re, lam_im, log_dt, b_re, b_im)
    bb_re = bb_re.reshape(n_half, gh, hch, p)
    bb_im = bb_im.reshape(n_half, gh, hch, p)
    bm = jnp.stack([jnp.concatenate([_block_diag(bb_re[j]), _block_diag(bb_im[j])], axis=1)
                    for j in range(n_half)]).astype(BF16)
    c_re_t = jnp.swapaxes(c_re, 1, 2).reshape(n_half, gh, p, hch)
    c_im_t = jnp.swapaxes(c_im, 1, 2).reshape(n_half, gh, p, hch)
    cm = jnp.stack([jnp.concatenate([_block_diag(c_re_t[j]), -_block_diag(c_im_t[j])], axis=0)
                    for j in range(n_half)]).astype(BF16)
    lre = jnp.broadcast_to(lb_re.reshape(1, g * p), (nb, g * p))
    lim = jnp.broadcast_to(lb_im.reshape(1, g * p), (nb, g * p))
    ya = _s5_mixer(proj, nb, bm, lre, lim, cm, d.reshape(1, width), glu_w.astype(BF16),
                   glu_b[None, :])

    lw = conv_w.shape[1]
    nblk = 2
    per = LRU_BLOCKS // nblk
    wa_bd = jnp.stack([_block_diag(wa[j * per:(j + 1) * per]) for j in range(nblk)]).astype(BF16)
    wi_bd = jnp.stack([_block_diag(wi[j * per:(j + 1) * per]) for j in range(nblk)]).astype(BF16)
    yb = _lru_mixer(proj, nb, conv_w, conv_b[None, :], wa_bd, ba[None, :], wi_bd, bi[None, :],
                    lam[None, :])

    w_out_b = w_out.astype(BF16)
    return _even_outproj(ya, yb, w_out_b[:width], w_out_b[width:], x)


def _rope_table_kernel(pos_ref, inv_ref, cos_ref, sin_ref):
    ang = pos_ref[...].astype(F32) * inv_ref[...]
    cos_ref[...] = jnp.cos(ang)
    sin_ref[...] = jnp.sin(ang)


def _rope_tables(positions):
    nb, seq = positions.shape
    half = ROPE_DIM // 2
    per_row = LANES // half
    n_tok = nb * seq
    inv_freq = 1.0 / (ROPE_THETA ** (jnp.arange(0, ROPE_DIM, 2, dtype=F32) / ROPE_DIM))
    pos_rep = jnp.repeat(positions.reshape(n_tok // per_row, per_row), half, axis=1)
    inv_rep = jnp.tile(inv_freq, per_row)[None, :]
    cos, sin = pl.pallas_call(
        _rope_table_kernel,
        out_shape=(jax.ShapeDtypeStruct(pos_rep.shape, F32), jax.ShapeDtypeStruct(pos_rep.shape, F32)),
        name="rope_table",
    )(pos_rep, inv_rep)
    cos = cos.reshape(n_tok, half)
    sin = sin.reshape(n_tok, half)
    cc = jnp.concatenate([cos, cos], axis=-1)
    ss = jnp.concatenate([-sin, sin], axis=-1)
    return cc, ss


def _mla_proj_kernel(x_ref, g_ref, win_ref, cqn_ref, ckvn_ref, wq_ref, wkv_ref, qg_ref, kg_ref,
                     cc_ref, ss_ref, qn_ref, qr_ref, kn_ref, kr_ref, v_ref):
    h = _rms(x_ref[...], g_ref[...])
    proj = _dot(h.astype(BF16), win_ref[...])
    c_q = _rms(proj[:, :Q_LORA], cqn_ref[...])
    c_kv = _rms(proj[:, Q_LORA:Q_LORA + KV_LORA], ckvn_ref[...])
    o = Q_LORA + KV_LORA
    k_r = proj[:, o:o + ROPE_DIM]
    k_sw = proj[:, o + ROPE_DIM:o + 2 * ROPE_DIM]
    cc = cc_ref[...]
    ss = ss_ref[...]
    qg = qg_ref[...]
    kg = kg_ref[...]
    q = _dot(c_q.astype(BF16), wq_ref[...])
    kv = _dot(c_kv.astype(BF16), wkv_ref[...])
    nh = MLA_HEADS
    scale = math.log2(math.e) / math.sqrt(QK_DIM)
    k_rot = k_r * kg[1:2, :ROPE_DIM] * cc + k_sw * kg[2:3, :ROPE_DIM] * ss
    k_r_ss = jnp.sum(k_r * k_r, axis=-1, keepdims=True)
    for hd in range(nh):
        q_n = q[:, hd * NOPE_DIM:(hd + 1) * NOPE_DIM]
        ro = nh * NOPE_DIM + hd * ROPE_DIM
        q_r = q[:, ro:ro + ROPE_DIM]
        q_sw = q[:, ro + nh * ROPE_DIM:ro + nh * ROPE_DIM + ROPE_DIM]
        ms = (jnp.sum(q_n * q_n, axis=-1, keepdims=True)
              + jnp.sum(q_r * q_r, axis=-1, keepdims=True)) / QK_DIM
        rs = lax.rsqrt(ms + RMS_EPS) * scale
        qn_ref[0, hd] = (q_n * rs * qg[0:1, :]).astype(qn_ref.dtype)
        qr_ref[0, hd] = ((q_r * qg[1:2, :ROPE_DIM] * cc + q_sw * qg[2:3, :ROPE_DIM] * ss) * rs
                         ).astype(qr_ref.dtype)
        ko = hd * (NOPE_DIM + V_DIM)
        k_n = kv[:, ko:ko + NOPE_DIM]
        ms_k = (jnp.sum(k_n * k_n, axis=-1, keepdims=True) + k_r_ss) / QK_DIM
        rs_k = lax.rsqrt(ms_k + RMS_EPS)
        kn_ref[0, hd] = (k_n * rs_k * kg[0:1, :]).astype(kn_ref.dtype)
        kr_ref[0, hd] = (k_rot * rs_k).astype(kr_ref.dtype)
        v_ref[0, hd] = kv[:, ko + NOPE_DIM:ko + NOPE_DIM + V_DIM].astype(v_ref.dtype)


def _swap_halves(a, axis=-1):
    h = a.shape[axis] // 2
    lo = lax.slice_in_dim(a, 0, h, axis=axis)
    hi = lax.slice_in_dim(a, h, 2 * h, axis=axis)
    return jnp.concatenate([hi, lo], axis=axis)


def _mla_proj(x, norm_g, positions, w_in, cq_norm, ckv_norm, w_uq, w_ukv, q_norm, k_norm):
    nb, seq, dm = x.shape
    nh = MLA_HEADS
    tm = ROW_TILE
    n_tok = nb * seq
    cc, ss = _rope_tables(positions)
    o = Q_LORA + KV_LORA
    w_kr = w_in[:, o:o + ROPE_DIM]
    w_in_x = jnp.concatenate([w_in, _swap_halves(w_kr)], axis=1).astype(BF16)
    wq = w_uq.reshape(Q_LORA, nh, QK_DIM)
    wq_n = wq[:, :, :NOPE_DIM].reshape(Q_LORA, nh * NOPE_DIM)
    wq_r = wq[:, :, NOPE_DIM:]
    wq_x = jnp.concatenate([wq_n, wq_r.reshape(Q_LORA, nh * ROPE_DIM),
                            _swap_halves(wq_r).reshape(Q_LORA, nh * ROPE_DIM)], axis=1).astype(BF16)

    def gains(gv):
        pad = jnp.zeros((NOPE_DIM - ROPE_DIM,), F32)
        r = gv[NOPE_DIM:]
        return jnp.stack([gv[:NOPE_DIM], jnp.concatenate([r, pad]),
                          jnp.concatenate([_swap_halves(r), pad])])

    x2 = x.reshape(n_tok, dm)
    per_b = seq // tm
    full = lambda a: pl.BlockSpec(a.shape, lambda i: (0,) * a.ndim)
    args = (x2, norm_g[None, :], w_in_x, cq_norm[None, :], ckv_norm[None, :], wq_x,
            w_ukv.astype(BF16), gains(q_norm), gains(k_norm), cc, ss)
    in_specs = [pl.BlockSpec((tm, dm), lambda i: (i, 0))] + [full(a) for a in args[1:9]] + [
        pl.BlockSpec((tm, ROPE_DIM), lambda i: (i, 0)), pl.BlockSpec((tm, ROPE_DIM), lambda i: (i, 0))]

    def head_spec(dh):
        return pl.BlockSpec((1, nh, tm, dh), lambda i: (i // per_b, 0, i % per_b, 0))

    def head_shape(dh):
        return jax.ShapeDtypeStruct((nb, nh, seq, dh), BF16)

    return pl.pallas_call(
        _mla_proj_kernel,
        out_shape=(head_shape(NOPE_DIM), head_shape(ROPE_DIM), head_shape(NOPE_DIM),
                   head_shape(ROPE_DIM), head_shape(V_DIM)),
        grid=(n_tok // tm,),
        in_specs=in_specs,
        out_specs=(head_spec(NOPE_DIM), head_spec(ROPE_DIM), head_spec(NOPE_DIM),
                   head_spec(ROPE_DIM), head_spec(V_DIM)),
        compiler_params=_cparams("parallel"),
        name="mla_proj",
    )(*args)


def _attention_kernel(qn_ref, qr_ref, kn_ref, kr_ref, v_ref, o_ref):
    seq = qn_ref.shape[2]
    t = ATT_TQ
    nt = (((1,), (1,)), ((), ()))
    tri = (lax.broadcasted_iota(jnp.int32, (t, t), 1) <= lax.broadcasted_iota(jnp.int32, (t, t), 0))
    ks = [jnp.concatenate([kn_ref[0, 0, j * t:(j + 1) * t, :], kr_ref[0, 0, j * t:(j + 1) * t, :]], axis=-1)
          for j in range(seq // t)]
    for i in range(seq // t):
        q = jnp.concatenate([qn_ref[0, 0, i * t:(i + 1) * t, :], qr_ref[0, 0, i * t:(i + 1) * t, :]], axis=-1)
        m = jnp.full((t, 1), NEG_INF, F32)
        l = jnp.zeros((t, 1), F32)
        acc = jnp.zeros((t, V_DIM), F32)
        for j in range(i + 1):
            s = lax.dot_general(q, ks[j], nt, preferred_element_type=F32)
            if j == i:
                s = jnp.where(tri, s, NEG_INF)
            m_new = jnp.maximum(m, jnp.max(s, axis=-1, keepdims=True))
            alpha = jnp.exp2(m - m_new)
            p = jnp.exp2(s - m_new)
            l = alpha * l + jnp.sum(p, axis=-1, keepdims=True)
            acc = alpha * acc + _dot(p.astype(BF16), v_ref[0, 0, j * t:(j + 1) * t, :])
            m = m_new
        o_ref[0, i * t:(i + 1) * t, :] = (acc / l).astype(o_ref.dtype)


def _attention(qn, qr, kn, kr, v):
    nb, nh, seq, _ = qn.shape

    def spec(dh):
        return pl.BlockSpec((1, 1, seq, dh), lambda b, h: (b, h, 0, 0))

    return pl.pallas_call(
        _attention_kernel,
        out_shape=jax.ShapeDtypeStruct((nb, seq, nh * V_DIM), BF16),
        grid=(nb, nh),
        in_specs=[spec(NOPE_DIM), spec(ROPE_DIM), spec(NOPE_DIM), spec(ROPE_DIM), spec(V_DIM)],
        out_specs=pl.BlockSpec((1, seq, V_DIM), lambda b, h: (b, 0, h)),
        compiler_params=_cparams("parallel", "parallel"),
        name="mla_attention",
    )(qn, qr, kn, kr, v)


def _proj_residual_kernel(a_ref, w_ref, x_ref, o_ref):
    o_ref[...] = x_ref[...] + _dot(a_ref[...], w_ref[...])


def _proj_residual(a, w, x2):
    n_tok, dm = x2.shape
    tm = ROW_TILE
    return pl.pallas_call(
        _proj_residual_kernel,
        out_shape=jax.ShapeDtypeStruct(x2.shape, x2.dtype),
        grid=(n_tok // tm,),
        in_specs=[pl.BlockSpec((tm, a.shape[1]), lambda i: (i, 0)),
                  pl.BlockSpec(w.shape, lambda i: (0, 0)),
                  pl.BlockSpec((tm, dm), lambda i: (i, 0))],
        out_specs=pl.BlockSpec((tm, dm), lambda i: (i, 0)),
        compiler_params=_cparams("parallel"),
        name="proj_residual",
    )(a, w, x2)


def _mla_mixer(x, norm_g, positions, w_in, cq_norm, ckv_norm, w_uq, w_ukv, q_norm, k_norm, w_o):
    nb, seq, dm = x.shape
    qn, qr, kn, kr, v = _mla_proj(x, norm_g, positions, w_in, cq_norm, ckv_norm, w_uq, w_ukv,
                                  q_norm, k_norm)
    o = _attention(qn, qr, kn, kr, v)
    out = _proj_residual(o.reshape(nb * seq, -1), w_o.astype(BF16), x.reshape(nb * seq, dm))
    return out.reshape(nb, seq, dm)


def _router_kernel(x_ref, g_ref, w_ref, b_ref, hn_ref, meta_ref, gate_ref, cnt_ref, base_ref):
    tm, dm = x_ref.shape
    ne = N_EXPERTS

    @pl.when(pl.program_id(0) == 0)
    def _():
        base_ref[...] = jnp.zeros_like(base_ref)

    hn = _rms(x_ref[...], g_ref[...])
    for c in range(ROW_CHUNKS):
        hn_ref[pl.ds(c, tm, stride=ROW_CHUNKS), :] = hn[:, c * LANES:(c + 1) * LANES]
    logits = _dot(hn.astype(BF16), w_ref[...]) + b_ref[...]
    lane = lax.broadcasted_iota(jnp.int32, logits.shape, 1)
    big = jnp.int32(1 << 20)

    def first_argmax(v):
        m = jnp.max(v, axis=-1, keepdims=True)
        idx = jnp.min(jnp.where(v == m, lane, big), axis=-1, keepdims=True)
        return m, idx

    is_g = (lane >= ne) & (lane < ne + N_GROUPS)
    gl = jnp.where(is_g, logits, NEG_INF)
    g_max, g_idx = first_argmax(gl)
    g_sel = g_idx - ne
    p_g = 1.0 / jnp.sum(jnp.where(is_g, jnp.exp(gl - g_max), 0.0), axis=-1, keepdims=True)
    in_grp = (lane >= g_sel * EXPERTS_PER_GROUP) & (lane < (g_sel + 1) * EXPERTS_PER_GROUP)
    el = jnp.where(in_grp, logits, NEG_INF)
    v0, e0 = first_argmax(el)
    v1, e1 = first_argmax(jnp.where(lane == e0, NEG_INF, el))
    t = jnp.exp(v1 - v0)
    gate0 = p_g / (1.0 + t)
    gate1 = p_g * t / (1.0 + t)

    oh0 = (lane == e0).astype(F32)
    oh1 = (lane == e1).astype(F32)
    both = oh0 + oh1
    r_i = lax.broadcasted_iota(jnp.int32, (tm, tm), 0)
    c_i = lax.broadcasted_iota(jnp.int32, (tm, tm), 1)
    strict_lower = (c_i < r_i).astype(BF16)
    before = _dot(strict_lower, both.astype(BF16)) + base_ref[0:1, :]
    rank0 = jnp.sum(oh0 * before, axis=-1, keepdims=True)
    rank1 = jnp.sum(oh1 * before, axis=-1, keepdims=True)
    base_ref[0:1, :] = base_ref[0:1, :] + jnp.sum(both, axis=0, keepdims=True)
    cnt_ref[...] = jnp.broadcast_to(base_ref[0:1, :], cnt_ref.shape).astype(jnp.int32)

    meta = jnp.where(lane == 0, e0, jnp.where(lane == 1, e1, jnp.where(
        lane == 2, rank0.astype(jnp.int32), jnp.where(lane == 3, rank1.astype(jnp.int32), 0))))
    meta_ref[...] = meta
    gate_ref[...] = jnp.where(lane == 0, gate0, jnp.where(lane == 1, gate1, 0.0))


def _router(x2, norm_g, w_group, b_group, w_expert, b_expert):
    n_tok, dm = x2.shape
    tm = ROW_TILE
    pad = LANES - N_EXPERTS - N_GROUPS
    w = jnp.concatenate([w_expert, w_group, jnp.zeros((dm, pad), F32)], axis=1).astype(BF16)
    b = jnp.concatenate([b_expert, b_group, jnp.zeros((pad,), F32)])[None, :]
    return pl.pallas_call(
        _router_kernel,
        out_shape=(jax.ShapeDtypeStruct((n_tok * ROW_CHUNKS, LANES), F32),
                   jax.ShapeDtypeStruct((n_tok, LANES), jnp.int32),
                   jax.ShapeDtypeStruct((n_tok, LANES), F32),
                   jax.ShapeDtypeStruct((SUBLANES, LANES), jnp.int32)),
        grid=(n_tok // tm,),
        in_specs=[pl.BlockSpec((tm, dm), lambda i: (i, 0)),
                  pl.BlockSpec((1, dm), lambda i: (0, 0)),
                  pl.BlockSpec((dm, LANES), lambda i: (0, 0)),
                  pl.BlockSpec((1, LANES), lambda i: (0, 0))],
        out_specs=(pl.BlockSpec((tm * ROW_CHUNKS, LANES), lambda i: (i, 0)),
                   pl.BlockSpec((tm, LANES), lambda i: (i, 0)),
                   pl.BlockSpec((tm, LANES), lambda i: (i, 0)),
                   pl.BlockSpec((SUBLANES, LANES), lambda i: (0, 0))),
        scratch_shapes=[pltpu.VMEM((SUBLANES, LANES), F32)],
        compiler_params=_cparams("arbitrary"),
        name="moe_router",
    )(x2, norm_g[None, :], w, b)


def _rows_from_tiles(ref, n, base=0):
    return jnp.concatenate([ref[pl.ds(base + c, n, stride=ROW_CHUNKS), :] for c in range(ROW_CHUNKS)], axis=-1)


def _expert_kernel(be_ref, bs_ref, bv_ref, nu_ref, tsrc_ref, tdst_ref,
                   hn_ref, w1_ref, w3_ref, w2_ref, y_ref,
                   xbuf, ybuf, sem_in, sem_out, w13_s, w2_s):
    b = pl.program_id(0)
    nu = nu_ref[0]
    slot = b % 2
    bm = xbuf.shape[1] // ROW_CHUNKS
    de = w1_ref.shape[3]

    def for_rows(n, fn):
        groups = lax.shift_right_logical(n, DMA_UNROLL.bit_length() - 1)

        def grp(g, c):
            for u in range(DMA_UNROLL):
                fn(g * DMA_UNROLL + u, u % 2)
            return c

        lax.fori_loop(0, groups, grp, 0)

        def rem(j, c):
            fn(j, 0)
            return c

        lax.fori_loop(groups * DMA_UNROLL, n, rem, 0)

    def gather_copy(blk, sl, j):
        src = pl.multiple_of(tsrc_ref[bs_ref[blk] + j], ROW_CHUNKS)
        dst = pl.multiple_of(j * ROW_CHUNKS, ROW_CHUNKS)
        return pltpu.make_async_copy(hn_ref.at[pl.ds(src, ROW_CHUNKS)],
                                     xbuf.at[sl, pl.ds(dst, ROW_CHUNKS)], sem_in.at[sl])

    def scatter_copy(blk, sl, j):
        src = pl.multiple_of(j * ROW_CHUNKS, ROW_CHUNKS)
        dst = pl.multiple_of(tdst_ref[bs_ref[blk] + j], ROW_CHUNKS)
        return pltpu.make_async_copy(ybuf.at[sl, pl.ds(src, ROW_CHUNKS)],
                                     y_ref.at[pl.ds(dst, ROW_CHUNKS)], sem_out.at[sl])

    def start_gather(blk, sl):
        for_rows(bv_ref[blk], lambda j, q: gather_copy(blk, sl, j).start(priority=q))

    def wait_gather(blk, sl):
        for_rows(bv_ref[blk], lambda j, q: gather_copy(blk, sl, j).wait())

    def start_scatter(blk, sl):
        for_rows(bv_ref[blk], lambda j, q: scatter_copy(blk, sl, j).start(priority=q))

    def wait_scatter(blk, sl):
        for_rows(bv_ref[blk], lambda j, q: scatter_copy(blk, sl, j).wait())

    @pl.when(b == 0)
    def _():
        xbuf[...] = jnp.zeros_like(xbuf)
        start_gather(0, 0)

    @pl.when(b + 1 < nu)
    def _():
        start_gather(b + 1, 1 - slot)

    @pl.when(b < nu)
    def _():
        wait_gather(b, slot)
        changed = jnp.logical_or(b == 0, be_ref[b] != be_ref[jnp.maximum(b - 1, 0)])

        @pl.when(changed)
        def _():
            w13_s[:, :de] = w1_ref[0, 0].astype(BF16)
            w13_s[:, de:] = w3_ref[0, 0].astype(BF16)
            w2_s[...] = w2_ref[0, 0].astype(BF16)

        x = _rows_from_tiles(xbuf.at[slot], bm).astype(BF16)
        h = _dot(x, w13_s[...])
        a = jax.nn.silu(h[:, :de]) * h[:, de:]
        y = _dot(a.astype(BF16), w2_s[...])

        @pl.when(b >= 2)
        def _():
            wait_scatter(b - 2, slot)

        yb = ybuf.at[slot]
        for c in range(ROW_CHUNKS):
            yb[pl.ds(c, bm, stride=ROW_CHUNKS), :] = y[:, c * LANES:(c + 1) * LANES]
        start_scatter(b, slot)

        @pl.when(b == nu - 1)
        def _():
            @pl.when(b >= 1)
            def _():
                wait_scatter(b - 1, 1 - slot)
            wait_scatter(b, slot)


def _experts(hn2, blk_expert, blk_start, blk_valid, n_used, tab_src, tab_dst, w1, w3, w2, layer):
    bm = MOE_BM
    n_blocks = blk_expert.shape[0]
    _, _, dm, de = w1.shape
    n_assign = tab_src.shape[0] - bm

    def w_map(b, be, bs, bv, nu, ts, td):
        return (layer, be[b], 0, 0)

    grid_spec = pltpu.PrefetchScalarGridSpec(
        num_scalar_prefetch=6,
        grid=(n_blocks,),
        in_specs=[pl.BlockSpec(memory_space=pl.ANY),
                  pl.BlockSpec((1, 1, dm, de), w_map),
                  pl.BlockSpec((1, 1, dm, de), w_map),
                  pl.BlockSpec((1, 1, de, dm), w_map)],
        out_specs=pl.BlockSpec(memory_space=pl.ANY),
        scratch_shapes=[pltpu.VMEM((2, bm * ROW_CHUNKS, LANES), F32),
                        pltpu.VMEM((2, bm * ROW_CHUNKS, LANES), F32),
                        pltpu.SemaphoreType.DMA((2,)),
                        pltpu.SemaphoreType.DMA((2,)),
                        pltpu.VMEM((dm, 2 * de), BF16),
                        pltpu.VMEM((de, dm), BF16)],
    )
    return pl.pallas_call(
        _expert_kernel,
        out_shape=jax.ShapeDtypeStruct((n_assign * ROW_CHUNKS, LANES), F32),
        grid_spec=grid_spec,
        compiler_params=_cparams("arbitrary"),
        name="moe_experts",
    )(blk_expert, blk_start, blk_valid, n_used, tab_src, tab_dst, hn2, w1, w3, w2)


def _combine_kernel(y0_ref, y1_ref, gate_ref, x_ref, o_ref):
    tm = x_ref.shape[0]
    g0 = gate_ref[:, 0:1]
    g1 = gate_ref[:, 1:2]
    for c in range(ROW_CHUNKS):
        y0 = y0_ref[pl.ds(c, tm, stride=ROW_CHUNKS), :]
        y1 = y1_ref[pl.ds(c, tm, stride=ROW_CHUNKS), :]
        o_ref[:, c * LANES:(c + 1) * LANES] = x_ref[:, c * LANES:(c + 1) * LANES] + (g0 * y0 + g1 * y1)


def _combine(y2, gates, x2):
    n_tok, dm = x2.shape
    tm = ROW_TILE
    n_steps = n_tok // tm
    return pl.pallas_call(
        _combine_kernel,
        out_shape=jax.ShapeDtypeStruct(x2.shape, x2.dtype),
        grid=(n_steps,),
        in_specs=[pl.BlockSpec((tm * ROW_CHUNKS, LANES), lambda i: (i, 0)),
                  pl.BlockSpec((tm * ROW_CHUNKS, LANES), lambda i: (i + n_steps, 0)),
                  pl.BlockSpec((tm, LANES), lambda i: (i, 0)),
                  pl.BlockSpec((tm, dm), lambda i: (i, 0))],
        out_specs=pl.BlockSpec((tm, dm), lambda i: (i, 0)),
        compiler_params=_cparams("parallel"),
        name="moe_combine",
    )(y2, y2, gates, x2)


def _moe(x, norm_g, w_group, b_group, w_expert, b_expert, w1, w3, w2, layer):
    nb, seq, dm = x.shape
    n_tok = nb * seq
    n_assign = n_tok * TOP_K
    x2 = x.reshape(n_tok, dm)
    hn2, meta, gates, counts = _router(x2, norm_g, w_group, b_group, w_expert, b_expert)
    bm = MOE_BM
    n_blocks = n_assign // bm + N_EXPERTS
    counts = counts[0, :N_EXPERTS]
    starts = jnp.cumsum(counts) - counts
    eid = meta[:, 0:TOP_K]
    e_iota = jnp.arange(N_EXPERTS, dtype=jnp.int32)
    seg_start = jnp.sum(jnp.where(eid[:, :, None] == e_iota, starts, 0), axis=-1)
    pos = (seg_start + meta[:, TOP_K:2 * TOP_K]).reshape(-1)
    order = jnp.zeros((n_assign + bm,), jnp.int32).at[pos].set(
        jnp.arange(n_assign, dtype=jnp.int32), unique_indices=True)
    tok = order >> 1
    tab_src = tok * ROW_CHUNKS
    tab_dst = ((order & 1) * n_tok + tok) * ROW_CHUNKS
    nblk = (counts + bm - 1) // bm
    blk_end = jnp.cumsum(nblk)
    b_iota = jnp.arange(n_blocks, dtype=jnp.int32)
    blk_expert = jnp.minimum(jnp.sum((b_iota[:, None] >= blk_end[None, :]).astype(jnp.int32), axis=1),
                             N_EXPERTS - 1)
    j = b_iota - (blk_end - nblk)[blk_expert]
    blk_start = (starts[blk_expert] + j * bm).astype(jnp.int32)
    blk_valid = jnp.clip(counts[blk_expert] - j * bm, 0, bm).astype(jnp.int32)
    blk_start = jnp.where(blk_valid > 0, blk_start, 0)
    n_used = blk_end[-1].astype(jnp.int32).reshape(1)

    y2 = _experts(hn2, blk_expert, blk_start, blk_valid, n_used, tab_src, tab_dst, w1, w3, w2, layer)
    out = _combine(y2, gates, x2)
    return out.reshape(nb, seq, dm)


def kernel(x, positions, ev_norm, ev_w_in, s5_lambda_re, s5_lambda_im, s5_log_dt, s5_b_re, s5_b_im,
           s5_c_re, s5_c_im, s5_d, s5_glu_w, s5_glu_b, lru_conv_w, lru_conv_b, lru_wa, lru_ba, lru_wi,
           lru_bi, lru_lambda, ev_w_out, od_norm, mla_w_in, mla_cq_norm, mla_ckv_norm, mla_w_uq,
           mla_w_ukv, mla_q_norm, mla_k_norm, mla_w_o, ffn_norm, moe_w_group, moe_b_group,
           moe_w_expert, moe_b_expert, moe_w1, moe_w3, moe_w2):
    depth = ffn_norm.shape[0]
    for layer in range(depth):
        j = layer // 2
        if layer % 2 == 0:
            x = _even_mixer(x, ev_norm[j], ev_w_in[j], s5_lambda_re[j], s5_lambda_im[j], s5_log_dt[j],
                            s5_b_re[j], s5_b_im[j], s5_c_re[j], s5_c_im[j], s5_d[j], s5_glu_w[j],
                            s5_glu_b[j], lru_conv_w[j], lru_conv_b[j], lru_wa[j], lru_ba[j], lru_wi[j],
                            lru_bi[j], lru_lambda[j], ev_w_out[j])
        else:
            x = _mla_mixer(x, od_norm[j], positions, mla_w_in[j], mla_cq_norm[j], mla_ckv_norm[j],
                           mla_w_uq[j], mla_w_ukv[j], mla_q_norm[j], mla_k_norm[j], mla_w_o[j])
        x = _moe(x, ffn_norm[layer], moe_w_group[layer], moe_b_group[layer], moe_w_expert[layer],
                 moe_b_expert[layer], moe_w1, moe_w3, moe_w2, layer)
    return x
```

```python
import functools
import math

import jax
import jax.numpy as jnp
from jax import lax
from jax.experimental import pallas as pl
from jax.experimental.pallas import tpu as pltpu

F32 = jnp.float32
BF16 = jnp.bfloat16

S5_GROUP_CH = 16
S5_STATE = 64
LRU_BLOCKS = 8
CONV_WIDTH = 4
RG_C = 8.0
MLA_HEADS = 8
Q_LORA = 384
KV_LORA = 256
NOPE_DIM = 128
ROPE_DIM = 64
V_DIM = 128
QK_DIM = NOPE_DIM + ROPE_DIM
ROPE_THETA = 10000.0
N_GROUPS = 8
EXPERTS_PER_GROUP = 8
N_EXPERTS = N_GROUPS * EXPERTS_PER_GROUP
TOP_K = 2
RMS_EPS = 1e-6
NEG_INF = -1e30

SUBLANES = 8
LANES = 128
ROW_CHUNKS = 8
VMEM_LIMIT = 48 * 1024 * 1024

EVEN_TL = 64
ROW_TILE = 512
MOE_BM = 256
ATT_TQ = 512
GATHER_TM = 256
DMA_UNROLL = 8


def _cparams(*sem):
    return pltpu.CompilerParams(dimension_semantics=tuple(sem), vmem_limit_bytes=VMEM_LIMIT)


def _rms(x, g):
    ms = jnp.mean(x * x, axis=-1, keepdims=True)
    return x * lax.rsqrt(ms + RMS_EPS) * g


def _dot(a, b):
    return jnp.dot(a, b, preferred_element_type=F32)


def _s5_discretize_kernel(lre_ref, lim_ref, ldt_ref, bre_ref, bim_ref,
                          lbre_ref, lbim_ref, bbre_ref, bbim_ref):
    lr = jnp.minimum(lre_ref[...], -1e-4)
    li = lim_ref[...]
    dt = jnp.exp(ldt_ref[...])
    mag = jnp.exp(lr * dt)
    lb_re = mag * jnp.cos(li * dt)
    lb_im = mag * jnp.sin(li * dt)
    den = lr * lr + li * li
    num_re = lb_re - 1.0
    f_re = (num_re * lr + lb_im * li) / den
    f_im = (lb_im * lr - num_re * li) / den
    lbre_ref[...] = lb_re
    lbim_ref[...] = lb_im
    br = bre_ref[...]
    bi = bim_ref[...]
    bbre_ref[...] = f_re[:, None, :] * br - f_im[:, None, :] * bi
    bbim_ref[...] = f_re[:, None, :] * bi + f_im[:, None, :] * br


def _s5_discretize(lam_re, lam_im, log_dt, b_re, b_im):
    g, p = lam_re.shape
    h = b_re.shape[-1]
    b_re_t = jnp.swapaxes(b_re, 1, 2)
    b_im_t = jnp.swapaxes(b_im, 1, 2)
    return pl.pallas_call(
        _s5_discretize_kernel,
        out_shape=(jax.ShapeDtypeStruct((g, p), F32), jax.ShapeDtypeStruct((g, p), F32),
                   jax.ShapeDtypeStruct((g, h, p), F32), jax.ShapeDtypeStruct((g, h, p), F32)),
        name="s5_discretize",
    )(lam_re, lam_im, log_dt[:, None], b_re_t, b_im_t)


def _even_inproj_kernel(x_ref, g_ref, w_ref, o_ref, h_ref):
    nb, tl, _ = x_ref.shape
    nc = h_ref.shape[0]
    g = g_ref[...]
    for b in range(nb):
        h = _rms(x_ref[b], g)
        for c in range(nc):
            h_ref[c, pl.ds(b, tl, stride=nb), :] = h[:, c * LANES:(c + 1) * LANES]
    h_all = jnp.concatenate([h_ref[c] for c in range(nc)], axis=-1)
    o_ref[...] = _dot(h_all.astype(BF16), w_ref[...]).astype(o_ref.dtype)


def _even_inproj(x, g, w):
    nb, seq, d = x.shape
    n_out = w.shape[1]
    tl = EVEN_TL
    return pl.pallas_call(
        _even_inproj_kernel,
        out_shape=jax.ShapeDtypeStruct((seq * nb, n_out), BF16),
        grid=(seq // tl,),
        in_specs=[pl.BlockSpec((nb, tl, d), lambda i: (0, i, 0)),
                  pl.BlockSpec((1, d), lambda i: (0, 0)),
                  pl.BlockSpec((d, n_out), lambda i: (0, 0))],
        out_specs=pl.BlockSpec((tl * nb, n_out), lambda i: (i, 0)),
        scratch_shapes=[pltpu.VMEM((d // LANES, tl * nb, LANES), F32)],
        compiler_params=_cparams("parallel"),
        name="even_inproj",
    )(x, g, w)


def _s5_kernel(u_ref, bm_ref, lre_ref, lim_ref, cm_ref, d_ref, gw_ref, gb_ref, o_ref,
               bu_ref, sb_ref, st_ref, *, nb):
    rows, width = u_ref.shape
    n_half = bm_ref.shape[0]
    kin = width // n_half
    ncol = bm_ref.shape[2]
    nre = ncol // 2
    chunk = 512
    steps = rows // nb

    @pl.when(pl.program_id(0) == 0)
    def _():
        st_ref[...] = jnp.zeros_like(st_ref)

    u = u_ref[...]
    for j in range(n_half):
        bu_ref[:, j * ncol:(j + 1) * ncol] = _dot(u[:, j * kin:(j + 1) * kin], bm_ref[j])

    for j in range(n_half):
        for c in range(nre // chunk):
            cr = j * ncol + c * chunk
            ci = cr + nre
            lc = j * nre + c * chunk
            lr = lre_ref[:, lc:lc + chunk]
            li = lim_ref[:, lc:lc + chunk]

            def body(i, carry, cr=cr, ci=ci, lr=lr, li=li):
                sr, si = carry
                r0 = pl.multiple_of(i * (2 * nb), 2 * nb)
                sr1 = lr * sr - li * si + bu_ref[pl.ds(r0, nb), cr:cr + chunk]
                si1 = lr * si + li * sr + bu_ref[pl.ds(r0, nb), ci:ci + chunk]
                sr2 = lr * sr1 - li * si1 + bu_ref[pl.ds(r0 + nb, nb), cr:cr + chunk]
                si2 = lr * si1 + li * sr1 + bu_ref[pl.ds(r0 + nb, nb), ci:ci + chunk]
                sb_ref[pl.ds(r0, 2 * nb), cr:cr + chunk] = jnp.concatenate([sr1, sr2], 0).astype(BF16)
                sb_ref[pl.ds(r0, 2 * nb), ci:ci + chunk] = jnp.concatenate([si1, si2], 0).astype(BF16)
                return sr2, si2

            sr, si = lax.fori_loop(0, steps // 2, body,
                                   (st_ref[:, cr:cr + chunk], st_ref[:, ci:ci + chunk]))
            st_ref[:, cr:cr + chunk] = sr
            st_ref[:, ci:ci + chunk] = si

    ys = [_dot(sb_ref[:, j * ncol:(j + 1) * ncol], cm_ref[j]) for j in range(n_half)]
    y = jnp.concatenate(ys, axis=-1) + d_ref[...] * u.astype(F32)
    y = jax.nn.gelu(y)
    z = _dot(y.astype(BF16), gw_ref[...]) + gb_ref[...]
    o_ref[...] = (y * jax.nn.sigmoid(z)).astype(o_ref.dtype)


def _s5_mixer(proj, nb, bm, lre, lim, cm, d, glu_w, glu_b):
    rows_total = proj.shape[0]
    width = d.shape[1]
    rows = EVEN_TL * nb
    n_half, _, ncol = bm.shape
    kern = functools.partial(_s5_kernel, nb=nb)
    return pl.pallas_call(
        kern,
        out_shape=jax.ShapeDtypeStruct((rows_total, width), BF16),
        grid=(rows_total // rows,),
        in_specs=[pl.BlockSpec((rows, width), lambda i: (i, 0)),
                  pl.BlockSpec(bm.shape, lambda i: (0, 0, 0)),
                  pl.BlockSpec(lre.shape, lambda i: (0, 0)),
                  pl.BlockSpec(lim.shape, lambda i: (0, 0)),
                  pl.BlockSpec(cm.shape, lambda i: (0, 0, 0)),
                  pl.BlockSpec((1, width), lambda i: (0, 0)),
                  pl.BlockSpec(glu_w.shape, lambda i: (0, 0)),
                  pl.BlockSpec((1, width), lambda i: (0, 0))],
        out_specs=pl.BlockSpec((rows, width), lambda i: (i, 0)),
        scratch_shapes=[pltpu.VMEM((rows, n_half * ncol), F32),
                        pltpu.VMEM((rows, n_half * ncol), BF16),
                        pltpu.VMEM((nb, n_half * ncol), F32)],
        compiler_params=_cparams("arbitrary"),
        name="s5_mixer",
    )(proj, bm, lre, lim, cm, d, glu_w, glu_b)


def _lru_kernel(x_ref, gate_ref, cw_ref, cb_ref, wa_ref, ba_ref, wi_ref, bi_ref, lam_ref, o_ref,
                xp_ref, a_ref, b_ref, h_ref, *, nb):
    rows, width = x_ref.shape
    halo = (CONV_WIDTH - 1) * nb
    steps = rows // nb

    @pl.when(pl.program_id(0) == 0)
    def _():
        xp_ref[0:halo, :] = jnp.zeros((halo, width), F32)
        h_ref[...] = jnp.zeros_like(h_ref)

    xp_ref[halo:halo + rows, :] = x_ref[...].astype(F32)
    xf = cb_ref[...] + cw_ref[0:1, :] * xp_ref[0:rows, :]
    for k in range(1, CONV_WIDTH):
        xf = xf + cw_ref[k:k + 1, :] * xp_ref[k * nb:k * nb + rows, :]
    xp_ref[0:halo, :] = xp_ref[rows:rows + halo, :]

    xb = xf.astype(BF16)
    nblk = wa_ref.shape[0]
    kb = width // nblk
    ga = jnp.concatenate([_dot(xb[:, j * kb:(j + 1) * kb], wa_ref[j]) for j in range(nblk)], -1)
    gi = jnp.concatenate([_dot(xb[:, j * kb:(j + 1) * kb], wi_ref[j]) for j in range(nblk)], -1)
    r = jax.nn.sigmoid(ga + ba_ref[...])
    ig = jax.nn.sigmoid(gi + bi_ref[...])
    log_a = RG_C * r * jax.nn.log_sigmoid(lam_ref[...])
    a_ref[...] = jnp.exp(log_a)
    th = jnp.tanh(log_a)
    b_ref[...] = jnp.sqrt(-2.0 * th / (1.0 - th)) * (ig * xf)

    def body(t, h):
        r0 = pl.multiple_of(t * nb, nb)
        h = a_ref[pl.ds(r0, nb), :] * h + b_ref[pl.ds(r0, nb), :]
        b_ref[pl.ds(r0, nb), :] = h
        return h

    h_ref[...] = lax.fori_loop(0, steps, body, h_ref[...])
    o_ref[...] = (b_ref[...] * jax.nn.gelu(gate_ref[...].astype(F32))).astype(o_ref.dtype)


def _lru_mixer(proj, nb, conv_w, conv_b, wa, ba, wi, bi, lam):
    rows_total = proj.shape[0]
    width = conv_w.shape[1]
    rows = EVEN_TL * nb
    halo = (CONV_WIDTH - 1) * nb
    kern = functools.partial(_lru_kernel, nb=nb)
    vec = pl.BlockSpec((1, width), lambda i: (0, 0))
    return pl.pallas_call(
        kern,
        out_shape=jax.ShapeDtypeStruct((rows_total, width), BF16),
        grid=(rows_total // rows,),
        in_specs=[pl.BlockSpec((rows, width), lambda i: (i, 1)),
                  pl.BlockSpec((rows, width), lambda i: (i, 2)),
                  pl.BlockSpec(conv_w.shape, lambda i: (0, 0)),
                  vec,
                  pl.BlockSpec(wa.shape, lambda i: (0, 0, 0)),
                  vec,
                  pl.BlockSpec(wi.shape, lambda i: (0, 0, 0)),
                  vec, vec],
        out_specs=pl.BlockSpec((rows, width), lambda i: (i, 0)),
        scratch_shapes=[pltpu.VMEM((rows + halo, width), F32),
                        pltpu.VMEM((rows, width), F32),
                        pltpu.VMEM((rows, width), F32),
                        pltpu.VMEM((nb, width), F32)],
        compiler_params=_cparams("arbitrary"),
        name="lru_mixer",
    )(proj, proj, conv_w, conv_b, wa, ba, wi, bi, lam)


def _even_outproj_kernel(ya_ref, yb_ref, wa_ref, wb_ref, x_ref, o_ref, y_ref):
    nb, tl, _ = x_ref.shape
    nc = y_ref.shape[0]
    y = _dot(ya_ref[...], wa_ref[...]) + _dot(yb_ref[...], wb_ref[...])
    for c in range(nc):
        y_ref[c] = y[:, c * LANES:(c + 1) * LANES]
    for b in range(nb):
        for c in range(nc):
            o_ref[b, :, c * LANES:(c + 1) * LANES] = (x_ref[b, :, c * LANES:(c + 1) * LANES]
                                                     + y_ref[c, pl.ds(b, tl, stride=nb), :])


def _even_outproj(ya, yb, w_a, w_b, x):
    nb, seq, d = x.shape
    tl = EVEN_TL
    rows = tl * nb
    wd = ya.shape[1]
    return pl.pallas_call(
        _even_outproj_kernel,
        out_shape=jax.ShapeDtypeStruct(x.shape, x.dtype),
        grid=(seq // tl,),
        in_specs=[pl.BlockSpec((rows, wd), lambda i: (i, 0)),
                  pl.BlockSpec((rows, wd), lambda i: (i, 0)),
                  pl.BlockSpec(w_a.shape, lambda i: (0, 0)),
                  pl.BlockSpec(w_b.shape, lambda i: (0, 0)),
                  pl.BlockSpec((nb, tl, d), lambda i: (0, i, 0))],
        out_specs=pl.BlockSpec((nb, tl, d), lambda i: (0, i, 0)),
        scratch_shapes=[pltpu.VMEM((d // LANES, rows, LANES), F32)],
        compiler_params=_cparams("parallel"),
        name="even_outproj",
    )(ya, yb, w_a, w_b, x)


def _block_diag(blocks):
    n, r, c = blocks.shape
    eye = jnp.eye(n, dtype=blocks.dtype)
    return jnp.einsum('nrc,nm->nrmc', blocks, eye).reshape(n * r, n * c)


def _even_mixer(x, norm_g, w_in, lam_re, lam_im, log_dt, b_re, b_im, c_re, c_im, d, glu_w, glu_b,
                conv_w, conv_b, wa, ba, wi, bi, lam, w_out):
    nb, seq, dm = x.shape
    g, p = lam_re.shape
    hch = b_re.shape[-1]
    width = g * hch
    n_half = 2
    gh = g // n_half

    proj = _even_inproj(x, norm_g[None, :], w_in.astype(BF16))

    lb_re, lb_im, bb_re, bb_im = _s5_discretize(lam_re, lam_im, log_dt, b_re, b_im)
    bb_re = bb_re.reshape(n_half, gh, hch, p)
    bb_im = bb_im.reshape(n_half, gh, hch, p)
    bm = jnp.stack([jnp.concatenate([_block_diag(bb_re[j]), _block_diag(bb_im[j])], axis=1)
                    for j in range(n_half)]).astype(BF16)
    c_re_t = jnp.swapaxes(c_re, 1, 2).reshape(n_half, gh, p, hch)
    c_im_t = jnp.swapaxes(c_im, 1, 2).reshape(n_half, gh, p, hch)
    cm = jnp.stack([jnp.concatenate([_block_diag(c_re_t[j]), -_block_diag(c_im_t[j])], axis=0)
                    for j in range(n_half)]).astype(BF16)
    lre = jnp.broadcast_to(lb_re.reshape(1, g * p), (nb, g * p))
    lim = jnp.broadcast_to(lb_im.reshape(1, g * p), (nb, g * p))
    ya = _s5_mixer(proj, nb, bm, lre, lim, cm, d.reshape(1, width), glu_w.astype(BF16),
                   glu_b[None, :])

    lw = conv_w.shape[1]
    nblk = 2
    per = LRU_BLOCKS // nblk
    wa_bd = jnp.stack([_block_diag(wa[j * per:(j + 1) * per]) for j in range(nblk)]).astype(BF16)
    wi_bd = jnp.stack([_block_diag(wi[j * per:(j + 1) * per]) for j in range(nblk)]).astype(BF16)
    yb = _lru_mixer(proj, nb, conv_w, conv_b[None, :], wa_bd, ba[None, :], wi_bd, bi[None, :],
                    lam[None, :])

    w_out_b = w_out.astype(BF16)
    return _even_outproj(ya, yb, w_out_b[:width], w_out_b[width:], x)


def _rope_table_kernel(pos_ref, inv_ref, cos_ref, sin_ref):
    ang = pos_ref[...].astype(F32) * inv_ref[...]
    cos_ref[...] = jnp.cos(ang)
    sin_ref[...] = jnp.sin(ang)


def _rope_tables(positions):
    nb, seq = positions.shape
    half = ROPE_DIM // 2
    per_row = LANES // half
    n_tok = nb * seq
    inv_freq = 1.0 / (ROPE_THETA ** (jnp.arange(0, ROPE_DIM, 2, dtype=F32) / ROPE_DIM))
    pos_rep = jnp.repeat(positions.reshape(n_tok // per_row, per_row), half, axis=1)
    inv_rep = jnp.tile(inv_freq, per_row)[None, :]
    cos, sin = pl.pallas_call(
        _rope_table_kernel,
        out_shape=(jax.ShapeDtypeStruct(pos_rep.shape, F32), jax.ShapeDtypeStruct(pos_rep.shape, F32)),
        name="rope_table",
    )(pos_rep, inv_rep)
    cos = cos.reshape(n_tok, half)
    sin = sin.reshape(n_tok, half)
    cc = jnp.concatenate([cos, cos], axis=-1)
    ss = jnp.concatenate([-sin, sin], axis=-1)
    return cc, ss


def _mla_proj_kernel(x_ref, g_ref, win_ref, cqn_ref, ckvn_ref, wq_ref, wkv_ref, qg_ref, kg_ref,
                     cc_ref, ss_ref, qn_ref, qr_ref, kn_ref, kr_ref, v_ref):
    h = _rms(x_ref[...], g_ref[...])
    proj = _dot(h.astype(BF16), win_ref[...])
    c_q = _rms(proj[:, :Q_LORA], cqn_ref[...])
    c_kv = _rms(proj[:, Q_LORA:Q_LORA + KV_LORA], ckvn_ref[...])
    o = Q_LORA + KV_LORA
    k_r = proj[:, o:o + ROPE_DIM]
    k_sw = proj[:, o + ROPE_DIM:o + 2 * ROPE_DIM]
    cc = cc_ref[...]
    ss = ss_ref[...]
    qg = qg_ref[...]
    kg = kg_ref[...]
    q = _dot(c_q.astype(BF16), wq_ref[...])
    kv = _dot(c_kv.astype(BF16), wkv_ref[...])
    nh = MLA_HEADS
    scale = math.log2(math.e) / math.sqrt(QK_DIM)
    k_rot = k_r * kg[1:2, :ROPE_DIM] * cc + k_sw * kg[2:3, :ROPE_DIM] * ss
    k_r_ss = jnp.sum(k_r * k_r, axis=-1, keepdims=True)
    for hd in range(nh):
        q_n = q[:, hd * NOPE_DIM:(hd + 1) * NOPE_DIM]
        ro = nh * NOPE_DIM + hd * ROPE_DIM
        q_r = q[:, ro:ro + ROPE_DIM]
        q_sw = q[:, ro + nh * ROPE_DIM:ro + nh * ROPE_DIM + ROPE_DIM]
        ms = (jnp.sum(q_n * q_n, axis=-1, keepdims=True)
              + jnp.sum(q_r * q_r, axis=-1, keepdims=True)) / QK_DIM
        rs = lax.rsqrt(ms + RMS_EPS) * scale
        qn_ref[0, hd] = (q_n * rs * qg[0:1, :]).astype(qn_ref.dtype)
        qr_ref[0, hd] = ((q_r * qg[1:2, :ROPE_DIM] * cc + q_sw * qg[2:3, :ROPE_DIM] * ss) * rs
                         ).astype(qr_ref.dtype)
        ko = hd * (NOPE_DIM + V_DIM)
        k_n = kv[:, ko:ko + NOPE_DIM]
        ms_k = (jnp.sum(k_n * k_n, axis=-1, keepdims=True) + k_r_ss) / QK_DIM
        rs_k = lax.rsqrt(ms_k + RMS_EPS)
        kn_ref[0, hd] = (k_n * rs_k * kg[0:1, :]).astype(kn_ref.dtype)
        kr_ref[0, hd] = (k_rot * rs_k).astype(kr_ref.dtype)
        v_ref[0, hd] = kv[:, ko + NOPE_DIM:ko + NOPE_DIM + V_DIM].astype(v_ref.dtype)


def _swap_halves(a, axis=-1):
    h = a.shape[axis] // 2
    lo = lax.slice_in_dim(a, 0, h, axis=axis)
    hi = lax.slice_in_dim(a, h, 2 * h, axis=axis)
    return jnp.concatenate([hi, lo], axis=axis)


def _mla_proj(x, norm_g, positions, w_in, cq_norm, ckv_norm, w_uq, w_ukv, q_norm, k_norm):
    nb, seq, dm = x.shape
    nh = MLA_HEADS
    tm = ROW_TILE
    n_tok = nb * seq
    cc, ss = _rope_tables(positions)
    o = Q_LORA + KV_LORA
    w_kr = w_in[:, o:o + ROPE_DIM]
    w_in_x = jnp.concatenate([w_in, _swap_halves(w_kr)], axis=1).astype(BF16)
    wq = w_uq.reshape(Q_LORA, nh, QK_DIM)
    wq_n = wq[:, :, :NOPE_DIM].reshape(Q_LORA, nh * NOPE_DIM)
    wq_r = wq[:, :, NOPE_DIM:]
    wq_x = jnp.concatenate([wq_n, wq_r.reshape(Q_LORA, nh * ROPE_DIM),
                            _swap_halves(wq_r).reshape(Q_LORA, nh * ROPE_DIM)], axis=1).astype(BF16)

    def gains(gv):
        pad = jnp.zeros((NOPE_DIM - ROPE_DIM,), F32)
        r = gv[NOPE_DIM:]
        return jnp.stack([gv[:NOPE_DIM], jnp.concatenate([r, pad]),
                          jnp.concatenate([_swap_halves(r), pad])])

    x2 = x.reshape(n_tok, dm)
    per_b = seq // tm
    full = lambda a: pl.BlockSpec(a.shape, lambda i: (0,) * a.ndim)
    args = (x2, norm_g[None, :], w_in_x, cq_norm[None, :], ckv_norm[None, :], wq_x,
            w_ukv.astype(BF16), gains(q_norm), gains(k_norm), cc, ss)
    in_specs = [pl.BlockSpec((tm, dm), lambda i: (i, 0))] + [full(a) for a in args[1:9]] + [
        pl.BlockSpec((tm, ROPE_DIM), lambda i: (i, 0)), pl.BlockSpec((tm, ROPE_DIM), lambda i: (i, 0))]

    def head_spec(dh):
        return pl.BlockSpec((1, nh, tm, dh), lambda i: (i // per_b, 0, i % per_b, 0))

    def head_shape(dh):
        return jax.ShapeDtypeStruct((nb, nh, seq, dh), BF16)

    return pl.pallas_call(
        _mla_proj_kernel,
        out_shape=(head_shape(NOPE_DIM), head_shape(ROPE_DIM), head_shape(NOPE_DIM),
                   head_shape(ROPE_DIM), head_shape(V_DIM)),
        grid=(n_tok // tm,),
        in_specs=in_specs,
        out_specs=(head_spec(NOPE_DIM), head_spec(ROPE_DIM), head_spec(NOPE_DIM),
                   head_spec(ROPE_DIM), head_spec(V_DIM)),
        compiler_params=_cparams("parallel"),
        name="mla_proj",
    )(*args)


def _attention_kernel(qn_ref, qr_ref, kn_ref, kr_ref, v_ref, o_ref):
    seq = qn_ref.shape[2]
    t = ATT_TQ
    nt = (((1,), (1,)), ((), ()))
    tri = (lax.broadcasted_iota(jnp.int32, (t, t), 1) <= lax.broadcasted_iota(jnp.int32, (t, t), 0))
    ks = [jnp.concatenate([kn_ref[0, 0, j * t:(j + 1) * t, :], kr_ref[0, 0, j * t:(j + 1) * t, :]], axis=-1)
          for j in range(seq // t)]
    for i in range(seq // t):
        q = jnp.concatenate([qn_ref[0, 0, i * t:(i + 1) * t, :], qr_ref[0, 0, i * t:(i + 1) * t, :]], axis=-1)
        m = jnp.full((t, 1), NEG_INF, F32)
        l = jnp.zeros((t, 1), F32)
        acc = jnp.zeros((t, V_DIM), F32)
        for j in range(i + 1):
            s = lax.dot_general(q, ks[j], nt, preferred_element_type=F32)
            if j == i:
                s = jnp.where(tri, s, NEG_INF)
            m_new = jnp.maximum(m, jnp.max(s, axis=-1, keepdims=True))
            alpha = jnp.exp2(m - m_new)
            p = jnp.exp2(s - m_new)
            l = alpha * l + jnp.sum(p, axis=-1, keepdims=True)
            acc = alpha * acc + _dot(p.astype(BF16), v_ref[0, 0, j * t:(j + 1) * t, :])
            m = m_new
        o_ref[0, i * t:(i + 1) * t, :] = (acc / l).astype(o_ref.dtype)


def _attention(qn, qr, kn, kr, v):
    nb, nh, seq, _ = qn.shape

    def spec(dh):
        return pl.BlockSpec((1, 1, seq, dh), lambda b, h: (b, h, 0, 0))

    return pl.pallas_call(
        _attention_kernel,
        out_shape=jax.ShapeDtypeStruct((nb, seq, nh * V_DIM), BF16),
        grid=(nb, nh),
        in_specs=[spec(NOPE_DIM), spec(ROPE_DIM), spec(NOPE_DIM), spec(ROPE_DIM), spec(V_DIM)],
        out_specs=pl.BlockSpec((1, seq, V_DIM), lambda b, h: (b, 0, h)),
        compiler_params=_cparams("parallel", "parallel"),
        name="mla_attention",
    )(qn, qr, kn, kr, v)


def _proj_residual_kernel(a_ref, w_ref, x_ref, o_ref):
    o_ref[...] = x_ref[...] + _dot(a_ref[...], w_ref[...])


def _proj_residual(a, w, x2):
    n_tok, dm = x2.shape
    tm = ROW_TILE
    return pl.pallas_call(
        _proj_residual_kernel,
        out_shape=jax.ShapeDtypeStruct(x2.shape, x2.dtype),
        grid=(n_tok // tm,),
        in_specs=[pl.BlockSpec((tm, a.shape[1]), lambda i: (i, 0)),
                  pl.BlockSpec(w.shape, lambda i: (0, 0)),
                  pl.BlockSpec((tm, dm), lambda i: (i, 0))],
        out_specs=pl.BlockSpec((tm, dm), lambda i: (i, 0)),
        compiler_params=_cparams("parallel"),
        name="proj_residual",
    )(a, w, x2)


def _mla_mixer(x, norm_g, positions, w_in, cq_norm, ckv_norm, w_uq, w_ukv, q_norm, k_norm, w_o):
    nb, seq, dm = x.shape
    qn, qr, kn, kr, v = _mla_proj(x, norm_g, positions, w_in, cq_norm, ckv_norm, w_uq, w_ukv,
                                  q_norm, k_norm)
    o = _attention(qn, qr, kn, kr, v)
    out = _proj_residual(o.reshape(nb * seq, -1), w_o.astype(BF16), x.reshape(nb * seq, dm))
    return out.reshape(nb, seq, dm)


def _router_kernel(x_ref, g_ref, w_ref, b_ref, hn_ref, meta_ref, gate_ref, cnt_ref, base_ref):
    tm, dm = x_ref.shape
    ne = N_EXPERTS

    @pl.when(pl.program_id(0) == 0)
    def _():
        base_ref[...] = jnp.zeros_like(base_ref)

    hn = _rms(x_ref[...], g_ref[...])
    for c in range(ROW_CHUNKS):
        hn_ref[pl.ds(c, tm, stride=ROW_CHUNKS), :] = hn[:, c * LANES:(c + 1) * LANES]
    logits = _dot(hn.astype(BF16), w_ref[...]) + b_ref[...]
    lane = lax.broadcasted_iota(jnp.int32, logits.shape, 1)
    big = jnp.int32(1 << 20)

    def first_argmax(v):
        m = jnp.max(v, axis=-1, keepdims=True)
        idx = jnp.min(jnp.where(v == m, lane, big), axis=-1, keepdims=True)
        return m, idx

    is_g = (lane >= ne) & (lane < ne + N_GROUPS)
    gl = jnp.where(is_g, logits, NEG_INF)
    g_max, g_idx = first_argmax(gl)
    g_sel = g_idx - ne
    p_g = 1.0 / jnp.sum(jnp.where(is_g, jnp.exp(gl - g_max), 0.0), axis=-1, keepdims=True)
    in_grp = (lane >= g_sel * EXPERTS_PER_GROUP) & (lane < (g_sel + 1) * EXPERTS_PER_GROUP)
    el = jnp.where(in_grp, logits, NEG_INF)
    v0, e0 = first_argmax(el)
    v1, e1 = first_argmax(jnp.where(lane == e0, NEG_INF, el))
    t = jnp.exp(v1 - v0)
    gate0 = p_g / (1.0 + t)
    gate1 = p_g * t / (1.0 + t)

    oh0 = (lane == e0).astype(F32)
    oh1 = (lane == e1).astype(F32)
    both = oh0 + oh1
    r_i = lax.broadcasted_iota(jnp.int32, (tm, tm), 0)
    c_i = lax.broadcasted_iota(jnp.int32, (tm, tm), 1)
    strict_lower = (c_i < r_i).astype(BF16)
    before = _dot(strict_lower, both.astype(BF16)) + base_ref[0:1, :]
    rank0 = jnp.sum(oh0 * before, axis=-1, keepdims=True)
    rank1 = jnp.sum(oh1 * before, axis=-1, keepdims=True)
    base_ref[0:1, :] = base_ref[0:1, :] + jnp.sum(both, axis=0, keepdims=True)
    cnt_ref[...] = jnp.broadcast_to(base_ref[0:1, :], cnt_ref.shape).astype(jnp.int32)

    meta = jnp.where(lane == 0, e0, jnp.where(lane == 1, e1, jnp.where(
        lane == 2, rank0.astype(jnp.int32), jnp.where(lane == 3, rank1.astype(jnp.int32), 0))))
    meta_ref[...] = meta
    gate_ref[...] = jnp.where(lane == 0, gate0, jnp.where(lane == 1, gate1, 0.0))


def _router(x2, norm_g, w_group, b_group, w_expert, b_expert):
    n_tok, dm = x2.shape
    tm = ROW_TILE
    pad = LANES - N_EXPERTS - N_GROUPS
    w = jnp.concatenate([w_expert, w_group, jnp.zeros((dm, pad), F32)], axis=1).astype(BF16)
    b = jnp.concatenate([b_expert, b_group, jnp.zeros((pad,), F32)])[None, :]
    return pl.pallas_call(
        _router_kernel,
        out_shape=(jax.ShapeDtypeStruct((n_tok * ROW_CHUNKS, LANES), F32),
                   jax.ShapeDtypeStruct((n_tok, LANES), jnp.int32),
                   jax.ShapeDtypeStruct((n_tok, LANES), F32),
                   jax.ShapeDtypeStruct((SUBLANES, LANES), jnp.int32)),
        grid=(n_tok // tm,),
        in_specs=[pl.BlockSpec((tm, dm), lambda i: (i, 0)),
                  pl.BlockSpec((1, dm), lambda i: (0, 0)),
                  pl.BlockSpec((dm, LANES), lambda i: (0, 0)),
                  pl.BlockSpec((1, LANES), lambda i: (0, 0))],
        out_specs=(pl.BlockSpec((tm * ROW_CHUNKS, LANES), lambda i: (i, 0)),
                   pl.BlockSpec((tm, LANES), lambda i: (i, 0)),
                   pl.BlockSpec((tm, LANES), lambda i: (i, 0)),
                   pl.BlockSpec((SUBLANES, LANES), lambda i: (0, 0))),
        scratch_shapes=[pltpu.VMEM((SUBLANES, LANES), F32)],
        compiler_params=_cparams("arbitrary"),
        name="moe_router",
    )(x2, norm_g[None, :], w, b)


def _dispatch_kernel(dest_ref, hn_ref, xd_ref, zero_ref, sem, zsem):
    tm = hn_ref.shape[0] // ROW_CHUNKS
    n_assign = pl.num_programs(0) * tm * TOP_K

    def tail_copy():
        return pltpu.make_async_copy(zero_ref, xd_ref.at[pl.ds(n_assign * ROW_CHUNKS, zero_ref.shape[0])], zsem)

    @pl.when(pl.program_id(0) == 0)
    def _():
        zero_ref[...] = jnp.zeros_like(zero_ref)
        tail_copy().start()

    def row_copy(t, k):
        src = pl.multiple_of(t * ROW_CHUNKS, ROW_CHUNKS)
        dst = pl.multiple_of(dest_ref[0, 0, 2 * t + k], ROW_CHUNKS)
        return pltpu.make_async_copy(hn_ref.at[pl.ds(src, ROW_CHUNKS)], xd_ref.at[pl.ds(dst, ROW_CHUNKS)], sem)

    def issue(i, c):
        for u in range(DMA_UNROLL):
            for k in range(TOP_K):
                row_copy(i * DMA_UNROLL + u, k).start(priority=k)
        return c

    lax.fori_loop(0, tm // DMA_UNROLL, issue, 0)

    def drain(i, c):
        for u in range(DMA_UNROLL):
            for k in range(TOP_K):
                row_copy(i * DMA_UNROLL + u, k).wait()
        return c

    lax.fori_loop(0, tm // DMA_UNROLL, drain, 0)

    @pl.when(pl.program_id(0) == 0)
    def _():
        tail_copy().wait()


def _dispatch(hn2, dest_off):
    tm = GATHER_TM
    n_tok = hn2.shape[0] // ROW_CHUNKS
    dest3 = dest_off.reshape(n_tok // tm, 1, 2 * tm)
    n_rows = n_tok * TOP_K + MOE_BM
    return pl.pallas_call(
        _dispatch_kernel,
        out_shape=jax.ShapeDtypeStruct((n_rows * ROW_CHUNKS, LANES), hn2.dtype),
        grid=(n_tok // tm,),
        in_specs=[pl.BlockSpec((1, 1, 2 * tm), lambda i: (i, 0, 0), memory_space=pltpu.SMEM),
                  pl.BlockSpec((tm * ROW_CHUNKS, LANES), lambda i: (i, 0))],
        out_specs=pl.BlockSpec(memory_space=pl.ANY),
        scratch_shapes=[pltpu.VMEM((MOE_BM * ROW_CHUNKS, LANES), hn2.dtype),
                        pltpu.SemaphoreType.DMA, pltpu.SemaphoreType.DMA],
        compiler_params=_cparams("arbitrary"),
        name="moe_dispatch",
    )(dest3, hn2)


def _rows_from_tiles(ref, n, base=0):
    return jnp.concatenate([ref[pl.ds(base + c, n, stride=ROW_CHUNKS), :] for c in range(ROW_CHUNKS)], axis=-1)


def _expert_kernel(be_ref, bs_ref, nu_ref, xd_ref, w1_ref, w3_ref, w2_ref, yd_ref,
                   xbuf, ybuf, sem_in, sem_out, w13_s, w2_s):
    b = pl.program_id(0)
    nu = nu_ref[0]
    slot = b % 2
    n_rows = xbuf.shape[1]
    bm = n_rows // ROW_CHUNKS
    de = w1_ref.shape[3]

    def in_copy(blk, sl):
        src = pl.multiple_of(bs_ref[blk] * ROW_CHUNKS, ROW_CHUNKS)
        return pltpu.make_async_copy(xd_ref.at[pl.ds(src, n_rows)], xbuf.at[sl], sem_in.at[sl])

    def out_copy(blk, sl):
        dst = pl.multiple_of(bs_ref[blk] * ROW_CHUNKS, ROW_CHUNKS)
        return pltpu.make_async_copy(ybuf.at[sl], yd_ref.at[pl.ds(dst, n_rows)], sem_out.at[sl])

    @pl.when(b == 0)
    def _():
        in_copy(0, 0).start()

    @pl.when(b + 1 < nu)
    def _():
        in_copy(b + 1, 1 - slot).start()

    @pl.when(b < nu)
    def _():
        in_copy(b, slot).wait()
        changed = jnp.logical_or(b == 0, be_ref[b] != be_ref[jnp.maximum(b - 1, 0)])

        @pl.when(changed)
        def _():
            w13_s[:, :de] = w1_ref[0, 0].astype(BF16)
            w13_s[:, de:] = w3_ref[0, 0].astype(BF16)
            w2_s[...] = w2_ref[0, 0].astype(BF16)

        x = _rows_from_tiles(xbuf.at[slot], bm).astype(BF16)
        h = _dot(x, w13_s[...])
        a = jax.nn.silu(h[:, :de]) * h[:, de:]
        y = _dot(a.astype(BF16), w2_s[...])
        yb = ybuf.at[slot]
        for c in range(ROW_CHUNKS):
            yb[pl.ds(c, bm, stride=ROW_CHUNKS), :] = y[:, c * LANES:(c + 1) * LANES]

        @pl.when(b >= 1)
        def _():
            out_copy(b - 1, 1 - slot).wait()
        out_copy(b, slot).start()

        @pl.when(b == nu - 1)
        def _():
            out_copy(b, slot).wait()
            tail = (pl.num_programs(0) - N_EXPERTS) * n_rows
            ybuf[1 - slot] = jnp.zeros((n_rows, LANES), F32)
            fill = pltpu.make_async_copy(ybuf.at[1 - slot], yd_ref.at[pl.ds(tail, n_rows)], sem_out.at[1 - slot])
            fill.start()
            fill.wait()


def _experts(xd, blk_expert, blk_start, n_used, w1, w3, w2, layer):
    bm = MOE_BM
    n_blocks = blk_expert.shape[0]
    _, _, dm, de = w1.shape

    def w_map(b, be, bs, nu):
        return (layer, be[b], 0, 0)

    grid_spec = pltpu.PrefetchScalarGridSpec(
        num_scalar_prefetch=3,
        grid=(n_blocks,),
        in_specs=[pl.BlockSpec(memory_space=pl.ANY),
                  pl.BlockSpec((1, 1, dm, de), w_map),
                  pl.BlockSpec((1, 1, dm, de), w_map),
                  pl.BlockSpec((1, 1, de, dm), w_map)],
        out_specs=pl.BlockSpec(memory_space=pl.ANY),
        scratch_shapes=[pltpu.VMEM((2, bm * ROW_CHUNKS, LANES), F32),
                        pltpu.VMEM((2, bm * ROW_CHUNKS, LANES), F32),
                        pltpu.SemaphoreType.DMA((2,)),
                        pltpu.SemaphoreType.DMA((2,)),
                        pltpu.VMEM((dm, 2 * de), BF16),
                        pltpu.VMEM((de, dm), BF16)],
    )
    return pl.pallas_call(
        _expert_kernel,
        out_shape=jax.ShapeDtypeStruct(xd.shape, F32),
        grid_spec=grid_spec,
        compiler_params=_cparams("arbitrary"),
        name="moe_experts",
    )(blk_expert, blk_start, n_used, xd, w1, w3, w2)


def _combine_kernel(dest_ref, dest_next_ref, yd_ref, gate_ref, x_ref, o_ref, buf_ref, sem):
    tm, dm = x_ref.shape
    i = pl.program_id(0)
    n_steps = pl.num_programs(0)
    slot = i % 2
    k_rows = tm * ROW_CHUNKS

    def row_copy(d_ref, sl, t, k):
        src = pl.multiple_of(d_ref[0, 0, 2 * t + k], ROW_CHUNKS)
        dst = pl.multiple_of(k * k_rows + t * ROW_CHUNKS, ROW_CHUNKS)
        return pltpu.make_async_copy(yd_ref.at[pl.ds(src, ROW_CHUNKS)],
                                     buf_ref.at[sl, pl.ds(dst, ROW_CHUNKS)], sem.at[sl])

    def issue_all(d_ref, sl):
        def issue(j, c):
            for u in range(DMA_UNROLL):
                for k in range(TOP_K):
                    row_copy(d_ref, sl, j * DMA_UNROLL + u, k).start(priority=k)
            return c
        lax.fori_loop(0, tm // DMA_UNROLL, issue, 0)

    @pl.when(i == 0)
    def _():
        issue_all(dest_ref, slot)

    @pl.when(i + 1 < n_steps)
    def _():
        issue_all(dest_next_ref, 1 - slot)

    def drain(j, c):
        for u in range(DMA_UNROLL):
            for k in range(TOP_K):
                row_copy(dest_ref, slot, j * DMA_UNROLL + u, k).wait()
        return c

    lax.fori_loop(0, tm // DMA_UNROLL, drain, 0)

    g0 = gate_ref[:, 0:1]
    g1 = gate_ref[:, 1:2]
    buf = buf_ref.at[slot]
    for c in range(ROW_CHUNKS):
        y0 = buf[pl.ds(c, tm, stride=ROW_CHUNKS), :]
        y1 = buf[pl.ds(k_rows + c, tm, stride=ROW_CHUNKS), :]
        o_ref[:, c * LANES:(c + 1) * LANES] = x_ref[:, c * LANES:(c + 1) * LANES] + (g0 * y0 + g1 * y1)


def _combine(yd, dest_off, gates, x2):
    n_tok, dm = x2.shape
    tm = GATHER_TM
    n_steps = n_tok // tm
    dest3 = dest_off.reshape(n_steps, 1, 2 * tm)
    return pl.pallas_call(
        _combine_kernel,
        out_shape=jax.ShapeDtypeStruct(x2.shape, x2.dtype),
        grid=(n_steps,),
        in_specs=[pl.BlockSpec((1, 1, 2 * tm), lambda i: (i, 0, 0), memory_space=pltpu.SMEM),
                  pl.BlockSpec((1, 1, 2 * tm), lambda i: (jnp.minimum(i + 1, n_steps - 1), 0, 0),
                               memory_space=pltpu.SMEM),
                  pl.BlockSpec(memory_space=pl.ANY),
                  pl.BlockSpec((tm, LANES), lambda i: (i, 0)),
                  pl.BlockSpec((tm, dm), lambda i: (i, 0))],
        out_specs=pl.BlockSpec((tm, dm), lambda i: (i, 0)),
        scratch_shapes=[pltpu.VMEM((2, TOP_K * tm * ROW_CHUNKS, LANES), F32),
                        pltpu.SemaphoreType.DMA((2,))],
        compiler_params=_cparams("arbitrary"),
        name="moe_combine",
    )(dest3, dest3, yd, gates, x2)


def _moe(x, norm_g, w_group, b_group, w_expert, b_expert, w1, w3, w2, layer):
    nb, seq, dm = x.shape
    n_tok = nb * seq
    n_assign = n_tok * TOP_K
    x2 = x.reshape(n_tok, dm)
    hn2, meta, gates, counts = _router(x2, norm_g, w_group, b_group, w_expert, b_expert)
    bm = MOE_BM
    n_blocks = n_assign // bm + N_EXPERTS
    counts = counts[0, :N_EXPERTS]
    starts = jnp.cumsum(counts) - counts
    eid = meta[:, 0:TOP_K]
    e_iota = jnp.arange(N_EXPERTS, dtype=jnp.int32)
    seg_start = jnp.sum(jnp.where(eid[:, :, None] == e_iota, starts, 0), axis=-1)
    dest_off = ((seg_start + meta[:, TOP_K:2 * TOP_K]) * ROW_CHUNKS).astype(jnp.int32).reshape(-1)
    nblk = (counts + bm - 1) // bm
    blk_end = jnp.cumsum(nblk)
    b_iota = jnp.arange(n_blocks, dtype=jnp.int32)
    blk_expert = jnp.minimum(jnp.sum((b_iota[:, None] >= blk_end[None, :]).astype(jnp.int32), axis=1),
                             N_EXPERTS - 1)
    j = b_iota - (blk_end - nblk)[blk_expert]
    n_used = blk_end[-1].astype(jnp.int32).reshape(1)
    blk_start = jnp.where(b_iota < n_used[0], starts[blk_expert] + j * bm, 0).astype(jnp.int32)

    xd = _dispatch(hn2, dest_off)
    yd = _experts(xd, blk_expert, blk_start, n_used, w1, w3, w2, layer)
    out = _combine(yd, dest_off, gates, x2)
    return out.reshape(nb, seq, dm)


def kernel(x, positions, ev_norm, ev_w_in, s5_lambda_re, s5_lambda_im, s5_log_dt, s5_b_re, s5_b_im,
           s5_c_re, s5_c_im, s5_d, s5_glu_w, s5_glu_b, lru_conv_w, lru_conv_b, lru_wa, lru_ba, lru_wi,
           lru_bi, lru_lambda, ev_w_out, od_norm, mla_w_in, mla_cq_norm, mla_ckv_norm, mla_w_uq,
           mla_w_ukv, mla_q_norm, mla_k_norm, mla_w_o, ffn_norm, moe_w_group, moe_b_group,
           moe_w_expert, moe_b_expert, moe_w1, moe_w3, moe_w2):
    depth = ffn_norm.shape[0]
    for layer in range(depth):
        j = layer // 2
        if layer % 2 == 0:
            x = _even_mixer(x, ev_norm[j], ev_w_in[j], s5_lambda_re[j], s5_lambda_im[j], s5_log_dt[j],
                            s5_b_re[j], s5_b_im[j], s5_c_re[j], s5_c_im[j], s5_d[j], s5_glu_w[j],
                            s5_glu_b[j], lru_conv_w[j], lru_conv_b[j], lru_wa[j], lru_ba[j], lru_wi[j],
                            lru_bi[j], lru_lambda[j], ev_w_out[j])
        else:
            x = _mla_mixer(x, od_norm[j], positions, mla_w_in[j], mla_cq_norm[j], mla_ckv_norm[j],
                           mla_w_uq[j], mla_w_ukv[j], mla_q_norm[j], mla_k_norm[j], mla_w_o[j])
        x = _moe(x, ffn_norm[layer], moe_w_group[layer], moe_b_group[layer], moe_w_expert[layer],
                 moe_b_expert[layer], moe_w1, moe_w3, moe_w2, layer)
    return x
```

```python
import functools
import math

import jax
import jax.numpy as jnp
from jax import lax
from jax.experimental import pallas as pl
from jax.experimental.pallas import tpu as pltpu

F32 = jnp.float32
BF16 = jnp.bfloat16

S5_GROUP_CH = 16
S5_STATE = 64
LRU_BLOCKS = 8
CONV_WIDTH = 4
RG_C = 8.0
MLA_HEADS = 8
Q_LORA = 384
KV_LORA = 256
NOPE_DIM = 128
ROPE_DIM = 64
V_DIM = 128
QK_DIM = NOPE_DIM + ROPE_DIM
ROPE_THETA = 10000.0
N_GROUPS = 8
EXPERTS_PER_GROUP = 8
N_EXPERTS = N_GROUPS * EXPERTS_PER_GROUP
TOP_K = 2
RMS_EPS = 1e-6
NEG_INF = -1e30

SUBLANES = 8
LANES = 128
ROW_CHUNKS = 8
VMEM_LIMIT = 48 * 1024 * 1024

EVEN_TL = 64
ROW_TILE = 512
MOE_BM = 256
ATT_TQ = 512
GATHER_TM = 256
DMA_UNROLL = 8


def _cparams(*sem):
    return pltpu.CompilerParams(dimension_semantics=tuple(sem), vmem_limit_bytes=VMEM_LIMIT)


def _rms(x, g):
    ms = jnp.mean(x * x, axis=-1, keepdims=True)
    return x * lax.rsqrt(ms + RMS_EPS) * g


def _dot(a, b):
    return jnp.dot(a, b, preferred_element_type=F32)


def _s5_discretize_kernel(lre_ref, lim_ref, ldt_ref, bre_ref, bim_ref,
                          lbre_ref, lbim_ref, bbre_ref, bbim_ref):
    lr = jnp.minimum(lre_ref[...], -1e-4)
    li = lim_ref[...]
    dt = jnp.exp(ldt_ref[...])
    mag = jnp.exp(lr * dt)
    lb_re = mag * jnp.cos(li * dt)
    lb_im = mag * jnp.sin(li * dt)
    den = lr * lr + li * li
    num_re = lb_re - 1.0
    f_re = (num_re * lr + lb_im * li) / den
    f_im = (lb_im * lr - num_re * li) / den
    lbre_ref[...] = lb_re
    lbim_ref[...] = lb_im
    br = bre_ref[...]
    bi = bim_ref[...]
    bbre_ref[...] = f_re[:, None, :] * br - f_im[:, None, :] * bi
    bbim_ref[...] = f_re[:, None, :] * bi + f_im[:, None, :] * br


def _s5_discretize(lam_re, lam_im, log_dt, b_re, b_im):
    g, p = lam_re.shape
    h = b_re.shape[-1]
    b_re_t = jnp.swapaxes(b_re, 1, 2)
    b_im_t = jnp.swapaxes(b_im, 1, 2)
    return pl.pallas_call(
        _s5_discretize_kernel,
        out_shape=(jax.ShapeDtypeStruct((g, p), F32), jax.ShapeDtypeStruct((g, p), F32),
                   jax.ShapeDtypeStruct((g, h, p), F32), jax.ShapeDtypeStruct((g, h, p), F32)),
        name="s5_discretize",
    )(lam_re, lam_im, log_dt[:, None], b_re_t, b_im_t)


def _even_inproj_kernel(x_ref, g_ref, w_ref, o_ref, h_ref):
    nb, tl, _ = x_ref.shape
    nc = h_ref.shape[0]
    g = g_ref[...]
    for b in range(nb):
        h = _rms(x_ref[b], g)
        for c in range(nc):
            h_ref[c, pl.ds(b, tl, stride=nb), :] = h[:, c * LANES:(c + 1) * LANES]
    h_all = jnp.concatenate([h_ref[c] for c in range(nc)], axis=-1)
    o_ref[...] = _dot(h_all.astype(BF16), w_ref[...]).astype(o_ref.dtype)


def _even_inproj(x, g, w):
    nb, seq, d = x.shape
    n_out = w.shape[1]
    tl = EVEN_TL
    return pl.pallas_call(
        _even_inproj_kernel,
        out_shape=jax.ShapeDtypeStruct((seq * nb, n_out), BF16),
        grid=(seq // tl,),
        in_specs=[pl.BlockSpec((nb, tl, d), lambda i: (0, i, 0)),
                  pl.BlockSpec((1, d), lambda i: (0, 0)),
                  pl.BlockSpec((d, n_out), lambda i: (0, 0))],
        out_specs=pl.BlockSpec((tl * nb, n_out), lambda i: (i, 0)),
        scratch_shapes=[pltpu.VMEM((d // LANES, tl * nb, LANES), F32)],
        compiler_params=_cparams("parallel"),
        name="even_inproj",
    )(x, g, w)


def _s5_kernel(u_ref, bm_ref, lre_ref, lim_ref, cm_ref, d_ref, gw_ref, gb_ref, o_ref,
               bu_ref, sb_ref, st_ref, *, nb):
    rows, width = u_ref.shape
    n_half = bm_ref.shape[0]
    kin = width // n_half
    ncol = bm_ref.shape[2]
    nre = ncol // 2
    chunk = 512
    steps = rows // nb

    @pl.when(pl.program_id(0) == 0)
    def _():
        st_ref[...] = jnp.zeros_like(st_ref)

    u = u_ref[...]
    for j in range(n_half):
        bu_ref[:, j * ncol:(j + 1) * ncol] = _dot(u[:, j * kin:(j + 1) * kin], bm_ref[j])

    for j in range(n_half):
        for c in range(nre // chunk):
            cr = j * ncol + c * chunk
            ci = cr + nre
            lc = j * nre + c * chunk
            lr = lre_ref[:, lc:lc + chunk]
            li = lim_ref[:, lc:lc + chunk]

            def body(i, carry, cr=cr, ci=ci, lr=lr, li=li):
                sr, si = carry
                r0 = pl.multiple_of(i * (2 * nb), 2 * nb)
                sr1 = lr * sr - li * si + bu_ref[pl.ds(r0, nb), cr:cr + chunk]
                si1 = lr * si + li * sr + bu_ref[pl.ds(r0, nb), ci:ci + chunk]
                sr2 = lr * sr1 - li * si1 + bu_ref[pl.ds(r0 + nb, nb), cr:cr + chunk]
                si2 = lr * si1 + li * sr1 + bu_ref[pl.ds(r0 + nb, nb), ci:ci + chunk]
                sb_ref[pl.ds(r0, 2 * nb), cr:cr + chunk] = jnp.concatenate([sr1, sr2], 0).astype(BF16)
                sb_ref[pl.ds(r0, 2 * nb), ci:ci + chunk] = jnp.concatenate([si1, si2], 0).astype(BF16)
                return sr2, si2

            sr, si = lax.fori_loop(0, steps // 2, body,
                                   (st_ref[:, cr:cr + chunk], st_ref[:, ci:ci + chunk]))
            st_ref[:, cr:cr + chunk] = sr
            st_ref[:, ci:ci + chunk] = si

    ys = [_dot(sb_ref[:, j * ncol:(j + 1) * ncol], cm_ref[j]) for j in range(n_half)]
    y = jnp.concatenate(ys, axis=-1) + d_ref[...] * u.astype(F32)
    y = jax.nn.gelu(y)
    z = _dot(y.astype(BF16), gw_ref[...]) + gb_ref[...]
    o_ref[...] = (y * jax.nn.sigmoid(z)).astype(o_ref.dtype)


def _s5_mixer(proj, nb, bm, lre, lim, cm, d, glu_w, glu_b):
    rows_total = proj.shape[0]
    width = d.shape[1]
    rows = EVEN_TL * nb
    n_half, _, ncol = bm.shape
    kern = functools.partial(_s5_kernel, nb=nb)
    return pl.pallas_call(
        kern,
        out_shape=jax.ShapeDtypeStruct((rows_total, width), BF16),
        grid=(rows_total // rows,),
        in_specs=[pl.BlockSpec((rows, width), lambda i: (i, 0)),
                  pl.BlockSpec(bm.shape, lambda i: (0, 0, 0)),
                  pl.BlockSpec(lre.shape, lambda i: (0, 0)),
                  pl.BlockSpec(lim.shape, lambda i: (0, 0)),
                  pl.BlockSpec(cm.shape, lambda i: (0, 0, 0)),
                  pl.BlockSpec((1, width), lambda i: (0, 0)),
                  pl.BlockSpec(glu_w.shape, lambda i: (0, 0)),
                  pl.BlockSpec((1, width), lambda i: (0, 0))],
        out_specs=pl.BlockSpec((rows, width), lambda i: (i, 0)),
        scratch_shapes=[pltpu.VMEM((rows, n_half * ncol), F32),
                        pltpu.VMEM((rows, n_half * ncol), BF16),
                        pltpu.VMEM((nb, n_half * ncol), F32)],
        compiler_params=_cparams("arbitrary"),
        name="s5_mixer",
    )(proj, bm, lre, lim, cm, d, glu_w, glu_b)


def _lru_kernel(x_ref, gate_ref, cw_ref, cb_ref, wa_ref, ba_ref, wi_ref, bi_ref, lam_ref, o_ref,
                xp_ref, a_ref, b_ref, h_ref, *, nb):
    rows, width = x_ref.shape
    halo = (CONV_WIDTH - 1) * nb
    steps = rows // nb

    @pl.when(pl.program_id(0) == 0)
    def _():
        xp_ref[0:halo, :] = jnp.zeros((halo, width), F32)
        h_ref[...] = jnp.zeros_like(h_ref)

    xp_ref[halo:halo + rows, :] = x_ref[...].astype(F32)
    xf = cb_ref[...] + cw_ref[0:1, :] * xp_ref[0:rows, :]
    for k in range(1, CONV_WIDTH):
        xf = xf + cw_ref[k:k + 1, :] * xp_ref[k * nb:k * nb + rows, :]
    xp_ref[0:halo, :] = xp_ref[rows:rows + halo, :]

    xb = xf.astype(BF16)
    nblk = wa_ref.shape[0]
    kb = width // nblk
    ga = jnp.concatenate([_dot(xb[:, j * kb:(j + 1) * kb], wa_ref[j]) for j in range(nblk)], -1)
    gi = jnp.concatenate([_dot(xb[:, j * kb:(j + 1) * kb], wi_ref[j]) for j in range(nblk)], -1)
    r = jax.nn.sigmoid(ga + ba_ref[...])
    ig = jax.nn.sigmoid(gi + bi_ref[...])
    log_a = RG_C * r * jax.nn.log_sigmoid(lam_ref[...])
    a_ref[...] = jnp.exp(log_a)
    th = jnp.tanh(log_a)
    b_ref[...] = jnp.sqrt(-2.0 * th / (1.0 - th)) * (ig * xf)

    def body(t, h):
        r0 = pl.multiple_of(t * nb, nb)
        h = a_ref[pl.ds(r0, nb), :] * h + b_ref[pl.ds(r0, nb), :]
        b_ref[pl.ds(r0, nb), :] = h
        return h

    h_ref[...] = lax.fori_loop(0, steps, body, h_ref[...])
    o_ref[...] = (b_ref[...] * jax.nn.gelu(gate_ref[...].astype(F32))).astype(o_ref.dtype)


def _lru_mixer(proj, nb, conv_w, conv_b, wa, ba, wi, bi, lam):
    rows_total = proj.shape[0]
    width = conv_w.shape[1]
    rows = EVEN_TL * nb
    halo = (CONV_WIDTH - 1) * nb
    kern = functools.partial(_lru_kernel, nb=nb)
    vec = pl.BlockSpec((1, width), lambda i: (0, 0))
    return pl.pallas_call(
        kern,
        out_shape=jax.ShapeDtypeStruct((rows_total, width), BF16),
        grid=(rows_total // rows,),
        in_specs=[pl.BlockSpec((rows, width), lambda i: (i, 1)),
                  pl.BlockSpec((rows, width), lambda i: (i, 2)),
                  pl.BlockSpec(conv_w.shape, lambda i: (0, 0)),
                  vec,
                  pl.BlockSpec(wa.shape, lambda i: (0, 0, 0)),
                  vec,
                  pl.BlockSpec(wi.shape, lambda i: (0, 0, 0)),
                  vec, vec],
        out_specs=pl.BlockSpec((rows, width), lambda i: (i, 0)),
        scratch_shapes=[pltpu.VMEM((rows + halo, width), F32),
                        pltpu.VMEM((rows, width), F32),
                        pltpu.VMEM((rows, width), F32),
                        pltpu.VMEM((nb, width), F32)],
        compiler_params=_cparams("arbitrary"),
        name="lru_mixer",
    )(proj, proj, conv_w, conv_b, wa, ba, wi, bi, lam)


def _even_outproj_kernel(ya_ref, yb_ref, wa_ref, wb_ref, x_ref, o_ref, y_ref):
    nb, tl, _ = x_ref.shape
    nc = y_ref.shape[0]
    y = _dot(ya_ref[...], wa_ref[...]) + _dot(yb_ref[...], wb_ref[...])
    for c in range(nc):
        y_ref[c] = y[:, c * LANES:(c + 1) * LANES]
    for b in range(nb):
        for c in range(nc):
            o_ref[b, :, c * LANES:(c + 1) * LANES] = (x_ref[b, :, c * LANES:(c + 1) * LANES]
                                                     + y_ref[c, pl.ds(b, tl, stride=nb), :])


def _even_outproj(ya, yb, w_a, w_b, x):
    nb, seq, d = x.shape
    tl = EVEN_TL
    rows = tl * nb
    wd = ya.shape[1]
    return pl.pallas_call(
        _even_outproj_kernel,
        out_shape=jax.ShapeDtypeStruct(x.shape, x.dtype),
        grid=(seq // tl,),
        in_specs=[pl.BlockSpec((rows, wd), lambda i: (i, 0)),
                  pl.BlockSpec((rows, wd), lambda i: (i, 0)),
                  pl.BlockSpec(w_a.shape, lambda i: (0, 0)),
                  pl.BlockSpec(w_b.shape, lambda i: (0, 0)),
                  pl.BlockSpec((nb, tl, d), lambda i: (0, i, 0))],
        out_specs=pl.BlockSpec((nb, tl, d), lambda i: (0, i, 0)),
        scratch_shapes=[pltpu.VMEM((d // LANES, rows, LANES), F32)],
        compiler_params=_cparams("parallel"),
        name="even_outproj",
    )(ya, yb, w_a, w_b, x)


def _block_diag(blocks):
    n, r, c = blocks.shape
    eye = jnp.eye(n, dtype=blocks.dtype)
    return jnp.einsum('nrc,nm->nrmc', blocks, eye).reshape(n * r, n * c)


def _even_mixer(x, norm_g, w_in, lam_re, lam_im, log_dt, b_re, b_im, c_re, c_im, d, glu_w, glu_b,
                conv_w, conv_b, wa, ba, wi, bi, lam, w_out):
    nb, seq, dm = x.shape
    g, p = lam_re.shape
    hch = b_re.shape[-1]
    width = g * hch
    n_half = 2
    gh = g // n_half

    proj = _even_inproj(x, norm_g[None, :], w_in.astype(BF16))

    lb_re, lb_im, bb_re, bb_im = _s5_discretize(lam_re, lam_im, log_dt, b_re, b_im)
    bb_re = bb_re.reshape(n_half, gh, hch, p)
    bb_im = bb_im.reshape(n_half, gh, hch, p)
    bm = jnp.stack([jnp.concatenate([_block_diag(bb_re[j]), _block_diag(bb_im[j])], axis=1)
                    for j in range(n_half)]).astype(BF16)
    c_re_t = jnp.swapaxes(c_re, 1, 2).reshape(n_half, gh, p, hch)
    c_im_t = jnp.swapaxes(c_im, 1, 2).reshape(n_half, gh, p, hch)
    cm = jnp.stack([jnp.concatenate([_block_diag(c_re_t[j]), -_block_diag(c_im_t[j])], axis=0)
                    for j in range(n_half)]).astype(BF16)
    lre = jnp.broadcast_to(lb_re.reshape(1, g * p), (nb, g * p))
    lim = jnp.broadcast_to(lb_im.reshape(1, g * p), (nb, g * p))
    ya = _s5_mixer(proj, nb, bm, lre, lim, cm, d.reshape(1, width), glu_w.astype(BF16),
                   glu_b[None, :])

    lw = conv_w.shape[1]
    nblk = 2
    per = LRU_BLOCKS // nblk
    wa_bd = jnp.stack([_block_diag(wa[j * per:(j + 1) * per]) for j in range(nblk)]).astype(BF16)
    wi_bd = jnp.stack([_block_diag(wi[j * per:(j + 1) * per]) for j in range(nblk)]).astype(BF16)
    yb = _lru_mixer(proj, nb, conv_w, conv_b[None, :], wa_bd, ba[None, :], wi_bd, bi[None, :],
                    lam[None, :])

    w_out_b = w_out.astype(BF16)
    return _even_outproj(ya, yb, w_out_b[:width], w_out_b[width:], x)


def _rope_table_kernel(pos_ref, inv_ref, cos_ref, sin_ref):
    ang = pos_ref[...].astype(F32) * inv_ref[...]
    cos_ref[...] = jnp.cos(ang)
    sin_ref[...] = jnp.sin(ang)


def _rope_tables(positions):
    nb, seq = positions.shape
    half = ROPE_DIM // 2
    per_row = LANES // half
    n_tok = nb * seq
    inv_freq = 1.0 / (ROPE_THETA ** (jnp.arange(0, ROPE_DIM, 2, dtype=F32) / ROPE_DIM))
    pos_rep = jnp.repeat(positions.reshape(n_tok // per_row, per_row), half, axis=1)
    inv_rep = jnp.tile(inv_freq, per_row)[None, :]
    cos, sin = pl.pallas_call(
        _rope_table_kernel,
        out_shape=(jax.ShapeDtypeStruct(pos_rep.shape, F32), jax.ShapeDtypeStruct(pos_rep.shape, F32)),
        name="rope_table",
    )(pos_rep, inv_rep)
    cos = cos.reshape(n_tok, half)
    sin = sin.reshape(n_tok, half)
    cc = jnp.concatenate([cos, cos], axis=-1)
    ss = jnp.concatenate([-sin, sin], axis=-1)
    return cc, ss


def _mla_proj_kernel(x_ref, g_ref, win_ref, cqn_ref, ckvn_ref, wq_ref, wkv_ref, qg_ref, kg_ref,
                     cc_ref, ss_ref, qn_ref, qrp_ref, k2_ref, v_ref):
    nh = MLA_HEADS
    h = _rms(x_ref[...], g_ref[...])
    proj = _dot(h.astype(BF16), win_ref[...])
    c_q = _rms(proj[:, :Q_LORA], cqn_ref[...])
    c_kv = _rms(proj[:, Q_LORA:Q_LORA + KV_LORA], ckvn_ref[...])
    o = Q_LORA + KV_LORA
    k_r2 = proj[:, o:o + LANES]
    k_sw2 = proj[:, o + LANES:o + 2 * LANES]
    cc2 = cc_ref[...]
    ss2 = ss_ref[...]
    qg = qg_ref[...]
    kg = kg_ref[...]
    lo = lax.broadcasted_iota(jnp.int32, cc2.shape, 1) < ROPE_DIM
    scale = math.log2(math.e) / math.sqrt(QK_DIM)

    q = _dot(c_q.astype(BF16), wq_ref[...])
    q_a = cc2 * qg[1:2, :]
    q_b = ss2 * qg[2:3, :]
    for j in range(nh // 2):
        ro = nh * NOPE_DIM + j * LANES
        q_rp = q[:, ro:ro + LANES]
        q_swp = q[:, ro + nh * ROPE_DIM:ro + nh * ROPE_DIM + LANES]
        rot = q_rp * q_a + q_swp * q_b
        sq_rp = q_rp * q_rp
        rs_pair = []
        for par in range(2):
            hd = 2 * j + par
            q_n = q[:, hd * NOPE_DIM:(hd + 1) * NOPE_DIM]
            own = lo if par == 0 else jnp.logical_not(lo)
            ms = jnp.sum(q_n * q_n + jnp.where(own, sq_rp, 0.0), axis=-1, keepdims=True) / QK_DIM
            rs = lax.rsqrt(ms + RMS_EPS) * scale
            qn_ref[0, hd] = (q_n * rs * qg[0:1, :]).astype(qn_ref.dtype)
            rs_pair.append(rs)
        qrp_ref[0, j] = (rot * jnp.where(lo, rs_pair[0], rs_pair[1])).astype(qrp_ref.dtype)

    kv = _dot(c_kv.astype(BF16), wkv_ref[...])
    k_rot2 = k_r2 * (cc2 * kg[1:2, :]) + k_sw2 * (ss2 * kg[2:3, :])
    k_r_ss = jnp.sum(jnp.where(lo, k_r2 * k_r2, 0.0), axis=-1, keepdims=True)
    for hd in range(nh):
        ko = hd * (NOPE_DIM + V_DIM)
        k_n = kv[:, ko:ko + NOPE_DIM]
        ms_k = (jnp.sum(k_n * k_n, axis=-1, keepdims=True) + k_r_ss) / QK_DIM
        rs_k = lax.rsqrt(ms_k + RMS_EPS)
        own = lo if hd % 2 == 0 else jnp.logical_not(lo)
        k2_ref[0, hd, :, 0:NOPE_DIM] = (k_n * rs_k * kg[0:1, :]).astype(k2_ref.dtype)
        k2_ref[0, hd, :, NOPE_DIM:] = jnp.where(own, k_rot2 * rs_k, 0.0).astype(k2_ref.dtype)
        v_ref[0, hd] = kv[:, ko + NOPE_DIM:ko + NOPE_DIM + V_DIM].astype(v_ref.dtype)


def _swap_halves(a, axis=-1):
    h = a.shape[axis] // 2
    lo = lax.slice_in_dim(a, 0, h, axis=axis)
    hi = lax.slice_in_dim(a, h, 2 * h, axis=axis)
    return jnp.concatenate([hi, lo], axis=axis)


def _mla_proj(x, norm_g, positions, w_in, cq_norm, ckv_norm, w_uq, w_ukv, q_norm, k_norm):
    nb, seq, dm = x.shape
    nh = MLA_HEADS
    tm = ROW_TILE
    n_tok = nb * seq
    cc, ss = _rope_tables(positions)
    cc2 = jnp.concatenate([cc, cc], axis=-1)
    ss2 = jnp.concatenate([ss, ss], axis=-1)
    o = Q_LORA + KV_LORA
    w_kr = w_in[:, o:o + ROPE_DIM]
    w_sw = _swap_halves(w_kr)
    w_in_x = jnp.concatenate([w_in[:, :o], w_kr, w_kr, w_sw, w_sw], axis=1).astype(BF16)
    wq = w_uq.reshape(Q_LORA, nh, QK_DIM)
    wq_n = wq[:, :, :NOPE_DIM].reshape(Q_LORA, nh * NOPE_DIM)
    wq_r = wq[:, :, NOPE_DIM:]
    wq_x = jnp.concatenate([wq_n, wq_r.reshape(Q_LORA, nh * ROPE_DIM),
                            _swap_halves(wq_r).reshape(Q_LORA, nh * ROPE_DIM)], axis=1).astype(BF16)

    def gains(gv):
        r = gv[NOPE_DIM:]
        rs = _swap_halves(r)
        return jnp.stack([gv[:NOPE_DIM], jnp.concatenate([r, r]), jnp.concatenate([rs, rs])])

    x2 = x.reshape(n_tok, dm)
    per_b = seq // tm
    full = lambda a: pl.BlockSpec(a.shape, lambda i: (0,) * a.ndim)
    args = (x2, norm_g[None, :], w_in_x, cq_norm[None, :], ckv_norm[None, :], wq_x,
            w_ukv.astype(BF16), gains(q_norm), gains(k_norm), cc2, ss2)
    in_specs = [pl.BlockSpec((tm, dm), lambda i: (i, 0))] + [full(a) for a in args[1:9]] + [
        pl.BlockSpec((tm, LANES), lambda i: (i, 0)), pl.BlockSpec((tm, LANES), lambda i: (i, 0))]

    def head_spec(n_heads, dh):
        return pl.BlockSpec((1, n_heads, tm, dh), lambda i: (i // per_b, 0, i % per_b, 0))

    def head_shape(n_heads, dh):
        return jax.ShapeDtypeStruct((nb, n_heads, seq, dh), BF16)

    return pl.pallas_call(
        _mla_proj_kernel,
        out_shape=(head_shape(nh, NOPE_DIM), head_shape(nh // 2, LANES), head_shape(nh, NOPE_DIM + LANES),
                   head_shape(nh, V_DIM)),
        grid=(n_tok // tm,),
        in_specs=in_specs,
        out_specs=(head_spec(nh, NOPE_DIM), head_spec(nh // 2, LANES), head_spec(nh, NOPE_DIM + LANES),
                   head_spec(nh, V_DIM)),
        compiler_params=_cparams("parallel"),
        name="mla_proj",
    )(*args)


def _attention_kernel(qn_ref, qrp_ref, k2_ref, v_ref, o_ref):
    seq = qn_ref.shape[2]
    t = ATT_TQ
    nt = (((1,), (1,)), ((), ()))
    tri = (lax.broadcasted_iota(jnp.int32, (t, t), 1) <= lax.broadcasted_iota(jnp.int32, (t, t), 0))
    for i in range(seq // t):
        q = jnp.concatenate([qn_ref[0, 0, i * t:(i + 1) * t, :], qrp_ref[0, 0, i * t:(i + 1) * t, :]], axis=-1)
        m = jnp.full((t, 1), NEG_INF, F32)
        l = jnp.zeros((t, 1), F32)
        acc = jnp.zeros((t, V_DIM), F32)
        for j in range(i + 1):
            s = lax.dot_general(q, k2_ref[0, 0, j * t:(j + 1) * t, :], nt, preferred_element_type=F32)
            if j == i:
                s = jnp.where(tri, s, NEG_INF)
            m_new = jnp.maximum(m, jnp.max(s, axis=-1, keepdims=True))
            alpha = jnp.exp2(m - m_new)
            p = jnp.exp2(s - m_new)
            l = alpha * l + jnp.sum(p, axis=-1, keepdims=True)
            acc = alpha * acc + _dot(p.astype(BF16), v_ref[0, 0, j * t:(j + 1) * t, :])
            m = m_new
        o_ref[0, i * t:(i + 1) * t, :] = (acc / l).astype(o_ref.dtype)


def _attention(qn, qrp, k2, v):
    nb, nh, seq, _ = qn.shape

    def spec(dh, per=1):
        return pl.BlockSpec((1, 1, seq, dh), lambda b, h: (b, h // per, 0, 0))

    return pl.pallas_call(
        _attention_kernel,
        out_shape=jax.ShapeDtypeStruct((nb, seq, nh * V_DIM), BF16),
        grid=(nb, nh),
        in_specs=[spec(NOPE_DIM), spec(LANES, 2), spec(NOPE_DIM + LANES), spec(V_DIM)],
        out_specs=pl.BlockSpec((1, seq, V_DIM), lambda b, h: (b, 0, h)),
        compiler_params=_cparams("parallel", "parallel"),
        name="mla_attention",
    )(qn, qrp, k2, v)


def _proj_residual_kernel(a_ref, w_ref, x_ref, o_ref):
    o_ref[...] = x_ref[...] + _dot(a_ref[...], w_ref[...])


def _proj_residual(a, w, x2):
    n_tok, dm = x2.shape
    tm = ROW_TILE
    return pl.pallas_call(
        _proj_residual_kernel,
        out_shape=jax.ShapeDtypeStruct(x2.shape, x2.dtype),
        grid=(n_tok // tm,),
        in_specs=[pl.BlockSpec((tm, a.shape[1]), lambda i: (i, 0)),
                  pl.BlockSpec(w.shape, lambda i: (0, 0)),
                  pl.BlockSpec((tm, dm), lambda i: (i, 0))],
        out_specs=pl.BlockSpec((tm, dm), lambda i: (i, 0)),
        compiler_params=_cparams("parallel"),
        name="proj_residual",
    )(a, w, x2)


def _mla_mixer(x, norm_g, positions, w_in, cq_norm, ckv_norm, w_uq, w_ukv, q_norm, k_norm, w_o):
    nb, seq, dm = x.shape
    qn, qrp, k2, v = _mla_proj(x, norm_g, positions, w_in, cq_norm, ckv_norm, w_uq, w_ukv, q_norm, k_norm)
    o = _attention(qn, qrp, k2, v)
    out = _proj_residual(o.reshape(nb * seq, -1), w_o.astype(BF16), x.reshape(nb * seq, dm))
    return out.reshape(nb, seq, dm)


def _router_kernel(x_ref, g_ref, w_ref, b_ref, hn_ref, meta_ref, gate_ref, cnt_ref, base_ref):
    tm, dm = x_ref.shape
    ne = N_EXPERTS

    @pl.when(pl.program_id(0) == 0)
    def _():
        base_ref[...] = jnp.zeros_like(base_ref)

    hn = _rms(x_ref[...], g_ref[...])
    for c in range(ROW_CHUNKS):
        hn_ref[pl.ds(c, tm, stride=ROW_CHUNKS), :] = hn[:, c * LANES:(c + 1) * LANES]
    logits = _dot(hn.astype(BF16), w_ref[...]) + b_ref[...]
    lane = lax.broadcasted_iota(jnp.int32, logits.shape, 1)
    big = jnp.int32(1 << 20)

    def first_argmax(v):
        m = jnp.max(v, axis=-1, keepdims=True)
        idx = jnp.min(jnp.where(v == m, lane, big), axis=-1, keepdims=True)
        return m, idx

    is_g = (lane >= ne) & (lane < ne + N_GROUPS)
    gl = jnp.where(is_g, logits, NEG_INF)
    g_max, g_idx = first_argmax(gl)
    g_sel = g_idx - ne
    p_g = 1.0 / jnp.sum(jnp.where(is_g, jnp.exp(gl - g_max), 0.0), axis=-1, keepdims=True)
    in_grp = (lane >= g_sel * EXPERTS_PER_GROUP) & (lane < (g_sel + 1) * EXPERTS_PER_GROUP)
    el = jnp.where(in_grp, logits, NEG_INF)
    v0, e0 = first_argmax(el)
    v1, e1 = first_argmax(jnp.where(lane == e0, NEG_INF, el))
    t = jnp.exp(v1 - v0)
    gate0 = p_g / (1.0 + t)
    gate1 = p_g * t / (1.0 + t)

    oh0 = (lane == e0).astype(F32)
    oh1 = (lane == e1).astype(F32)
    both = oh0 + oh1
    r_i = lax.broadcasted_iota(jnp.int32, (tm, tm), 0)
    c_i = lax.broadcasted_iota(jnp.int32, (tm, tm), 1)
    strict_lower = (c_i < r_i).astype(BF16)
    before = _dot(strict_lower, both.astype(BF16)) + base_ref[0:1, :]
    rank0 = jnp.sum(oh0 * before, axis=-1, keepdims=True)
    rank1 = jnp.sum(oh1 * before, axis=-1, keepdims=True)
    base_ref[0:1, :] = base_ref[0:1, :] + jnp.sum(both, axis=0, keepdims=True)
    cnt_ref[...] = jnp.broadcast_to(base_ref[0:1, :], cnt_ref.shape).astype(jnp.int32)

    meta = jnp.where(lane == 0, e0, jnp.where(lane == 1, e1, jnp.where(
        lane == 2, rank0.astype(jnp.int32), jnp.where(lane == 3, rank1.astype(jnp.int32), 0))))
    meta_ref[...] = meta
    gate_ref[...] = jnp.where(lane == 0, gate0, jnp.where(lane == 1, gate1, 0.0))


def _router(x2, norm_g, w_group, b_group, w_expert, b_expert):
    n_tok, dm = x2.shape
    tm = ROW_TILE
    pad = LANES - N_EXPERTS - N_GROUPS
    w = jnp.concatenate([w_expert, w_group, jnp.zeros((dm, pad), F32)], axis=1).astype(BF16)
    b = jnp.concatenate([b_expert, b_group, jnp.zeros((pad,), F32)])[None, :]
    return pl.pallas_call(
        _router_kernel,
        out_shape=(jax.ShapeDtypeStruct((n_tok * ROW_CHUNKS, LANES), F32),
                   jax.ShapeDtypeStruct((n_tok, LANES), jnp.int32),
                   jax.ShapeDtypeStruct((n_tok, LANES), F32),
                   jax.ShapeDtypeStruct((SUBLANES, LANES), jnp.int32)),
        grid=(n_tok // tm,),
        in_specs=[pl.BlockSpec((tm, dm), lambda i: (i, 0)),
                  pl.BlockSpec((1, dm), lambda i: (0, 0)),
                  pl.BlockSpec((dm, LANES), lambda i: (0, 0)),
                  pl.BlockSpec((1, LANES), lambda i: (0, 0))],
        out_specs=(pl.BlockSpec((tm * ROW_CHUNKS, LANES), lambda i: (i, 0)),
                   pl.BlockSpec((tm, LANES), lambda i: (i, 0)),
                   pl.BlockSpec((tm, LANES), lambda i: (i, 0)),
                   pl.BlockSpec((SUBLANES, LANES), lambda i: (0, 0))),
        scratch_shapes=[pltpu.VMEM((SUBLANES, LANES), F32)],
        compiler_params=_cparams("arbitrary"),
        name="moe_router",
    )(x2, norm_g[None, :], w, b)


def _dispatch_kernel(dest_ref, hn_ref, xd_ref, zero_ref, sem, zsem):
    tm = hn_ref.shape[0] // ROW_CHUNKS
    n_assign = pl.num_programs(0) * tm * TOP_K

    def tail_copy():
        return pltpu.make_async_copy(zero_ref, xd_ref.at[pl.ds(n_assign * ROW_CHUNKS, zero_ref.shape[0])], zsem)

    @pl.when(pl.program_id(0) == 0)
    def _():
        zero_ref[...] = jnp.zeros_like(zero_ref)
        tail_copy().start()

    def row_copy(t, k):
        src = pl.multiple_of(t * ROW_CHUNKS, ROW_CHUNKS)
        dst = pl.multiple_of(dest_ref[0, 0, 2 * t + k], ROW_CHUNKS)
        return pltpu.make_async_copy(hn_ref.at[pl.ds(src, ROW_CHUNKS)], xd_ref.at[pl.ds(dst, ROW_CHUNKS)], sem)

    def issue(i, c):
        for u in range(DMA_UNROLL):
            for k in range(TOP_K):
                row_copy(i * DMA_UNROLL + u, k).start(priority=k)
        return c

    lax.fori_loop(0, tm // DMA_UNROLL, issue, 0)

    def drain(i, c):
        for u in range(DMA_UNROLL):
            for k in range(TOP_K):
                row_copy(i * DMA_UNROLL + u, k).wait()
        return c

    lax.fori_loop(0, tm // DMA_UNROLL, drain, 0)

    @pl.when(pl.program_id(0) == 0)
    def _():
        tail_copy().wait()


def _dispatch(hn2, dest_off):
    tm = GATHER_TM
    n_tok = hn2.shape[0] // ROW_CHUNKS
    dest3 = dest_off.reshape(n_tok // tm, 1, 2 * tm)
    n_rows = n_tok * TOP_K + MOE_BM
    return pl.pallas_call(
        _dispatch_kernel,
        out_shape=jax.ShapeDtypeStruct((n_rows * ROW_CHUNKS, LANES), hn2.dtype),
        grid=(n_tok // tm,),
        in_specs=[pl.BlockSpec((1, 1, 2 * tm), lambda i: (i, 0, 0), memory_space=pltpu.SMEM),
                  pl.BlockSpec((tm * ROW_CHUNKS, LANES), lambda i: (i, 0))],
        out_specs=pl.BlockSpec(memory_space=pl.ANY),
        scratch_shapes=[pltpu.VMEM((MOE_BM * ROW_CHUNKS, LANES), hn2.dtype),
                        pltpu.SemaphoreType.DMA, pltpu.SemaphoreType.DMA],
        compiler_params=_cparams("arbitrary"),
        name="moe_dispatch",
    )(dest3, hn2)


def _rows_from_tiles(ref, n, base=0):
    return jnp.concatenate([ref[pl.ds(base + c, n, stride=ROW_CHUNKS), :] for c in range(ROW_CHUNKS)], axis=-1)


def _expert_kernel(be_ref, bs_ref, nu_ref, xd_ref, w1_ref, w3_ref, w2_ref, yd_ref,
                   xbuf, ybuf, sem_in, sem_out, w13_s, w2_s):
    b = pl.program_id(0)
    nu = nu_ref[0]
    slot = b % 2
    n_rows = xbuf.shape[1]
    bm = n_rows // ROW_CHUNKS
    de = w1_ref.shape[3]

    def in_copy(blk, sl):
        src = pl.multiple_of(bs_ref[blk] * ROW_CHUNKS, ROW_CHUNKS)
        return pltpu.make_async_copy(xd_ref.at[pl.ds(src, n_rows)], xbuf.at[sl], sem_in.at[sl])

    def out_copy(blk, sl):
        dst = pl.multiple_of(bs_ref[blk] * ROW_CHUNKS, ROW_CHUNKS)
        return pltpu.make_async_copy(ybuf.at[sl], yd_ref.at[pl.ds(dst, n_rows)], sem_out.at[sl])

    @pl.when(b == 0)
    def _():
        in_copy(0, 0).start()

    @pl.when(b + 1 < nu)
    def _():
        in_copy(b + 1, 1 - slot).start()

    @pl.when(b < nu)
    def _():
        in_copy(b, slot).wait()
        changed = jnp.logical_or(b == 0, be_ref[b] != be_ref[jnp.maximum(b - 1, 0)])

        @pl.when(changed)
        def _():
            w13_s[:, :de] = w1_ref[0, 0].astype(BF16)
            w13_s[:, de:] = w3_ref[0, 0].astype(BF16)
            w2_s[...] = w2_ref[0, 0].astype(BF16)

        x = _rows_from_tiles(xbuf.at[slot], bm).astype(BF16)
        h = _dot(x, w13_s[...])
        a = jax.nn.silu(h[:, :de]) * h[:, de:]
        y = _dot(a.astype(BF16), w2_s[...])
        yb = ybuf.at[slot]
        for c in range(ROW_CHUNKS):
            yb[pl.ds(c, bm, stride=ROW_CHUNKS), :] = y[:, c * LANES:(c + 1) * LANES]

        @pl.when(b >= 1)
        def _():
            out_copy(b - 1, 1 - slot).wait()
        out_copy(b, slot).start()

        @pl.when(b == nu - 1)
        def _():
            out_copy(b, slot).wait()
            tail = (pl.num_programs(0) - N_EXPERTS) * n_rows
            ybuf[1 - slot] = jnp.zeros((n_rows, LANES), F32)
            fill = pltpu.make_async_copy(ybuf.at[1 - slot], yd_ref.at[pl.ds(tail, n_rows)], sem_out.at[1 - slot])
            fill.start()
            fill.wait()


def _experts(xd, blk_expert, blk_start, n_used, w1, w3, w2, layer):
    bm = MOE_BM
    n_blocks = blk_expert.shape[0]
    _, _, dm, de = w1.shape

    def w_map(b, be, bs, nu):
        return (layer, be[b], 0, 0)

    grid_spec = pltpu.PrefetchScalarGridSpec(
        num_scalar_prefetch=3,
        grid=(n_blocks,),
        in_specs=[pl.BlockSpec(memory_space=pl.ANY),
                  pl.BlockSpec((1, 1, dm, de), w_map),
                  pl.BlockSpec((1, 1, dm, de), w_map),
                  pl.BlockSpec((1, 1, de, dm), w_map)],
        out_specs=pl.BlockSpec(memory_space=pl.ANY),
        scratch_shapes=[pltpu.VMEM((2, bm * ROW_CHUNKS, LANES), F32),
                        pltpu.VMEM((2, bm * ROW_CHUNKS, LANES), F32),
                        pltpu.SemaphoreType.DMA((2,)),
                        pltpu.SemaphoreType.DMA((2,)),
                        pltpu.VMEM((dm, 2 * de), BF16),
                        pltpu.VMEM((de, dm), BF16)],
    )
    return pl.pallas_call(
        _expert_kernel,
        out_shape=jax.ShapeDtypeStruct(xd.shape, F32),
        grid_spec=grid_spec,
        compiler_params=_cparams("arbitrary"),
        name="moe_experts",
    )(blk_expert, blk_start, n_used, xd, w1, w3, w2)


def _combine_kernel(dest_ref, dest_next_ref, yd_ref, gate_ref, x_ref, o_ref, buf_ref, sem):
    tm, dm = x_ref.shape
    i = pl.program_id(0)
    n_steps = pl.num_programs(0)
    slot = i % 2
    k_rows = tm * ROW_CHUNKS

    def row_copy(d_ref, sl, t, k):
        src = pl.multiple_of(d_ref[0, 0, 2 * t + k], ROW_CHUNKS)
        dst = pl.multiple_of(k * k_rows + t * ROW_CHUNKS, ROW_CHUNKS)
        return pltpu.make_async_copy(yd_ref.at[pl.ds(src, ROW_CHUNKS)],
                                     buf_ref.at[sl, pl.ds(dst, ROW_CHUNKS)], sem.at[sl])

    def issue_all(d_ref, sl):
        def issue(j, c):
            for u in range(DMA_UNROLL):
                for k in range(TOP_K):
                    row_copy(d_ref, sl, j * DMA_UNROLL + u, k).start(priority=k)
            return c
        lax.fori_loop(0, tm // DMA_UNROLL, issue, 0)

    @pl.when(i == 0)
    def _():
        issue_all(dest_ref, slot)

    @pl.when(i + 1 < n_steps)
    def _():
        issue_all(dest_next_ref, 1 - slot)

    def drain(j, c):
        for u in range(DMA_UNROLL):
            for k in range(TOP_K):
                row_copy(dest_ref, slot, j * DMA_UNROLL + u, k).wait()
        return c

    lax.fori_loop(0, tm // DMA_UNROLL, drain, 0)

    g0 = gate_ref[:, 0:1]
    g1 = gate_ref[:, 1:2]
    buf = buf_ref.at[slot]
    for c in range(ROW_CHUNKS):
        y0 = buf[pl.ds(c, tm, stride=ROW_CHUNKS), :]
        y1 = buf[pl.ds(k_rows + c, tm, stride=ROW_CHUNKS), :]
        o_ref[:, c * LANES:(c + 1) * LANES] = x_ref[:, c * LANES:(c + 1) * LANES] + (g0 * y0 + g1 * y1)


def _combine(yd, dest_off, gates, x2):
    n_tok, dm = x2.shape
    tm = GATHER_TM
    n_steps = n_tok // tm
    dest3 = dest_off.reshape(n_steps, 1, 2 * tm)
    return pl.pallas_call(
        _combine_kernel,
        out_shape=jax.ShapeDtypeStruct(x2.shape, x2.dtype),
        grid=(n_steps,),
        in_specs=[pl.BlockSpec((1, 1, 2 * tm), lambda i: (i, 0, 0), memory_space=pltpu.SMEM),
                  pl.BlockSpec((1, 1, 2 * tm), lambda i: (jnp.minimum(i + 1, n_steps - 1), 0, 0),
                               memory_space=pltpu.SMEM),
                  pl.BlockSpec(memory_space=pl.ANY),
                  pl.BlockSpec((tm, LANES), lambda i: (i, 0)),
                  pl.BlockSpec((tm, dm), lambda i: (i, 0))],
        out_specs=pl.BlockSpec((tm, dm), lambda i: (i, 0)),
        scratch_shapes=[pltpu.VMEM((2, TOP_K * tm * ROW_CHUNKS, LANES), F32),
                        pltpu.SemaphoreType.DMA((2,))],
        compiler_params=_cparams("arbitrary"),
        name="moe_combine",
    )(dest3, dest3, yd, gates, x2)


def _moe(x, norm_g, w_group, b_group, w_expert, b_expert, w1, w3, w2, layer):
    nb, seq, dm = x.shape
    n_tok = nb * seq
    n_assign = n_tok * TOP_K
    x2 = x.reshape(n_tok, dm)
    hn2, meta, gates, counts = _router(x2, norm_g, w_group, b_group, w_expert, b_expert)
    bm = MOE_BM
    n_blocks = n_assign // bm + N_EXPERTS
    counts = counts[0, :N_EXPERTS]
    starts = jnp.cumsum(counts) - counts
    eid = meta[:, 0:TOP_K]
    e_iota = jnp.arange(N_EXPERTS, dtype=jnp.int32)
    seg_start = jnp.sum(jnp.where(eid[:, :, None] == e_iota, starts, 0), axis=-1)
    dest_off = ((seg_start + meta[:, TOP_K:2 * TOP_K]) * ROW_CHUNKS).astype(jnp.int32).reshape(-1)
    nblk = (counts + bm - 1) // bm
    blk_end = jnp.cumsum(nblk)
    b_iota = jnp.arange(n_blocks, dtype=jnp.int32)
    blk_expert = jnp.minimum(jnp.sum((b_iota[:, None] >= blk_end[None, :]).astype(jnp.int32), axis=1),
                             N_EXPERTS - 1)
    j = b_iota - (blk_end - nblk)[blk_expert]
    n_used = blk_end[-1].astype(jnp.int32).reshape(1)
    blk_start = jnp.where(b_iota < n_used[0], starts[blk_expert] + j * bm, 0).astype(jnp.int32)

    xd = _dispatch(hn2, dest_off)
    yd = _experts(xd, blk_expert, blk_start, n_used, w1, w3, w2, layer)
    out = _combine(yd, dest_off, gates, x2)
    return out.reshape(nb, seq, dm)


def kernel(x, positions, ev_norm, ev_w_in, s5_lambda_re, s5_lambda_im, s5_log_dt, s5_b_re, s5_b_im,
           s5_c_re, s5_c_im, s5_d, s5_glu_w, s5_glu_b, lru_conv_w, lru_conv_b, lru_wa, lru_ba, lru_wi,
           lru_bi, lru_lambda, ev_w_out, od_norm, mla_w_in, mla_cq_norm, mla_ckv_norm, mla_w_uq,
           mla_w_ukv, mla_q_norm, mla_k_norm, mla_w_o, ffn_norm, moe_w_group, moe_b_group,
           moe_w_expert, moe_b_expert, moe_w1, moe_w3, moe_w2):
    depth = ffn_norm.shape[0]
    for layer in range(depth):
        j = layer // 2
        if layer % 2 == 0:
            x = _even_mixer(x, ev_norm[j], ev_w_in[j], s5_lambda_re[j], s5_lambda_im[j], s5_log_dt[j],
                            s5_b_re[j], s5_b_im[j], s5_c_re[j], s5_c_im[j], s5_d[j], s5_glu_w[j],
                            s5_glu_b[j], lru_conv_w[j], lru_conv_b[j], lru_wa[j], lru_ba[j], lru_wi[j],
                            lru_bi[j], lru_lambda[j], ev_w_out[j])
        else:
            x = _mla_mixer(x, od_norm[j], positions, mla_w_in[j], mla_cq_norm[j], mla_ckv_norm[j],
                           mla_w_uq[j], mla_w_ukv[j], mla_q_norm[j], mla_k_norm[j], mla_w_o[j])
        x = _moe(x, ffn_norm[layer], moe_w_group[layer], moe_b_group[layer], moe_w_expert[layer],
                 moe_b_expert[layer], moe_w1, moe_w3, moe_w2, layer)
    return x
```

```python
import functools
import math

import jax
import jax.numpy as jnp
from jax import lax
from jax.experimental import pallas as pl
from jax.experimental.pallas import tpu as pltpu

F32 = jnp.float32
BF16 = jnp.bfloat16

S5_GROUP_CH = 16
S5_STATE = 64
LRU_BLOCKS = 8
CONV_WIDTH = 4
RG_C = 8.0
MLA_HEADS = 8
Q_LORA = 384
KV_LORA = 256
NOPE_DIM = 128
ROPE_DIM = 64
V_DIM = 128
QK_DIM = NOPE_DIM + ROPE_DIM
ROPE_THETA = 10000.0
N_GROUPS = 8
EXPERTS_PER_GROUP = 8
N_EXPERTS = N_GROUPS * EXPERTS_PER_GROUP
TOP_K = 2
RMS_EPS = 1e-6
NEG_INF = -1e30

SUBLANES = 8
LANES = 128
ROW_CHUNKS = 8
VMEM_LIMIT = 48 * 1024 * 1024

EVEN_TL = 64
ROW_TILE = 512
MOE_BM = 256
ATT_TQ = 512
GATHER_TM = 256
DMA_UNROLL = 8


def _cparams(*sem):
    return pltpu.CompilerParams(dimension_semantics=tuple(sem), vmem_limit_bytes=VMEM_LIMIT)


def _rms(x, g):
    ms = jnp.mean(x * x, axis=-1, keepdims=True)
    return x * lax.rsqrt(ms + RMS_EPS) * g


def _dot(a, b):
    return jnp.dot(a, b, preferred_element_type=F32)


def _s5_discretize_kernel(lre_ref, lim_ref, ldt_ref, bre_ref, bim_ref,
                          lbre_ref, lbim_ref, bbre_ref, bbim_ref):
    lr = jnp.minimum(lre_ref[...], -1e-4)
    li = lim_ref[...]
    dt = jnp.exp(ldt_ref[...])
    mag = jnp.exp(lr * dt)
    lb_re = mag * jnp.cos(li * dt)
    lb_im = mag * jnp.sin(li * dt)
    den = lr * lr + li * li
    num_re = lb_re - 1.0
    f_re = (num_re * lr + lb_im * li) / den
    f_im = (lb_im * lr - num_re * li) / den
    lbre_ref[...] = lb_re
    lbim_ref[...] = lb_im
    br = bre_ref[...]
    bi = bim_ref[...]
    bbre_ref[...] = f_re[:, None, :] * br - f_im[:, None, :] * bi
    bbim_ref[...] = f_re[:, None, :] * bi + f_im[:, None, :] * br


def _s5_discretize(lam_re, lam_im, log_dt, b_re, b_im):
    g, p = lam_re.shape
    h = b_re.shape[-1]
    b_re_t = jnp.swapaxes(b_re, 1, 2)
    b_im_t = jnp.swapaxes(b_im, 1, 2)
    return pl.pallas_call(
        _s5_discretize_kernel,
        out_shape=(jax.ShapeDtypeStruct((g, p), F32), jax.ShapeDtypeStruct((g, p), F32),
                   jax.ShapeDtypeStruct((g, h, p), F32), jax.ShapeDtypeStruct((g, h, p), F32)),
        name="s5_discretize",
    )(lam_re, lam_im, log_dt[:, None], b_re_t, b_im_t)


def _even_inproj_kernel(x_ref, g_ref, w_ref, o_ref, h_ref):
    nb, tl, _ = x_ref.shape
    nc = h_ref.shape[0]
    g = g_ref[...]
    for b in range(nb):
        h = _rms(x_ref[b], g)
        for c in range(nc):
            h_ref[c, pl.ds(b, tl, stride=nb), :] = h[:, c * LANES:(c + 1) * LANES]
    h_all = jnp.concatenate([h_ref[c] for c in range(nc)], axis=-1)
    o_ref[...] = _dot(h_all.astype(BF16), w_ref[...]).astype(o_ref.dtype)


def _even_inproj(x, g, w):
    nb, seq, d = x.shape
    n_out = w.shape[1]
    tl = EVEN_TL
    return pl.pallas_call(
        _even_inproj_kernel,
        out_shape=jax.ShapeDtypeStruct((seq * nb, n_out), BF16),
        grid=(seq // tl,),
        in_specs=[pl.BlockSpec((nb, tl, d), lambda i: (0, i, 0)),
                  pl.BlockSpec((1, d), lambda i: (0, 0)),
                  pl.BlockSpec((d, n_out), lambda i: (0, 0))],
        out_specs=pl.BlockSpec((tl * nb, n_out), lambda i: (i, 0)),
        scratch_shapes=[pltpu.VMEM((d // LANES, tl * nb, LANES), F32)],
        compiler_params=_cparams("parallel"),
        name="even_inproj",
    )(x, g, w)


def _s5_kernel(u_ref, bm_ref, lre_ref, lim_ref, cm_ref, d_ref, gw_ref, gb_ref, o_ref,
               bu_ref, sb_ref, st_ref, *, nb):
    rows, width = u_ref.shape
    n_half = bm_ref.shape[0]
    kin = width // n_half
    ncol = bm_ref.shape[2]
    nre = ncol // 2
    chunk = 512
    steps = rows // nb

    @pl.when(pl.program_id(0) == 0)
    def _():
        st_ref[...] = jnp.zeros_like(st_ref)

    u = u_ref[...]
    for j in range(n_half):
        bu_ref[:, j * ncol:(j + 1) * ncol] = _dot(u[:, j * kin:(j + 1) * kin], bm_ref[j])

    for j in range(n_half):
        for c in range(nre // chunk):
            cr = j * ncol + c * chunk
            ci = cr + nre
            lc = j * nre + c * chunk
            lr = lre_ref[:, lc:lc + chunk]
            li = lim_ref[:, lc:lc + chunk]

            def body(i, carry, cr=cr, ci=ci, lr=lr, li=li):
                sr, si = carry
                r0 = pl.multiple_of(i * (2 * nb), 2 * nb)
                sr1 = lr * sr - li * si + bu_ref[pl.ds(r0, nb), cr:cr + chunk]
                si1 = lr * si + li * sr + bu_ref[pl.ds(r0, nb), ci:ci + chunk]
                sr2 = lr * sr1 - li * si1 + bu_ref[pl.ds(r0 + nb, nb), cr:cr + chunk]
                si2 = lr * si1 + li * sr1 + bu_ref[pl.ds(r0 + nb, nb), ci:ci + chunk]
                sb_ref[pl.ds(r0, 2 * nb), cr:cr + chunk] = jnp.concatenate([sr1, sr2], 0).astype(BF16)
                sb_ref[pl.ds(r0, 2 * nb), ci:ci + chunk] = jnp.concatenate([si1, si2], 0).astype(BF16)
                return sr2, si2

            sr, si = lax.fori_loop(0, steps // 2, body,
                                   (st_ref[:, cr:cr + chunk], st_ref[:, ci:ci + chunk]))
            st_ref[:, cr:cr + chunk] = sr
            st_ref[:, ci:ci + chunk] = si

    ys = [_dot(sb_ref[:, j * ncol:(j + 1) * ncol], cm_ref[j]) for j in range(n_half)]
    y = jnp.concatenate(ys, axis=-1) + d_ref[...] * u.astype(F32)
    y = jax.nn.gelu(y)
    z = _dot(y.astype(BF16), gw_ref[...]) + gb_ref[...]
    o_ref[...] = (y * jax.nn.sigmoid(z)).astype(o_ref.dtype)


def _s5_mixer(proj, nb, bm, lre, lim, cm, d, glu_w, glu_b):
    rows_total = proj.shape[0]
    width = d.shape[1]
    rows = EVEN_TL * nb
    n_half, _, ncol = bm.shape
    kern = functools.partial(_s5_kernel, nb=nb)
    return pl.pallas_call(
        kern,
        out_shape=jax.ShapeDtypeStruct((rows_total, width), BF16),
        grid=(rows_total // rows,),
        in_specs=[pl.BlockSpec((rows, width), lambda i: (i, 0)),
                  pl.BlockSpec(bm.shape, lambda i: (0, 0, 0)),
                  pl.BlockSpec(lre.shape, lambda i: (0, 0)),
                  pl.BlockSpec(lim.shape, lambda i: (0, 0)),
                  pl.BlockSpec(cm.shape, lambda i: (0, 0, 0)),
                  pl.BlockSpec((1, width), lambda i: (0, 0)),
                  pl.BlockSpec(glu_w.shape, lambda i: (0, 0)),
                  pl.BlockSpec((1, width), lambda i: (0, 0))],
        out_specs=pl.BlockSpec((rows, width), lambda i: (i, 0)),
        scratch_shapes=[pltpu.VMEM((rows, n_half * ncol), F32),
                        pltpu.VMEM((rows, n_half * ncol), BF16),
                        pltpu.VMEM((nb, n_half * ncol), F32)],
        compiler_params=_cparams("arbitrary"),
        name="s5_mixer",
    )(proj, bm, lre, lim, cm, d, glu_w, glu_b)


def _lru_kernel(x_ref, gate_ref, cw_ref, cb_ref, wa_ref, ba_ref, wi_ref, bi_ref, lam_ref, o_ref,
                xp_ref, a_ref, b_ref, h_ref, *, nb):
    rows, width = x_ref.shape
    halo = (CONV_WIDTH - 1) * nb
    steps = rows // nb

    @pl.when(pl.program_id(0) == 0)
    def _():
        xp_ref[0:halo, :] = jnp.zeros((halo, width), F32)
        h_ref[...] = jnp.zeros_like(h_ref)

    xp_ref[halo:halo + rows, :] = x_ref[...].astype(F32)
    xf = cb_ref[...] + cw_ref[0:1, :] * xp_ref[0:rows, :]
    for k in range(1, CONV_WIDTH):
        xf = xf + cw_ref[k:k + 1, :] * xp_ref[k * nb:k * nb + rows, :]
    xp_ref[0:halo, :] = xp_ref[rows:rows + halo, :]

    xb = xf.astype(BF16)
    nblk = wa_ref.shape[0]
    kb = width // nblk
    ga = jnp.concatenate([_dot(xb[:, j * kb:(j + 1) * kb], wa_ref[j]) for j in range(nblk)], -1)
    gi = jnp.concatenate([_dot(xb[:, j * kb:(j + 1) * kb], wi_ref[j]) for j in range(nblk)], -1)
    r = jax.nn.sigmoid(ga + ba_ref[...])
    ig = jax.nn.sigmoid(gi + bi_ref[...])
    log_a = RG_C * r * jax.nn.log_sigmoid(lam_ref[...])
    a_ref[...] = jnp.exp(log_a)
    th = jnp.tanh(log_a)
    b_ref[...] = jnp.sqrt(-2.0 * th / (1.0 - th)) * (ig * xf)

    def body(t, h):
        r0 = pl.multiple_of(t * nb, nb)
        h = a_ref[pl.ds(r0, nb), :] * h + b_ref[pl.ds(r0, nb), :]
        b_ref[pl.ds(r0, nb), :] = h
        return h

    h_ref[...] = lax.fori_loop(0, steps, body, h_ref[...])
    o_ref[...] = (b_ref[...] * jax.nn.gelu(gate_ref[...].astype(F32))).astype(o_ref.dtype)


def _lru_mixer(proj, nb, conv_w, conv_b, wa, ba, wi, bi, lam):
    rows_total = proj.shape[0]
    width = conv_w.shape[1]
    rows = EVEN_TL * nb
    halo = (CONV_WIDTH - 1) * nb
    kern = functools.partial(_lru_kernel, nb=nb)
    vec = pl.BlockSpec((1, width), lambda i: (0, 0))
    return pl.pallas_call(
        kern,
        out_shape=jax.ShapeDtypeStruct((rows_total, width), BF16),
        grid=(rows_total // rows,),
        in_specs=[pl.BlockSpec((rows, width), lambda i: (i, 1)),
                  pl.BlockSpec((rows, width), lambda i: (i, 2)),
                  pl.BlockSpec(conv_w.shape, lambda i: (0, 0)),
                  vec,
                  pl.BlockSpec(wa.shape, lambda i: (0, 0, 0)),
                  vec,
                  pl.BlockSpec(wi.shape, lambda i: (0, 0, 0)),
                  vec, vec],
        out_specs=pl.BlockSpec((rows, width), lambda i: (i, 0)),
        scratch_shapes=[pltpu.VMEM((rows + halo, width), F32),
                        pltpu.VMEM((rows, width), F32),
                        pltpu.VMEM((rows, width), F32),
                        pltpu.VMEM((nb, width), F32)],
        compiler_params=_cparams("arbitrary"),
        name="lru_mixer",
    )(proj, proj, conv_w, conv_b, wa, ba, wi, bi, lam)


def _even_outproj_kernel(ya_ref, yb_ref, wa_ref, wb_ref, x_ref, o_ref, y_ref):
    nb, tl, _ = x_ref.shape
    nc = y_ref.shape[0]
    y = _dot(ya_ref[...], wa_ref[...]) + _dot(yb_ref[...], wb_ref[...])
    for c in range(nc):
        y_ref[c] = y[:, c * LANES:(c + 1) * LANES]
    for b in range(nb):
        for c in range(nc):
            o_ref[b, :, c * LANES:(c + 1) * LANES] = (x_ref[b, :, c * LANES:(c + 1) * LANES]
                                                     + y_ref[c, pl.ds(b, tl, stride=nb), :])


def _even_outproj(ya, yb, w_a, w_b, x):
    nb, seq, d = x.shape
    tl = EVEN_TL
    rows = tl * nb
    wd = ya.shape[1]
    return pl.pallas_call(
        _even_outproj_kernel,
        out_shape=jax.ShapeDtypeStruct(x.shape, x.dtype),
        grid=(seq // tl,),
        in_specs=[pl.BlockSpec((rows, wd), lambda i: (i, 0)),
                  pl.BlockSpec((rows, wd), lambda i: (i, 0)),
                  pl.BlockSpec(w_a.shape, lambda i: (0, 0)),
                  pl.BlockSpec(w_b.shape, lambda i: (0, 0)),
                  pl.BlockSpec((nb, tl, d), lambda i: (0, i, 0))],
        out_specs=pl.BlockSpec((nb, tl, d), lambda i: (0, i, 0)),
        scratch_shapes=[pltpu.VMEM((d // LANES, rows, LANES), F32)],
        compiler_params=_cparams("parallel"),
        name="even_outproj",
    )(ya, yb, w_a, w_b, x)


def _block_diag(blocks):
    n, r, c = blocks.shape
    eye = jnp.eye(n, dtype=blocks.dtype)
    return jnp.einsum('nrc,nm->nrmc', blocks, eye).reshape(n * r, n * c)


def _even_mixer(x, norm_g, w_in, lam_re, lam_im, log_dt, b_re, b_im, c_re, c_im, d, glu_w, glu_b,
                conv_w, conv_b, wa, ba, wi, bi, lam, w_out):
    nb, seq, dm = x.shape
    g, p = lam_re.shape
    hch = b_re.shape[-1]
    width = g * hch
    n_half = 2
    gh = g // n_half

    proj = _even_inproj(x, norm_g[None, :], w_in.astype(BF16))

    lb_re, lb_im, bb_re, bb_im = _s5_discretize(lam_re, lam_im, log_dt, b_re, b_im)
    bb_re = bb_re.reshape(n_half, gh, hch, p)
    bb_im = bb_im.reshape(n_half, gh, hch, p)
    bm = jnp.stack([jnp.concatenate([_block_diag(bb_re[j]), _block_diag(bb_im[j])], axis=1)
                    for j in range(n_half)]).astype(BF16)
    c_re_t = jnp.swapaxes(c_re, 1, 2).reshape(n_half, gh, p, hch)
    c_im_t = jnp.swapaxes(c_im, 1, 2).reshape(n_half, gh, p, hch)
    cm = jnp.stack([jnp.concatenate([_block_diag(c_re_t[j]), -_block_diag(c_im_t[j])], axis=0)
                    for j in range(n_half)]).astype(BF16)
    lre = jnp.broadcast_to(lb_re.reshape(1, g * p), (nb, g * p))
    lim = jnp.broadcast_to(lb_im.reshape(1, g * p), (nb, g * p))
    ya = _s5_mixer(proj, nb, bm, lre, lim, cm, d.reshape(1, width), glu_w.astype(BF16),
                   glu_b[None, :])

    lw = conv_w.shape[1]
    nblk = 2
    per = LRU_BLOCKS // nblk
    wa_bd = jnp.stack([_block_diag(wa[j * per:(j + 1) * per]) for j in range(nblk)]).astype(BF16)
    wi_bd = jnp.stack([_block_diag(wi[j * per:(j + 1) * per]) for j in range(nblk)]).astype(BF16)
    yb = _lru_mixer(proj, nb, conv_w, conv_b[None, :], wa_bd, ba[None, :], wi_bd, bi[None, :],
                    lam[None, :])

    w_out_b = w_out.astype(BF16)
    return _even_outproj(ya, yb, w_out_b[:width], w_out_b[width:], x)


def _rope_table_kernel(pos_ref, inv_ref, cos_ref, sin_ref):
    ang = pos_ref[...].astype(F32) * inv_ref[...]
    cos_ref[...] = jnp.cos(ang)
    sin_ref[...] = jnp.sin(ang)


def _rope_tables(positions):
    nb, seq = positions.shape
    half = ROPE_DIM // 2
    per_row = LANES // half
    n_tok = nb * seq
    inv_freq = 1.0 / (ROPE_THETA ** (jnp.arange(0, ROPE_DIM, 2, dtype=F32) / ROPE_DIM))
    pos_rep = jnp.repeat(positions.reshape(n_tok // per_row, per_row), half, axis=1)
    inv_rep = jnp.tile(inv_freq, per_row)[None, :]
    cos, sin = pl.pallas_call(
        _rope_table_kernel,
        out_shape=(jax.ShapeDtypeStruct(pos_rep.shape, F32), jax.ShapeDtypeStruct(pos_rep.shape, F32)),
        name="rope_table",
    )(pos_rep, inv_rep)
    cos = cos.reshape(n_tok, half)
    sin = sin.reshape(n_tok, half)
    cc = jnp.concatenate([cos, cos], axis=-1)
    ss = jnp.concatenate([-sin, sin], axis=-1)
    return cc, ss


def _mla_proj_kernel(x_ref, g_ref, win_ref, cqn_ref, ckvn_ref, wq_ref, wkv_ref, qg_ref, kg_ref,
                     cc_ref, ss_ref, qn_ref, qrp_ref, k2_ref, v_ref):
    nh = MLA_HEADS
    h = _rms(x_ref[...], g_ref[...])
    proj = _dot(h.astype(BF16), win_ref[...])
    c_q = _rms(proj[:, :Q_LORA], cqn_ref[...])
    c_kv = _rms(proj[:, Q_LORA:Q_LORA + KV_LORA], ckvn_ref[...])
    o = Q_LORA + KV_LORA
    k_r2 = proj[:, o:o + LANES]
    k_sw2 = proj[:, o + LANES:o + 2 * LANES]
    cc2 = cc_ref[...]
    ss2 = ss_ref[...]
    qg = qg_ref[...]
    kg = kg_ref[...]
    lo = lax.broadcasted_iota(jnp.int32, cc2.shape, 1) < ROPE_DIM
    scale = math.log2(math.e) / math.sqrt(QK_DIM)

    q = _dot(c_q.astype(BF16), wq_ref[...])
    q_a = cc2 * qg[1:2, :]
    q_b = ss2 * qg[2:3, :]
    for j in range(nh // 2):
        ro = nh * NOPE_DIM + j * LANES
        q_rp = q[:, ro:ro + LANES]
        q_swp = q[:, ro + nh * ROPE_DIM:ro + nh * ROPE_DIM + LANES]
        rot = q_rp * q_a + q_swp * q_b
        sq_rp = q_rp * q_rp
        rs_pair = []
        for par in range(2):
            hd = 2 * j + par
            q_n = q[:, hd * NOPE_DIM:(hd + 1) * NOPE_DIM]
            own = lo if par == 0 else jnp.logical_not(lo)
            ms = jnp.sum(q_n * q_n + jnp.where(own, sq_rp, 0.0), axis=-1, keepdims=True) / QK_DIM
            rs = lax.rsqrt(ms + RMS_EPS) * scale
            qn_ref[0, hd] = (q_n * rs * qg[0:1, :]).astype(qn_ref.dtype)
            rs_pair.append(rs)
        qrp_ref[0, j] = (rot * jnp.where(lo, rs_pair[0], rs_pair[1])).astype(qrp_ref.dtype)

    kv = _dot(c_kv.astype(BF16), wkv_ref[...])
    k_rot2 = k_r2 * (cc2 * kg[1:2, :]) + k_sw2 * (ss2 * kg[2:3, :])
    k_r_ss = jnp.sum(jnp.where(lo, k_r2 * k_r2, 0.0), axis=-1, keepdims=True)
    for hd in range(nh):
        ko = hd * (NOPE_DIM + V_DIM)
        k_n = kv[:, ko:ko + NOPE_DIM]
        ms_k = (jnp.sum(k_n * k_n, axis=-1, keepdims=True) + k_r_ss) / QK_DIM
        rs_k = lax.rsqrt(ms_k + RMS_EPS)
        own = lo if hd % 2 == 0 else jnp.logical_not(lo)
        k2_ref[0, hd, :, 0:NOPE_DIM] = (k_n * rs_k * kg[0:1, :]).astype(k2_ref.dtype)
        k2_ref[0, hd, :, NOPE_DIM:] = jnp.where(own, k_rot2 * rs_k, 0.0).astype(k2_ref.dtype)
        v_ref[0, hd] = kv[:, ko + NOPE_DIM:ko + NOPE_DIM + V_DIM].astype(v_ref.dtype)


def _swap_halves(a, axis=-1):
    h = a.shape[axis] // 2
    lo = lax.slice_in_dim(a, 0, h, axis=axis)
    hi = lax.slice_in_dim(a, h, 2 * h, axis=axis)
    return jnp.concatenate([hi, lo], axis=axis)


def _mla_proj(x, norm_g, positions, w_in, cq_norm, ckv_norm, w_uq, w_ukv, q_norm, k_norm):
    nb, seq, dm = x.shape
    nh = MLA_HEADS
    tm = ROW_TILE
    n_tok = nb * seq
    cc, ss = _rope_tables(positions)
    cc2 = jnp.concatenate([cc, cc], axis=-1)
    ss2 = jnp.concatenate([ss, ss], axis=-1)
    o = Q_LORA + KV_LORA
    w_kr = w_in[:, o:o + ROPE_DIM]
    w_sw = _swap_halves(w_kr)
    w_in_x = jnp.concatenate([w_in[:, :o], w_kr, w_kr, w_sw, w_sw], axis=1).astype(BF16)
    wq = w_uq.reshape(Q_LORA, nh, QK_DIM)
    wq_n = wq[:, :, :NOPE_DIM].reshape(Q_LORA, nh * NOPE_DIM)
    wq_r = wq[:, :, NOPE_DIM:]
    wq_x = jnp.concatenate([wq_n, wq_r.reshape(Q_LORA, nh * ROPE_DIM),
                            _swap_halves(wq_r).reshape(Q_LORA, nh * ROPE_DIM)], axis=1).astype(BF16)

    def gains(gv):
        r = gv[NOPE_DIM:]
        rs = _swap_halves(r)
        return jnp.stack([gv[:NOPE_DIM], jnp.concatenate([r, r]), jnp.concatenate([rs, rs])])

    x2 = x.reshape(n_tok, dm)
    per_b = seq // tm
    full = lambda a: pl.BlockSpec(a.shape, lambda i: (0,) * a.ndim)
    args = (x2, norm_g[None, :], w_in_x, cq_norm[None, :], ckv_norm[None, :], wq_x,
            w_ukv.astype(BF16), gains(q_norm), gains(k_norm), cc2, ss2)
    in_specs = [pl.BlockSpec((tm, dm), lambda i: (i, 0))] + [full(a) for a in args[1:9]] + [
        pl.BlockSpec((tm, LANES), lambda i: (i, 0)), pl.BlockSpec((tm, LANES), lambda i: (i, 0))]

    def head_spec(n_heads, dh):
        return pl.BlockSpec((1, n_heads, tm, dh), lambda i: (i // per_b, 0, i % per_b, 0))

    def head_shape(n_heads, dh):
        return jax.ShapeDtypeStruct((nb, n_heads, seq, dh), BF16)

    return pl.pallas_call(
        _mla_proj_kernel,
        out_shape=(head_shape(nh, NOPE_DIM), head_shape(nh // 2, LANES), head_shape(nh, NOPE_DIM + LANES),
                   head_shape(nh, V_DIM)),
        grid=(n_tok // tm,),
        in_specs=in_specs,
        out_specs=(head_spec(nh, NOPE_DIM), head_spec(nh // 2, LANES), head_spec(nh, NOPE_DIM + LANES),
                   head_spec(nh, V_DIM)),
        compiler_params=_cparams("parallel"),
        name="mla_proj",
    )(*args)


def _attention_kernel(qn_ref, qrp_ref, k2_ref, v_ref, o_ref):
    seq = qn_ref.shape[2]
    t = ATT_TQ
    nt = (((1,), (1,)), ((), ()))
    tri = (lax.broadcasted_iota(jnp.int32, (t, t), 1) <= lax.broadcasted_iota(jnp.int32, (t, t), 0))
    for i in range(seq // t):
        q = jnp.concatenate([qn_ref[0, 0, i * t:(i + 1) * t, :], qrp_ref[0, 0, i * t:(i + 1) * t, :]], axis=-1)
        m = jnp.full((t, 1), NEG_INF, F32)
        l = jnp.zeros((t, 1), F32)
        acc = jnp.zeros((t, V_DIM), F32)
        for j in range(i + 1):
            s = lax.dot_general(q, k2_ref[0, 0, j * t:(j + 1) * t, :], nt, preferred_element_type=F32)
            if j == i:
                s = jnp.where(tri, s, NEG_INF)
            m_new = jnp.maximum(m, jnp.max(s, axis=-1, keepdims=True))
            alpha = jnp.exp2(m - m_new)
            p = jnp.exp2(s - m_new)
            l = alpha * l + jnp.sum(p, axis=-1, keepdims=True)
            acc = alpha * acc + _dot(p.astype(BF16), v_ref[0, 0, j * t:(j + 1) * t, :])
            m = m_new
        o_ref[0, i * t:(i + 1) * t, :] = (acc / l).astype(o_ref.dtype)


def _attention(qn, qrp, k2, v):
    nb, nh, seq, _ = qn.shape

    def spec(dh, per=1):
        return pl.BlockSpec((1, 1, seq, dh), lambda b, h: (b, h // per, 0, 0))

    return pl.pallas_call(
        _attention_kernel,
        out_shape=jax.ShapeDtypeStruct((nb, seq, nh * V_DIM), BF16),
        grid=(nb, nh),
        in_specs=[spec(NOPE_DIM), spec(LANES, 2), spec(NOPE_DIM + LANES), spec(V_DIM)],
        out_specs=pl.BlockSpec((1, seq, V_DIM), lambda b, h: (b, 0, h)),
        compiler_params=_cparams("parallel", "parallel"),
        name="mla_attention",
    )(qn, qrp, k2, v)


def _proj_residual_kernel(a_ref, w_ref, x_ref, o_ref):
    o_ref[...] = x_ref[...] + _dot(a_ref[...], w_ref[...])


def _proj_residual(a, w, x2):
    n_tok, dm = x2.shape
    tm = ROW_TILE
    return pl.pallas_call(
        _proj_residual_kernel,
        out_shape=jax.ShapeDtypeStruct(x2.shape, x2.dtype),
        grid=(n_tok // tm,),
        in_specs=[pl.BlockSpec((tm, a.shape[1]), lambda i: (i, 0)),
                  pl.BlockSpec(w.shape, lambda i: (0, 0)),
                  pl.BlockSpec((tm, dm), lambda i: (i, 0))],
        out_specs=pl.BlockSpec((tm, dm), lambda i: (i, 0)),
        compiler_params=_cparams("parallel"),
        name="proj_residual",
    )(a, w, x2)


def _mla_mixer(x, norm_g, positions, w_in, cq_norm, ckv_norm, w_uq, w_ukv, q_norm, k_norm, w_o):
    nb, seq, dm = x.shape
    qn, qrp, k2, v = _mla_proj(x, norm_g, positions, w_in, cq_norm, ckv_norm, w_uq, w_ukv, q_norm, k_norm)
    o = _attention(qn, qrp, k2, v)
    out = _proj_residual(o.reshape(nb * seq, -1), w_o.astype(BF16), x.reshape(nb * seq, dm))
    return out.reshape(nb, seq, dm)


def _router_kernel(x_ref, g_ref, w_ref, b_ref, hn_ref, meta_ref, gate_ref, cnt_ref, base_ref):
    tm, dm = x_ref.shape
    ne = N_EXPERTS

    @pl.when(pl.program_id(0) == 0)
    def _():
        base_ref[...] = jnp.zeros_like(base_ref)

    hn = _rms(x_ref[...], g_ref[...])
    for c in range(ROW_CHUNKS):
        hn_ref[pl.ds(c, tm, stride=ROW_CHUNKS), :] = hn[:, c * LANES:(c + 1) * LANES]
    logits = _dot(hn.astype(BF16), w_ref[...]) + b_ref[...]
    lane = lax.broadcasted_iota(jnp.int32, logits.shape, 1)
    big = jnp.int32(1 << 20)

    def first_argmax(v):
        m = jnp.max(v, axis=-1, keepdims=True)
        idx = jnp.min(jnp.where(v == m, lane, big), axis=-1, keepdims=True)
        return m, idx

    is_g = (lane >= ne) & (lane < ne + N_GROUPS)
    gl = jnp.where(is_g, logits, NEG_INF)
    g_max, g_idx = first_argmax(gl)
    g_sel = g_idx - ne
    p_g = 1.0 / jnp.sum(jnp.where(is_g, jnp.exp(gl - g_max), 0.0), axis=-1, keepdims=True)
    in_grp = (lane >= g_sel * EXPERTS_PER_GROUP) & (lane < (g_sel + 1) * EXPERTS_PER_GROUP)
    el = jnp.where(in_grp, logits, NEG_INF)
    v0, e0 = first_argmax(el)
    v1, e1 = first_argmax(jnp.where(lane == e0, NEG_INF, el))
    t = jnp.exp(v1 - v0)
    gate0 = p_g / (1.0 + t)
    gate1 = p_g * t / (1.0 + t)

    oh0 = (lane == e0).astype(F32)
    oh1 = (lane == e1).astype(F32)
    both = oh0 + oh1
    r_i = lax.broadcasted_iota(jnp.int32, (tm, tm), 0)
    c_i = lax.broadcasted_iota(jnp.int32, (tm, tm), 1)
    strict_lower = (c_i < r_i).astype(BF16)
    before = _dot(strict_lower, both.astype(BF16)) + base_ref[0:1, :]
    rank0 = jnp.sum(oh0 * before, axis=-1, keepdims=True)
    rank1 = jnp.sum(oh1 * before, axis=-1, keepdims=True)
    base_ref[0:1, :] = base_ref[0:1, :] + jnp.sum(both, axis=0, keepdims=True)
    cnt_ref[...] = jnp.broadcast_to(base_ref[0:1, :], cnt_ref.shape).astype(jnp.int32)

    meta = jnp.where(lane == 0, e0, jnp.where(lane == 1, e1, jnp.where(
        lane == 2, rank0.astype(jnp.int32), jnp.where(lane == 3, rank1.astype(jnp.int32), 0))))
    meta_ref[...] = meta
    gate_ref[...] = jnp.where(lane == 0, gate0, jnp.where(lane == 1, gate1, 0.0))


def _router(x2, norm_g, w_group, b_group, w_expert, b_expert):
    n_tok, dm = x2.shape
    tm = ROW_TILE
    pad = LANES - N_EXPERTS - N_GROUPS
    w = jnp.concatenate([w_expert, w_group, jnp.zeros((dm, pad), F32)], axis=1).astype(BF16)
    b = jnp.concatenate([b_expert, b_group, jnp.zeros((pad,), F32)])[None, :]
    return pl.pallas_call(
        _router_kernel,
        out_shape=(jax.ShapeDtypeStruct((n_tok * ROW_CHUNKS, LANES), F32),
                   jax.ShapeDtypeStruct((n_tok, LANES), jnp.int32),
                   jax.ShapeDtypeStruct((n_tok, LANES), F32),
                   jax.ShapeDtypeStruct((SUBLANES, LANES), jnp.int32)),
        grid=(n_tok // tm,),
        in_specs=[pl.BlockSpec((tm, dm), lambda i: (i, 0)),
                  pl.BlockSpec((1, dm), lambda i: (0, 0)),
                  pl.BlockSpec((dm, LANES), lambda i: (0, 0)),
                  pl.BlockSpec((1, LANES), lambda i: (0, 0))],
        out_specs=(pl.BlockSpec((tm * ROW_CHUNKS, LANES), lambda i: (i, 0)),
                   pl.BlockSpec((tm, LANES), lambda i: (i, 0)),
                   pl.BlockSpec((tm, LANES), lambda i: (i, 0)),
                   pl.BlockSpec((SUBLANES, LANES), lambda i: (0, 0))),
        scratch_shapes=[pltpu.VMEM((SUBLANES, LANES), F32)],
        compiler_params=_cparams("arbitrary"),
        name="moe_router",
    )(x2, norm_g[None, :], w, b)


def _dispatch_kernel(dest_ref, hn_ref, xd_ref, zero_ref, sem, zsem):
    tm = hn_ref.shape[0] // ROW_CHUNKS
    n_assign = pl.num_programs(0) * tm * TOP_K

    def tail_copy():
        return pltpu.make_async_copy(zero_ref, xd_ref.at[pl.ds(n_assign * ROW_CHUNKS, zero_ref.shape[0])], zsem)

    @pl.when(pl.program_id(0) == 0)
    def _():
        zero_ref[...] = jnp.zeros_like(zero_ref)
        tail_copy().start()

    def row_copy(t, k):
        src = pl.multiple_of(t * ROW_CHUNKS, ROW_CHUNKS)
        dst = pl.multiple_of(dest_ref[0, 0, 2 * t + k], ROW_CHUNKS)
        return pltpu.make_async_copy(hn_ref.at[pl.ds(src, ROW_CHUNKS)], xd_ref.at[pl.ds(dst, ROW_CHUNKS)], sem)

    def issue(i, c):
        for u in range(DMA_UNROLL):
            for k in range(TOP_K):
                row_copy(i * DMA_UNROLL + u, k).start(priority=k)
        return c

    lax.fori_loop(0, tm // DMA_UNROLL, issue, 0)

    def drain(i, c):
        for u in range(DMA_UNROLL):
            for k in range(TOP_K):
                row_copy(i * DMA_UNROLL + u, k).wait()
        return c

    lax.fori_loop(0, tm // DMA_UNROLL, drain, 0)

    @pl.when(pl.program_id(0) == 0)
    def _():
        tail_copy().wait()


def _dispatch(hn2, dest_off):
    tm = GATHER_TM
    n_tok = hn2.shape[0] // ROW_CHUNKS
    dest3 = dest_off.reshape(n_tok // tm, 1, 2 * tm)
    n_rows = n_tok * TOP_K + MOE_BM
    return pl.pallas_call(
        _dispatch_kernel,
        out_shape=jax.ShapeDtypeStruct((n_rows * ROW_CHUNKS, LANES), hn2.dtype),
        grid=(n_tok // tm,),
        in_specs=[pl.BlockSpec((1, 1, 2 * tm), lambda i: (i, 0, 0), memory_space=pltpu.SMEM),
                  pl.BlockSpec((tm * ROW_CHUNKS, LANES), lambda i: (i, 0))],
        out_specs=pl.BlockSpec(memory_space=pl.ANY),
        scratch_shapes=[pltpu.VMEM((MOE_BM * ROW_CHUNKS, LANES), hn2.dtype),
                        pltpu.SemaphoreType.DMA, pltpu.SemaphoreType.DMA],
        compiler_params=_cparams("arbitrary"),
        name="moe_dispatch",
    )(dest3, hn2)


def _rows_from_tiles(ref, n, base=0):
    return jnp.concatenate([ref[pl.ds(base + c, n, stride=ROW_CHUNKS), :] for c in range(ROW_CHUNKS)], axis=-1)


def _expert_kernel(be_ref, bs_ref, first_ref, nxt_ref, nu_ref, xd_ref, w1_ref, w3_ref, w2_ref, yd_ref,
                   xbuf, ybuf, w1buf, w3buf, w2buf, w13_s, w2_s, sem_in, sem_out, sem_w, *, layer):
    nu = nu_ref[0]
    n_rows = xbuf.shape[1]
    bm = n_rows // ROW_CHUNKS
    de = w1buf.shape[2]

    def in_copy(blk, sl):
        src = pl.multiple_of(bs_ref[blk] * ROW_CHUNKS, ROW_CHUNKS)
        return pltpu.make_async_copy(xd_ref.at[pl.ds(src, n_rows)], xbuf.at[sl], sem_in.at[sl])

    def out_copy(blk, sl):
        dst = pl.multiple_of(bs_ref[blk] * ROW_CHUNKS, ROW_CHUNKS)
        return pltpu.make_async_copy(ybuf.at[sl], yd_ref.at[pl.ds(dst, n_rows)], sem_out.at[sl])

    def w_copies(e, ws):
        return (pltpu.make_async_copy(w1_ref.at[layer, e], w1buf.at[ws], sem_w.at[ws]),
                pltpu.make_async_copy(w3_ref.at[layer, e], w3buf.at[ws], sem_w.at[ws]),
                pltpu.make_async_copy(w2_ref.at[layer, e], w2buf.at[ws], sem_w.at[ws]))

    in_copy(0, 0).start()
    for cp in w_copies(be_ref[0], 1):
        cp.start()

    def block(b, ws):
        slot = b % 2

        @pl.when(b + 1 < nu)
        def _():
            in_copy(b + 1, 1 - slot).start()

        first = first_ref[b] == 1
        ws = jnp.where(first, 1 - ws, ws)

        @pl.when(first)
        def _():
            for cp in w_copies(be_ref[b], ws):
                cp.wait()
            w13_s[:, :de] = w1buf[ws].astype(BF16)
            w13_s[:, de:] = w3buf[ws].astype(BF16)
            w2_s[...] = w2buf[ws].astype(BF16)

            @pl.when(nxt_ref[b] >= 0)
            def _():
                for cp in w_copies(nxt_ref[b], 1 - ws):
                    cp.start()

        in_copy(b, slot).wait()
        x = _rows_from_tiles(xbuf.at[slot], bm).astype(BF16)
        h = _dot(x, w13_s[...])
        a = jax.nn.silu(h[:, :de]) * h[:, de:]
        y = _dot(a.astype(BF16), w2_s[...])
        yb = ybuf.at[slot]
        for c in range(ROW_CHUNKS):
            yb[pl.ds(c, bm, stride=ROW_CHUNKS), :] = y[:, c * LANES:(c + 1) * LANES]

        @pl.when(b >= 1)
        def _():
            out_copy(b - 1, 1 - slot).wait()
        out_copy(b, slot).start()
        return ws

    lax.fori_loop(0, nu, block, jnp.int32(0))

    last = nu - 1
    out_copy(last, last % 2).wait()
    tail = yd_ref.shape[0] - n_rows
    fill_slot = 1 - last % 2
    ybuf[fill_slot] = jnp.zeros((n_rows, LANES), F32)
    fill = pltpu.make_async_copy(ybuf.at[fill_slot], yd_ref.at[pl.ds(tail, n_rows)], sem_out.at[fill_slot])
    fill.start()
    fill.wait()


def _experts(xd, blk_expert, blk_start, blk_first, blk_next, n_used, w1, w3, w2, layer):
    bm = MOE_BM
    _, _, dm, de = w1.shape
    any_spec = pl.BlockSpec(memory_space=pl.ANY)
    grid_spec = pltpu.PrefetchScalarGridSpec(
        num_scalar_prefetch=5,
        grid=(1,),
        in_specs=[any_spec, any_spec, any_spec, any_spec],
        out_specs=any_spec,
        scratch_shapes=[pltpu.VMEM((2, bm * ROW_CHUNKS, LANES), F32),
                        pltpu.VMEM((2, bm * ROW_CHUNKS, LANES), F32),
                        pltpu.VMEM((2, dm, de), F32),
                        pltpu.VMEM((2, dm, de), F32),
                        pltpu.VMEM((2, de, dm), F32),
                        pltpu.VMEM((dm, 2 * de), BF16),
                        pltpu.VMEM((de, dm), BF16),
                        pltpu.SemaphoreType.DMA((2,)),
                        pltpu.SemaphoreType.DMA((2,)),
                        pltpu.SemaphoreType.DMA((2,))],
    )
    return pl.pallas_call(
        functools.partial(_expert_kernel, layer=layer),
        out_shape=jax.ShapeDtypeStruct(xd.shape, F32),
        grid_spec=grid_spec,
        compiler_params=_cparams("arbitrary"),
        name="moe_experts",
    )(blk_expert, blk_start, blk_first, blk_next, n_used, xd, w1, w3, w2)


def _combine_kernel(dest_ref, dest_next_ref, yd_ref, gate_ref, x_ref, o_ref, buf_ref, sem):
    tm, dm = x_ref.shape
    i = pl.program_id(0)
    n_steps = pl.num_programs(0)
    slot = i % 2
    k_rows = tm * ROW_CHUNKS

    def row_copy(d_ref, sl, t, k):
        src = pl.multiple_of(d_ref[0, 0, 2 * t + k], ROW_CHUNKS)
        dst = pl.multiple_of(k * k_rows + t * ROW_CHUNKS, ROW_CHUNKS)
        return pltpu.make_async_copy(yd_ref.at[pl.ds(src, ROW_CHUNKS)],
                                     buf_ref.at[sl, pl.ds(dst, ROW_CHUNKS)], sem.at[sl])

    def issue_all(d_ref, sl):
        def issue(j, c):
            for u in range(DMA_UNROLL):
                for k in range(TOP_K):
                    row_copy(d_ref, sl, j * DMA_UNROLL + u, k).start(priority=k)
            return c
        lax.fori_loop(0, tm // DMA_UNROLL, issue, 0)

    @pl.when(i == 0)
    def _():
        issue_all(dest_ref, slot)

    @pl.when(i + 1 < n_steps)
    def _():
        issue_all(dest_next_ref, 1 - slot)

    def drain(j, c):
        for u in range(DMA_UNROLL):
            for k in range(TOP_K):
                row_copy(dest_ref, slot, j * DMA_UNROLL + u, k).wait()
        return c

    lax.fori_loop(0, tm // DMA_UNROLL, drain, 0)

    g0 = gate_ref[:, 0:1]
    g1 = gate_ref[:, 1:2]
    buf = buf_ref.at[slot]
    for c in range(ROW_CHUNKS):
        y0 = buf[pl.ds(c, tm, stride=ROW_CHUNKS), :]
        y1 = buf[pl.ds(k_rows + c, tm, stride=ROW_CHUNKS), :]
        o_ref[:, c * LANES:(c + 1) * LANES] = x_ref[:, c * LANES:(c + 1) * LANES] + (g0 * y0 + g1 * y1)


def _combine(yd, dest_off, gates, x2):
    n_tok, dm = x2.shape
    tm = GATHER_TM
    n_steps = n_tok // tm
    dest3 = dest_off.reshape(n_steps, 1, 2 * tm)
    return pl.pallas_call(
        _combine_kernel,
        out_shape=jax.ShapeDtypeStruct(x2.shape, x2.dtype),
        grid=(n_steps,),
        in_specs=[pl.BlockSpec((1, 1, 2 * tm), lambda i: (i, 0, 0), memory_space=pltpu.SMEM),
                  pl.BlockSpec((1, 1, 2 * tm), lambda i: (jnp.minimum(i + 1, n_steps - 1), 0, 0),
                               memory_space=pltpu.SMEM),
                  pl.BlockSpec(memory_space=pl.ANY),
                  pl.BlockSpec((tm, LANES), lambda i: (i, 0)),
                  pl.BlockSpec((tm, dm), lambda i: (i, 0))],
        out_specs=pl.BlockSpec((tm, dm), lambda i: (i, 0)),
        scratch_shapes=[pltpu.VMEM((2, TOP_K * tm * ROW_CHUNKS, LANES), F32),
                        pltpu.SemaphoreType.DMA((2,))],
        compiler_params=_cparams("arbitrary"),
        name="moe_combine",
    )(dest3, dest3, yd, gates, x2)


def _moe(x, norm_g, w_group, b_group, w_expert, b_expert, w1, w3, w2, layer):
    nb, seq, dm = x.shape
    n_tok = nb * seq
    n_assign = n_tok * TOP_K
    x2 = x.reshape(n_tok, dm)
    hn2, meta, gates, counts = _router(x2, norm_g, w_group, b_group, w_expert, b_expert)
    bm = MOE_BM
    n_blocks = n_assign // bm + N_EXPERTS
    counts = counts[0, :N_EXPERTS]
    starts = jnp.cumsum(counts) - counts
    eid = meta[:, 0:TOP_K]
    e_iota = jnp.arange(N_EXPERTS, dtype=jnp.int32)
    seg_start = jnp.sum(jnp.where(eid[:, :, None] == e_iota, starts, 0), axis=-1)
    dest_off = ((seg_start + meta[:, TOP_K:2 * TOP_K]) * ROW_CHUNKS).astype(jnp.int32).reshape(-1)
    nblk = (counts + bm - 1) // bm
    blk_end = jnp.cumsum(nblk)
    b_iota = jnp.arange(n_blocks, dtype=jnp.int32)
    blk_expert = jnp.minimum(jnp.sum((b_iota[:, None] >= blk_end[None, :]).astype(jnp.int32), axis=1),
                             N_EXPERTS - 1)
    j = b_iota - (blk_end - nblk)[blk_expert]
    n_used = blk_end[-1].astype(jnp.int32).reshape(1)
    blk_start = jnp.where(b_iota < n_used[0], starts[blk_expert] + j * bm, 0).astype(jnp.int32)
    blk_first = (j == 0).astype(jnp.int32)
    cand = jnp.where(counts > 0, e_iota, N_EXPERTS)
    later = lax.cummin(cand[::-1])[::-1]
    next_e = jnp.concatenate([later[1:], jnp.full((1,), N_EXPERTS, jnp.int32)])
    blk_next = jnp.where(next_e[blk_expert] < N_EXPERTS, next_e[blk_expert], -1).astype(jnp.int32)

    xd = _dispatch(hn2, dest_off)
    yd = _experts(xd, blk_expert, blk_start, blk_first, blk_next, n_used, w1, w3, w2, layer)
    out = _combine(yd, dest_off, gates, x2)
    return out.reshape(nb, seq, dm)


def kernel(x, positions, ev_norm, ev_w_in, s5_lambda_re, s5_lambda_im, s5_log_dt, s5_b_re, s5_b_im,
           s5_c_re, s5_c_im, s5_d, s5_glu_w, s5_glu_b, lru_conv_w, lru_conv_b, lru_wa, lru_ba, lru_wi,
           lru_bi, lru_lambda, ev_w_out, od_norm, mla_w_in, mla_cq_norm, mla_ckv_norm, mla_w_uq,
           mla_w_ukv, mla_q_norm, mla_k_norm, mla_w_o, ffn_norm, moe_w_group, moe_b_group,
           moe_w_expert, moe_b_expert, moe_w1, moe_w3, moe_w2):
    depth = ffn_norm.shape[0]
    for layer in range(depth):
        j = layer // 2
        if layer % 2 == 0:
            x = _even_mixer(x, ev_norm[j], ev_w_in[j], s5_lambda_re[j], s5_lambda_im[j], s5_log_dt[j],
                            s5_b_re[j], s5_b_im[j], s5_c_re[j], s5_c_im[j], s5_d[j], s5_glu_w[j],
                            s5_glu_b[j], lru_conv_w[j], lru_conv_b[j], lru_wa[j], lru_ba[j], lru_wi[j],
                            lru_bi[j], lru_lambda[j], ev_w_out[j])
        else:
            x = _mla_mixer(x, od_norm[j], positions, mla_w_in[j], mla_cq_norm[j], mla_ckv_norm[j],
                           mla_w_uq[j], mla_w_ukv[j], mla_q_norm[j], mla_k_norm[j], mla_w_o[j])
        x = _moe(x, ffn_norm[layer], moe_w_group[layer], moe_b_group[layer], moe_w_expert[layer],
                 moe_b_expert[layer], moe_w1, moe_w3, moe_w2, layer)
    return x
```

```python
import functools
import math

import jax
import jax.numpy as jnp
from jax import lax
from jax.experimental import pallas as pl
from jax.experimental.pallas import tpu as pltpu

F32 = jnp.float32
BF16 = jnp.bfloat16

S5_GROUP_CH = 16
S5_STATE = 64
LRU_BLOCKS = 8
CONV_WIDTH = 4
RG_C = 8.0
MLA_HEADS = 8
Q_LORA = 384
KV_LORA = 256
NOPE_DIM = 128
ROPE_DIM = 64
V_DIM = 128
QK_DIM = NOPE_DIM + ROPE_DIM
ROPE_THETA = 10000.0
N_GROUPS = 8
EXPERTS_PER_GROUP = 8
N_EXPERTS = N_GROUPS * EXPERTS_PER_GROUP
TOP_K = 2
RMS_EPS = 1e-6
NEG_INF = -1e30

SUBLANES = 8
LANES = 128
ROW_CHUNKS = 8
VMEM_LIMIT = 48 * 1024 * 1024

EVEN_TL = 64
ROW_TILE = 512
MOE_BM = 256
ATT_TQ = 512
GATHER_TM = 256
DMA_UNROLL = 8


def _cparams(*sem):
    return pltpu.CompilerParams(dimension_semantics=tuple(sem), vmem_limit_bytes=VMEM_LIMIT)


def _rms(x, g):
    ms = jnp.mean(x * x, axis=-1, keepdims=True)
    return x * lax.rsqrt(ms + RMS_EPS) * g


def _dot(a, b):
    return jnp.dot(a, b, preferred_element_type=F32)


def _s5_discretize_kernel(lre_ref, lim_ref, ldt_ref, bre_ref, bim_ref,
                          lbre_ref, lbim_ref, bbre_ref, bbim_ref):
    lr = jnp.minimum(lre_ref[...], -1e-4)
    li = lim_ref[...]
    dt = jnp.exp(ldt_ref[...])
    mag = jnp.exp(lr * dt)
    lb_re = mag * jnp.cos(li * dt)
    lb_im = mag * jnp.sin(li * dt)
    den = lr * lr + li * li
    num_re = lb_re - 1.0
    f_re = (num_re * lr + lb_im * li) / den
    f_im = (lb_im * lr - num_re * li) / den
    lbre_ref[...] = lb_re
    lbim_ref[...] = lb_im
    br = bre_ref[...]
    bi = bim_ref[...]
    bbre_ref[...] = f_re[:, None, :] * br - f_im[:, None, :] * bi
    bbim_ref[...] = f_re[:, None, :] * bi + f_im[:, None, :] * br


def _s5_discretize(lam_re, lam_im, log_dt, b_re, b_im):
    g, p = lam_re.shape
    h = b_re.shape[-1]
    b_re_t = jnp.swapaxes(b_re, 1, 2)
    b_im_t = jnp.swapaxes(b_im, 1, 2)
    return pl.pallas_call(
        _s5_discretize_kernel,
        out_shape=(jax.ShapeDtypeStruct((g, p), F32), jax.ShapeDtypeStruct((g, p), F32),
                   jax.ShapeDtypeStruct((g, h, p), F32), jax.ShapeDtypeStruct((g, h, p), F32)),
        name="s5_discretize",
    )(lam_re, lam_im, log_dt[:, None], b_re_t, b_im_t)


def _even_inproj_kernel(x_ref, g_ref, w_ref, o_ref, h_ref):
    nb, tl, _ = x_ref.shape
    nc = h_ref.shape[0]
    g = g_ref[...]
    for b in range(nb):
        h = _rms(x_ref[b], g)
        for c in range(nc):
            h_ref[c, pl.ds(b, tl, stride=nb), :] = h[:, c * LANES:(c + 1) * LANES]
    h_all = jnp.concatenate([h_ref[c] for c in range(nc)], axis=-1)
    o_ref[...] = _dot(h_all.astype(BF16), w_ref[...]).astype(o_ref.dtype)


def _even_inproj(x, g, w):
    nb, seq, d = x.shape
    n_out = w.shape[1]
    tl = EVEN_TL
    return pl.pallas_call(
        _even_inproj_kernel,
        out_shape=jax.ShapeDtypeStruct((seq * nb, n_out), BF16),
        grid=(seq // tl,),
        in_specs=[pl.BlockSpec((nb, tl, d), lambda i: (0, i, 0)),
                  pl.BlockSpec((1, d), lambda i: (0, 0)),
                  pl.BlockSpec((d, n_out), lambda i: (0, 0))],
        out_specs=pl.BlockSpec((tl * nb, n_out), lambda i: (i, 0)),
        scratch_shapes=[pltpu.VMEM((d // LANES, tl * nb, LANES), F32)],
        compiler_params=_cparams("parallel"),
        name="even_inproj",
    )(x, g, w)


def _s5_kernel(u_ref, bm_ref, lre_ref, lim_ref, cm_ref, d_ref, gw_ref, gb_ref, o_ref,
               bu_ref, sb_ref, st_ref, *, nb):
    rows, width = u_ref.shape
    n_half = bm_ref.shape[0]
    kin = width // n_half
    ncol = bm_ref.shape[2]
    nre = ncol // 2
    chunk = 512
    steps = rows // nb

    @pl.when(pl.program_id(0) == 0)
    def _():
        st_ref[...] = jnp.zeros_like(st_ref)

    u = u_ref[...]
    for j in range(n_half):
        bu_ref[:, j * ncol:(j + 1) * ncol] = _dot(u[:, j * kin:(j + 1) * kin], bm_ref[j])

    for j in range(n_half):
        for c in range(nre // chunk):
            cr = j * ncol + c * chunk
            ci = cr + nre
            lc = j * nre + c * chunk
            lr = lre_ref[:, lc:lc + chunk]
            li = lim_ref[:, lc:lc + chunk]

            def body(i, carry, cr=cr, ci=ci, lr=lr, li=li):
                sr, si = carry
                r0 = pl.multiple_of(i * (2 * nb), 2 * nb)
                sr1 = lr * sr - li * si + bu_ref[pl.ds(r0, nb), cr:cr + chunk]
                si1 = lr * si + li * sr + bu_ref[pl.ds(r0, nb), ci:ci + chunk]
                sr2 = lr * sr1 - li * si1 + bu_ref[pl.ds(r0 + nb, nb), cr:cr + chunk]
                si2 = lr * si1 + li * sr1 + bu_ref[pl.ds(r0 + nb, nb), ci:ci + chunk]
                sb_ref[pl.ds(r0, 2 * nb), cr:cr + chunk] = jnp.concatenate([sr1, sr2], 0).astype(BF16)
                sb_ref[pl.ds(r0, 2 * nb), ci:ci + chunk] = jnp.concatenate([si1, si2], 0).astype(BF16)
                return sr2, si2

            sr, si = lax.fori_loop(0, steps // 2, body,
                                   (st_ref[:, cr:cr + chunk], st_ref[:, ci:ci + chunk]))
            st_ref[:, cr:cr + chunk] = sr
            st_ref[:, ci:ci + chunk] = si

    ys = [_dot(sb_ref[:, j * ncol:(j + 1) * ncol], cm_ref[j]) for j in range(n_half)]
    y = jnp.concatenate(ys, axis=-1) + d_ref[...] * u.astype(F32)
    y = jax.nn.gelu(y)
    z = _dot(y.astype(BF16), gw_ref[...]) + gb_ref[...]
    o_ref[...] = (y * jax.nn.sigmoid(z)).astype(o_ref.dtype)


def _s5_mixer(proj, nb, bm, lre, lim, cm, d, glu_w, glu_b):
    rows_total = proj.shape[0]
    width = d.shape[1]
    rows = EVEN_TL * nb
    n_half, _, ncol = bm.shape
    kern = functools.partial(_s5_kernel, nb=nb)
    return pl.pallas_call(
        kern,
        out_shape=jax.ShapeDtypeStruct((rows_total, width), BF16),
        grid=(rows_total // rows,),
        in_specs=[pl.BlockSpec((rows, width), lambda i: (i, 0)),
                  pl.BlockSpec(bm.shape, lambda i: (0, 0, 0)),
                  pl.BlockSpec(lre.shape, lambda i: (0, 0)),
                  pl.BlockSpec(lim.shape, lambda i: (0, 0)),
                  pl.BlockSpec(cm.shape, lambda i: (0, 0, 0)),
                  pl.BlockSpec((1, width), lambda i: (0, 0)),
                  pl.BlockSpec(glu_w.shape, lambda i: (0, 0)),
                  pl.BlockSpec((1, width), lambda i: (0, 0))],
        out_specs=pl.BlockSpec((rows, width), lambda i: (i, 0)),
        scratch_shapes=[pltpu.VMEM((rows, n_half * ncol), F32),
                        pltpu.VMEM((rows, n_half * ncol), BF16),
                        pltpu.VMEM((nb, n_half * ncol), F32)],
        compiler_params=_cparams("arbitrary"),
        name="s5_mixer",
    )(proj, bm, lre, lim, cm, d, glu_w, glu_b)


def _lru_kernel(x_ref, gate_ref, cw_ref, cb_ref, wa_ref, ba_ref, wi_ref, bi_ref, lam_ref, o_ref,
                xp_ref, a_ref, b_ref, h_ref, *, nb):
    rows, width = x_ref.shape
    halo = (CONV_WIDTH - 1) * nb
    steps = rows // nb

    @pl.when(pl.program_id(0) == 0)
    def _():
        xp_ref[0:halo, :] = jnp.zeros((halo, width), F32)
        h_ref[...] = jnp.zeros_like(h_ref)

    xp_ref[halo:halo + rows, :] = x_ref[...].astype(F32)
    xf = cb_ref[...] + cw_ref[0:1, :] * xp_ref[0:rows, :]
    for k in range(1, CONV_WIDTH):
        xf = xf + cw_ref[k:k + 1, :] * xp_ref[k * nb:k * nb + rows, :]
    xp_ref[0:halo, :] = xp_ref[rows:rows + halo, :]

    xb = xf.astype(BF16)
    nblk = wa_ref.shape[0]
    kb = width // nblk
    ga = jnp.concatenate([_dot(xb[:, j * kb:(j + 1) * kb], wa_ref[j]) for j in range(nblk)], -1)
    gi = jnp.concatenate([_dot(xb[:, j * kb:(j + 1) * kb], wi_ref[j]) for j in range(nblk)], -1)
    r = jax.nn.sigmoid(ga + ba_ref[...])
    ig = jax.nn.sigmoid(gi + bi_ref[...])
    log_a = RG_C * r * jax.nn.log_sigmoid(lam_ref[...])
    a_ref[...] = jnp.exp(log_a)
    th = jnp.tanh(log_a)
    b_ref[...] = jnp.sqrt(-2.0 * th / (1.0 - th)) * (ig * xf)

    def body(t, h):
        r0 = pl.multiple_of(t * nb, nb)
        h = a_ref[pl.ds(r0, nb), :] * h + b_ref[pl.ds(r0, nb), :]
        b_ref[pl.ds(r0, nb), :] = h
        return h

    h_ref[...] = lax.fori_loop(0, steps, body, h_ref[...])
    o_ref[...] = (b_ref[...] * jax.nn.gelu(gate_ref[...].astype(F32))).astype(o_ref.dtype)


def _lru_mixer(proj, nb, conv_w, conv_b, wa, ba, wi, bi, lam):
    rows_total = proj.shape[0]
    width = conv_w.shape[1]
    rows = EVEN_TL * nb
    halo = (CONV_WIDTH - 1) * nb
    kern = functools.partial(_lru_kernel, nb=nb)
    vec = pl.BlockSpec((1, width), lambda i: (0, 0))
    return pl.pallas_call(
        kern,
        out_shape=jax.ShapeDtypeStruct((rows_total, width), BF16),
        grid=(rows_total // rows,),
        in_specs=[pl.BlockSpec((rows, width), lambda i: (i, 1)),
                  pl.BlockSpec((rows, width), lambda i: (i, 2)),
                  pl.BlockSpec(conv_w.shape, lambda i: (0, 0)),
                  vec,
                  pl.BlockSpec(wa.shape, lambda i: (0, 0, 0)),
                  vec,
                  pl.BlockSpec(wi.shape, lambda i: (0, 0, 0)),
                  vec, vec],
        out_specs=pl.BlockSpec((rows, width), lambda i: (i, 0)),
        scratch_shapes=[pltpu.VMEM((rows + halo, width), F32),
                        pltpu.VMEM((rows, width), F32),
                        pltpu.VMEM((rows, width), F32),
                        pltpu.VMEM((nb, width), F32)],
        compiler_params=_cparams("arbitrary"),
        name="lru_mixer",
    )(proj, proj, conv_w, conv_b, wa, ba, wi, bi, lam)


def _even_outproj_kernel(ya_ref, yb_ref, wa_ref, wb_ref, x_ref, o_ref, y_ref):
    nb, tl, _ = x_ref.shape
    nc = y_ref.shape[0]
    y = _dot(ya_ref[...], wa_ref[...]) + _dot(yb_ref[...], wb_ref[...])
    for c in range(nc):
        y_ref[c] = y[:, c * LANES:(c + 1) * LANES]
    for b in range(nb):
        for c in range(nc):
            o_ref[b, :, c * LANES:(c + 1) * LANES] = (x_ref[b, :, c * LANES:(c + 1) * LANES]
                                                     + y_ref[c, pl.ds(b, tl, stride=nb), :])


def _even_outproj(ya, yb, w_a, w_b, x):
    nb, seq, d = x.shape
    tl = EVEN_TL
    rows = tl * nb
    wd = ya.shape[1]
    return pl.pallas_call(
        _even_outproj_kernel,
        out_shape=jax.ShapeDtypeStruct(x.shape, x.dtype),
        grid=(seq // tl,),
        in_specs=[pl.BlockSpec((rows, wd), lambda i: (i, 0)),
                  pl.BlockSpec((rows, wd), lambda i: (i, 0)),
                  pl.BlockSpec(w_a.shape, lambda i: (0, 0)),
                  pl.BlockSpec(w_b.shape, lambda i: (0, 0)),
                  pl.BlockSpec((nb, tl, d), lambda i: (0, i, 0))],
        out_specs=pl.BlockSpec((nb, tl, d), lambda i: (0, i, 0)),
        scratch_shapes=[pltpu.VMEM((d // LANES, rows, LANES), F32)],
        compiler_params=_cparams("parallel"),
        name="even_outproj",
    )(ya, yb, w_a, w_b, x)


def _block_diag(blocks):
    n, r, c = blocks.shape
    eye = jnp.eye(n, dtype=blocks.dtype)
    return jnp.einsum('nrc,nm->nrmc', blocks, eye).reshape(n * r, n * c)


def _even_mixer(x, norm_g, w_in, lam_re, lam_im, log_dt, b_re, b_im, c_re, c_im, d, glu_w, glu_b,
                conv_w, conv_b, wa, ba, wi, bi, lam, w_out):
    nb, seq, dm = x.shape
    g, p = lam_re.shape
    hch = b_re.shape[-1]
    width = g * hch
    n_half = 2
    gh = g // n_half

    proj = _even_inproj(x, norm_g[None, :], w_in.astype(BF16))

    lb_re, lb_im, bb_re, bb_im = _s5_discretize(lam_re, lam_im, log_dt, b_re, b_im)
    bb_re = bb_re.reshape(n_half, gh, hch, p)
    bb_im = bb_im.reshape(n_half, gh, hch, p)
    bm = jnp.stack([jnp.concatenate([_block_diag(bb_re[j]), _block_diag(bb_im[j])], axis=1)
                    for j in range(n_half)]).astype(BF16)
    c_re_t = jnp.swapaxes(c_re, 1, 2).reshape(n_half, gh, p, hch)
    c_im_t = jnp.swapaxes(c_im, 1, 2).reshape(n_half, gh, p, hch)
    cm = jnp.stack([jnp.concatenate([_block_diag(c_re_t[j]), -_block_diag(c_im_t[j])], axis=0)
                    for j in range(n_half)]).astype(BF16)
    lre = jnp.broadcast_to(lb_re.reshape(1, g * p), (nb, g * p))
    lim = jnp.broadcast_to(lb_im.reshape(1, g * p), (nb, g * p))
    ya = _s5_mixer(proj, nb, bm, lre, lim, cm, d.reshape(1, width), glu_w.astype(BF16),
                   glu_b[None, :])

    lw = conv_w.shape[1]
    nblk = 2
    per = LRU_BLOCKS // nblk
    wa_bd = jnp.stack([_block_diag(wa[j * per:(j + 1) * per]) for j in range(nblk)]).astype(BF16)
    wi_bd = jnp.stack([_block_diag(wi[j * per:(j + 1) * per]) for j in range(nblk)]).astype(BF16)
    yb = _lru_mixer(proj, nb, conv_w, conv_b[None, :], wa_bd, ba[None, :], wi_bd, bi[None, :],
                    lam[None, :])

    w_out_b = w_out.astype(BF16)
    return _even_outproj(ya, yb, w_out_b[:width], w_out_b[width:], x)


def _rope_table_kernel(pos_ref, inv_ref, cos_ref, sin_ref):
    ang = pos_ref[...].astype(F32) * inv_ref[...]
    cos_ref[...] = jnp.cos(ang)
    sin_ref[...] = jnp.sin(ang)


def _rope_tables(positions):
    nb, seq = positions.shape
    half = ROPE_DIM // 2
    per_row = LANES // half
    n_tok = nb * seq
    inv_freq = 1.0 / (ROPE_THETA ** (jnp.arange(0, ROPE_DIM, 2, dtype=F32) / ROPE_DIM))
    pos_rep = jnp.repeat(positions.reshape(n_tok // per_row, per_row), half, axis=1)
    inv_rep = jnp.tile(inv_freq, per_row)[None, :]
    cos, sin = pl.pallas_call(
        _rope_table_kernel,
        out_shape=(jax.ShapeDtypeStruct(pos_rep.shape, F32), jax.ShapeDtypeStruct(pos_rep.shape, F32)),
        name="rope_table",
    )(pos_rep, inv_rep)
    cos = cos.reshape(n_tok, half)
    sin = sin.reshape(n_tok, half)
    cc = jnp.concatenate([cos, cos], axis=-1)
    ss = jnp.concatenate([-sin, sin], axis=-1)
    return cc, ss


def _mla_proj_kernel(x_ref, g_ref, win_ref, cqn_ref, ckvn_ref, wq_ref, wkv_ref, qg_ref, kg_ref,
                     cc_ref, ss_ref, qn_ref, qrp_ref, k2_ref, v_ref):
    nh = MLA_HEADS
    h = _rms(x_ref[...], g_ref[...])
    proj = _dot(h.astype(BF16), win_ref[...])
    c_q = _rms(proj[:, :Q_LORA], cqn_ref[...])
    c_kv = _rms(proj[:, Q_LORA:Q_LORA + KV_LORA], ckvn_ref[...])
    o = Q_LORA + KV_LORA
    k_r2 = proj[:, o:o + LANES]
    k_sw2 = proj[:, o + LANES:o + 2 * LANES]
    cc2 = cc_ref[...]
    ss2 = ss_ref[...]
    qg = qg_ref[...]
    kg = kg_ref[...]
    lo = lax.broadcasted_iota(jnp.int32, cc2.shape, 1) < ROPE_DIM
    scale = math.log2(math.e) / math.sqrt(QK_DIM)

    q = _dot(c_q.astype(BF16), wq_ref[...])
    q_a = cc2 * qg[1:2, :]
    q_b = ss2 * qg[2:3, :]
    for j in range(nh // 2):
        ro = nh * NOPE_DIM + j * LANES
        q_rp = q[:, ro:ro + LANES]
        q_swp = q[:, ro + nh * ROPE_DIM:ro + nh * ROPE_DIM + LANES]
        rot = q_rp * q_a + q_swp * q_b
        sq_rp = q_rp * q_rp
        rs_pair = []
        for par in range(2):
            hd = 2 * j + par
            q_n = q[:, hd * NOPE_DIM:(hd + 1) * NOPE_DIM]
            own = lo if par == 0 else jnp.logical_not(lo)
            ms = jnp.sum(q_n * q_n + jnp.where(own, sq_rp, 0.0), axis=-1, keepdims=True) / QK_DIM
            rs = lax.rsqrt(ms + RMS_EPS) * scale
            qn_ref[0, hd] = (q_n * rs * qg[0:1, :]).astype(qn_ref.dtype)
            rs_pair.append(rs)
        qrp_ref[0, j] = (rot * jnp.where(lo, rs_pair[0], rs_pair[1])).astype(qrp_ref.dtype)

    kv = _dot(c_kv.astype(BF16), wkv_ref[...])
    k_rot2 = k_r2 * (cc2 * kg[1:2, :]) + k_sw2 * (ss2 * kg[2:3, :])
    k_r_ss = jnp.sum(jnp.where(lo, k_r2 * k_r2, 0.0), axis=-1, keepdims=True)
    for hd in range(nh):
        ko = hd * (NOPE_DIM + V_DIM)
        k_n = kv[:, ko:ko + NOPE_DIM]
        ms_k = (jnp.sum(k_n * k_n, axis=-1, keepdims=True) + k_r_ss) / QK_DIM
        rs_k = lax.rsqrt(ms_k + RMS_EPS)
        own = lo if hd % 2 == 0 else jnp.logical_not(lo)
        k2_ref[0, hd, :, 0:NOPE_DIM] = (k_n * rs_k * kg[0:1, :]).astype(k2_ref.dtype)
        k2_ref[0, hd, :, NOPE_DIM:] = jnp.where(own, k_rot2 * rs_k, 0.0).astype(k2_ref.dtype)
        v_ref[0, hd] = kv[:, ko + NOPE_DIM:ko + NOPE_DIM + V_DIM].astype(v_ref.dtype)


def _swap_halves(a, axis=-1):
    h = a.shape[axis] // 2
    lo = lax.slice_in_dim(a, 0, h, axis=axis)
    hi = lax.slice_in_dim(a, h, 2 * h, axis=axis)
    return jnp.concatenate([hi, lo], axis=axis)


def _mla_proj(x, norm_g, positions, w_in, cq_norm, ckv_norm, w_uq, w_ukv, q_norm, k_norm):
    nb, seq, dm = x.shape
    nh = MLA_HEADS
    tm = ROW_TILE
    n_tok = nb * seq
    cc, ss = _rope_tables(positions)
    cc2 = jnp.concatenate([cc, cc], axis=-1)
    ss2 = jnp.concatenate([ss, ss], axis=-1)
    o = Q_LORA + KV_LORA
    w_kr = w_in[:, o:o + ROPE_DIM]
    w_sw = _swap_halves(w_kr)
    w_in_x = jnp.concatenate([w_in[:, :o], w_kr, w_kr, w_sw, w_sw], axis=1).astype(BF16)
    wq = w_uq.reshape(Q_LORA, nh, QK_DIM)
    wq_n = wq[:, :, :NOPE_DIM].reshape(Q_LORA, nh * NOPE_DIM)
    wq_r = wq[:, :, NOPE_DIM:]
    wq_x = jnp.concatenate([wq_n, wq_r.reshape(Q_LORA, nh * ROPE_DIM),
                            _swap_halves(wq_r).reshape(Q_LORA, nh * ROPE_DIM)], axis=1).astype(BF16)

    def gains(gv):
        r = gv[NOPE_DIM:]
        rs = _swap_halves(r)
        return jnp.stack([gv[:NOPE_DIM], jnp.concatenate([r, r]), jnp.concatenate([rs, rs])])

    x2 = x.reshape(n_tok, dm)
    per_b = seq // tm
    full = lambda a: pl.BlockSpec(a.shape, lambda i: (0,) * a.ndim)
    args = (x2, norm_g[None, :], w_in_x, cq_norm[None, :], ckv_norm[None, :], wq_x,
            w_ukv.astype(BF16), gains(q_norm), gains(k_norm), cc2, ss2)
    in_specs = [pl.BlockSpec((tm, dm), lambda i: (i, 0))] + [full(a) for a in args[1:9]] + [
        pl.BlockSpec((tm, LANES), lambda i: (i, 0)), pl.BlockSpec((tm, LANES), lambda i: (i, 0))]

    def head_spec(n_heads, dh):
        return pl.BlockSpec((1, n_heads, tm, dh), lambda i: (i // per_b, 0, i % per_b, 0))

    def head_shape(n_heads, dh):
        return jax.ShapeDtypeStruct((nb, n_heads, seq, dh), BF16)

    return pl.pallas_call(
        _mla_proj_kernel,
        out_shape=(head_shape(nh, NOPE_DIM), head_shape(nh // 2, LANES), head_shape(nh, NOPE_DIM + LANES),
                   head_shape(nh, V_DIM)),
        grid=(n_tok // tm,),
        in_specs=in_specs,
        out_specs=(head_spec(nh, NOPE_DIM), head_spec(nh // 2, LANES), head_spec(nh, NOPE_DIM + LANES),
                   head_spec(nh, V_DIM)),
        compiler_params=_cparams("parallel"),
        name="mla_proj",
    )(*args)


def _attention_kernel(qn_ref, qrp_ref, k2_ref, v_ref, o_ref):
    seq = qn_ref.shape[2]
    t = ATT_TQ
    nt = (((1,), (1,)), ((), ()))
    tri = (lax.broadcasted_iota(jnp.int32, (t, t), 1) <= lax.broadcasted_iota(jnp.int32, (t, t), 0))
    for i in range(seq // t):
        q = jnp.concatenate([qn_ref[0, 0, i * t:(i + 1) * t, :], qrp_ref[0, 0, i * t:(i + 1) * t, :]], axis=-1)
        m = jnp.full((t, 1), NEG_INF, F32)
        l = jnp.zeros((t, 1), F32)
        acc = jnp.zeros((t, V_DIM), F32)
        for j in range(i + 1):
            s = lax.dot_general(q, k2_ref[0, 0, j * t:(j + 1) * t, :], nt, preferred_element_type=F32)
            if j == i:
                s = jnp.where(tri, s, NEG_INF)
            m_new = jnp.maximum(m, jnp.max(s, axis=-1, keepdims=True))
            alpha = jnp.exp2(m - m_new)
            p = jnp.exp2(s - m_new)
            l = alpha * l + jnp.sum(p, axis=-1, keepdims=True)
            acc = alpha * acc + _dot(p.astype(BF16), v_ref[0, 0, j * t:(j + 1) * t, :])
            m = m_new
        o_ref[0, i * t:(i + 1) * t, :] = (acc / l).astype(o_ref.dtype)


def _attention(qn, qrp, k2, v):
    nb, nh, seq, _ = qn.shape

    def spec(dh, per=1):
        return pl.BlockSpec((1, 1, seq, dh), lambda b, h: (b, h // per, 0, 0))

    return pl.pallas_call(
        _attention_kernel,
        out_shape=jax.ShapeDtypeStruct((nb, seq, nh * V_DIM), BF16),
        grid=(nb, nh),
        in_specs=[spec(NOPE_DIM), spec(LANES, 2), spec(NOPE_DIM + LANES), spec(V_DIM)],
        out_specs=pl.BlockSpec((1, seq, V_DIM), lambda b, h: (b, 0, h)),
        compiler_params=_cparams("parallel", "parallel"),
        name="mla_attention",
    )(qn, qrp, k2, v)


def _proj_residual_kernel(a_ref, w_ref, x_ref, o_ref):
    o_ref[...] = x_ref[...] + _dot(a_ref[...], w_ref[...])


def _proj_residual(a, w, x2):
    n_tok, dm = x2.shape
    tm = ROW_TILE
    return pl.pallas_call(
        _proj_residual_kernel,
        out_shape=jax.ShapeDtypeStruct(x2.shape, x2.dtype),
        grid=(n_tok // tm,),
        in_specs=[pl.BlockSpec((tm, a.shape[1]), lambda i: (i, 0)),
                  pl.BlockSpec(w.shape, lambda i: (0, 0)),
                  pl.BlockSpec((tm, dm), lambda i: (i, 0))],
        out_specs=pl.BlockSpec((tm, dm), lambda i: (i, 0)),
        compiler_params=_cparams("parallel"),
        name="proj_residual",
    )(a, w, x2)


def _mla_mixer(x, norm_g, positions, w_in, cq_norm, ckv_norm, w_uq, w_ukv, q_norm, k_norm, w_o):
    nb, seq, dm = x.shape
    qn, qrp, k2, v = _mla_proj(x, norm_g, positions, w_in, cq_norm, ckv_norm, w_uq, w_ukv, q_norm, k_norm)
    o = _attention(qn, qrp, k2, v)
    out = _proj_residual(o.reshape(nb * seq, -1), w_o.astype(BF16), x.reshape(nb * seq, dm))
    return out.reshape(nb, seq, dm)


def _router_kernel(x_ref, g_ref, w_ref, b_ref, meta_ref, gate_ref, cnt_ref, base_ref):
    tm, dm = x_ref.shape
    ne = N_EXPERTS

    @pl.when(pl.program_id(0) == 0)
    def _():
        base_ref[...] = jnp.zeros_like(base_ref)

    hn = _rms(x_ref[...], g_ref[...])
    logits = _dot(hn.astype(BF16), w_ref[...]) + b_ref[...]
    lane = lax.broadcasted_iota(jnp.int32, logits.shape, 1)
    big = jnp.int32(1 << 20)

    def first_argmax(v):
        m = jnp.max(v, axis=-1, keepdims=True)
        idx = jnp.min(jnp.where(v == m, lane, big), axis=-1, keepdims=True)
        return m, idx

    is_g = (lane >= ne) & (lane < ne + N_GROUPS)
    gl = jnp.where(is_g, logits, NEG_INF)
    g_max, g_idx = first_argmax(gl)
    g_sel = g_idx - ne
    p_g = 1.0 / jnp.sum(jnp.where(is_g, jnp.exp(gl - g_max), 0.0), axis=-1, keepdims=True)
    in_grp = (lane >= g_sel * EXPERTS_PER_GROUP) & (lane < (g_sel + 1) * EXPERTS_PER_GROUP)
    el = jnp.where(in_grp, logits, NEG_INF)
    v0, e0 = first_argmax(el)
    v1, e1 = first_argmax(jnp.where(lane == e0, NEG_INF, el))
    t = jnp.exp(v1 - v0)
    gate0 = p_g / (1.0 + t)
    gate1 = p_g * t / (1.0 + t)

    oh0 = (lane == e0).astype(F32)
    oh1 = (lane == e1).astype(F32)
    both = oh0 + oh1
    r_i = lax.broadcasted_iota(jnp.int32, (tm, tm), 0)
    c_i = lax.broadcasted_iota(jnp.int32, (tm, tm), 1)
    strict_lower = (c_i < r_i).astype(BF16)
    before = _dot(strict_lower, both.astype(BF16)) + base_ref[0:1, :]
    rank0 = jnp.sum(oh0 * before, axis=-1, keepdims=True)
    rank1 = jnp.sum(oh1 * before, axis=-1, keepdims=True)
    base_ref[0:1, :] = base_ref[0:1, :] + jnp.sum(both, axis=0, keepdims=True)
    cnt_ref[...] = jnp.broadcast_to(base_ref[0:1, :], cnt_ref.shape).astype(jnp.int32)

    meta = jnp.where(lane == 0, e0, jnp.where(lane == 1, e1, jnp.where(
        lane == 2, rank0.astype(jnp.int32), jnp.where(lane == 3, rank1.astype(jnp.int32), 0))))
    meta_ref[...] = meta
    gate_ref[...] = jnp.where(lane == 0, gate0, jnp.where(lane == 1, gate1, 0.0))


def _router(x2, norm_g, w_group, b_group, w_expert, b_expert):
    n_tok, dm = x2.shape
    tm = ROW_TILE
    pad = LANES - N_EXPERTS - N_GROUPS
    w = jnp.concatenate([w_expert, w_group, jnp.zeros((dm, pad), F32)], axis=1).astype(BF16)
    b = jnp.concatenate([b_expert, b_group, jnp.zeros((pad,), F32)])[None, :]
    return pl.pallas_call(
        _router_kernel,
        out_shape=(jax.ShapeDtypeStruct((n_tok, LANES), jnp.int32),
                   jax.ShapeDtypeStruct((n_tok, LANES), F32),
                   jax.ShapeDtypeStruct((SUBLANES, LANES), jnp.int32)),
        grid=(n_tok // tm,),
        in_specs=[pl.BlockSpec((tm, dm), lambda i: (i, 0)),
                  pl.BlockSpec((1, dm), lambda i: (0, 0)),
                  pl.BlockSpec((dm, LANES), lambda i: (0, 0)),
                  pl.BlockSpec((1, LANES), lambda i: (0, 0))],
        out_specs=(pl.BlockSpec((tm, LANES), lambda i: (i, 0)),
                   pl.BlockSpec((tm, LANES), lambda i: (i, 0)),
                   pl.BlockSpec((SUBLANES, LANES), lambda i: (0, 0))),
        scratch_shapes=[pltpu.VMEM((SUBLANES, LANES), F32)],
        compiler_params=_cparams("arbitrary"),
        name="moe_router",
    )(x2, norm_g[None, :], w, b)


def _dispatch_kernel(dest_ref, x_ref, g_ref, xd_ref, hn_ref, zero_ref, sem, zsem):
    tm = x_ref.shape[0]
    n_assign = pl.num_programs(0) * tm * TOP_K

    def tail_copy():
        return pltpu.make_async_copy(zero_ref, xd_ref.at[pl.ds(n_assign * ROW_CHUNKS, zero_ref.shape[0])], zsem)

    @pl.when(pl.program_id(0) == 0)
    def _():
        zero_ref[...] = jnp.zeros_like(zero_ref)
        tail_copy().start()

    hn = _rms(x_ref[...], g_ref[...])
    for c in range(ROW_CHUNKS):
        hn_ref[pl.ds(c, tm, stride=ROW_CHUNKS), :] = hn[:, c * LANES:(c + 1) * LANES]

    def row_copy(t, k):
        src = pl.multiple_of(t * ROW_CHUNKS, ROW_CHUNKS)
        dst = pl.multiple_of(dest_ref[0, 0, 2 * t + k], ROW_CHUNKS)
        return pltpu.make_async_copy(hn_ref.at[pl.ds(src, ROW_CHUNKS)], xd_ref.at[pl.ds(dst, ROW_CHUNKS)], sem)

    def issue(i, c):
        for u in range(DMA_UNROLL):
            for k in range(TOP_K):
                row_copy(i * DMA_UNROLL + u, k).start(priority=k)
        return c

    lax.fori_loop(0, tm // DMA_UNROLL, issue, 0)

    def drain(i, c):
        for u in range(DMA_UNROLL):
            for k in range(TOP_K):
                row_copy(i * DMA_UNROLL + u, k).wait()
        return c

    lax.fori_loop(0, tm // DMA_UNROLL, drain, 0)

    @pl.when(pl.program_id(0) == 0)
    def _():
        tail_copy().wait()


def _dispatch(x2, norm_g, dest_off):
    tm = GATHER_TM
    n_tok, dm = x2.shape
    dest3 = dest_off.reshape(n_tok // tm, 1, 2 * tm)
    n_rows = n_tok * TOP_K + MOE_BM
    return pl.pallas_call(
        _dispatch_kernel,
        out_shape=jax.ShapeDtypeStruct((n_rows * ROW_CHUNKS, LANES), F32),
        grid=(n_tok // tm,),
        in_specs=[pl.BlockSpec((1, 1, 2 * tm), lambda i: (i, 0, 0), memory_space=pltpu.SMEM),
                  pl.BlockSpec((tm, dm), lambda i: (i, 0)),
                  pl.BlockSpec((1, dm), lambda i: (0, 0))],
        out_specs=pl.BlockSpec(memory_space=pl.ANY),
        scratch_shapes=[pltpu.VMEM((tm * ROW_CHUNKS, LANES), F32),
                        pltpu.VMEM((MOE_BM * ROW_CHUNKS, LANES), F32),
                        pltpu.SemaphoreType.DMA, pltpu.SemaphoreType.DMA],
        compiler_params=_cparams("arbitrary"),
        name="moe_dispatch",
    )(dest3, x2, norm_g[None, :])


def _rows_from_tiles(ref, n, base=0):
    return jnp.concatenate([ref[pl.ds(base + c, n, stride=ROW_CHUNKS), :] for c in range(ROW_CHUNKS)], axis=-1)


def _expert_kernel(be_ref, bs_ref, first_ref, nxt_ref, nu_ref, xd_ref, w1_ref, w3_ref, w2_ref, yd_ref,
                   xbuf, ybuf, w1buf, w3buf, w2buf, w13_s, w2_s, sem_in, sem_out, sem_w, *, layer):
    nu = nu_ref[0]
    n_rows = xbuf.shape[1]
    bm = n_rows // ROW_CHUNKS
    de = w1buf.shape[2]

    def in_copy(blk, sl):
        src = pl.multiple_of(bs_ref[blk] * ROW_CHUNKS, ROW_CHUNKS)
        return pltpu.make_async_copy(xd_ref.at[pl.ds(src, n_rows)], xbuf.at[sl], sem_in.at[sl])

    def out_copy(blk, sl):
        dst = pl.multiple_of(bs_ref[blk] * ROW_CHUNKS, ROW_CHUNKS)
        return pltpu.make_async_copy(ybuf.at[sl], yd_ref.at[pl.ds(dst, n_rows)], sem_out.at[sl])

    def w_copies(e, ws):
        return (pltpu.make_async_copy(w1_ref.at[layer, e], w1buf.at[ws], sem_w.at[ws]),
                pltpu.make_async_copy(w3_ref.at[layer, e], w3buf.at[ws], sem_w.at[ws]),
                pltpu.make_async_copy(w2_ref.at[layer, e], w2buf.at[ws], sem_w.at[ws]))

    in_copy(0, 0).start()
    for cp in w_copies(be_ref[0], 1):
        cp.start()

    def block(b, ws):
        slot = b % 2

        @pl.when(b + 1 < nu)
        def _():
            in_copy(b + 1, 1 - slot).start()

        first = first_ref[b] == 1
        ws = jnp.where(first, 1 - ws, ws)

        @pl.when(first)
        def _():
            for cp in w_copies(be_ref[b], ws):
                cp.wait()
            w13_s[:, :de] = w1buf[ws].astype(BF16)
            w13_s[:, de:] = w3buf[ws].astype(BF16)
            w2_s[...] = w2buf[ws].astype(BF16)

            @pl.when(nxt_ref[b] >= 0)
            def _():
                for cp in w_copies(nxt_ref[b], 1 - ws):
                    cp.start()

        in_copy(b, slot).wait()
        x = _rows_from_tiles(xbuf.at[slot], bm).astype(BF16)
        h = _dot(x, w13_s[...])
        a = jax.nn.silu(h[:, :de]) * h[:, de:]
        y = _dot(a.astype(BF16), w2_s[...])
        yb = ybuf.at[slot]
        for c in range(ROW_CHUNKS):
            yb[pl.ds(c, bm, stride=ROW_CHUNKS), :] = y[:, c * LANES:(c + 1) * LANES]

        @pl.when(b >= 1)
        def _():
            out_copy(b - 1, 1 - slot).wait()
        out_copy(b, slot).start()
        return ws

    lax.fori_loop(0, nu, block, jnp.int32(0))

    last = nu - 1
    out_copy(last, last % 2).wait()
    tail = yd_ref.shape[0] - n_rows
    fill_slot = 1 - last % 2
    ybuf[fill_slot] = jnp.zeros((n_rows, LANES), F32)
    fill = pltpu.make_async_copy(ybuf.at[fill_slot], yd_ref.at[pl.ds(tail, n_rows)], sem_out.at[fill_slot])
    fill.start()
    fill.wait()


def _experts(xd, blk_expert, blk_start, blk_first, blk_next, n_used, w1, w3, w2, layer):
    bm = MOE_BM
    _, _, dm, de = w1.shape
    any_spec = pl.BlockSpec(memory_space=pl.ANY)
    grid_spec = pltpu.PrefetchScalarGridSpec(
        num_scalar_prefetch=5,
        grid=(1,),
        in_specs=[any_spec, any_spec, any_spec, any_spec],
        out_specs=any_spec,
        scratch_shapes=[pltpu.VMEM((2, bm * ROW_CHUNKS, LANES), F32),
                        pltpu.VMEM((2, bm * ROW_CHUNKS, LANES), F32),
                        pltpu.VMEM((2, dm, de), F32),
                        pltpu.VMEM((2, dm, de), F32),
                        pltpu.VMEM((2, de, dm), F32),
                        pltpu.VMEM((dm, 2 * de), BF16),
                        pltpu.VMEM((de, dm), BF16),
                        pltpu.SemaphoreType.DMA((2,)),
                        pltpu.SemaphoreType.DMA((2,)),
                        pltpu.SemaphoreType.DMA((2,))],
    )
    return pl.pallas_call(
        functools.partial(_expert_kernel, layer=layer),
        out_shape=jax.ShapeDtypeStruct(xd.shape, F32),
        grid_spec=grid_spec,
        compiler_params=_cparams("arbitrary"),
        name="moe_experts",
    )(blk_expert, blk_start, blk_first, blk_next, n_used, xd, w1, w3, w2)


def _combine_kernel(dest_ref, dest_next_ref, yd_ref, gate_ref, x_ref, o_ref, buf_ref, sem):
    tm, dm = x_ref.shape
    i = pl.program_id(0)
    n_steps = pl.num_programs(0)
    slot = i % 2
    k_rows = tm * ROW_CHUNKS

    def row_copy(d_ref, sl, t, k):
        src = pl.multiple_of(d_ref[0, 0, 2 * t + k], ROW_CHUNKS)
        dst = pl.multiple_of(k * k_rows + t * ROW_CHUNKS, ROW_CHUNKS)
        return pltpu.make_async_copy(yd_ref.at[pl.ds(src, ROW_CHUNKS)],
                                     buf_ref.at[sl, pl.ds(dst, ROW_CHUNKS)], sem.at[sl])

    def issue_all(d_ref, sl):
        def issue(j, c):
            for u in range(DMA_UNROLL):
                for k in range(TOP_K):
                    row_copy(d_ref, sl, j * DMA_UNROLL + u, k).start(priority=k)
            return c
        lax.fori_loop(0, tm // DMA_UNROLL, issue, 0)

    @pl.when(i == 0)
    def _():
        issue_all(dest_ref, slot)

    @pl.when(i + 1 < n_steps)
    def _():
        issue_all(dest_next_ref, 1 - slot)

    def drain(j, c):
        for u in range(DMA_UNROLL):
            for k in range(TOP_K):
                row_copy(dest_ref, slot, j * DMA_UNROLL + u, k).wait()
        return c

    lax.fori_loop(0, tm // DMA_UNROLL, drain, 0)

    g0 = gate_ref[:, 0:1]
    g1 = gate_ref[:, 1:2]
    buf = buf_ref.at[slot]
    for c in range(ROW_CHUNKS):
        y0 = buf[pl.ds(c, tm, stride=ROW_CHUNKS), :]
        y1 = buf[pl.ds(k_rows + c, tm, stride=ROW_CHUNKS), :]
        o_ref[:, c * LANES:(c + 1) * LANES] = x_ref[:, c * LANES:(c + 1) * LANES] + (g0 * y0 + g1 * y1)


def _combine(yd, dest_off, gates, x2):
    n_tok, dm = x2.shape
    tm = GATHER_TM
    n_steps = n_tok // tm
    dest3 = dest_off.reshape(n_steps, 1, 2 * tm)
    return pl.pallas_call(
        _combine_kernel,
        out_shape=jax.ShapeDtypeStruct(x2.shape, x2.dtype),
        grid=(n_steps,),
        in_specs=[pl.BlockSpec((1, 1, 2 * tm), lambda i: (i, 0, 0), memory_space=pltpu.SMEM),
                  pl.BlockSpec((1, 1, 2 * tm), lambda i: (jnp.minimum(i + 1, n_steps - 1), 0, 0),
                               memory_space=pltpu.SMEM),
                  pl.BlockSpec(memory_space=pl.ANY),
                  pl.BlockSpec((tm, LANES), lambda i: (i, 0)),
                  pl.BlockSpec((tm, dm), lambda i: (i, 0))],
        out_specs=pl.BlockSpec((tm, dm), lambda i: (i, 0)),
        scratch_shapes=[pltpu.VMEM((2, TOP_K * tm * ROW_CHUNKS, LANES), F32),
                        pltpu.SemaphoreType.DMA((2,))],
        compiler_params=_cparams("arbitrary"),
        name="moe_combine",
    )(dest3, dest3, yd, gates, x2)


def _moe(x, norm_g, w_group, b_group, w_expert, b_expert, w1, w3, w2, layer):
    nb, seq, dm = x.shape
    n_tok = nb * seq
    n_assign = n_tok * TOP_K
    x2 = x.reshape(n_tok, dm)
    meta, gates, counts = _router(x2, norm_g, w_group, b_group, w_expert, b_expert)
    bm = MOE_BM
    n_blocks = n_assign // bm + N_EXPERTS
    counts = counts[0, :N_EXPERTS]
    starts = jnp.cumsum(counts) - counts
    eid = meta[:, 0:TOP_K]
    e_iota = jnp.arange(N_EXPERTS, dtype=jnp.int32)
    seg_start = jnp.sum(jnp.where(eid[:, :, None] == e_iota, starts, 0), axis=-1)
    dest_off = ((seg_start + meta[:, TOP_K:2 * TOP_K]) * ROW_CHUNKS).astype(jnp.int32).reshape(-1)
    nblk = (counts + bm - 1) // bm
    blk_end = jnp.cumsum(nblk)
    b_iota = jnp.arange(n_blocks, dtype=jnp.int32)
    blk_expert = jnp.minimum(jnp.sum((b_iota[:, None] >= blk_end[None, :]).astype(jnp.int32), axis=1),
                             N_EXPERTS - 1)
    j = b_iota - (blk_end - nblk)[blk_expert]
    n_used = blk_end[-1].astype(jnp.int32).reshape(1)
    blk_start = jnp.where(b_iota < n_used[0], starts[blk_expert] + j * bm, 0).astype(jnp.int32)
    blk_first = (j == 0).astype(jnp.int32)
    cand = jnp.where(counts > 0, e_iota, N_EXPERTS)
    later = lax.cummin(cand[::-1])[::-1]
    next_e = jnp.concatenate([later[1:], jnp.full((1,), N_EXPERTS, jnp.int32)])
    blk_next = jnp.where(next_e[blk_expert] < N_EXPERTS, next_e[blk_expert], -1).astype(jnp.int32)

    xd = _dispatch(x2, norm_g, dest_off)
    yd = _experts(xd, blk_expert, blk_start, blk_first, blk_next, n_used, w1, w3, w2, layer)
    out = _combine(yd, dest_off, gates, x2)
    return out.reshape(nb, seq, dm)


def kernel(x, positions, ev_norm, ev_w_in, s5_lambda_re, s5_lambda_im, s5_log_dt, s5_b_re, s5_b_im,
           s5_c_re, s5_c_im, s5_d, s5_glu_w, s5_glu_b, lru_conv_w, lru_conv_b, lru_wa, lru_ba, lru_wi,
           lru_bi, lru_lambda, ev_w_out, od_norm, mla_w_in, mla_cq_norm, mla_ckv_norm, mla_w_uq,
           mla_w_ukv, mla_q_norm, mla_k_norm, mla_w_o, ffn_norm, moe_w_group, moe_b_group,
           moe_w_expert, moe_b_expert, moe_w1, moe_w3, moe_w2):
    depth = ffn_norm.shape[0]
    for layer in range(depth):
        j = layer // 2
        if layer % 2 == 0:
            x = _even_mixer(x, ev_norm[j], ev_w_in[j], s5_lambda_re[j], s5_lambda_im[j], s5_log_dt[j],
                            s5_b_re[j], s5_b_im[j], s5_c_re[j], s5_c_im[j], s5_d[j], s5_glu_w[j],
                            s5_glu_b[j], lru_conv_w[j], lru_conv_b[j], lru_wa[j], lru_ba[j], lru_wi[j],
                            lru_bi[j], lru_lambda[j], ev_w_out[j])
        else:
            x = _mla_mixer(x, od_norm[j], positions, mla_w_in[j], mla_cq_norm[j], mla_ckv_norm[j],
                           mla_w_uq[j], mla_w_ukv[j], mla_q_norm[j], mla_k_norm[j], mla_w_o[j])
        x = _moe(x, ffn_norm[layer], moe_w_group[layer], moe_b_group[layer], moe_w_expert[layer],
                 moe_b_expert[layer], moe_w1, moe_w3, moe_w2, layer)
    return x
```

```python
import functools
import math

import jax
import jax.numpy as jnp
from jax import lax
from jax.experimental import pallas as pl
from jax.experimental.pallas import tpu as pltpu

F32 = jnp.float32
BF16 = jnp.bfloat16

S5_GROUP_CH = 16
S5_STATE = 64
LRU_BLOCKS = 8
CONV_WIDTH = 4
RG_C = 8.0
MLA_HEADS = 8
Q_LORA = 384
KV_LORA = 256
NOPE_DIM = 128
ROPE_DIM = 64
V_DIM = 128
QK_DIM = NOPE_DIM + ROPE_DIM
ROPE_THETA = 10000.0
N_GROUPS = 8
EXPERTS_PER_GROUP = 8
N_EXPERTS = N_GROUPS * EXPERTS_PER_GROUP
TOP_K = 2
RMS_EPS = 1e-6
NEG_INF = -1e30

SUBLANES = 8
LANES = 128
ROW_CHUNKS = 8
VMEM_LIMIT = 48 * 1024 * 1024

EVEN_TL = 64
ROW_TILE = 512
MOE_BM = 256
ATT_TQ = 512
GATHER_TM = 512
DMA_UNROLL = 8


def _cparams(*sem):
    return pltpu.CompilerParams(dimension_semantics=tuple(sem), vmem_limit_bytes=VMEM_LIMIT)


def _rms(x, g):
    ms = jnp.mean(x * x, axis=-1, keepdims=True)
    return x * lax.rsqrt(ms + RMS_EPS) * g


def _dot(a, b):
    return jnp.dot(a, b, preferred_element_type=F32)


def _s5_discretize_kernel(lre_ref, lim_ref, ldt_ref, bre_ref, bim_ref,
                          lbre_ref, lbim_ref, bbre_ref, bbim_ref):
    lr = jnp.minimum(lre_ref[...], -1e-4)
    li = lim_ref[...]
    dt = jnp.exp(ldt_ref[...])
    mag = jnp.exp(lr * dt)
    lb_re = mag * jnp.cos(li * dt)
    lb_im = mag * jnp.sin(li * dt)
    den = lr * lr + li * li
    num_re = lb_re - 1.0
    f_re = (num_re * lr + lb_im * li) / den
    f_im = (lb_im * lr - num_re * li) / den
    lbre_ref[...] = lb_re
    lbim_ref[...] = lb_im
    br = bre_ref[...]
    bi = bim_ref[...]
    bbre_ref[...] = f_re[:, None, :] * br - f_im[:, None, :] * bi
    bbim_ref[...] = f_re[:, None, :] * bi + f_im[:, None, :] * br


def _s5_discretize(lam_re, lam_im, log_dt, b_re, b_im):
    g, p = lam_re.shape
    h = b_re.shape[-1]
    b_re_t = jnp.swapaxes(b_re, 1, 2)
    b_im_t = jnp.swapaxes(b_im, 1, 2)
    return pl.pallas_call(
        _s5_discretize_kernel,
        out_shape=(jax.ShapeDtypeStruct((g, p), F32), jax.ShapeDtypeStruct((g, p), F32),
                   jax.ShapeDtypeStruct((g, h, p), F32), jax.ShapeDtypeStruct((g, h, p), F32)),
        name="s5_discretize",
    )(lam_re, lam_im, log_dt[:, None], b_re_t, b_im_t)


def _even_inproj_kernel(x_ref, g_ref, w_ref, o_ref, h_ref):
    nb, tl, _ = x_ref.shape
    nc = h_ref.shape[0]
    g = g_ref[...]
    for b in range(nb):
        h = _rms(x_ref[b], g)
        for c in range(nc):
            h_ref[c, pl.ds(b, tl, stride=nb), :] = h[:, c * LANES:(c + 1) * LANES]
    h_all = jnp.concatenate([h_ref[c] for c in range(nc)], axis=-1)
    o_ref[...] = _dot(h_all.astype(BF16), w_ref[...]).astype(o_ref.dtype)


def _even_inproj(x, g, w):
    nb, seq, d = x.shape
    n_out = w.shape[1]
    tl = EVEN_TL
    return pl.pallas_call(
        _even_inproj_kernel,
        out_shape=jax.ShapeDtypeStruct((seq * nb, n_out), BF16),
        grid=(seq // tl,),
        in_specs=[pl.BlockSpec((nb, tl, d), lambda i: (0, i, 0)),
                  pl.BlockSpec((1, d), lambda i: (0, 0)),
                  pl.BlockSpec((d, n_out), lambda i: (0, 0))],
        out_specs=pl.BlockSpec((tl * nb, n_out), lambda i: (i, 0)),
        scratch_shapes=[pltpu.VMEM((d // LANES, tl * nb, LANES), F32)],
        compiler_params=_cparams("parallel"),
        name="even_inproj",
    )(x, g, w)


def _s5_kernel(u_ref, bm_ref, lre_ref, lim_ref, cm_ref, d_ref, gw_ref, gb_ref, o_ref,
               bu_ref, sb_ref, st_ref, *, nb):
    rows, width = u_ref.shape
    n_half = bm_ref.shape[0]
    kin = width // n_half
    ncol = bm_ref.shape[2]
    nre = ncol // 2
    chunk = 512
    steps = rows // nb

    @pl.when(pl.program_id(0) == 0)
    def _():
        st_ref[...] = jnp.zeros_like(st_ref)

    u = u_ref[...]
    for j in range(n_half):
        bu_ref[:, j * ncol:(j + 1) * ncol] = _dot(u[:, j * kin:(j + 1) * kin], bm_ref[j])

    for j in range(n_half):
        for c in range(nre // chunk):
            cr = j * ncol + c * chunk
            ci = cr + nre
            lc = j * nre + c * chunk
            lr = lre_ref[:, lc:lc + chunk]
            li = lim_ref[:, lc:lc + chunk]

            def body(i, carry, cr=cr, ci=ci, lr=lr, li=li):
                sr, si = carry
                r0 = pl.multiple_of(i * (2 * nb), 2 * nb)
                sr1 = lr * sr - li * si + bu_ref[pl.ds(r0, nb), cr:cr + chunk]
                si1 = lr * si + li * sr + bu_ref[pl.ds(r0, nb), ci:ci + chunk]
                sr2 = lr * sr1 - li * si1 + bu_ref[pl.ds(r0 + nb, nb), cr:cr + chunk]
                si2 = lr * si1 + li * sr1 + bu_ref[pl.ds(r0 + nb, nb), ci:ci + chunk]
                sb_ref[pl.ds(r0, 2 * nb), cr:cr + chunk] = jnp.concatenate([sr1, sr2], 0).astype(BF16)
                sb_ref[pl.ds(r0, 2 * nb), ci:ci + chunk] = jnp.concatenate([si1, si2], 0).astype(BF16)
                return sr2, si2

            sr, si = lax.fori_loop(0, steps // 2, body,
                                   (st_ref[:, cr:cr + chunk], st_ref[:, ci:ci + chunk]))
            st_ref[:, cr:cr + chunk] = sr
            st_ref[:, ci:ci + chunk] = si

    ys = [_dot(sb_ref[:, j * ncol:(j + 1) * ncol], cm_ref[j]) for j in range(n_half)]
    y = jnp.concatenate(ys, axis=-1) + d_ref[...] * u.astype(F32)
    y = jax.nn.gelu(y)
    z = _dot(y.astype(BF16), gw_ref[...]) + gb_ref[...]
    o_ref[...] = (y * jax.nn.sigmoid(z)).astype(o_ref.dtype)


def _s5_mixer(proj, nb, bm, lre, lim, cm, d, glu_w, glu_b):
    rows_total = proj.shape[0]
    width = d.shape[1]
    rows = EVEN_TL * nb
    n_half, _, ncol = bm.shape
    kern = functools.partial(_s5_kernel, nb=nb)
    return pl.pallas_call(
        kern,
        out_shape=jax.ShapeDtypeStruct((rows_total, width), BF16),
        grid=(rows_total // rows,),
        in_specs=[pl.BlockSpec((rows, width), lambda i: (i, 0)),
                  pl.BlockSpec(bm.shape, lambda i: (0, 0, 0)),
                  pl.BlockSpec(lre.shape, lambda i: (0, 0)),
                  pl.BlockSpec(lim.shape, lambda i: (0, 0)),
                  pl.BlockSpec(cm.shape, lambda i: (0, 0, 0)),
                  pl.BlockSpec((1, width), lambda i: (0, 0)),
                  pl.BlockSpec(glu_w.shape, lambda i: (0, 0)),
                  pl.BlockSpec((1, width), lambda i: (0, 0))],
        out_specs=pl.BlockSpec((rows, width), lambda i: (i, 0)),
        scratch_shapes=[pltpu.VMEM((rows, n_half * ncol), F32),
                        pltpu.VMEM((rows, n_half * ncol), BF16),
                        pltpu.VMEM((nb, n_half * ncol), F32)],
        compiler_params=_cparams("arbitrary"),
        name="s5_mixer",
    )(proj, bm, lre, lim, cm, d, glu_w, glu_b)


def _lru_kernel(x_ref, gate_ref, cw_ref, cb_ref, wa_ref, ba_ref, wi_ref, bi_ref, lam_ref, o_ref,
                xp_ref, a_ref, b_ref, h_ref, *, nb):
    rows, width = x_ref.shape
    halo = (CONV_WIDTH - 1) * nb
    steps = rows // nb

    @pl.when(pl.program_id(0) == 0)
    def _():
        xp_ref[0:halo, :] = jnp.zeros((halo, width), F32)
        h_ref[...] = jnp.zeros_like(h_ref)

    xp_ref[halo:halo + rows, :] = x_ref[...].astype(F32)
    xf = cb_ref[...] + cw_ref[0:1, :] * xp_ref[0:rows, :]
    for k in range(1, CONV_WIDTH):
        xf = xf + cw_ref[k:k + 1, :] * xp_ref[k * nb:k * nb + rows, :]
    xp_ref[0:halo, :] = xp_ref[rows:rows + halo, :]

    xb = xf.astype(BF16)
    nblk = wa_ref.shape[0]
    kb = width // nblk
    ga = jnp.concatenate([_dot(xb[:, j * kb:(j + 1) * kb], wa_ref[j]) for j in range(nblk)], -1)
    gi = jnp.concatenate([_dot(xb[:, j * kb:(j + 1) * kb], wi_ref[j]) for j in range(nblk)], -1)
    r = jax.nn.sigmoid(ga + ba_ref[...])
    ig = jax.nn.sigmoid(gi + bi_ref[...])
    log_a = RG_C * r * jax.nn.log_sigmoid(lam_ref[...])
    a_ref[...] = jnp.exp(log_a)
    th = jnp.tanh(log_a)
    b_ref[...] = jnp.sqrt(-2.0 * th / (1.0 - th)) * (ig * xf)

    def body(t, h):
        r0 = pl.multiple_of(t * nb, nb)
        h = a_ref[pl.ds(r0, nb), :] * h + b_ref[pl.ds(r0, nb), :]
        b_ref[pl.ds(r0, nb), :] = h
        return h

    h_ref[...] = lax.fori_loop(0, steps, body, h_ref[...])
    o_ref[...] = (b_ref[...] * jax.nn.gelu(gate_ref[...].astype(F32))).astype(o_ref.dtype)


def _lru_mixer(proj, nb, conv_w, conv_b, wa, ba, wi, bi, lam):
    rows_total = proj.shape[0]
    width = conv_w.shape[1]
    rows = EVEN_TL * nb
    halo = (CONV_WIDTH - 1) * nb
    kern = functools.partial(_lru_kernel, nb=nb)
    vec = pl.BlockSpec((1, width), lambda i: (0, 0))
    return pl.pallas_call(
        kern,
        out_shape=jax.ShapeDtypeStruct((rows_total, width), BF16),
        grid=(rows_total // rows,),
        in_specs=[pl.BlockSpec((rows, width), lambda i: (i, 1)),
                  pl.BlockSpec((rows, width), lambda i: (i, 2)),
                  pl.BlockSpec(conv_w.shape, lambda i: (0, 0)),
                  vec,
                  pl.BlockSpec(wa.shape, lambda i: (0, 0, 0)),
                  vec,
                  pl.BlockSpec(wi.shape, lambda i: (0, 0, 0)),
                  vec, vec],
        out_specs=pl.BlockSpec((rows, width), lambda i: (i, 0)),
        scratch_shapes=[pltpu.VMEM((rows + halo, width), F32),
                        pltpu.VMEM((rows, width), F32),
                        pltpu.VMEM((rows, width), F32),
                        pltpu.VMEM((nb, width), F32)],
        compiler_params=_cparams("arbitrary"),
        name="lru_mixer",
    )(proj, proj, conv_w, conv_b, wa, ba, wi, bi, lam)


def _even_outproj_kernel(ya_ref, yb_ref, wa_ref, wb_ref, x_ref, o_ref, y_ref):
    nb, tl, _ = x_ref.shape
    nc = y_ref.shape[0]
    y = _dot(ya_ref[...], wa_ref[...]) + _dot(yb_ref[...], wb_ref[...])
    for c in range(nc):
        y_ref[c] = y[:, c * LANES:(c + 1) * LANES]
    for b in range(nb):
        for c in range(nc):
            o_ref[b, :, c * LANES:(c + 1) * LANES] = (x_ref[b, :, c * LANES:(c + 1) * LANES]
                                                     + y_ref[c, pl.ds(b, tl, stride=nb), :])


def _even_outproj(ya, yb, w_a, w_b, x):
    nb, seq, d = x.shape
    tl = EVEN_TL
    rows = tl * nb
    wd = ya.shape[1]
    return pl.pallas_call(
        _even_outproj_kernel,
        out_shape=jax.ShapeDtypeStruct(x.shape, x.dtype),
        grid=(seq // tl,),
        in_specs=[pl.BlockSpec((rows, wd), lambda i: (i, 0)),
                  pl.BlockSpec((rows, wd), lambda i: (i, 0)),
                  pl.BlockSpec(w_a.shape, lambda i: (0, 0)),
                  pl.BlockSpec(w_b.shape, lambda i: (0, 0)),
                  pl.BlockSpec((nb, tl, d), lambda i: (0, i, 0))],
        out_specs=pl.BlockSpec((nb, tl, d), lambda i: (0, i, 0)),
        scratch_shapes=[pltpu.VMEM((d // LANES, rows, LANES), F32)],
        compiler_params=_cparams("parallel"),
        name="even_outproj",
    )(ya, yb, w_a, w_b, x)


def _block_diag(blocks):
    n, r, c = blocks.shape
    eye = jnp.eye(n, dtype=blocks.dtype)
    return jnp.einsum('nrc,nm->nrmc', blocks, eye).reshape(n * r, n * c)


def _even_mixer(x, norm_g, w_in, lam_re, lam_im, log_dt, b_re, b_im, c_re, c_im, d, glu_w, glu_b,
                conv_w, conv_b, wa, ba, wi, bi, lam, w_out):
    nb, seq, dm = x.shape
    g, p = lam_re.shape
    hch = b_re.shape[-1]
    width = g * hch
    n_half = 2
    gh = g // n_half

    proj = _even_inproj(x, norm_g[None, :], w_in.astype(BF16))

    lb_re, lb_im, bb_re, bb_im = _s5_discretize(lam_re, lam_im, log_dt, b_re, b_im)
    bb_re = bb_re.reshape(n_half, gh, hch, p)
    bb_im = bb_im.reshape(n_half, gh, hch, p)
    bm = jnp.stack([jnp.concatenate([_block_diag(bb_re[j]), _block_diag(bb_im[j])], axis=1)
                    for j in range(n_half)]).astype(BF16)
    c_re_t = jnp.swapaxes(c_re, 1, 2).reshape(n_half, gh, p, hch)
    c_im_t = jnp.swapaxes(c_im, 1, 2).reshape(n_half, gh, p, hch)
    cm = jnp.stack([jnp.concatenate([_block_diag(c_re_t[j]), -_block_diag(c_im_t[j])], axis=0)
                    for j in range(n_half)]).astype(BF16)
    lre = jnp.broadcast_to(lb_re.reshape(1, g * p), (nb, g * p))
    lim = jnp.broadcast_to(lb_im.reshape(1, g * p), (nb, g * p))
    ya = _s5_mixer(proj, nb, bm, lre, lim, cm, d.reshape(1, width), glu_w.astype(BF16),
                   glu_b[None, :])

    lw = conv_w.shape[1]
    nblk = 2
    per = LRU_BLOCKS // nblk
    wa_bd = jnp.stack([_block_diag(wa[j * per:(j + 1) * per]) for j in range(nblk)]).astype(BF16)
    wi_bd = jnp.stack([_block_diag(wi[j * per:(j + 1) * per]) for j in range(nblk)]).astype(BF16)
    yb = _lru_mixer(proj, nb, conv_w, conv_b[None, :], wa_bd, ba[None, :], wi_bd, bi[None, :],
                    lam[None, :])

    w_out_b = w_out.astype(BF16)
    return _even_outproj(ya, yb, w_out_b[:width], w_out_b[width:], x)


def _rope_table_kernel(pos_ref, inv_ref, cos_ref, sin_ref):
    ang = pos_ref[...].astype(F32) * inv_ref[...]
    cos_ref[...] = jnp.cos(ang)
    sin_ref[...] = jnp.sin(ang)


def _rope_tables(positions):
    nb, seq = positions.shape
    half = ROPE_DIM // 2
    per_row = LANES // half
    n_tok = nb * seq
    inv_freq = 1.0 / (ROPE_THETA ** (jnp.arange(0, ROPE_DIM, 2, dtype=F32) / ROPE_DIM))
    pos_rep = jnp.repeat(positions.reshape(n_tok // per_row, per_row), half, axis=1)
    inv_rep = jnp.tile(inv_freq, per_row)[None, :]
    cos, sin = pl.pallas_call(
        _rope_table_kernel,
        out_shape=(jax.ShapeDtypeStruct(pos_rep.shape, F32), jax.ShapeDtypeStruct(pos_rep.shape, F32)),
        name="rope_table",
    )(pos_rep, inv_rep)
    cos = cos.reshape(n_tok, half)
    sin = sin.reshape(n_tok, half)
    cc = jnp.concatenate([cos, cos], axis=-1)
    ss = jnp.concatenate([-sin, sin], axis=-1)
    return cc, ss


def _mla_proj_kernel(x_ref, g_ref, win_ref, cqn_ref, ckvn_ref, wq_ref, wkv_ref, qg_ref, kg_ref,
                     cc_ref, ss_ref, qn_ref, qrp_ref, k2_ref, v_ref):
    nh = MLA_HEADS
    h = _rms(x_ref[...], g_ref[...])
    proj = _dot(h.astype(BF16), win_ref[...])
    c_q = _rms(proj[:, :Q_LORA], cqn_ref[...])
    c_kv = _rms(proj[:, Q_LORA:Q_LORA + KV_LORA], ckvn_ref[...])
    o = Q_LORA + KV_LORA
    k_r2 = proj[:, o:o + LANES]
    k_sw2 = proj[:, o + LANES:o + 2 * LANES]
    cc2 = cc_ref[...]
    ss2 = ss_ref[...]
    qg = qg_ref[...]
    kg = kg_ref[...]
    lo = lax.broadcasted_iota(jnp.int32, cc2.shape, 1) < ROPE_DIM
    scale = math.log2(math.e) / math.sqrt(QK_DIM)

    q = _dot(c_q.astype(BF16), wq_ref[...])
    q_a = cc2 * qg[1:2, :]
    q_b = ss2 * qg[2:3, :]
    for j in range(nh // 2):
        ro = nh * NOPE_DIM + j * LANES
        q_rp = q[:, ro:ro + LANES]
        q_swp = q[:, ro + nh * ROPE_DIM:ro + nh * ROPE_DIM + LANES]
        rot = q_rp * q_a + q_swp * q_b
        sq_rp = q_rp * q_rp
        rs_pair = []
        for par in range(2):
            hd = 2 * j + par
            q_n = q[:, hd * NOPE_DIM:(hd + 1) * NOPE_DIM]
            own = lo if par == 0 else jnp.logical_not(lo)
            ms = jnp.sum(q_n * q_n + jnp.where(own, sq_rp, 0.0), axis=-1, keepdims=True) / QK_DIM
            rs = lax.rsqrt(ms + RMS_EPS) * scale
            qn_ref[0, hd] = (q_n * rs * qg[0:1, :]).astype(qn_ref.dtype)
            rs_pair.append(rs)
        qrp_ref[0, j] = (rot * jnp.where(lo, rs_pair[0], rs_pair[1])).astype(qrp_ref.dtype)

    kv = _dot(c_kv.astype(BF16), wkv_ref[...])
    k_rot2 = k_r2 * (cc2 * kg[1:2, :]) + k_sw2 * (ss2 * kg[2:3, :])
    k_r_ss = jnp.sum(jnp.where(lo, k_r2 * k_r2, 0.0), axis=-1, keepdims=True)
    for hd in range(nh):
        ko = hd * (NOPE_DIM + V_DIM)
        k_n = kv[:, ko:ko + NOPE_DIM]
        ms_k = (jnp.sum(k_n * k_n, axis=-1, keepdims=True) + k_r_ss) / QK_DIM
        rs_k = lax.rsqrt(ms_k + RMS_EPS)
        own = lo if hd % 2 == 0 else jnp.logical_not(lo)
        k2_ref[0, hd, :, 0:NOPE_DIM] = (k_n * rs_k * kg[0:1, :]).astype(k2_ref.dtype)
        k2_ref[0, hd, :, NOPE_DIM:] = jnp.where(own, k_rot2 * rs_k, 0.0).astype(k2_ref.dtype)
        v_ref[0, hd] = kv[:, ko + NOPE_DIM:ko + NOPE_DIM + V_DIM].astype(v_ref.dtype)


def _swap_halves(a, axis=-1):
    h = a.shape[axis] // 2
    lo = lax.slice_in_dim(a, 0, h, axis=axis)
    hi = lax.slice_in_dim(a, h, 2 * h, axis=axis)
    return jnp.concatenate([hi, lo], axis=axis)


def _mla_proj(x, norm_g, positions, w_in, cq_norm, ckv_norm, w_uq, w_ukv, q_norm, k_norm):
    nb, seq, dm = x.shape
    nh = MLA_HEADS
    tm = ROW_TILE
    n_tok = nb * seq
    cc, ss = _rope_tables(positions)
    cc2 = jnp.concatenate([cc, cc], axis=-1)
    ss2 = jnp.concatenate([ss, ss], axis=-1)
    o = Q_LORA + KV_LORA
    w_kr = w_in[:, o:o + ROPE_DIM]
    w_sw = _swap_halves(w_kr)
    w_in_x = jnp.concatenate([w_in[:, :o], w_kr, w_kr, w_sw, w_sw], axis=1).astype(BF16)
    wq = w_uq.reshape(Q_LORA, nh, QK_DIM)
    wq_n = wq[:, :, :NOPE_DIM].reshape(Q_LORA, nh * NOPE_DIM)
    wq_r = wq[:, :, NOPE_DIM:]
    wq_x = jnp.concatenate([wq_n, wq_r.reshape(Q_LORA, nh * ROPE_DIM),
                            _swap_halves(wq_r).reshape(Q_LORA, nh * ROPE_DIM)], axis=1).astype(BF16)

    def gains(gv):
        r = gv[NOPE_DIM:]
        rs = _swap_halves(r)
        return jnp.stack([gv[:NOPE_DIM], jnp.concatenate([r, r]), jnp.concatenate([rs, rs])])

    x2 = x.reshape(n_tok, dm)
    per_b = seq // tm
    full = lambda a: pl.BlockSpec(a.shape, lambda i: (0,) * a.ndim)
    args = (x2, norm_g[None, :], w_in_x, cq_norm[None, :], ckv_norm[None, :], wq_x,
            w_ukv.astype(BF16), gains(q_norm), gains(k_norm), cc2, ss2)
    in_specs = [pl.BlockSpec((tm, dm), lambda i: (i, 0))] + [full(a) for a in args[1:9]] + [
        pl.BlockSpec((tm, LANES), lambda i: (i, 0)), pl.BlockSpec((tm, LANES), lambda i: (i, 0))]

    def head_spec(n_heads, dh):
        return pl.BlockSpec((1, n_heads, tm, dh), lambda i: (i // per_b, 0, i % per_b, 0))

    def head_shape(n_heads, dh):
        return jax.ShapeDtypeStruct((nb, n_heads, seq, dh), BF16)

    return pl.pallas_call(
        _mla_proj_kernel,
        out_shape=(head_shape(nh, NOPE_DIM), head_shape(nh // 2, LANES), head_shape(nh, NOPE_DIM + LANES),
                   head_shape(nh, V_DIM)),
        grid=(n_tok // tm,),
        in_specs=in_specs,
        out_specs=(head_spec(nh, NOPE_DIM), head_spec(nh // 2, LANES), head_spec(nh, NOPE_DIM + LANES),
                   head_spec(nh, V_DIM)),
        compiler_params=_cparams("parallel"),
        name="mla_proj",
    )(*args)


def _attention_kernel(qn_ref, qrp_ref, k2_ref, v_ref, o_ref):
    seq = qn_ref.shape[2]
    t = ATT_TQ
    nt = (((1,), (1,)), ((), ()))
    tri = (lax.broadcasted_iota(jnp.int32, (t, t), 1) <= lax.broadcasted_iota(jnp.int32, (t, t), 0))
    for i in range(seq // t):
        q = jnp.concatenate([qn_ref[0, 0, i * t:(i + 1) * t, :], qrp_ref[0, 0, i * t:(i + 1) * t, :]], axis=-1)
        m = jnp.full((t, 1), NEG_INF, F32)
        l = jnp.zeros((t, 1), F32)
        acc = jnp.zeros((t, V_DIM), F32)
        for j in range(i + 1):
            s = lax.dot_general(q, k2_ref[0, 0, j * t:(j + 1) * t, :], nt, preferred_element_type=F32)
            if j == i:
                s = jnp.where(tri, s, NEG_INF)
            m_new = jnp.maximum(m, jnp.max(s, axis=-1, keepdims=True))
            alpha = jnp.exp2(m - m_new)
            p = jnp.exp2(s - m_new)
            l = alpha * l + jnp.sum(p, axis=-1, keepdims=True)
            acc = alpha * acc + _dot(p.astype(BF16), v_ref[0, 0, j * t:(j + 1) * t, :])
            m = m_new
        o_ref[0, i * t:(i + 1) * t, :] = (acc / l).astype(o_ref.dtype)


def _attention(qn, qrp, k2, v):
    nb, nh, seq, _ = qn.shape

    def spec(dh, per=1):
        return pl.BlockSpec((1, 1, seq, dh), lambda b, h: (b, h // per, 0, 0))

    return pl.pallas_call(
        _attention_kernel,
        out_shape=jax.ShapeDtypeStruct((nb, seq, nh * V_DIM), BF16),
        grid=(nb, nh),
        in_specs=[spec(NOPE_DIM), spec(LANES, 2), spec(NOPE_DIM + LANES), spec(V_DIM)],
        out_specs=pl.BlockSpec((1, seq, V_DIM), lambda b, h: (b, 0, h)),
        compiler_params=_cparams("parallel", "parallel"),
        name="mla_attention",
    )(qn, qrp, k2, v)


def _proj_residual_kernel(a_ref, w_ref, x_ref, o_ref):
    o_ref[...] = x_ref[...] + _dot(a_ref[...], w_ref[...])


def _proj_residual(a, w, x2):
    n_tok, dm = x2.shape
    tm = ROW_TILE
    return pl.pallas_call(
        _proj_residual_kernel,
        out_shape=jax.ShapeDtypeStruct(x2.shape, x2.dtype),
        grid=(n_tok // tm,),
        in_specs=[pl.BlockSpec((tm, a.shape[1]), lambda i: (i, 0)),
                  pl.BlockSpec(w.shape, lambda i: (0, 0)),
                  pl.BlockSpec((tm, dm), lambda i: (i, 0))],
        out_specs=pl.BlockSpec((tm, dm), lambda i: (i, 0)),
        compiler_params=_cparams("parallel"),
        name="proj_residual",
    )(a, w, x2)


def _mla_mixer(x, norm_g, positions, w_in, cq_norm, ckv_norm, w_uq, w_ukv, q_norm, k_norm, w_o):
    nb, seq, dm = x.shape
    qn, qrp, k2, v = _mla_proj(x, norm_g, positions, w_in, cq_norm, ckv_norm, w_uq, w_ukv, q_norm, k_norm)
    o = _attention(qn, qrp, k2, v)
    out = _proj_residual(o.reshape(nb * seq, -1), w_o.astype(BF16), x.reshape(nb * seq, dm))
    return out.reshape(nb, seq, dm)


def _router_kernel(x_ref, g_ref, w_ref, b_ref, meta_ref, gate_ref, cnt_ref, base_ref):
    tm, dm = x_ref.shape
    ne = N_EXPERTS

    @pl.when(pl.program_id(0) == 0)
    def _():
        base_ref[...] = jnp.zeros_like(base_ref)

    hn = _rms(x_ref[...], g_ref[...])
    logits = _dot(hn.astype(BF16), w_ref[...]) + b_ref[...]
    lane = lax.broadcasted_iota(jnp.int32, logits.shape, 1)
    big = jnp.int32(1 << 20)

    def first_argmax(v):
        m = jnp.max(v, axis=-1, keepdims=True)
        idx = jnp.min(jnp.where(v == m, lane, big), axis=-1, keepdims=True)
        return m, idx

    is_g = (lane >= ne) & (lane < ne + N_GROUPS)
    gl = jnp.where(is_g, logits, NEG_INF)
    g_max, g_idx = first_argmax(gl)
    g_sel = g_idx - ne
    p_g = 1.0 / jnp.sum(jnp.where(is_g, jnp.exp(gl - g_max), 0.0), axis=-1, keepdims=True)
    in_grp = (lane >= g_sel * EXPERTS_PER_GROUP) & (lane < (g_sel + 1) * EXPERTS_PER_GROUP)
    el = jnp.where(in_grp, logits, NEG_INF)
    v0, e0 = first_argmax(el)
    v1, e1 = first_argmax(jnp.where(lane == e0, NEG_INF, el))
    t = jnp.exp(v1 - v0)
    gate0 = p_g / (1.0 + t)
    gate1 = p_g * t / (1.0 + t)

    oh0 = (lane == e0).astype(F32)
    oh1 = (lane == e1).astype(F32)
    both = oh0 + oh1
    r_i = lax.broadcasted_iota(jnp.int32, (tm, tm), 0)
    c_i = lax.broadcasted_iota(jnp.int32, (tm, tm), 1)
    strict_lower = (c_i < r_i).astype(BF16)
    before = _dot(strict_lower, both.astype(BF16)) + base_ref[0:1, :]
    rank0 = jnp.sum(oh0 * before, axis=-1, keepdims=True)
    rank1 = jnp.sum(oh1 * before, axis=-1, keepdims=True)
    base_ref[0:1, :] = base_ref[0:1, :] + jnp.sum(both, axis=0, keepdims=True)
    cnt_ref[...] = jnp.broadcast_to(base_ref[0:1, :], cnt_ref.shape).astype(jnp.int32)

    meta = jnp.where(lane == 0, e0, jnp.where(lane == 1, e1, jnp.where(
        lane == 2, rank0.astype(jnp.int32), jnp.where(lane == 3, rank1.astype(jnp.int32), 0))))
    meta_ref[...] = meta
    gate_ref[...] = jnp.where(lane == 0, gate0, jnp.where(lane == 1, gate1, 0.0))


def _router(x2, norm_g, w_group, b_group, w_expert, b_expert):
    n_tok, dm = x2.shape
    tm = ROW_TILE
    pad = LANES - N_EXPERTS - N_GROUPS
    w = jnp.concatenate([w_expert, w_group, jnp.zeros((dm, pad), F32)], axis=1).astype(BF16)
    b = jnp.concatenate([b_expert, b_group, jnp.zeros((pad,), F32)])[None, :]
    return pl.pallas_call(
        _router_kernel,
        out_shape=(jax.ShapeDtypeStruct((n_tok, LANES), jnp.int32),
                   jax.ShapeDtypeStruct((n_tok, LANES), F32),
                   jax.ShapeDtypeStruct((SUBLANES, LANES), jnp.int32)),
        grid=(n_tok // tm,),
        in_specs=[pl.BlockSpec((tm, dm), lambda i: (i, 0)),
                  pl.BlockSpec((1, dm), lambda i: (0, 0)),
                  pl.BlockSpec((dm, LANES), lambda i: (0, 0)),
                  pl.BlockSpec((1, LANES), lambda i: (0, 0))],
        out_specs=(pl.BlockSpec((tm, LANES), lambda i: (i, 0)),
                   pl.BlockSpec((tm, LANES), lambda i: (i, 0)),
                   pl.BlockSpec((SUBLANES, LANES), lambda i: (0, 0))),
        scratch_shapes=[pltpu.VMEM((SUBLANES, LANES), F32)],
        compiler_params=_cparams("arbitrary"),
        name="moe_router",
    )(x2, norm_g[None, :], w, b)


def _dispatch_kernel(dest_ref, dest_prev_ref, x_ref, g_ref, xd_ref, hn_ref, zero_ref, sem, zsem):
    tm = x_ref.shape[0]
    i = pl.program_id(0)
    n_steps = pl.num_programs(0)
    slot = i % 2
    n_assign = n_steps * tm * TOP_K

    def tail_copy():
        return pltpu.make_async_copy(zero_ref, xd_ref.at[pl.ds(n_assign * ROW_CHUNKS, zero_ref.shape[0])], zsem)

    @pl.when(i == 0)
    def _():
        zero_ref[...] = jnp.zeros_like(zero_ref)
        tail_copy().start()

    hn = _rms(x_ref[...], g_ref[...])
    hbuf = hn_ref.at[slot]
    for c in range(ROW_CHUNKS):
        hbuf[pl.ds(c, tm, stride=ROW_CHUNKS), :] = hn[:, c * LANES:(c + 1) * LANES]

    def row_copy(d_ref, sl, t, k):
        src = pl.multiple_of(t * ROW_CHUNKS, ROW_CHUNKS)
        dst = pl.multiple_of(d_ref[0, 0, 2 * t + k], ROW_CHUNKS)
        return pltpu.make_async_copy(hn_ref.at[sl, pl.ds(src, ROW_CHUNKS)], xd_ref.at[pl.ds(dst, ROW_CHUNKS)],
                                     sem.at[sl])

    def issue(j, c):
        for u in range(DMA_UNROLL):
            for k in range(TOP_K):
                row_copy(dest_ref, slot, j * DMA_UNROLL + u, k).start(priority=k)
        return c

    lax.fori_loop(0, tm // DMA_UNROLL, issue, 0)

    def drain_all(d_ref, sl):
        def drain(j, c):
            for u in range(DMA_UNROLL):
                for k in range(TOP_K):
                    row_copy(d_ref, sl, j * DMA_UNROLL + u, k).wait()
            return c
        lax.fori_loop(0, tm // DMA_UNROLL, drain, 0)

    @pl.when(i >= 1)
    def _():
        drain_all(dest_prev_ref, 1 - slot)

    @pl.when(i == n_steps - 1)
    def _():
        drain_all(dest_ref, slot)

    @pl.when(i == 0)
    def _():
        tail_copy().wait()


def _dispatch(x2, norm_g, dest_off):
    tm = GATHER_TM
    n_tok, dm = x2.shape
    dest3 = dest_off.reshape(n_tok // tm, 1, 2 * tm)
    n_rows = n_tok * TOP_K + MOE_BM
    return pl.pallas_call(
        _dispatch_kernel,
        out_shape=jax.ShapeDtypeStruct((n_rows * ROW_CHUNKS, LANES), F32),
        grid=(n_tok // tm,),
        in_specs=[pl.BlockSpec((1, 1, 2 * tm), lambda i: (i, 0, 0), memory_space=pltpu.SMEM),
                  pl.BlockSpec((1, 1, 2 * tm), lambda i: (jnp.maximum(i - 1, 0), 0, 0), memory_space=pltpu.SMEM),
                  pl.BlockSpec((tm, dm), lambda i: (i, 0)),
                  pl.BlockSpec((1, dm), lambda i: (0, 0))],
        out_specs=pl.BlockSpec(memory_space=pl.ANY),
        scratch_shapes=[pltpu.VMEM((2, tm * ROW_CHUNKS, LANES), F32),
                        pltpu.VMEM((MOE_BM * ROW_CHUNKS, LANES), F32),
                        pltpu.SemaphoreType.DMA((2,)), pltpu.SemaphoreType.DMA],
        compiler_params=_cparams("arbitrary"),
        name="moe_dispatch",
    )(dest3, dest3, x2, norm_g[None, :])


def _rows_from_tiles(ref, n, base=0):
    return jnp.concatenate([ref[pl.ds(base + c, n, stride=ROW_CHUNKS), :] for c in range(ROW_CHUNKS)], axis=-1)


def _expert_kernel(be_ref, bs_ref, first_ref, nxt_ref, nu_ref, xd_ref, w1_ref, w3_ref, w2_ref, yd_ref,
                   xbuf, ybuf, w1buf, w3buf, w2buf, w13_s, w2_s, sem_in, sem_out, sem_w, *, layer):
    nu = nu_ref[0]
    n_rows = xbuf.shape[1]
    bm = n_rows // ROW_CHUNKS
    de = w1buf.shape[2]

    def in_copy(blk, sl):
        src = pl.multiple_of(bs_ref[blk] * ROW_CHUNKS, ROW_CHUNKS)
        return pltpu.make_async_copy(xd_ref.at[pl.ds(src, n_rows)], xbuf.at[sl], sem_in.at[sl])

    def out_copy(blk, sl):
        dst = pl.multiple_of(bs_ref[blk] * ROW_CHUNKS, ROW_CHUNKS)
        return pltpu.make_async_copy(ybuf.at[sl], yd_ref.at[pl.ds(dst, n_rows)], sem_out.at[sl])

    def w_copies(e, ws):
        return (pltpu.make_async_copy(w1_ref.at[layer, e], w1buf.at[ws], sem_w.at[ws]),
                pltpu.make_async_copy(w3_ref.at[layer, e], w3buf.at[ws], sem_w.at[ws]),
                pltpu.make_async_copy(w2_ref.at[layer, e], w2buf.at[ws], sem_w.at[ws]))

    in_copy(0, 0).start()
    for cp in w_copies(be_ref[0], 1):
        cp.start()

    def block(b, ws):
        slot = b % 2

        @pl.when(b + 1 < nu)
        def _():
            in_copy(b + 1, 1 - slot).start()

        first = first_ref[b] == 1
        ws = jnp.where(first, 1 - ws, ws)

        @pl.when(first)
        def _():
            for cp in w_copies(be_ref[b], ws):
                cp.wait()
            w13_s[:, :de] = w1buf[ws].astype(BF16)
            w13_s[:, de:] = w3buf[ws].astype(BF16)
            w2_s[...] = w2buf[ws].astype(BF16)

            @pl.when(nxt_ref[b] >= 0)
            def _():
                for cp in w_copies(nxt_ref[b], 1 - ws):
                    cp.start()

        in_copy(b, slot).wait()
        x = _rows_from_tiles(xbuf.at[slot], bm).astype(BF16)
        h = _dot(x, w13_s[...])
        a = jax.nn.silu(h[:, :de]) * h[:, de:]
        y = _dot(a.astype(BF16), w2_s[...])
        yb = ybuf.at[slot]
        for c in range(ROW_CHUNKS):
            yb[pl.ds(c, bm, stride=ROW_CHUNKS), :] = y[:, c * LANES:(c + 1) * LANES]

        @pl.when(b >= 1)
        def _():
            out_copy(b - 1, 1 - slot).wait()
        out_copy(b, slot).start()
        return ws

    lax.fori_loop(0, nu, block, jnp.int32(0))

    last = nu - 1
    out_copy(last, last % 2).wait()
    tail = yd_ref.shape[0] - n_rows
    fill_slot = 1 - last % 2
    ybuf[fill_slot] = jnp.zeros((n_rows, LANES), F32)
    fill = pltpu.make_async_copy(ybuf.at[fill_slot], yd_ref.at[pl.ds(tail, n_rows)], sem_out.at[fill_slot])
    fill.start()
    fill.wait()


def _experts(xd, blk_expert, blk_start, blk_first, blk_next, n_used, w1, w3, w2, layer):
    bm = MOE_BM
    _, _, dm, de = w1.shape
    any_spec = pl.BlockSpec(memory_space=pl.ANY)
    grid_spec = pltpu.PrefetchScalarGridSpec(
        num_scalar_prefetch=5,
        grid=(1,),
        in_specs=[any_spec, any_spec, any_spec, any_spec],
        out_specs=any_spec,
        scratch_shapes=[pltpu.VMEM((2, bm * ROW_CHUNKS, LANES), F32),
                        pltpu.VMEM((2, bm * ROW_CHUNKS, LANES), F32),
                        pltpu.VMEM((2, dm, de), F32),
                        pltpu.VMEM((2, dm, de), F32),
                        pltpu.VMEM((2, de, dm), F32),
                        pltpu.VMEM((dm, 2 * de), BF16),
                        pltpu.VMEM((de, dm), BF16),
                        pltpu.SemaphoreType.DMA((2,)),
                        pltpu.SemaphoreType.DMA((2,)),
                        pltpu.SemaphoreType.DMA((2,))],
    )
    return pl.pallas_call(
        functools.partial(_expert_kernel, layer=layer),
        out_shape=jax.ShapeDtypeStruct(xd.shape, F32),
        grid_spec=grid_spec,
        compiler_params=_cparams("arbitrary"),
        name="moe_experts",
    )(blk_expert, blk_start, blk_first, blk_next, n_used, xd, w1, w3, w2)


def _combine_kernel(dest_ref, dest_next_ref, yd_ref, gate_ref, x_ref, o_ref, buf_ref, sem):
    tm, dm = x_ref.shape
    i = pl.program_id(0)
    n_steps = pl.num_programs(0)
    slot = i % 2
    k_rows = tm * ROW_CHUNKS

    def row_copy(d_ref, sl, t, k):
        src = pl.multiple_of(d_ref[0, 0, 2 * t + k], ROW_CHUNKS)
        dst = pl.multiple_of(k * k_rows + t * ROW_CHUNKS, ROW_CHUNKS)
        return pltpu.make_async_copy(yd_ref.at[pl.ds(src, ROW_CHUNKS)],
                                     buf_ref.at[sl, pl.ds(dst, ROW_CHUNKS)], sem.at[sl])

    def issue_all(d_ref, sl):
        def issue(j, c):
            for u in range(DMA_UNROLL):
                for k in range(TOP_K):
                    row_copy(d_ref, sl, j * DMA_UNROLL + u, k).start(priority=k)
            return c
        lax.fori_loop(0, tm // DMA_UNROLL, issue, 0)

    @pl.when(i == 0)
    def _():
        issue_all(dest_ref, slot)

    @pl.when(i + 1 < n_steps)
    def _():
        issue_all(dest_next_ref, 1 - slot)

    def drain(j, c):
        for u in range(DMA_UNROLL):
            for k in range(TOP_K):
                row_copy(dest_ref, slot, j * DMA_UNROLL + u, k).wait()
        return c

    lax.fori_loop(0, tm // DMA_UNROLL, drain, 0)

    g0 = gate_ref[:, 0:1]
    g1 = gate_ref[:, 1:2]
    buf = buf_ref.at[slot]
    for c in range(ROW_CHUNKS):
        y0 = buf[pl.ds(c, tm, stride=ROW_CHUNKS), :]
        y1 = buf[pl.ds(k_rows + c, tm, stride=ROW_CHUNKS), :]
        o_ref[:, c * LANES:(c + 1) * LANES] = x_ref[:, c * LANES:(c + 1) * LANES] + (g0 * y0 + g1 * y1)


def _combine(yd, dest_off, gates, x2):
    n_tok, dm = x2.shape
    tm = GATHER_TM
    n_steps = n_tok // tm
    dest3 = dest_off.reshape(n_steps, 1, 2 * tm)
    return pl.pallas_call(
        _combine_kernel,
        out_shape=jax.ShapeDtypeStruct(x2.shape, x2.dtype),
        grid=(n_steps,),
        in_specs=[pl.BlockSpec((1, 1, 2 * tm), lambda i: (i, 0, 0), memory_space=pltpu.SMEM),
                  pl.BlockSpec((1, 1, 2 * tm), lambda i: (jnp.minimum(i + 1, n_steps - 1), 0, 0),
                               memory_space=pltpu.SMEM),
                  pl.BlockSpec(memory_space=pl.ANY),
                  pl.BlockSpec((tm, LANES), lambda i: (i, 0)),
                  pl.BlockSpec((tm, dm), lambda i: (i, 0))],
        out_specs=pl.BlockSpec((tm, dm), lambda i: (i, 0)),
        scratch_shapes=[pltpu.VMEM((2, TOP_K * tm * ROW_CHUNKS, LANES), F32),
                        pltpu.SemaphoreType.DMA((2,))],
        compiler_params=_cparams("arbitrary"),
        name="moe_combine",
    )(dest3, dest3, yd, gates, x2)


def _moe(x, norm_g, w_group, b_group, w_expert, b_expert, w1, w3, w2, layer):
    nb, seq, dm = x.shape
    n_tok = nb * seq
    n_assign = n_tok * TOP_K
    x2 = x.reshape(n_tok, dm)
    meta, gates, counts = _router(x2, norm_g, w_group, b_group, w_expert, b_expert)
    bm = MOE_BM
    n_blocks = n_assign // bm + N_EXPERTS
    counts = counts[0, :N_EXPERTS]
    starts = jnp.cumsum(counts) - counts
    eid = meta[:, 0:TOP_K]
    e_iota = jnp.arange(N_EXPERTS, dtype=jnp.int32)
    seg_start = jnp.sum(jnp.where(eid[:, :, None] == e_iota, starts, 0), axis=-1)
    dest_off = ((seg_start + meta[:, TOP_K:2 * TOP_K]) * ROW_CHUNKS).astype(jnp.int32).reshape(-1)
    nblk = (counts + bm - 1) // bm
    blk_end = jnp.cumsum(nblk)
    b_iota = jnp.arange(n_blocks, dtype=jnp.int32)
    blk_expert = jnp.minimum(jnp.sum((b_iota[:, None] >= blk_end[None, :]).astype(jnp.int32), axis=1),
                             N_EXPERTS - 1)
    j = b_iota - (blk_end - nblk)[blk_expert]
    n_used = blk_end[-1].astype(jnp.int32).reshape(1)
    blk_start = jnp.where(b_iota < n_used[0], starts[blk_expert] + j * bm, 0).astype(jnp.int32)
    blk_first = (j == 0).astype(jnp.int32)
    cand = jnp.where(counts > 0, e_iota, N_EXPERTS)
    later = lax.cummin(cand[::-1])[::-1]
    next_e = jnp.concatenate([later[1:], jnp.full((1,), N_EXPERTS, jnp.int32)])
    blk_next = jnp.where(next_e[blk_expert] < N_EXPERTS, next_e[blk_expert], -1).astype(jnp.int32)

    xd = _dispatch(x2, norm_g, dest_off)
    yd = _experts(xd, blk_expert, blk_start, blk_first, blk_next, n_used, w1, w3, w2, layer)
    out = _combine(yd, dest_off, gates, x2)
    return out.reshape(nb, seq, dm)


def kernel(x, positions, ev_norm, ev_w_in, s5_lambda_re, s5_lambda_im, s5_log_dt, s5_b_re, s5_b_im,
           s5_c_re, s5_c_im, s5_d, s5_glu_w, s5_glu_b, lru_conv_w, lru_conv_b, lru_wa, lru_ba, lru_wi,
           lru_bi, lru_lambda, ev_w_out, od_norm, mla_w_in, mla_cq_norm, mla_ckv_norm, mla_w_uq,
           mla_w_ukv, mla_q_norm, mla_k_norm, mla_w_o, ffn_norm, moe_w_group, moe_b_group,
           moe_w_expert, moe_b_expert, moe_w1, moe_w3, moe_w2):
    depth = ffn_norm.shape[0]
    for layer in range(depth):
        j = layer // 2
        if layer % 2 == 0:
            x = _even_mixer(x, ev_norm[j], ev_w_in[j], s5_lambda_re[j], s5_lambda_im[j], s5_log_dt[j],
                            s5_b_re[j], s5_b_im[j], s5_c_re[j], s5_c_im[j], s5_d[j], s5_glu_w[j],
                            s5_glu_b[j], lru_conv_w[j], lru_conv_b[j], lru_wa[j], lru_ba[j], lru_wi[j],
                            lru_bi[j], lru_lambda[j], ev_w_out[j])
        else:
            x = _mla_mixer(x, od_norm[j], positions, mla_w_in[j], mla_cq_norm[j], mla_ckv_norm[j],
                           mla_w_uq[j], mla_w_ukv[j], mla_q_norm[j], mla_k_norm[j], mla_w_o[j])
        x = _moe(x, ffn_norm[layer], moe_w_group[layer], moe_b_group[layer], moe_w_expert[layer],
                 moe_b_expert[layer], moe_w1, moe_w3, moe_w2, layer)
    return x
```

```python
import functools
import math

import jax
import jax.numpy as jnp
from jax import lax
from jax.experimental import pallas as pl
from jax.experimental.pallas import tpu as pltpu

F32 = jnp.float32
BF16 = jnp.bfloat16

S5_GROUP_CH = 16
S5_STATE = 64
LRU_BLOCKS = 8
CONV_WIDTH = 4
RG_C = 8.0
MLA_HEADS = 8
Q_LORA = 384
KV_LORA = 256
NOPE_DIM = 128
ROPE_DIM = 64
V_DIM = 128
QK_DIM = NOPE_DIM + ROPE_DIM
ROPE_THETA = 10000.0
N_GROUPS = 8
EXPERTS_PER_GROUP = 8
N_EXPERTS = N_GROUPS * EXPERTS_PER_GROUP
TOP_K = 2
RMS_EPS = 1e-6
NEG_INF = -1e30

SUBLANES = 8
LANES = 128
ROW_CHUNKS = 8
VMEM_LIMIT = 48 * 1024 * 1024

EVEN_TL = 64
ROW_TILE = 512
MOE_BM = 256
ATT_TQ = 512
GATHER_TM = 512
DMA_UNROLL = 8


def _cparams(*sem):
    return pltpu.CompilerParams(dimension_semantics=tuple(sem), vmem_limit_bytes=VMEM_LIMIT)


def _rms(x, g):
    ms = jnp.mean(x * x, axis=-1, keepdims=True)
    return x * lax.rsqrt(ms + RMS_EPS) * g


def _dot(a, b):
    return jnp.dot(a, b, preferred_element_type=F32)


def _s5_discretize_kernel(lre_ref, lim_ref, ldt_ref, bre_ref, bim_ref,
                          lbre_ref, lbim_ref, bbre_ref, bbim_ref):
    lr = jnp.minimum(lre_ref[...], -1e-4)
    li = lim_ref[...]
    dt = jnp.exp(ldt_ref[...])
    mag = jnp.exp(lr * dt)
    lb_re = mag * jnp.cos(li * dt)
    lb_im = mag * jnp.sin(li * dt)
    den = lr * lr + li * li
    num_re = lb_re - 1.0
    f_re = (num_re * lr + lb_im * li) / den
    f_im = (lb_im * lr - num_re * li) / den
    lbre_ref[...] = lb_re
    lbim_ref[...] = lb_im
    br = bre_ref[...]
    bi = bim_ref[...]
    bbre_ref[...] = f_re[:, None, :] * br - f_im[:, None, :] * bi
    bbim_ref[...] = f_re[:, None, :] * bi + f_im[:, None, :] * br


def _s5_discretize(lam_re, lam_im, log_dt, b_re, b_im):
    g, p = lam_re.shape
    h = b_re.shape[-1]
    b_re_t = jnp.swapaxes(b_re, 1, 2)
    b_im_t = jnp.swapaxes(b_im, 1, 2)
    return pl.pallas_call(
        _s5_discretize_kernel,
        out_shape=(jax.ShapeDtypeStruct((g, p), F32), jax.ShapeDtypeStruct((g, p), F32),
                   jax.ShapeDtypeStruct((g, h, p), F32), jax.ShapeDtypeStruct((g, h, p), F32)),
        name="s5_discretize",
    )(lam_re, lam_im, log_dt[:, None], b_re_t, b_im_t)


def _even_inproj_kernel(x_ref, g_ref, w_ref, o_ref, h_ref):
    nb, tl, _ = x_ref.shape
    nc = h_ref.shape[0]
    g = g_ref[...]
    for b in range(nb):
        h = _rms(x_ref[b], g)
        for c in range(nc):
            h_ref[c, pl.ds(b, tl, stride=nb), :] = h[:, c * LANES:(c + 1) * LANES]
    h_all = jnp.concatenate([h_ref[c] for c in range(nc)], axis=-1)
    o_ref[...] = _dot(h_all.astype(BF16), w_ref[...]).astype(o_ref.dtype)


def _even_inproj(x, g, w):
    nb, seq, d = x.shape
    n_out = w.shape[1]
    tl = EVEN_TL
    return pl.pallas_call(
        _even_inproj_kernel,
        out_shape=jax.ShapeDtypeStruct((seq * nb, n_out), BF16),
        grid=(seq // tl,),
        in_specs=[pl.BlockSpec((nb, tl, d), lambda i: (0, i, 0)),
                  pl.BlockSpec((1, d), lambda i: (0, 0)),
                  pl.BlockSpec((d, n_out), lambda i: (0, 0))],
        out_specs=pl.BlockSpec((tl * nb, n_out), lambda i: (i, 0)),
        scratch_shapes=[pltpu.VMEM((d // LANES, tl * nb, LANES), F32)],
        compiler_params=_cparams("parallel"),
        name="even_inproj",
    )(x, g, w)


def _s5_kernel(u_ref, bm_ref, lre_ref, lim_ref, cm_ref, d_ref, gw_ref, gb_ref, o_ref,
               bu_ref, sb_ref, st_ref, *, nb):
    rows, width = u_ref.shape
    n_half = bm_ref.shape[0]
    kin = width // n_half
    ncol = bm_ref.shape[2]
    nre = ncol // 2
    chunk = 512
    steps = rows // nb

    @pl.when(pl.program_id(0) == 0)
    def _():
        st_ref[...] = jnp.zeros_like(st_ref)

    u = u_ref[...]
    for j in range(n_half):
        bu_ref[:, j * ncol:(j + 1) * ncol] = _dot(u[:, j * kin:(j + 1) * kin], bm_ref[j])

    for j in range(n_half):
        for c in range(nre // chunk):
            cr = j * ncol + c * chunk
            ci = cr + nre
            lc = j * nre + c * chunk
            lr = lre_ref[:, lc:lc + chunk]
            li = lim_ref[:, lc:lc + chunk]

            def body(i, carry, cr=cr, ci=ci, lr=lr, li=li):
                sr, si = carry
                r0 = pl.multiple_of(i * (2 * nb), 2 * nb)
                sr1 = lr * sr - li * si + bu_ref[pl.ds(r0, nb), cr:cr + chunk]
                si1 = lr * si + li * sr + bu_ref[pl.ds(r0, nb), ci:ci + chunk]
                sr2 = lr * sr1 - li * si1 + bu_ref[pl.ds(r0 + nb, nb), cr:cr + chunk]
                si2 = lr * si1 + li * sr1 + bu_ref[pl.ds(r0 + nb, nb), ci:ci + chunk]
                sb_ref[pl.ds(r0, 2 * nb), cr:cr + chunk] = jnp.concatenate([sr1, sr2], 0).astype(BF16)
                sb_ref[pl.ds(r0, 2 * nb), ci:ci + chunk] = jnp.concatenate([si1, si2], 0).astype(BF16)
                return sr2, si2

            sr, si = lax.fori_loop(0, steps // 2, body,
                                   (st_ref[:, cr:cr + chunk], st_ref[:, ci:ci + chunk]))
            st_ref[:, cr:cr + chunk] = sr
            st_ref[:, ci:ci + chunk] = si

    ys = [_dot(sb_ref[:, j * ncol:(j + 1) * ncol], cm_ref[j]) for j in range(n_half)]
    y = jnp.concatenate(ys, axis=-1) + d_ref[...] * u.astype(F32)
    y = jax.nn.gelu(y)
    z = _dot(y.astype(BF16), gw_ref[...]) + gb_ref[...]
    o_ref[...] = (y * jax.nn.sigmoid(z)).astype(o_ref.dtype)


def _s5_mixer(proj, nb, bm, lre, lim, cm, d, glu_w, glu_b):
    rows_total = proj.shape[0]
    width = d.shape[1]
    rows = EVEN_TL * nb
    n_half, _, ncol = bm.shape
    kern = functools.partial(_s5_kernel, nb=nb)
    return pl.pallas_call(
        kern,
        out_shape=jax.ShapeDtypeStruct((rows_total, width), BF16),
        grid=(rows_total // rows,),
        in_specs=[pl.BlockSpec((rows, width), lambda i: (i, 0)),
                  pl.BlockSpec(bm.shape, lambda i: (0, 0, 0)),
                  pl.BlockSpec(lre.shape, lambda i: (0, 0)),
                  pl.BlockSpec(lim.shape, lambda i: (0, 0)),
                  pl.BlockSpec(cm.shape, lambda i: (0, 0, 0)),
                  pl.BlockSpec((1, width), lambda i: (0, 0)),
                  pl.BlockSpec(glu_w.shape, lambda i: (0, 0)),
                  pl.BlockSpec((1, width), lambda i: (0, 0))],
        out_specs=pl.BlockSpec((rows, width), lambda i: (i, 0)),
        scratch_shapes=[pltpu.VMEM((rows, n_half * ncol), F32),
                        pltpu.VMEM((rows, n_half * ncol), BF16),
                        pltpu.VMEM((nb, n_half * ncol), F32)],
        compiler_params=_cparams("arbitrary"),
        name="s5_mixer",
    )(proj, bm, lre, lim, cm, d, glu_w, glu_b)


def _lru_kernel(x_ref, gate_ref, cw_ref, cb_ref, wa_ref, ba_ref, wi_ref, bi_ref, lam_ref, o_ref,
                xp_ref, a_ref, b_ref, h_ref, *, nb):
    rows, width = x_ref.shape
    halo = (CONV_WIDTH - 1) * nb
    steps = rows // nb

    @pl.when(pl.program_id(0) == 0)
    def _():
        xp_ref[0:halo, :] = jnp.zeros((halo, width), F32)
        h_ref[...] = jnp.zeros_like(h_ref)

    xp_ref[halo:halo + rows, :] = x_ref[...].astype(F32)
    xf = cb_ref[...] + cw_ref[0:1, :] * xp_ref[0:rows, :]
    for k in range(1, CONV_WIDTH):
        xf = xf + cw_ref[k:k + 1, :] * xp_ref[k * nb:k * nb + rows, :]
    xp_ref[0:halo, :] = xp_ref[rows:rows + halo, :]

    xb = xf.astype(BF16)
    nblk = wa_ref.shape[0]
    kb = width // nblk
    ga = jnp.concatenate([_dot(xb[:, j * kb:(j + 1) * kb], wa_ref[j]) for j in range(nblk)], -1)
    gi = jnp.concatenate([_dot(xb[:, j * kb:(j + 1) * kb], wi_ref[j]) for j in range(nblk)], -1)
    r = jax.nn.sigmoid(ga + ba_ref[...])
    ig = jax.nn.sigmoid(gi + bi_ref[...])
    log_a = RG_C * r * jax.nn.log_sigmoid(lam_ref[...])
    a_ref[...] = jnp.exp(log_a)
    th = jnp.tanh(log_a)
    b_ref[...] = jnp.sqrt(-2.0 * th / (1.0 - th)) * (ig * xf)

    def body(t, h):
        r0 = pl.multiple_of(t * nb, nb)
        h = a_ref[pl.ds(r0, nb), :] * h + b_ref[pl.ds(r0, nb), :]
        b_ref[pl.ds(r0, nb), :] = h
        return h

    h_ref[...] = lax.fori_loop(0, steps, body, h_ref[...])
    o_ref[...] = (b_ref[...] * jax.nn.gelu(gate_ref[...].astype(F32))).astype(o_ref.dtype)


def _lru_mixer(proj, nb, conv_w, conv_b, wa, ba, wi, bi, lam):
    rows_total = proj.shape[0]
    width = conv_w.shape[1]
    rows = EVEN_TL * nb
    halo = (CONV_WIDTH - 1) * nb
    kern = functools.partial(_lru_kernel, nb=nb)
    vec = pl.BlockSpec((1, width), lambda i: (0, 0))
    return pl.pallas_call(
        kern,
        out_shape=jax.ShapeDtypeStruct((rows_total, width), BF16),
        grid=(rows_total // rows,),
        in_specs=[pl.BlockSpec((rows, width), lambda i: (i, 1)),
                  pl.BlockSpec((rows, width), lambda i: (i, 2)),
                  pl.BlockSpec(conv_w.shape, lambda i: (0, 0)),
                  vec,
                  pl.BlockSpec(wa.shape, lambda i: (0, 0, 0)),
                  vec,
                  pl.BlockSpec(wi.shape, lambda i: (0, 0, 0)),
                  vec, vec],
        out_specs=pl.BlockSpec((rows, width), lambda i: (i, 0)),
        scratch_shapes=[pltpu.VMEM((rows + halo, width), F32),
                        pltpu.VMEM((rows, width), F32),
                        pltpu.VMEM((rows, width), F32),
                        pltpu.VMEM((nb, width), F32)],
        compiler_params=_cparams("arbitrary"),
        name="lru_mixer",
    )(proj, proj, conv_w, conv_b, wa, ba, wi, bi, lam)


def _even_outproj_kernel(ya_ref, yb_ref, wa_ref, wb_ref, x_ref, o_ref, y_ref):
    nb, tl, _ = x_ref.shape
    nc = y_ref.shape[0]
    y = _dot(ya_ref[...], wa_ref[...]) + _dot(yb_ref[...], wb_ref[...])
    for c in range(nc):
        y_ref[c] = y[:, c * LANES:(c + 1) * LANES]
    for b in range(nb):
        for c in range(nc):
            o_ref[b, :, c * LANES:(c + 1) * LANES] = (x_ref[b, :, c * LANES:(c + 1) * LANES]
                                                     + y_ref[c, pl.ds(b, tl, stride=nb), :])


def _even_outproj(ya, yb, w_a, w_b, x):
    nb, seq, d = x.shape
    tl = EVEN_TL
    rows = tl * nb
    wd = ya.shape[1]
    return pl.pallas_call(
        _even_outproj_kernel,
        out_shape=jax.ShapeDtypeStruct(x.shape, x.dtype),
        grid=(seq // tl,),
        in_specs=[pl.BlockSpec((rows, wd), lambda i: (i, 0)),
                  pl.BlockSpec((rows, wd), lambda i: (i, 0)),
                  pl.BlockSpec(w_a.shape, lambda i: (0, 0)),
                  pl.BlockSpec(w_b.shape, lambda i: (0, 0)),
                  pl.BlockSpec((nb, tl, d), lambda i: (0, i, 0))],
        out_specs=pl.BlockSpec((nb, tl, d), lambda i: (0, i, 0)),
        scratch_shapes=[pltpu.VMEM((d // LANES, rows, LANES), F32)],
        compiler_params=_cparams("parallel"),
        name="even_outproj",
    )(ya, yb, w_a, w_b, x)


def _block_diag(blocks):
    n, r, c = blocks.shape
    tiled = jnp.tile(blocks.reshape(n * r, c), (1, n))
    row_blk = jnp.arange(n * r, dtype=jnp.int32)[:, None] // r
    col_blk = jnp.arange(n * c, dtype=jnp.int32)[None, :] // c
    return jnp.where(row_blk == col_blk, tiled, jnp.zeros((), blocks.dtype))


def _even_mixer(x, norm_g, w_in, lam_re, lam_im, log_dt, b_re, b_im, c_re, c_im, d, glu_w, glu_b,
                conv_w, conv_b, wa, ba, wi, bi, lam, w_out):
    nb, seq, dm = x.shape
    g, p = lam_re.shape
    hch = b_re.shape[-1]
    width = g * hch
    n_half = 2
    gh = g // n_half

    proj = _even_inproj(x, norm_g[None, :], w_in.astype(BF16))

    lb_re, lb_im, bb_re, bb_im = _s5_discretize(lam_re, lam_im, log_dt, b_re, b_im)
    bb_re = bb_re.reshape(n_half, gh, hch, p)
    bb_im = bb_im.reshape(n_half, gh, hch, p)
    bm = jnp.stack([jnp.concatenate([_block_diag(bb_re[j]), _block_diag(bb_im[j])], axis=1)
                    for j in range(n_half)]).astype(BF16)
    c_re_t = jnp.swapaxes(c_re, 1, 2).reshape(n_half, gh, p, hch)
    c_im_t = jnp.swapaxes(c_im, 1, 2).reshape(n_half, gh, p, hch)
    cm = jnp.stack([jnp.concatenate([_block_diag(c_re_t[j]), -_block_diag(c_im_t[j])], axis=0)
                    for j in range(n_half)]).astype(BF16)
    lre = jnp.broadcast_to(lb_re.reshape(1, g * p), (nb, g * p))
    lim = jnp.broadcast_to(lb_im.reshape(1, g * p), (nb, g * p))
    ya = _s5_mixer(proj, nb, bm, lre, lim, cm, d.reshape(1, width), glu_w.astype(BF16),
                   glu_b[None, :])

    lw = conv_w.shape[1]
    nblk = 2
    per = LRU_BLOCKS // nblk
    wa_bd = jnp.stack([_block_diag(wa[j * per:(j + 1) * per]) for j in range(nblk)]).astype(BF16)
    wi_bd = jnp.stack([_block_diag(wi[j * per:(j + 1) * per]) for j in range(nblk)]).astype(BF16)
    yb = _lru_mixer(proj, nb, conv_w, conv_b[None, :], wa_bd, ba[None, :], wi_bd, bi[None, :],
                    lam[None, :])

    w_out_b = w_out.astype(BF16)
    return _even_outproj(ya, yb, w_out_b[:width], w_out_b[width:], x)


def _rope_table_kernel(pos_ref, inv_ref, cos_ref, sin_ref):
    ang = pos_ref[...].astype(F32) * inv_ref[...]
    cos_ref[...] = jnp.cos(ang)
    sin_ref[...] = jnp.sin(ang)


def _rope_tables(positions):
    nb, seq = positions.shape
    half = ROPE_DIM // 2
    per_row = LANES // half
    n_tok = nb * seq
    inv_freq = 1.0 / (ROPE_THETA ** (jnp.arange(0, ROPE_DIM, 2, dtype=F32) / ROPE_DIM))
    pos_rep = jnp.repeat(positions.reshape(n_tok // per_row, per_row), half, axis=1)
    inv_rep = jnp.tile(inv_freq, per_row)[None, :]
    cos, sin = pl.pallas_call(
        _rope_table_kernel,
        out_shape=(jax.ShapeDtypeStruct(pos_rep.shape, F32), jax.ShapeDtypeStruct(pos_rep.shape, F32)),
        name="rope_table",
    )(pos_rep, inv_rep)
    cos = cos.reshape(n_tok, half)
    sin = sin.reshape(n_tok, half)
    cc = jnp.concatenate([cos, cos], axis=-1)
    ss = jnp.concatenate([-sin, sin], axis=-1)
    return cc, ss


def _mla_proj_kernel(x_ref, g_ref, win_ref, cqn_ref, ckvn_ref, wq_ref, wkv_ref, qg_ref, kg_ref,
                     cc_ref, ss_ref, qn_ref, qrp_ref, k2_ref, v_ref):
    nh = MLA_HEADS
    h = _rms(x_ref[...], g_ref[...])
    proj = _dot(h.astype(BF16), win_ref[...])
    c_q = _rms(proj[:, :Q_LORA], cqn_ref[...])
    c_kv = _rms(proj[:, Q_LORA:Q_LORA + KV_LORA], ckvn_ref[...])
    o = Q_LORA + KV_LORA
    k_r2 = proj[:, o:o + LANES]
    k_sw2 = proj[:, o + LANES:o + 2 * LANES]
    cc2 = cc_ref[...]
    ss2 = ss_ref[...]
    qg = qg_ref[...]
    kg = kg_ref[...]
    lo = lax.broadcasted_iota(jnp.int32, cc2.shape, 1) < ROPE_DIM
    scale = math.log2(math.e) / math.sqrt(QK_DIM)

    q = _dot(c_q.astype(BF16), wq_ref[...])
    q_a = cc2 * qg[1:2, :]
    q_b = ss2 * qg[2:3, :]
    for j in range(nh // 2):
        ro = nh * NOPE_DIM + j * LANES
        q_rp = q[:, ro:ro + LANES]
        q_swp = q[:, ro + nh * ROPE_DIM:ro + nh * ROPE_DIM + LANES]
        rot = q_rp * q_a + q_swp * q_b
        sq_rp = q_rp * q_rp
        rs_pair = []
        for par in range(2):
            hd = 2 * j + par
            q_n = q[:, hd * NOPE_DIM:(hd + 1) * NOPE_DIM]
            own = lo if par == 0 else jnp.logical_not(lo)
            ms = jnp.sum(q_n * q_n + jnp.where(own, sq_rp, 0.0), axis=-1, keepdims=True) / QK_DIM
            rs = lax.rsqrt(ms + RMS_EPS) * scale
            qn_ref[0, hd] = (q_n * rs * qg[0:1, :]).astype(qn_ref.dtype)
            rs_pair.append(rs)
        qrp_ref[0, j] = (rot * jnp.where(lo, rs_pair[0], rs_pair[1])).astype(qrp_ref.dtype)

    kv = _dot(c_kv.astype(BF16), wkv_ref[...])
    k_rot2 = k_r2 * (cc2 * kg[1:2, :]) + k_sw2 * (ss2 * kg[2:3, :])
    k_r_ss = jnp.sum(jnp.where(lo, k_r2 * k_r2, 0.0), axis=-1, keepdims=True)
    for hd in range(nh):
        ko = hd * (NOPE_DIM + V_DIM)
        k_n = kv[:, ko:ko + NOPE_DIM]
        ms_k = (jnp.sum(k_n * k_n, axis=-1, keepdims=True) + k_r_ss) / QK_DIM
        rs_k = lax.rsqrt(ms_k + RMS_EPS)
        own = lo if hd % 2 == 0 else jnp.logical_not(lo)
        k2_ref[0, hd, :, 0:NOPE_DIM] = (k_n * rs_k * kg[0:1, :]).astype(k2_ref.dtype)
        k2_ref[0, hd, :, NOPE_DIM:] = jnp.where(own, k_rot2 * rs_k, 0.0).astype(k2_ref.dtype)
        v_ref[0, hd] = kv[:, ko + NOPE_DIM:ko + NOPE_DIM + V_DIM].astype(v_ref.dtype)


def _swap_halves(a, axis=-1):
    h = a.shape[axis] // 2
    lo = lax.slice_in_dim(a, 0, h, axis=axis)
    hi = lax.slice_in_dim(a, h, 2 * h, axis=axis)
    return jnp.concatenate([hi, lo], axis=axis)


def _mla_proj(x, norm_g, positions, w_in, cq_norm, ckv_norm, w_uq, w_ukv, q_norm, k_norm):
    nb, seq, dm = x.shape
    nh = MLA_HEADS
    tm = ROW_TILE
    n_tok = nb * seq
    cc, ss = _rope_tables(positions)
    cc2 = jnp.concatenate([cc, cc], axis=-1)
    ss2 = jnp.concatenate([ss, ss], axis=-1)
    o = Q_LORA + KV_LORA
    w_kr = w_in[:, o:o + ROPE_DIM]
    w_sw = _swap_halves(w_kr)
    w_in_x = jnp.concatenate([w_in[:, :o], w_kr, w_kr, w_sw, w_sw], axis=1).astype(BF16)
    wq = w_uq.reshape(Q_LORA, nh, QK_DIM)
    wq_n = wq[:, :, :NOPE_DIM].reshape(Q_LORA, nh * NOPE_DIM)
    wq_r = wq[:, :, NOPE_DIM:]
    wq_x = jnp.concatenate([wq_n, wq_r.reshape(Q_LORA, nh * ROPE_DIM),
                            _swap_halves(wq_r).reshape(Q_LORA, nh * ROPE_DIM)], axis=1).astype(BF16)

    def gains(gv):
        r = gv[NOPE_DIM:]
        rs = _swap_halves(r)
        return jnp.stack([gv[:NOPE_DIM], jnp.concatenate([r, r]), jnp.concatenate([rs, rs])])

    x2 = x.reshape(n_tok, dm)
    per_b = seq // tm
    full = lambda a: pl.BlockSpec(a.shape, lambda i: (0,) * a.ndim)
    args = (x2, norm_g[None, :], w_in_x, cq_norm[None, :], ckv_norm[None, :], wq_x,
            w_ukv.astype(BF16), gains(q_norm), gains(k_norm), cc2, ss2)
    in_specs = [pl.BlockSpec((tm, dm), lambda i: (i, 0))] + [full(a) for a in args[1:9]] + [
        pl.BlockSpec((tm, LANES), lambda i: (i, 0)), pl.BlockSpec((tm, LANES), lambda i: (i, 0))]

    def head_spec(n_heads, dh):
        return pl.BlockSpec((1, n_heads, tm, dh), lambda i: (i // per_b, 0, i % per_b, 0))

    def head_shape(n_heads, dh):
        return jax.ShapeDtypeStruct((nb, n_heads, seq, dh), BF16)

    return pl.pallas_call(
        _mla_proj_kernel,
        out_shape=(head_shape(nh, NOPE_DIM), head_shape(nh // 2, LANES), head_shape(nh, NOPE_DIM + LANES),
                   head_shape(nh, V_DIM)),
        grid=(n_tok // tm,),
        in_specs=in_specs,
        out_specs=(head_spec(nh, NOPE_DIM), head_spec(nh // 2, LANES), head_spec(nh, NOPE_DIM + LANES),
                   head_spec(nh, V_DIM)),
        compiler_params=_cparams("parallel"),
        name="mla_proj",
    )(*args)


def _attention_kernel(qn_ref, qrp_ref, k2_ref, v_ref, o_ref):
    seq = qn_ref.shape[2]
    t = ATT_TQ
    nt = (((1,), (1,)), ((), ()))
    tri = (lax.broadcasted_iota(jnp.int32, (t, t), 1) <= lax.broadcasted_iota(jnp.int32, (t, t), 0))
    for i in range(seq // t):
        q = jnp.concatenate([qn_ref[0, 0, i * t:(i + 1) * t, :], qrp_ref[0, 0, i * t:(i + 1) * t, :]], axis=-1)
        m = jnp.full((t, 1), NEG_INF, F32)
        l = jnp.zeros((t, 1), F32)
        acc = jnp.zeros((t, V_DIM), F32)
        for j in range(i + 1):
            s = lax.dot_general(q, k2_ref[0, 0, j * t:(j + 1) * t, :], nt, preferred_element_type=F32)
            if j == i:
                s = jnp.where(tri, s, NEG_INF)
            m_new = jnp.maximum(m, jnp.max(s, axis=-1, keepdims=True))
            alpha = jnp.exp2(m - m_new)
            p = jnp.exp2(s - m_new)
            l = alpha * l + jnp.sum(p, axis=-1, keepdims=True)
            acc = alpha * acc + _dot(p.astype(BF16), v_ref[0, 0, j * t:(j + 1) * t, :])
            m = m_new
        o_ref[0, i * t:(i + 1) * t, :] = (acc / l).astype(o_ref.dtype)


def _attention(qn, qrp, k2, v):
    nb, nh, seq, _ = qn.shape

    def spec(dh, per=1):
        return pl.BlockSpec((1, 1, seq, dh), lambda b, h: (b, h // per, 0, 0))

    return pl.pallas_call(
        _attention_kernel,
        out_shape=jax.ShapeDtypeStruct((nb, seq, nh * V_DIM), BF16),
        grid=(nb, nh),
        in_specs=[spec(NOPE_DIM), spec(LANES, 2), spec(NOPE_DIM + LANES), spec(V_DIM)],
        out_specs=pl.BlockSpec((1, seq, V_DIM), lambda b, h: (b, 0, h)),
        compiler_params=_cparams("parallel", "parallel"),
        name="mla_attention",
    )(qn, qrp, k2, v)


def _proj_residual_kernel(a_ref, w_ref, x_ref, o_ref):
    o_ref[...] = x_ref[...] + _dot(a_ref[...], w_ref[...])


def _proj_residual(a, w, x2):
    n_tok, dm = x2.shape
    tm = ROW_TILE
    return pl.pallas_call(
        _proj_residual_kernel,
        out_shape=jax.ShapeDtypeStruct(x2.shape, x2.dtype),
        grid=(n_tok // tm,),
        in_specs=[pl.BlockSpec((tm, a.shape[1]), lambda i: (i, 0)),
                  pl.BlockSpec(w.shape, lambda i: (0, 0)),
                  pl.BlockSpec((tm, dm), lambda i: (i, 0))],
        out_specs=pl.BlockSpec((tm, dm), lambda i: (i, 0)),
        compiler_params=_cparams("parallel"),
        name="proj_residual",
    )(a, w, x2)


def _mla_mixer(x, norm_g, positions, w_in, cq_norm, ckv_norm, w_uq, w_ukv, q_norm, k_norm, w_o):
    nb, seq, dm = x.shape
    qn, qrp, k2, v = _mla_proj(x, norm_g, positions, w_in, cq_norm, ckv_norm, w_uq, w_ukv, q_norm, k_norm)
    o = _attention(qn, qrp, k2, v)
    out = _proj_residual(o.reshape(nb * seq, -1), w_o.astype(BF16), x.reshape(nb * seq, dm))
    return out.reshape(nb, seq, dm)


def _router_kernel(x_ref, g_ref, w_ref, b_ref, tri_ref, meta_ref, gate_ref, cnt_ref, base_ref):
    tm, dm = x_ref.shape
    ne = N_EXPERTS

    @pl.when(pl.program_id(0) == 0)
    def _():
        base_ref[...] = jnp.zeros_like(base_ref)

    hn = _rms(x_ref[...], g_ref[...])
    logits = _dot(hn.astype(BF16), w_ref[...]) + b_ref[...]
    lane = lax.broadcasted_iota(jnp.int32, logits.shape, 1)
    big = jnp.int32(1 << 20)

    def first_argmax(v):
        m = jnp.max(v, axis=-1, keepdims=True)
        idx = jnp.min(jnp.where(v == m, lane, big), axis=-1, keepdims=True)
        return m, idx

    is_g = (lane >= ne) & (lane < ne + N_GROUPS)
    gl = jnp.where(is_g, logits, NEG_INF)
    g_max, g_idx = first_argmax(gl)
    g_sel = g_idx - ne
    p_g = 1.0 / jnp.sum(jnp.where(is_g, jnp.exp(gl - g_max), 0.0), axis=-1, keepdims=True)
    in_grp = (lane >= g_sel * EXPERTS_PER_GROUP) & (lane < (g_sel + 1) * EXPERTS_PER_GROUP)
    el = jnp.where(in_grp, logits, NEG_INF)
    v0, e0 = first_argmax(el)
    v1, e1 = first_argmax(jnp.where(lane == e0, NEG_INF, el))
    t = jnp.exp(v1 - v0)
    gate0 = p_g / (1.0 + t)
    gate1 = p_g * t / (1.0 + t)

    oh0 = (lane == e0).astype(F32)
    oh1 = (lane == e1).astype(F32)
    both = oh0 + oh1
    before = _dot(tri_ref[...], both.astype(BF16)) + base_ref[0:1, :]
    rank0 = jnp.sum(oh0 * before, axis=-1, keepdims=True)
    rank1 = jnp.sum(oh1 * before, axis=-1, keepdims=True)
    base_ref[0:1, :] = base_ref[0:1, :] + jnp.sum(both, axis=0, keepdims=True)
    cnt_ref[...] = jnp.broadcast_to(base_ref[0:1, :], cnt_ref.shape).astype(jnp.int32)

    meta = jnp.where(lane == 0, e0, jnp.where(lane == 1, e1, jnp.where(
        lane == 2, rank0.astype(jnp.int32), jnp.where(lane == 3, rank1.astype(jnp.int32), 0))))
    meta_ref[...] = meta
    gate_ref[...] = jnp.where(lane == 0, gate0, jnp.where(lane == 1, gate1, 0.0))


def _router(x2, norm_g, w_group, b_group, w_expert, b_expert):
    n_tok, dm = x2.shape
    tm = ROW_TILE
    pad = LANES - N_EXPERTS - N_GROUPS
    w = jnp.concatenate([w_expert, w_group, jnp.zeros((dm, pad), F32)], axis=1).astype(BF16)
    b = jnp.concatenate([b_expert, b_group, jnp.zeros((pad,), F32)])[None, :]
    idx = jnp.arange(tm, dtype=jnp.int32)
    tri = (idx[None, :] < idx[:, None]).astype(BF16)
    return pl.pallas_call(
        _router_kernel,
        out_shape=(jax.ShapeDtypeStruct((n_tok, LANES), jnp.int32),
                   jax.ShapeDtypeStruct((n_tok, LANES), F32),
                   jax.ShapeDtypeStruct((SUBLANES, LANES), jnp.int32)),
        grid=(n_tok // tm,),
        in_specs=[pl.BlockSpec((tm, dm), lambda i: (i, 0)),
                  pl.BlockSpec((1, dm), lambda i: (0, 0)),
                  pl.BlockSpec((dm, LANES), lambda i: (0, 0)),
                  pl.BlockSpec((1, LANES), lambda i: (0, 0)),
                  pl.BlockSpec((tm, tm), lambda i: (0, 0))],
        out_specs=(pl.BlockSpec((tm, LANES), lambda i: (i, 0)),
                   pl.BlockSpec((tm, LANES), lambda i: (i, 0)),
                   pl.BlockSpec((SUBLANES, LANES), lambda i: (0, 0))),
        scratch_shapes=[pltpu.VMEM((SUBLANES, LANES), F32)],
        compiler_params=_cparams("arbitrary"),
        name="moe_router",
    )(x2, norm_g[None, :], w, b, tri)


def _dispatch_kernel(dest_ref, dest_prev_ref, x_ref, g_ref, xd_ref, hn_ref, zero_ref, sem, zsem):
    tm = x_ref.shape[0]
    i = pl.program_id(0)
    n_steps = pl.num_programs(0)
    slot = i % 2
    n_assign = n_steps * tm * TOP_K

    def tail_copy():
        return pltpu.make_async_copy(zero_ref, xd_ref.at[pl.ds(n_assign * ROW_CHUNKS, zero_ref.shape[0])], zsem)

    @pl.when(i == 0)
    def _():
        zero_ref[...] = jnp.zeros_like(zero_ref)
        tail_copy().start()

    hn = _rms(x_ref[...], g_ref[...])
    hbuf = hn_ref.at[slot]
    for c in range(ROW_CHUNKS):
        hbuf[pl.ds(c, tm, stride=ROW_CHUNKS), :] = hn[:, c * LANES:(c + 1) * LANES]

    def row_copy(d_ref, sl, t, k):
        src = pl.multiple_of(t * ROW_CHUNKS, ROW_CHUNKS)
        dst = pl.multiple_of(d_ref[0, 0, 2 * t + k], ROW_CHUNKS)
        return pltpu.make_async_copy(hn_ref.at[sl, pl.ds(src, ROW_CHUNKS)], xd_ref.at[pl.ds(dst, ROW_CHUNKS)],
                                     sem.at[sl])

    def issue(j, c):
        for u in range(DMA_UNROLL):
            for k in range(TOP_K):
                row_copy(dest_ref, slot, j * DMA_UNROLL + u, k).start(priority=k)
        return c

    lax.fori_loop(0, tm // DMA_UNROLL, issue, 0)

    def drain_all(d_ref, sl):
        def drain(j, c):
            for u in range(DMA_UNROLL):
                for k in range(TOP_K):
                    row_copy(d_ref, sl, j * DMA_UNROLL + u, k).wait()
            return c
        lax.fori_loop(0, tm // DMA_UNROLL, drain, 0)

    @pl.when(i >= 1)
    def _():
        drain_all(dest_prev_ref, 1 - slot)

    @pl.when(i == n_steps - 1)
    def _():
        drain_all(dest_ref, slot)

    @pl.when(i == 0)
    def _():
        tail_copy().wait()


def _dispatch(x2, norm_g, dest_off):
    tm = GATHER_TM
    n_tok, dm = x2.shape
    dest3 = dest_off.reshape(n_tok // tm, 1, 2 * tm)
    n_rows = n_tok * TOP_K + MOE_BM
    return pl.pallas_call(
        _dispatch_kernel,
        out_shape=jax.ShapeDtypeStruct((n_rows * ROW_CHUNKS, LANES), F32),
        grid=(n_tok // tm,),
        in_specs=[pl.BlockSpec((1, 1, 2 * tm), lambda i: (i, 0, 0), memory_space=pltpu.SMEM),
                  pl.BlockSpec((1, 1, 2 * tm), lambda i: (jnp.maximum(i - 1, 0), 0, 0), memory_space=pltpu.SMEM),
                  pl.BlockSpec((tm, dm), lambda i: (i, 0)),
                  pl.BlockSpec((1, dm), lambda i: (0, 0))],
        out_specs=pl.BlockSpec(memory_space=pl.ANY),
        scratch_shapes=[pltpu.VMEM((2, tm * ROW_CHUNKS, LANES), F32),
                        pltpu.VMEM((MOE_BM * ROW_CHUNKS, LANES), F32),
                        pltpu.SemaphoreType.DMA((2,)), pltpu.SemaphoreType.DMA],
        compiler_params=_cparams("arbitrary"),
        name="moe_dispatch",
    )(dest3, dest3, x2, norm_g[None, :])


def _rows_from_tiles(ref, n, base=0):
    return jnp.concatenate([ref[pl.ds(base + c, n, stride=ROW_CHUNKS), :] for c in range(ROW_CHUNKS)], axis=-1)


def _expert_kernel(be_ref, bs_ref, first_ref, nxt_ref, nu_ref, xd_ref, w1_ref, w3_ref, w2_ref, yd_ref,
                   xbuf, ybuf, w1buf, w3buf, w2buf, w13_s, w2_s, sem_in, sem_out, sem_w, *, layer):
    nu = nu_ref[0]
    n_rows = xbuf.shape[1]
    bm = n_rows // ROW_CHUNKS
    de = w1buf.shape[2]

    def in_copy(blk, sl):
        src = pl.multiple_of(bs_ref[blk] * ROW_CHUNKS, ROW_CHUNKS)
        return pltpu.make_async_copy(xd_ref.at[pl.ds(src, n_rows)], xbuf.at[sl], sem_in.at[sl])

    def out_copy(blk, sl):
        dst = pl.multiple_of(bs_ref[blk] * ROW_CHUNKS, ROW_CHUNKS)
        return pltpu.make_async_copy(ybuf.at[sl], yd_ref.at[pl.ds(dst, n_rows)], sem_out.at[sl])

    def w_copies(e, ws):
        return (pltpu.make_async_copy(w1_ref.at[layer, e], w1buf.at[ws], sem_w.at[ws]),
                pltpu.make_async_copy(w3_ref.at[layer, e], w3buf.at[ws], sem_w.at[ws]),
                pltpu.make_async_copy(w2_ref.at[layer, e], w2buf.at[ws], sem_w.at[ws]))

    in_copy(0, 0).start()
    for cp in w_copies(be_ref[0], 1):
        cp.start(priority=1)

    def block(b, ws):
        slot = b % 2

        @pl.when(b + 1 < nu)
        def _():
            in_copy(b + 1, 1 - slot).start()

        first = first_ref[b] == 1
        ws = jnp.where(first, 1 - ws, ws)

        @pl.when(first)
        def _():
            for cp in w_copies(be_ref[b], ws):
                cp.wait()
            w13_s[:, :de] = w1buf[ws].astype(BF16)
            w13_s[:, de:] = w3buf[ws].astype(BF16)
            w2_s[...] = w2buf[ws].astype(BF16)

            @pl.when(nxt_ref[b] >= 0)
            def _():
                for cp in w_copies(nxt_ref[b], 1 - ws):
                    cp.start(priority=1)

        in_copy(b, slot).wait()
        x = _rows_from_tiles(xbuf.at[slot], bm).astype(BF16)
        h = _dot(x, w13_s[...])
        a = jax.nn.silu(h[:, :de]) * h[:, de:]
        y = _dot(a.astype(BF16), w2_s[...])
        yb = ybuf.at[slot]
        for c in range(ROW_CHUNKS):
            yb[pl.ds(c, bm, stride=ROW_CHUNKS), :] = y[:, c * LANES:(c + 1) * LANES]

        @pl.when(b >= 1)
        def _():
            out_copy(b - 1, 1 - slot).wait()
        out_copy(b, slot).start()
        return ws

    lax.fori_loop(0, nu, block, jnp.int32(0))

    last = nu - 1
    out_copy(last, last % 2).wait()
    tail = yd_ref.shape[0] - n_rows
    fill_slot = 1 - last % 2
    ybuf[fill_slot] = jnp.zeros((n_rows, LANES), F32)
    fill = pltpu.make_async_copy(ybuf.at[fill_slot], yd_ref.at[pl.ds(tail, n_rows)], sem_out.at[fill_slot])
    fill.start()
    fill.wait()


def _experts(xd, blk_expert, blk_start, blk_first, blk_next, n_used, w1, w3, w2, layer):
    bm = MOE_BM
    _, _, dm, de = w1.shape
    any_spec = pl.BlockSpec(memory_space=pl.ANY)
    grid_spec = pltpu.PrefetchScalarGridSpec(
        num_scalar_prefetch=5,
        grid=(1,),
        in_specs=[any_spec, any_spec, any_spec, any_spec],
        out_specs=any_spec,
        scratch_shapes=[pltpu.VMEM((2, bm * ROW_CHUNKS, LANES), F32),
                        pltpu.VMEM((2, bm * ROW_CHUNKS, LANES), F32),
                        pltpu.VMEM((2, dm, de), F32),
                        pltpu.VMEM((2, dm, de), F32),
                        pltpu.VMEM((2, de, dm), F32),
                        pltpu.VMEM((dm, 2 * de), BF16),
                        pltpu.VMEM((de, dm), BF16),
                        pltpu.SemaphoreType.DMA((2,)),
                        pltpu.SemaphoreType.DMA((2,)),
                        pltpu.SemaphoreType.DMA((2,))],
    )
    return pl.pallas_call(
        functools.partial(_expert_kernel, layer=layer),
        out_shape=jax.ShapeDtypeStruct(xd.shape, F32),
        grid_spec=grid_spec,
        compiler_params=_cparams("arbitrary"),
        name="moe_experts",
    )(blk_expert, blk_start, blk_first, blk_next, n_used, xd, w1, w3, w2)


def _combine_kernel(dest_ref, dest_next_ref, yd_ref, gate_ref, x_ref, o_ref, buf_ref, sem):
    tm, dm = x_ref.shape
    i = pl.program_id(0)
    n_steps = pl.num_programs(0)
    slot = i % 2
    k_rows = tm * ROW_CHUNKS

    def row_copy(d_ref, sl, t, k):
        src = pl.multiple_of(d_ref[0, 0, 2 * t + k], ROW_CHUNKS)
        dst = pl.multiple_of(k * k_rows + t * ROW_CHUNKS, ROW_CHUNKS)
        return pltpu.make_async_copy(yd_ref.at[pl.ds(src, ROW_CHUNKS)],
                                     buf_ref.at[sl, pl.ds(dst, ROW_CHUNKS)], sem.at[sl])

    def issue_all(d_ref, sl):
        def issue(j, c):
            for u in range(DMA_UNROLL):
                for k in range(TOP_K):
                    row_copy(d_ref, sl, j * DMA_UNROLL + u, k).start(priority=k)
            return c
        lax.fori_loop(0, tm // DMA_UNROLL, issue, 0)

    @pl.when(i == 0)
    def _():
        issue_all(dest_ref, slot)

    @pl.when(i + 1 < n_steps)
    def _():
        issue_all(dest_next_ref, 1 - slot)

    def drain(j, c):
        for u in range(DMA_UNROLL):
            for k in range(TOP_K):
                row_copy(dest_ref, slot, j * DMA_UNROLL + u, k).wait()
        return c

    lax.fori_loop(0, tm // DMA_UNROLL, drain, 0)

    g0 = gate_ref[:, 0:1]
    g1 = gate_ref[:, 1:2]
    buf = buf_ref.at[slot]
    for c in range(ROW_CHUNKS):
        y0 = buf[pl.ds(c, tm, stride=ROW_CHUNKS), :]
        y1 = buf[pl.ds(k_rows + c, tm, stride=ROW_CHUNKS), :]
        o_ref[:, c * LANES:(c + 1) * LANES] = x_ref[:, c * LANES:(c + 1) * LANES] + (g0 * y0 + g1 * y1)


def _combine(yd, dest_off, gates, x2):
    n_tok, dm = x2.shape
    tm = GATHER_TM
    n_steps = n_tok // tm
    dest3 = dest_off.reshape(n_steps, 1, 2 * tm)
    return pl.pallas_call(
        _combine_kernel,
        out_shape=jax.ShapeDtypeStruct(x2.shape, x2.dtype),
        grid=(n_steps,),
        in_specs=[pl.BlockSpec((1, 1, 2 * tm), lambda i: (i, 0, 0), memory_space=pltpu.SMEM),
                  pl.BlockSpec((1, 1, 2 * tm), lambda i: (jnp.minimum(i + 1, n_steps - 1), 0, 0),
                               memory_space=pltpu.SMEM),
                  pl.BlockSpec(memory_space=pl.ANY),
                  pl.BlockSpec((tm, LANES), lambda i: (i, 0)),
                  pl.BlockSpec((tm, dm), lambda i: (i, 0))],
        out_specs=pl.BlockSpec((tm, dm), lambda i: (i, 0)),
        scratch_shapes=[pltpu.VMEM((2, TOP_K * tm * ROW_CHUNKS, LANES), F32),
                        pltpu.SemaphoreType.DMA((2,))],
        compiler_params=_cparams("arbitrary"),
        name="moe_combine",
    )(dest3, dest3, yd, gates, x2)


def _moe(x, norm_g, w_group, b_group, w_expert, b_expert, w1, w3, w2, layer):
    nb, seq, dm = x.shape
    n_tok = nb * seq
    n_assign = n_tok * TOP_K
    x2 = x.reshape(n_tok, dm)
    meta, gates, counts = _router(x2, norm_g, w_group, b_group, w_expert, b_expert)
    bm = MOE_BM
    n_blocks = n_assign // bm + N_EXPERTS
    counts = counts[0, :N_EXPERTS]
    starts = jnp.cumsum(counts) - counts
    eid = meta[:, 0:TOP_K]
    e_iota = jnp.arange(N_EXPERTS, dtype=jnp.int32)
    seg_start = jnp.sum(jnp.where(eid[:, :, None] == e_iota, starts, 0), axis=-1)
    dest_off = ((seg_start + meta[:, TOP_K:2 * TOP_K]) * ROW_CHUNKS).astype(jnp.int32).reshape(-1)
    nblk = (counts + bm - 1) // bm
    blk_end = jnp.cumsum(nblk)
    b_iota = jnp.arange(n_blocks, dtype=jnp.int32)
    blk_expert = jnp.minimum(jnp.sum((b_iota[:, None] >= blk_end[None, :]).astype(jnp.int32), axis=1),
                             N_EXPERTS - 1)
    j = b_iota - (blk_end - nblk)[blk_expert]
    n_used = blk_end[-1].astype(jnp.int32).reshape(1)
    blk_start = jnp.where(b_iota < n_used[0], starts[blk_expert] + j * bm, 0).astype(jnp.int32)
    blk_first = (j == 0).astype(jnp.int32)
    cand = jnp.where(counts > 0, e_iota, N_EXPERTS)
    later = lax.cummin(cand[::-1])[::-1]
    next_e = jnp.concatenate([later[1:], jnp.full((1,), N_EXPERTS, jnp.int32)])
    blk_next = jnp.where(next_e[blk_expert] < N_EXPERTS, next_e[blk_expert], -1).astype(jnp.int32)

    xd = _dispatch(x2, norm_g, dest_off)
    yd = _experts(xd, blk_expert, blk_start, blk_first, blk_next, n_used, w1, w3, w2, layer)
    out = _combine(yd, dest_off, gates, x2)
    return out.reshape(nb, seq, dm)


def kernel(x, positions, ev_norm, ev_w_in, s5_lambda_re, s5_lambda_im, s5_log_dt, s5_b_re, s5_b_im,
           s5_c_re, s5_c_im, s5_d, s5_glu_w, s5_glu_b, lru_conv_w, lru_conv_b, lru_wa, lru_ba, lru_wi,
           lru_bi, lru_lambda, ev_w_out, od_norm, mla_w_in, mla_cq_norm, mla_ckv_norm, mla_w_uq,
           mla_w_ukv, mla_q_norm, mla_k_norm, mla_w_o, ffn_norm, moe_w_group, moe_b_group,
           moe_w_expert, moe_b_expert, moe_w1, moe_w3, moe_w2):
    depth = ffn_norm.shape[0]
    for layer in range(depth):
        j = layer // 2
        if layer % 2 == 0:
            x = _even_mixer(x, ev_norm[j], ev_w_in[j], s5_lambda_re[j], s5_lambda_im[j], s5_log_dt[j],
                            s5_b_re[j], s5_b_im[j], s5_c_re[j], s5_c_im[j], s5_d[j], s5_glu_w[j],
                            s5_glu_b[j], lru_conv_w[j], lru_conv_b[j], lru_wa[j], lru_ba[j], lru_wi[j],
                            lru_bi[j], lru_lambda[j], ev_w_out[j])
        else:
            x = _mla_mixer(x, od_norm[j], positions, mla_w_in[j], mla_cq_norm[j], mla_ckv_norm[j],
                           mla_w_uq[j], mla_w_ukv[j], mla_q_norm[j], mla_k_norm[j], mla_w_o[j])
        x = _moe(x, ffn_norm[layer], moe_w_group[layer], moe_b_group[layer], moe_w_expert[layer],
                 moe_b_expert[layer], moe_w1, moe_w3, moe_w2, layer)
    return x
```

```python
import functools
import math

import jax
import jax.numpy as jnp
from jax import lax
from jax.experimental import pallas as pl
from jax.experimental.pallas import tpu as pltpu

F32 = jnp.float32
BF16 = jnp.bfloat16

S5_GROUP_CH = 16
S5_STATE = 64
LRU_BLOCKS = 8
CONV_WIDTH = 4
RG_C = 8.0
MLA_HEADS = 8
Q_LORA = 384
KV_LORA = 256
NOPE_DIM = 128
ROPE_DIM = 64
V_DIM = 128
QK_DIM = NOPE_DIM + ROPE_DIM
ROPE_THETA = 10000.0
N_GROUPS = 8
EXPERTS_PER_GROUP = 8
N_EXPERTS = N_GROUPS * EXPERTS_PER_GROUP
TOP_K = 2
RMS_EPS = 1e-6
NEG_INF = -1e30

SUBLANES = 8
LANES = 128
ROW_CHUNKS = 8
VMEM_LIMIT = 48 * 1024 * 1024

EVEN_TL = 128
ROW_TILE = 512
MOE_BM = 256
ATT_TQ = 512
GATHER_TM = 512
DMA_UNROLL = 8


def _cparams(*sem):
    return pltpu.CompilerParams(dimension_semantics=tuple(sem), vmem_limit_bytes=VMEM_LIMIT)


def _rms(x, g):
    ms = jnp.mean(x * x, axis=-1, keepdims=True)
    return x * lax.rsqrt(ms + RMS_EPS) * g


def _dot(a, b):
    return jnp.dot(a, b, preferred_element_type=F32)


def _s5_discretize_kernel(lre_ref, lim_ref, ldt_ref, bre_ref, bim_ref,
                          lbre_ref, lbim_ref, bbre_ref, bbim_ref):
    lr = jnp.minimum(lre_ref[...], -1e-4)
    li = lim_ref[...]
    dt = jnp.exp(ldt_ref[...])
    mag = jnp.exp(lr * dt)
    lb_re = mag * jnp.cos(li * dt)
    lb_im = mag * jnp.sin(li * dt)
    den = lr * lr + li * li
    num_re = lb_re - 1.0
    f_re = (num_re * lr + lb_im * li) / den
    f_im = (lb_im * lr - num_re * li) / den
    lbre_ref[...] = lb_re
    lbim_ref[...] = lb_im
    br = bre_ref[...]
    bi = bim_ref[...]
    bbre_ref[...] = f_re[:, None, :] * br - f_im[:, None, :] * bi
    bbim_ref[...] = f_re[:, None, :] * bi + f_im[:, None, :] * br


def _s5_discretize(lam_re, lam_im, log_dt, b_re, b_im):
    g, p = lam_re.shape
    h = b_re.shape[-1]
    b_re_t = jnp.swapaxes(b_re, 1, 2)
    b_im_t = jnp.swapaxes(b_im, 1, 2)
    return pl.pallas_call(
        _s5_discretize_kernel,
        out_shape=(jax.ShapeDtypeStruct((g, p), F32), jax.ShapeDtypeStruct((g, p), F32),
                   jax.ShapeDtypeStruct((g, h, p), F32), jax.ShapeDtypeStruct((g, h, p), F32)),
        name="s5_discretize",
    )(lam_re, lam_im, log_dt[:, None], b_re_t, b_im_t)


def _even_inproj_kernel(x_ref, g_ref, w_ref, o_ref, h_ref):
    nb, tl, _ = x_ref.shape
    nc = h_ref.shape[0]
    g = g_ref[...]
    for b in range(nb):
        h = _rms(x_ref[b], g)
        for c in range(nc):
            h_ref[c, pl.ds(b, tl, stride=nb), :] = h[:, c * LANES:(c + 1) * LANES]
    h_all = jnp.concatenate([h_ref[c] for c in range(nc)], axis=-1)
    o_ref[...] = _dot(h_all.astype(BF16), w_ref[...]).astype(o_ref.dtype)


def _even_inproj(x, g, w):
    nb, seq, d = x.shape
    n_out = w.shape[1]
    tl = EVEN_TL
    return pl.pallas_call(
        _even_inproj_kernel,
        out_shape=jax.ShapeDtypeStruct((seq * nb, n_out), BF16),
        grid=(seq // tl,),
        in_specs=[pl.BlockSpec((nb, tl, d), lambda i: (0, i, 0)),
                  pl.BlockSpec((1, d), lambda i: (0, 0)),
                  pl.BlockSpec((d, n_out), lambda i: (0, 0))],
        out_specs=pl.BlockSpec((tl * nb, n_out), lambda i: (i, 0)),
        scratch_shapes=[pltpu.VMEM((d // LANES, tl * nb, LANES), F32)],
        compiler_params=_cparams("parallel"),
        name="even_inproj",
    )(x, g, w)


def _s5_kernel(u_ref, bm_ref, lre_ref, lim_ref, cm_ref, d_ref, gw_ref, gb_ref, o_ref,
               bu_ref, sb_ref, st_ref, *, nb):
    rows, width = u_ref.shape
    n_half = bm_ref.shape[0]
    kin = width // n_half
    ncol = bm_ref.shape[2]
    nre = ncol // 2
    chunk = 512
    steps = rows // nb

    @pl.when(pl.program_id(0) == 0)
    def _():
        st_ref[...] = jnp.zeros_like(st_ref)

    u = u_ref[...]
    for j in range(n_half):
        bu_ref[:, j * ncol:(j + 1) * ncol] = _dot(u[:, j * kin:(j + 1) * kin], bm_ref[j])

    for j in range(n_half):
        for c in range(nre // chunk):
            cr = j * ncol + c * chunk
            ci = cr + nre
            lc = j * nre + c * chunk
            lr = lre_ref[:, lc:lc + chunk]
            li = lim_ref[:, lc:lc + chunk]

            def body(i, carry, cr=cr, ci=ci, lr=lr, li=li):
                sr, si = carry
                r0 = pl.multiple_of(i * (2 * nb), 2 * nb)
                sr1 = lr * sr - li * si + bu_ref[pl.ds(r0, nb), cr:cr + chunk]
                si1 = lr * si + li * sr + bu_ref[pl.ds(r0, nb), ci:ci + chunk]
                sr2 = lr * sr1 - li * si1 + bu_ref[pl.ds(r0 + nb, nb), cr:cr + chunk]
                si2 = lr * si1 + li * sr1 + bu_ref[pl.ds(r0 + nb, nb), ci:ci + chunk]
                sb_ref[pl.ds(r0, 2 * nb), cr:cr + chunk] = jnp.concatenate([sr1, sr2], 0).astype(BF16)
                sb_ref[pl.ds(r0, 2 * nb), ci:ci + chunk] = jnp.concatenate([si1, si2], 0).astype(BF16)
                return sr2, si2

            sr, si = lax.fori_loop(0, steps // 2, body,
                                   (st_ref[:, cr:cr + chunk], st_ref[:, ci:ci + chunk]))
            st_ref[:, cr:cr + chunk] = sr
            st_ref[:, ci:ci + chunk] = si

    ys = [_dot(sb_ref[:, j * ncol:(j + 1) * ncol], cm_ref[j]) for j in range(n_half)]
    y = jnp.concatenate(ys, axis=-1) + d_ref[...] * u.astype(F32)
    y = jax.nn.gelu(y)
    z = _dot(y.astype(BF16), gw_ref[...]) + gb_ref[...]
    o_ref[...] = (y * jax.nn.sigmoid(z)).astype(o_ref.dtype)


def _s5_mixer(proj, nb, bm, lre, lim, cm, d, glu_w, glu_b):
    rows_total = proj.shape[0]
    width = d.shape[1]
    rows = EVEN_TL * nb
    n_half, _, ncol = bm.shape
    kern = functools.partial(_s5_kernel, nb=nb)
    return pl.pallas_call(
        kern,
        out_shape=jax.ShapeDtypeStruct((rows_total, width), BF16),
        grid=(rows_total // rows,),
        in_specs=[pl.BlockSpec((rows, width), lambda i: (i, 0)),
                  pl.BlockSpec(bm.shape, lambda i: (0, 0, 0)),
                  pl.BlockSpec(lre.shape, lambda i: (0, 0)),
                  pl.BlockSpec(lim.shape, lambda i: (0, 0)),
                  pl.BlockSpec(cm.shape, lambda i: (0, 0, 0)),
                  pl.BlockSpec((1, width), lambda i: (0, 0)),
                  pl.BlockSpec(glu_w.shape, lambda i: (0, 0)),
                  pl.BlockSpec((1, width), lambda i: (0, 0))],
        out_specs=pl.BlockSpec((rows, width), lambda i: (i, 0)),
        scratch_shapes=[pltpu.VMEM((rows, n_half * ncol), F32),
                        pltpu.VMEM((rows, n_half * ncol), BF16),
                        pltpu.VMEM((nb, n_half * ncol), F32)],
        compiler_params=_cparams("arbitrary"),
        name="s5_mixer",
    )(proj, bm, lre, lim, cm, d, glu_w, glu_b)


def _lru_kernel(x_ref, gate_ref, cw_ref, cb_ref, wa_ref, ba_ref, wi_ref, bi_ref, lam_ref, o_ref,
                xp_ref, a_ref, b_ref, h_ref, *, nb):
    rows, width = x_ref.shape
    halo = (CONV_WIDTH - 1) * nb
    steps = rows // nb

    @pl.when(pl.program_id(0) == 0)
    def _():
        xp_ref[0:halo, :] = jnp.zeros((halo, width), F32)
        h_ref[...] = jnp.zeros_like(h_ref)

    xp_ref[halo:halo + rows, :] = x_ref[...].astype(F32)
    xf = cb_ref[...] + cw_ref[0:1, :] * xp_ref[0:rows, :]
    for k in range(1, CONV_WIDTH):
        xf = xf + cw_ref[k:k + 1, :] * xp_ref[k * nb:k * nb + rows, :]
    xp_ref[0:halo, :] = xp_ref[rows:rows + halo, :]

    xb = xf.astype(BF16)
    nblk = wa_ref.shape[0]
    kb = width // nblk
    ga = jnp.concatenate([_dot(xb[:, j * kb:(j + 1) * kb], wa_ref[j]) for j in range(nblk)], -1)
    gi = jnp.concatenate([_dot(xb[:, j * kb:(j + 1) * kb], wi_ref[j]) for j in range(nblk)], -1)
    r = jax.nn.sigmoid(ga + ba_ref[...])
    ig = jax.nn.sigmoid(gi + bi_ref[...])
    log_a = RG_C * r * jax.nn.log_sigmoid(lam_ref[...])
    a_ref[...] = jnp.exp(log_a)
    th = jnp.tanh(log_a)
    b_ref[...] = jnp.sqrt(-2.0 * th / (1.0 - th)) * (ig * xf)

    def body(t, h):
        r0 = pl.multiple_of(t * nb, nb)
        h = a_ref[pl.ds(r0, nb), :] * h + b_ref[pl.ds(r0, nb), :]
        b_ref[pl.ds(r0, nb), :] = h
        return h

    h_ref[...] = lax.fori_loop(0, steps, body, h_ref[...])
    o_ref[...] = (b_ref[...] * jax.nn.gelu(gate_ref[...].astype(F32))).astype(o_ref.dtype)


def _lru_mixer(proj, nb, conv_w, conv_b, wa, ba, wi, bi, lam):
    rows_total = proj.shape[0]
    width = conv_w.shape[1]
    rows = EVEN_TL * nb
    halo = (CONV_WIDTH - 1) * nb
    kern = functools.partial(_lru_kernel, nb=nb)
    vec = pl.BlockSpec((1, width), lambda i: (0, 0))
    return pl.pallas_call(
        kern,
        out_shape=jax.ShapeDtypeStruct((rows_total, width), BF16),
        grid=(rows_total // rows,),
        in_specs=[pl.BlockSpec((rows, width), lambda i: (i, 1)),
                  pl.BlockSpec((rows, width), lambda i: (i, 2)),
                  pl.BlockSpec(conv_w.shape, lambda i: (0, 0)),
                  vec,
                  pl.BlockSpec(wa.shape, lambda i: (0, 0, 0)),
                  vec,
                  pl.BlockSpec(wi.shape, lambda i: (0, 0, 0)),
                  vec, vec],
        out_specs=pl.BlockSpec((rows, width), lambda i: (i, 0)),
        scratch_shapes=[pltpu.VMEM((rows + halo, width), F32),
                        pltpu.VMEM((rows, width), F32),
                        pltpu.VMEM((rows, width), F32),
                        pltpu.VMEM((nb, width), F32)],
        compiler_params=_cparams("arbitrary"),
        name="lru_mixer",
    )(proj, proj, conv_w, conv_b, wa, ba, wi, bi, lam)


def _even_outproj_kernel(ya_ref, yb_ref, wa_ref, wb_ref, x_ref, o_ref, y_ref):
    nb, tl, _ = x_ref.shape
    nc = y_ref.shape[0]
    y = _dot(ya_ref[...], wa_ref[...]) + _dot(yb_ref[...], wb_ref[...])
    for c in range(nc):
        y_ref[c] = y[:, c * LANES:(c + 1) * LANES]
    for b in range(nb):
        for c in range(nc):
            o_ref[b, :, c * LANES:(c + 1) * LANES] = (x_ref[b, :, c * LANES:(c + 1) * LANES]
                                                     + y_ref[c, pl.ds(b, tl, stride=nb), :])


def _even_outproj(ya, yb, w_a, w_b, x):
    nb, seq, d = x.shape
    tl = EVEN_TL
    rows = tl * nb
    wd = ya.shape[1]
    return pl.pallas_call(
        _even_outproj_kernel,
        out_shape=jax.ShapeDtypeStruct(x.shape, x.dtype),
        grid=(seq // tl,),
        in_specs=[pl.BlockSpec((rows, wd), lambda i: (i, 0)),
                  pl.BlockSpec((rows, wd), lambda i: (i, 0)),
                  pl.BlockSpec(w_a.shape, lambda i: (0, 0)),
                  pl.BlockSpec(w_b.shape, lambda i: (0, 0)),
                  pl.BlockSpec((nb, tl, d), lambda i: (0, i, 0))],
        out_specs=pl.BlockSpec((nb, tl, d), lambda i: (0, i, 0)),
        scratch_shapes=[pltpu.VMEM((d // LANES, rows, LANES), F32)],
        compiler_params=_cparams("parallel"),
        name="even_outproj",
    )(ya, yb, w_a, w_b, x)


def _block_diag(blocks):
    n, r, c = blocks.shape
    tiled = jnp.tile(blocks.reshape(n * r, c), (1, n))
    row_blk = jnp.arange(n * r, dtype=jnp.int32)[:, None] // r
    col_blk = jnp.arange(n * c, dtype=jnp.int32)[None, :] // c
    return jnp.where(row_blk == col_blk, tiled, jnp.zeros((), blocks.dtype))


def _even_mixer(x, norm_g, w_in, lam_re, lam_im, log_dt, b_re, b_im, c_re, c_im, d, glu_w, glu_b,
                conv_w, conv_b, wa, ba, wi, bi, lam, w_out):
    nb, seq, dm = x.shape
    g, p = lam_re.shape
    hch = b_re.shape[-1]
    width = g * hch
    n_half = 2
    gh = g // n_half

    proj = _even_inproj(x, norm_g[None, :], w_in.astype(BF16))

    lb_re, lb_im, bb_re, bb_im = _s5_discretize(lam_re, lam_im, log_dt, b_re, b_im)
    bb_re = bb_re.reshape(n_half, gh, hch, p)
    bb_im = bb_im.reshape(n_half, gh, hch, p)
    bm = jnp.stack([jnp.concatenate([_block_diag(bb_re[j]), _block_diag(bb_im[j])], axis=1)
                    for j in range(n_half)]).astype(BF16)
    c_re_t = jnp.swapaxes(c_re, 1, 2).reshape(n_half, gh, p, hch)
    c_im_t = jnp.swapaxes(c_im, 1, 2).reshape(n_half, gh, p, hch)
    cm = jnp.stack([jnp.concatenate([_block_diag(c_re_t[j]), -_block_diag(c_im_t[j])], axis=0)
                    for j in range(n_half)]).astype(BF16)
    lre = jnp.broadcast_to(lb_re.reshape(1, g * p), (nb, g * p))
    lim = jnp.broadcast_to(lb_im.reshape(1, g * p), (nb, g * p))
    ya = _s5_mixer(proj, nb, bm, lre, lim, cm, d.reshape(1, width), glu_w.astype(BF16),
                   glu_b[None, :])

    lw = conv_w.shape[1]
    nblk = 2
    per = LRU_BLOCKS // nblk
    wa_bd = jnp.stack([_block_diag(wa[j * per:(j + 1) * per]) for j in range(nblk)]).astype(BF16)
    wi_bd = jnp.stack([_block_diag(wi[j * per:(j + 1) * per]) for j in range(nblk)]).astype(BF16)
    yb = _lru_mixer(proj, nb, conv_w, conv_b[None, :], wa_bd, ba[None, :], wi_bd, bi[None, :],
                    lam[None, :])

    w_out_b = w_out.astype(BF16)
    return _even_outproj(ya, yb, w_out_b[:width], w_out_b[width:], x)


def _rope_table_kernel(pos_ref, inv_ref, cc_ref, ss_ref):
    tr = pos_ref.shape[0]
    half = ROPE_DIM // 2
    per_row = LANES // half
    ang = pos_ref[...].astype(F32) * inv_ref[...]
    cos = jnp.cos(ang)
    sin = jnp.sin(ang)
    lane = lax.broadcasted_iota(jnp.int32, ang.shape, 1)
    sign = jnp.where((lane & (ROPE_DIM - 1)) < half, -1.0, 1.0)

    def spread(a, j):
        g = jnp.where(lane // half == j, a, 0.0)
        out = g
        for k in range(1, per_row):
            out = out + pltpu.roll(g, k * half, 1)
        return out

    for j in range(per_row):
        cc_ref[pl.ds(j, tr, stride=per_row), :] = spread(cos, j)
        ss_ref[pl.ds(j, tr, stride=per_row), :] = spread(sin, j) * sign


def _rope_tables(positions):
    nb, seq = positions.shape
    half = ROPE_DIM // 2
    per_row = LANES // half
    n_tok = nb * seq
    tr = ROW_TILE
    inv_freq = 1.0 / (ROPE_THETA ** (jnp.arange(0, ROPE_DIM, 2, dtype=F32) / ROPE_DIM))
    pos_rep = jnp.repeat(positions.reshape(n_tok // per_row, per_row), half, axis=1)
    inv_rep = jnp.tile(inv_freq, per_row)[None, :]
    return pl.pallas_call(
        _rope_table_kernel,
        out_shape=(jax.ShapeDtypeStruct((n_tok, LANES), F32), jax.ShapeDtypeStruct((n_tok, LANES), F32)),
        grid=(n_tok // per_row // tr,),
        in_specs=[pl.BlockSpec((tr, LANES), lambda i: (i, 0)),
                  pl.BlockSpec((1, LANES), lambda i: (0, 0))],
        out_specs=(pl.BlockSpec((tr * per_row, LANES), lambda i: (i, 0)),
                   pl.BlockSpec((tr * per_row, LANES), lambda i: (i, 0))),
        compiler_params=_cparams("parallel"),
        name="rope_table",
    )(pos_rep, inv_rep)


def _mla_proj_kernel(x_ref, g_ref, win_ref, cqn_ref, ckvn_ref, wq_ref, wkv_ref, qg_ref, kg_ref,
                     cc_ref, ss_ref, qn_ref, qrp_ref, k2_ref, v_ref):
    nh = MLA_HEADS
    h = _rms(x_ref[...], g_ref[...])
    proj = _dot(h.astype(BF16), win_ref[...])
    c_q = _rms(proj[:, :Q_LORA], cqn_ref[...])
    c_kv = _rms(proj[:, Q_LORA:Q_LORA + KV_LORA], ckvn_ref[...])
    o = Q_LORA + KV_LORA
    k_r2 = proj[:, o:o + LANES]
    k_sw2 = proj[:, o + LANES:o + 2 * LANES]
    cc2 = cc_ref[...]
    ss2 = ss_ref[...]
    qg = qg_ref[...]
    kg = kg_ref[...]
    lo = lax.broadcasted_iota(jnp.int32, cc2.shape, 1) < ROPE_DIM
    scale = math.log2(math.e) / math.sqrt(QK_DIM)

    q = _dot(c_q.astype(BF16), wq_ref[...])
    q_a = cc2 * qg[1:2, :]
    q_b = ss2 * qg[2:3, :]
    for j in range(nh // 2):
        ro = nh * NOPE_DIM + j * LANES
        q_rp = q[:, ro:ro + LANES]
        q_swp = q[:, ro + nh * ROPE_DIM:ro + nh * ROPE_DIM + LANES]
        rot = q_rp * q_a + q_swp * q_b
        sq_rp = q_rp * q_rp
        rs_pair = []
        for par in range(2):
            hd = 2 * j + par
            q_n = q[:, hd * NOPE_DIM:(hd + 1) * NOPE_DIM]
            own = lo if par == 0 else jnp.logical_not(lo)
            ms = jnp.sum(q_n * q_n + jnp.where(own, sq_rp, 0.0), axis=-1, keepdims=True) / QK_DIM
            rs = lax.rsqrt(ms + RMS_EPS) * scale
            qn_ref[0, hd] = (q_n * rs * qg[0:1, :]).astype(qn_ref.dtype)
            rs_pair.append(rs)
        qrp_ref[0, j] = (rot * jnp.where(lo, rs_pair[0], rs_pair[1])).astype(qrp_ref.dtype)

    kv = _dot(c_kv.astype(BF16), wkv_ref[...])
    k_rot2 = k_r2 * (cc2 * kg[1:2, :]) + k_sw2 * (ss2 * kg[2:3, :])
    k_r_ss = jnp.sum(jnp.where(lo, k_r2 * k_r2, 0.0), axis=-1, keepdims=True)
    for hd in range(nh):
        ko = hd * (NOPE_DIM + V_DIM)
        k_n = kv[:, ko:ko + NOPE_DIM]
        ms_k = (jnp.sum(k_n * k_n, axis=-1, keepdims=True) + k_r_ss) / QK_DIM
        rs_k = lax.rsqrt(ms_k + RMS_EPS)
        own = lo if hd % 2 == 0 else jnp.logical_not(lo)
        k2_ref[0, hd, :, 0:NOPE_DIM] = (k_n * rs_k * kg[0:1, :]).astype(k2_ref.dtype)
        k2_ref[0, hd, :, NOPE_DIM:] = jnp.where(own, k_rot2 * rs_k, 0.0).astype(k2_ref.dtype)
        v_ref[0, hd] = kv[:, ko + NOPE_DIM:ko + NOPE_DIM + V_DIM].astype(v_ref.dtype)


def _swap_halves(a, axis=-1):
    h = a.shape[axis] // 2
    lo = lax.slice_in_dim(a, 0, h, axis=axis)
    hi = lax.slice_in_dim(a, h, 2 * h, axis=axis)
    return jnp.concatenate([hi, lo], axis=axis)


def _mla_proj(x, norm_g, positions, w_in, cq_norm, ckv_norm, w_uq, w_ukv, q_norm, k_norm):
    nb, seq, dm = x.shape
    nh = MLA_HEADS
    tm = ROW_TILE
    n_tok = nb * seq
    cc2, ss2 = _rope_tables(positions)
    o = Q_LORA + KV_LORA
    w_kr = w_in[:, o:o + ROPE_DIM]
    w_sw = _swap_halves(w_kr)
    w_in_x = jnp.concatenate([w_in[:, :o], w_kr, w_kr, w_sw, w_sw], axis=1).astype(BF16)
    wq = w_uq.reshape(Q_LORA, nh, QK_DIM)
    wq_n = wq[:, :, :NOPE_DIM].reshape(Q_LORA, nh * NOPE_DIM)
    wq_r = wq[:, :, NOPE_DIM:]
    wq_x = jnp.concatenate([wq_n, wq_r.reshape(Q_LORA, nh * ROPE_DIM),
                            _swap_halves(wq_r).reshape(Q_LORA, nh * ROPE_DIM)], axis=1).astype(BF16)

    def gains(gv):
        r = gv[NOPE_DIM:]
        rs = _swap_halves(r)
        return jnp.stack([gv[:NOPE_DIM], jnp.concatenate([r, r]), jnp.concatenate([rs, rs])])

    x2 = x.reshape(n_tok, dm)
    per_b = seq // tm
    full = lambda a: pl.BlockSpec(a.shape, lambda i: (0,) * a.ndim)
    args = (x2, norm_g[None, :], w_in_x, cq_norm[None, :], ckv_norm[None, :], wq_x,
            w_ukv.astype(BF16), gains(q_norm), gains(k_norm), cc2, ss2)
    in_specs = [pl.BlockSpec((tm, dm), lambda i: (i, 0))] + [full(a) for a in args[1:9]] + [
        pl.BlockSpec((tm, LANES), lambda i: (i, 0)), pl.BlockSpec((tm, LANES), lambda i: (i, 0))]

    def head_spec(n_heads, dh):
        return pl.BlockSpec((1, n_heads, tm, dh), lambda i: (i // per_b, 0, i % per_b, 0))

    def head_shape(n_heads, dh):
        return jax.ShapeDtypeStruct((nb, n_heads, seq, dh), BF16)

    return pl.pallas_call(
        _mla_proj_kernel,
        out_shape=(head_shape(nh, NOPE_DIM), head_shape(nh // 2, LANES), head_shape(nh, NOPE_DIM + LANES),
                   head_shape(nh, V_DIM)),
        grid=(n_tok // tm,),
        in_specs=in_specs,
        out_specs=(head_spec(nh, NOPE_DIM), head_spec(nh // 2, LANES), head_spec(nh, NOPE_DIM + LANES),
                   head_spec(nh, V_DIM)),
        compiler_params=_cparams("parallel"),
        name="mla_proj",
    )(*args)


def _attention_kernel(qn_ref, qrp_ref, k2_ref, v_ref, o_ref):
    seq = qn_ref.shape[2]
    t = ATT_TQ
    nt = (((1,), (1,)), ((), ()))
    tri = (lax.broadcasted_iota(jnp.int32, (t, t), 1) <= lax.broadcasted_iota(jnp.int32, (t, t), 0))
    for i in range(seq // t):
        q = jnp.concatenate([qn_ref[0, 0, i * t:(i + 1) * t, :], qrp_ref[0, 0, i * t:(i + 1) * t, :]], axis=-1)
        m = jnp.full((t, 1), NEG_INF, F32)
        l = jnp.zeros((t, 1), F32)
        acc = jnp.zeros((t, V_DIM), F32)
        for j in range(i + 1):
            s = lax.dot_general(q, k2_ref[0, 0, j * t:(j + 1) * t, :], nt, preferred_element_type=F32)
            if j == i:
                s = jnp.where(tri, s, NEG_INF)
            m_new = jnp.maximum(m, jnp.max(s, axis=-1, keepdims=True))
            alpha = jnp.exp2(m - m_new)
            p = jnp.exp2(s - m_new)
            l = alpha * l + jnp.sum(p, axis=-1, keepdims=True)
            acc = alpha * acc + _dot(p.astype(BF16), v_ref[0, 0, j * t:(j + 1) * t, :])
            m = m_new
        o_ref[0, i * t:(i + 1) * t, :] = (acc / l).astype(o_ref.dtype)


def _attention(qn, qrp, k2, v):
    nb, nh, seq, _ = qn.shape

    def spec(dh, per=1):
        return pl.BlockSpec((1, 1, seq, dh), lambda b, h: (b, h // per, 0, 0))

    return pl.pallas_call(
        _attention_kernel,
        out_shape=jax.ShapeDtypeStruct((nb, seq, nh * V_DIM), BF16),
        grid=(nb, nh),
        in_specs=[spec(NOPE_DIM), spec(LANES, 2), spec(NOPE_DIM + LANES), spec(V_DIM)],
        out_specs=pl.BlockSpec((1, seq, V_DIM), lambda b, h: (b, 0, h)),
        compiler_params=_cparams("parallel", "parallel"),
        name="mla_attention",
    )(qn, qrp, k2, v)


def _proj_residual_kernel(a_ref, w_ref, x_ref, o_ref):
    o_ref[...] = x_ref[...] + _dot(a_ref[...], w_ref[...])


def _proj_residual(a, w, x2):
    n_tok, dm = x2.shape
    tm = ROW_TILE
    return pl.pallas_call(
        _proj_residual_kernel,
        out_shape=jax.ShapeDtypeStruct(x2.shape, x2.dtype),
        grid=(n_tok // tm,),
        in_specs=[pl.BlockSpec((tm, a.shape[1]), lambda i: (i, 0)),
                  pl.BlockSpec(w.shape, lambda i: (0, 0)),
                  pl.BlockSpec((tm, dm), lambda i: (i, 0))],
        out_specs=pl.BlockSpec((tm, dm), lambda i: (i, 0)),
        compiler_params=_cparams("parallel"),
        name="proj_residual",
    )(a, w, x2)


def _mla_mixer(x, norm_g, positions, w_in, cq_norm, ckv_norm, w_uq, w_ukv, q_norm, k_norm, w_o):
    nb, seq, dm = x.shape
    qn, qrp, k2, v = _mla_proj(x, norm_g, positions, w_in, cq_norm, ckv_norm, w_uq, w_ukv, q_norm, k_norm)
    o = _attention(qn, qrp, k2, v)
    out = _proj_residual(o.reshape(nb * seq, -1), w_o.astype(BF16), x.reshape(nb * seq, dm))
    return out.reshape(nb, seq, dm)


def _router_kernel(x_ref, g_ref, w_ref, b_ref, tri_ref, meta_ref, gate_ref, cnt_ref, base_ref):
    tm, dm = x_ref.shape
    ne = N_EXPERTS

    @pl.when(pl.program_id(0) == 0)
    def _():
        base_ref[...] = jnp.zeros_like(base_ref)

    hn = _rms(x_ref[...], g_ref[...])
    logits = _dot(hn.astype(BF16), w_ref[...]) + b_ref[...]
    lane = lax.broadcasted_iota(jnp.int32, logits.shape, 1)
    big = jnp.int32(1 << 20)

    def first_argmax(v):
        m = jnp.max(v, axis=-1, keepdims=True)
        idx = jnp.min(jnp.where(v == m, lane, big), axis=-1, keepdims=True)
        return m, idx

    is_g = (lane >= ne) & (lane < ne + N_GROUPS)
    gl = jnp.where(is_g, logits, NEG_INF)
    g_max, g_idx = first_argmax(gl)
    g_sel = g_idx - ne
    p_g = 1.0 / jnp.sum(jnp.where(is_g, jnp.exp(gl - g_max), 0.0), axis=-1, keepdims=True)
    in_grp = (lane >= g_sel * EXPERTS_PER_GROUP) & (lane < (g_sel + 1) * EXPERTS_PER_GROUP)
    el = jnp.where(in_grp, logits, NEG_INF)
    v0, e0 = first_argmax(el)
    v1, e1 = first_argmax(jnp.where(lane == e0, NEG_INF, el))
    t = jnp.exp(v1 - v0)
    gate0 = p_g / (1.0 + t)
    gate1 = p_g * t / (1.0 + t)

    oh0 = (lane == e0).astype(F32)
    oh1 = (lane == e1).astype(F32)
    both = oh0 + oh1
    before = _dot(tri_ref[...], both.astype(BF16)) + base_ref[0:1, :]
    rank0 = jnp.sum(oh0 * before, axis=-1, keepdims=True)
    rank1 = jnp.sum(oh1 * before, axis=-1, keepdims=True)
    base_ref[0:1, :] = base_ref[0:1, :] + jnp.sum(both, axis=0, keepdims=True)
    cnt_ref[...] = jnp.broadcast_to(base_ref[0:1, :], cnt_ref.shape).astype(jnp.int32)

    meta = jnp.where(lane == 0, e0, jnp.where(lane == 1, e1, jnp.where(
        lane == 2, rank0.astype(jnp.int32), jnp.where(lane == 3, rank1.astype(jnp.int32), 0))))
    meta_ref[...] = meta
    gate_ref[...] = jnp.where(lane == 0, gate0, jnp.where(lane == 1, gate1, 0.0))


def _router(x2, norm_g, w_group, b_group, w_expert, b_expert):
    n_tok, dm = x2.shape
    tm = ROW_TILE
    pad = LANES - N_EXPERTS - N_GROUPS
    w = jnp.concatenate([w_expert, w_group, jnp.zeros((dm, pad), F32)], axis=1).astype(BF16)
    b = jnp.concatenate([b_expert, b_group, jnp.zeros((pad,), F32)])[None, :]
    idx = jnp.arange(tm, dtype=jnp.int32)
    tri = (idx[None, :] < idx[:, None]).astype(BF16)
    return pl.pallas_call(
        _router_kernel,
        out_shape=(jax.ShapeDtypeStruct((n_tok, LANES), jnp.int32),
                   jax.ShapeDtypeStruct((n_tok, LANES), F32),
                   jax.ShapeDtypeStruct((SUBLANES, LANES), jnp.int32)),
        grid=(n_tok // tm,),
        in_specs=[pl.BlockSpec((tm, dm), lambda i: (i, 0)),
                  pl.BlockSpec((1, dm), lambda i: (0, 0)),
                  pl.BlockSpec((dm, LANES), lambda i: (0, 0)),
                  pl.BlockSpec((1, LANES), lambda i: (0, 0)),
                  pl.BlockSpec((tm, tm), lambda i: (0, 0))],
        out_specs=(pl.BlockSpec((tm, LANES), lambda i: (i, 0)),
                   pl.BlockSpec((tm, LANES), lambda i: (i, 0)),
                   pl.BlockSpec((SUBLANES, LANES), lambda i: (0, 0))),
        scratch_shapes=[pltpu.VMEM((SUBLANES, LANES), F32)],
        compiler_params=_cparams("arbitrary"),
        name="moe_router",
    )(x2, norm_g[None, :], w, b, tri)


def _dispatch_kernel(dest_ref, dest_prev_ref, x_ref, g_ref, xd_ref, hn_ref, zero_ref, sem, zsem):
    tm = x_ref.shape[0]
    i = pl.program_id(0)
    n_steps = pl.num_programs(0)
    slot = i % 2
    n_assign = n_steps * tm * TOP_K

    def tail_copy():
        return pltpu.make_async_copy(zero_ref, xd_ref.at[pl.ds(n_assign * ROW_CHUNKS, zero_ref.shape[0])], zsem)

    @pl.when(i == 0)
    def _():
        zero_ref[...] = jnp.zeros_like(zero_ref)
        tail_copy().start()

    hn = _rms(x_ref[...], g_ref[...])
    hbuf = hn_ref.at[slot]
    for c in range(ROW_CHUNKS):
        hbuf[pl.ds(c, tm, stride=ROW_CHUNKS), :] = hn[:, c * LANES:(c + 1) * LANES]

    def row_copy(d_ref, sl, t, k):
        src = pl.multiple_of(t * ROW_CHUNKS, ROW_CHUNKS)
        dst = pl.multiple_of(d_ref[0, 0, 2 * t + k], ROW_CHUNKS)
        return pltpu.make_async_copy(hn_ref.at[sl, pl.ds(src, ROW_CHUNKS)], xd_ref.at[pl.ds(dst, ROW_CHUNKS)],
                                     sem.at[sl])

    def issue(j, c):
        for u in range(DMA_UNROLL):
            for k in range(TOP_K):
                row_copy(dest_ref, slot, j * DMA_UNROLL + u, k).start(priority=k)
        return c

    lax.fori_loop(0, tm // DMA_UNROLL, issue, 0)

    def drain_all(d_ref, sl):
        def drain(j, c):
            for u in range(DMA_UNROLL):
                for k in range(TOP_K):
                    row_copy(d_ref, sl, j * DMA_UNROLL + u, k).wait()
            return c
        lax.fori_loop(0, tm // DMA_UNROLL, drain, 0)

    @pl.when(i >= 1)
    def _():
        drain_all(dest_prev_ref, 1 - slot)

    @pl.when(i == n_steps - 1)
    def _():
        drain_all(dest_ref, slot)

    @pl.when(i == 0)
    def _():
        tail_copy().wait()


def _dispatch(x2, norm_g, dest_off):
    tm = GATHER_TM
    n_tok, dm = x2.shape
    dest3 = dest_off.reshape(n_tok // tm, 1, 2 * tm)
    n_rows = n_tok * TOP_K + MOE_BM
    return pl.pallas_call(
        _dispatch_kernel,
        out_shape=jax.ShapeDtypeStruct((n_rows * ROW_CHUNKS, LANES), F32),
        grid=(n_tok // tm,),
        in_specs=[pl.BlockSpec((1, 1, 2 * tm), lambda i: (i, 0, 0), memory_space=pltpu.SMEM),
                  pl.BlockSpec((1, 1, 2 * tm), lambda i: (jnp.maximum(i - 1, 0), 0, 0), memory_space=pltpu.SMEM),
                  pl.BlockSpec((tm, dm), lambda i: (i, 0)),
                  pl.BlockSpec((1, dm), lambda i: (0, 0))],
        out_specs=pl.BlockSpec(memory_space=pl.ANY),
        scratch_shapes=[pltpu.VMEM((2, tm * ROW_CHUNKS, LANES), F32),
                        pltpu.VMEM((MOE_BM * ROW_CHUNKS, LANES), F32),
                        pltpu.SemaphoreType.DMA((2,)), pltpu.SemaphoreType.DMA],
        compiler_params=_cparams("arbitrary"),
        name="moe_dispatch",
    )(dest3, dest3, x2, norm_g[None, :])


def _rows_from_tiles(ref, n, base=0):
    return jnp.concatenate([ref[pl.ds(base + c, n, stride=ROW_CHUNKS), :] for c in range(ROW_CHUNKS)], axis=-1)


def _expert_kernel(be_ref, bs_ref, first_ref, nxt_ref, nu_ref, xd_ref, w1_ref, w3_ref, w2_ref, yd_ref,
                   xbuf, ybuf, w1buf, w3buf, w2buf, w13_s, w2_s, sem_in, sem_out, sem_w, *, layer):
    nu = nu_ref[0]
    n_rows = xbuf.shape[1]
    bm = n_rows // ROW_CHUNKS
    de = w1buf.shape[2]

    def in_copy(blk, sl):
        src = pl.multiple_of(bs_ref[blk] * ROW_CHUNKS, ROW_CHUNKS)
        return pltpu.make_async_copy(xd_ref.at[pl.ds(src, n_rows)], xbuf.at[sl], sem_in.at[sl])

    def out_copy(blk, sl):
        dst = pl.multiple_of(bs_ref[blk] * ROW_CHUNKS, ROW_CHUNKS)
        return pltpu.make_async_copy(ybuf.at[sl], yd_ref.at[pl.ds(dst, n_rows)], sem_out.at[sl])

    def w_copies(e, ws):
        return (pltpu.make_async_copy(w1_ref.at[layer, e], w1buf.at[ws], sem_w.at[ws]),
                pltpu.make_async_copy(w3_ref.at[layer, e], w3buf.at[ws], sem_w.at[ws]),
                pltpu.make_async_copy(w2_ref.at[layer, e], w2buf.at[ws], sem_w.at[ws]))

    in_copy(0, 0).start()
    for cp in w_copies(be_ref[0], 1):
        cp.start(priority=1)

    def block(b, ws):
        slot = b % 2

        @pl.when(b + 1 < nu)
        def _():
            in_copy(b + 1, 1 - slot).start()

        first = first_ref[b] == 1
        ws = jnp.where(first, 1 - ws, ws)

        @pl.when(first)
        def _():
            for cp in w_copies(be_ref[b], ws):
                cp.wait()
            w13_s[:, :de] = w1buf[ws].astype(BF16)
            w13_s[:, de:] = w3buf[ws].astype(BF16)
            w2_s[...] = w2buf[ws].astype(BF16)

            @pl.when(nxt_ref[b] >= 0)
            def _():
                for cp in w_copies(nxt_ref[b], 1 - ws):
                    cp.start(priority=1)

        in_copy(b, slot).wait()
        x = _rows_from_tiles(xbuf.at[slot], bm).astype(BF16)
        h = _dot(x, w13_s[...])
        a = jax.nn.silu(h[:, :de]) * h[:, de:]
        y = _dot(a.astype(BF16), w2_s[...])
        yb = ybuf.at[slot]
        for c in range(ROW_CHUNKS):
            yb[pl.ds(c, bm, stride=ROW_CHUNKS), :] = y[:, c * LANES:(c + 1) * LANES]

        @pl.when(b >= 1)
        def _():
            out_copy(b - 1, 1 - slot).wait()
        out_copy(b, slot).start()
        return ws

    lax.fori_loop(0, nu, block, jnp.int32(0))

    last = nu - 1
    out_copy(last, last % 2).wait()
    tail = yd_ref.shape[0] - n_rows
    fill_slot = 1 - last % 2
    ybuf[fill_slot] = jnp.zeros((n_rows, LANES), F32)
    fill = pltpu.make_async_copy(ybuf.at[fill_slot], yd_ref.at[pl.ds(tail, n_rows)], sem_out.at[fill_slot])
    fill.start()
    fill.wait()


def _experts(xd, blk_expert, blk_start, blk_first, blk_next, n_used, w1, w3, w2, layer):
    bm = MOE_BM
    _, _, dm, de = w1.shape
    any_spec = pl.BlockSpec(memory_space=pl.ANY)
    grid_spec = pltpu.PrefetchScalarGridSpec(
        num_scalar_prefetch=5,
        grid=(1,),
        in_specs=[any_spec, any_spec, any_spec, any_spec],
        out_specs=any_spec,
        scratch_shapes=[pltpu.VMEM((2, bm * ROW_CHUNKS, LANES), F32),
                        pltpu.VMEM((2, bm * ROW_CHUNKS, LANES), F32),
                        pltpu.VMEM((2, dm, de), F32),
                        pltpu.VMEM((2, dm, de), F32),
                        pltpu.VMEM((2, de, dm), F32),
                        pltpu.VMEM((dm, 2 * de), BF16),
                        pltpu.VMEM((de, dm), BF16),
                        pltpu.SemaphoreType.DMA((2,)),
                        pltpu.SemaphoreType.DMA((2,)),
                        pltpu.SemaphoreType.DMA((2,))],
    )
    return pl.pallas_call(
        functools.partial(_expert_kernel, layer=layer),
        out_shape=jax.ShapeDtypeStruct(xd.shape, F32),
        grid_spec=grid_spec,
        compiler_params=_cparams("arbitrary"),
        name="moe_experts",
    )(blk_expert, blk_start, blk_first, blk_next, n_used, xd, w1, w3, w2)


def _combine_kernel(dest_ref, dest_next_ref, yd_ref, gate_ref, x_ref, o_ref, buf_ref, sem):
    tm, dm = x_ref.shape
    i = pl.program_id(0)
    n_steps = pl.num_programs(0)
    slot = i % 2
    k_rows = tm * ROW_CHUNKS

    def row_copy(d_ref, sl, t, k):
        src = pl.multiple_of(d_ref[0, 0, 2 * t + k], ROW_CHUNKS)
        dst = pl.multiple_of(k * k_rows + t * ROW_CHUNKS, ROW_CHUNKS)
        return pltpu.make_async_copy(yd_ref.at[pl.ds(src, ROW_CHUNKS)],
                                     buf_ref.at[sl, pl.ds(dst, ROW_CHUNKS)], sem.at[sl])

    def issue_all(d_ref, sl):
        def issue(j, c):
            for u in range(DMA_UNROLL):
                for k in range(TOP_K):
                    row_copy(d_ref, sl, j * DMA_UNROLL + u, k).start(priority=k)
            return c
        lax.fori_loop(0, tm // DMA_UNROLL, issue, 0)

    @pl.when(i == 0)
    def _():
        issue_all(dest_ref, slot)

    @pl.when(i + 1 < n_steps)
    def _():
        issue_all(dest_next_ref, 1 - slot)

    def drain(j, c):
        for u in range(DMA_UNROLL):
            for k in range(TOP_K):
                row_copy(dest_ref, slot, j * DMA_UNROLL + u, k).wait()
        return c

    lax.fori_loop(0, tm // DMA_UNROLL, drain, 0)

    g0 = gate_ref[:, 0:1]
    g1 = gate_ref[:, 1:2]
    buf = buf_ref.at[slot]
    for c in range(ROW_CHUNKS):
        y0 = buf[pl.ds(c, tm, stride=ROW_CHUNKS), :]
        y1 = buf[pl.ds(k_rows + c, tm, stride=ROW_CHUNKS), :]
        o_ref[:, c * LANES:(c + 1) * LANES] = x_ref[:, c * LANES:(c + 1) * LANES] + (g0 * y0 + g1 * y1)


def _combine(yd, dest_off, gates, x2):
    n_tok, dm = x2.shape
    tm = GATHER_TM
    n_steps = n_tok // tm
    dest3 = dest_off.reshape(n_steps, 1, 2 * tm)
    return pl.pallas_call(
        _combine_kernel,
        out_shape=jax.ShapeDtypeStruct(x2.shape, x2.dtype),
        grid=(n_steps,),
        in_specs=[pl.BlockSpec((1, 1, 2 * tm), lambda i: (i, 0, 0), memory_space=pltpu.SMEM),
                  pl.BlockSpec((1, 1, 2 * tm), lambda i: (jnp.minimum(i + 1, n_steps - 1), 0, 0),
                               memory_space=pltpu.SMEM),
                  pl.BlockSpec(memory_space=pl.ANY),
                  pl.BlockSpec((tm, LANES), lambda i: (i, 0)),
                  pl.BlockSpec((tm, dm), lambda i: (i, 0))],
        out_specs=pl.BlockSpec((tm, dm), lambda i: (i, 0)),
        scratch_shapes=[pltpu.VMEM((2, TOP_K * tm * ROW_CHUNKS, LANES), F32),
                        pltpu.SemaphoreType.DMA((2,))],
        compiler_params=_cparams("arbitrary"),
        name="moe_combine",
    )(dest3, dest3, yd, gates, x2)


def _moe(x, norm_g, w_group, b_group, w_expert, b_expert, w1, w3, w2, layer):
    nb, seq, dm = x.shape
    n_tok = nb * seq
    n_assign = n_tok * TOP_K
    x2 = x.reshape(n_tok, dm)
    meta, gates, counts = _router(x2, norm_g, w_group, b_group, w_expert, b_expert)
    bm = MOE_BM
    n_blocks = n_assign // bm + N_EXPERTS
    counts = counts[0, :N_EXPERTS]
    starts = jnp.cumsum(counts) - counts
    eid = meta[:, 0:TOP_K]
    e_iota = jnp.arange(N_EXPERTS, dtype=jnp.int32)
    seg_start = jnp.sum(jnp.where(eid[:, :, None] == e_iota, starts, 0), axis=-1)
    dest_off = ((seg_start + meta[:, TOP_K:2 * TOP_K]) * ROW_CHUNKS).astype(jnp.int32).reshape(-1)
    nblk = (counts + bm - 1) // bm
    blk_end = jnp.cumsum(nblk)
    b_iota = jnp.arange(n_blocks, dtype=jnp.int32)
    blk_expert = jnp.minimum(jnp.sum((b_iota[:, None] >= blk_end[None, :]).astype(jnp.int32), axis=1),
                             N_EXPERTS - 1)
    j = b_iota - (blk_end - nblk)[blk_expert]
    n_used = blk_end[-1].astype(jnp.int32).reshape(1)
    blk_start = jnp.where(b_iota < n_used[0], starts[blk_expert] + j * bm, 0).astype(jnp.int32)
    blk_first = (j == 0).astype(jnp.int32)
    cand = jnp.where(counts > 0, e_iota, N_EXPERTS)
    later = lax.cummin(cand[::-1])[::-1]
    next_e = jnp.concatenate([later[1:], jnp.full((1,), N_EXPERTS, jnp.int32)])
    blk_next = jnp.where(next_e[blk_expert] < N_EXPERTS, next_e[blk_expert], -1).astype(jnp.int32)

    xd = _dispatch(x2, norm_g, dest_off)
    yd = _experts(xd, blk_expert, blk_start, blk_first, blk_next, n_used, w1, w3, w2, layer)
    out = _combine(yd, dest_off, gates, x2)
    return out.reshape(nb, seq, dm)


def kernel(x, positions, ev_norm, ev_w_in, s5_lambda_re, s5_lambda_im, s5_log_dt, s5_b_re, s5_b_im,
           s5_c_re, s5_c_im, s5_d, s5_glu_w, s5_glu_b, lru_conv_w, lru_conv_b, lru_wa, lru_ba, lru_wi,
           lru_bi, lru_lambda, ev_w_out, od_norm, mla_w_in, mla_cq_norm, mla_ckv_norm, mla_w_uq,
           mla_w_ukv, mla_q_norm, mla_k_norm, mla_w_o, ffn_norm, moe_w_group, moe_b_group,
           moe_w_expert, moe_b_expert, moe_w1, moe_w3, moe_w2):
    depth = ffn_norm.shape[0]
    for layer in range(depth):
        j = layer // 2
        if layer % 2 == 0:
            x = _even_mixer(x, ev_norm[j], ev_w_in[j], s5_lambda_re[j], s5_lambda_im[j], s5_log_dt[j],
                            s5_b_re[j], s5_b_im[j], s5_c_re[j], s5_c_im[j], s5_d[j], s5_glu_w[j],
                            s5_glu_b[j], lru_conv_w[j], lru_conv_b[j], lru_wa[j], lru_ba[j], lru_wi[j],
                            lru_bi[j], lru_lambda[j], ev_w_out[j])
        else:
            x = _mla_mixer(x, od_norm[j], positions, mla_w_in[j], mla_cq_norm[j], mla_ckv_norm[j],
                           mla_w_uq[j], mla_w_ukv[j], mla_q_norm[j], mla_k_norm[j], mla_w_o[j])
        x = _moe(x, ffn_norm[layer], moe_w_group[layer], moe_b_group[layer], moe_w_expert[layer],
                 moe_b_expert[layer], moe_w1, moe_w3, moe_w2, layer)
    return x
```

```python
import functools
import math

import jax
import jax.numpy as jnp
from jax import lax
from jax.experimental import pallas as pl
from jax.experimental.pallas import tpu as pltpu

F32 = jnp.float32
BF16 = jnp.bfloat16

S5_GROUP_CH = 16
S5_STATE = 64
LRU_BLOCKS = 8
CONV_WIDTH = 4
RG_C = 8.0
MLA_HEADS = 8
Q_LORA = 384
KV_LORA = 256
NOPE_DIM = 128
ROPE_DIM = 64
V_DIM = 128
QK_DIM = NOPE_DIM + ROPE_DIM
ROPE_THETA = 10000.0
N_GROUPS = 8
EXPERTS_PER_GROUP = 8
N_EXPERTS = N_GROUPS * EXPERTS_PER_GROUP
TOP_K = 2
RMS_EPS = 1e-6
NEG_INF = -1e30

SUBLANES = 8
LANES = 128
ROW_CHUNKS = 8
VMEM_LIMIT = 48 * 1024 * 1024

EVEN_TL = 128
ROW_TILE = 512
MOE_BM = 512
ATT_TQ = 512
GATHER_TM = 512
DMA_UNROLL = 8


def _cparams(*sem):
    return pltpu.CompilerParams(dimension_semantics=tuple(sem), vmem_limit_bytes=VMEM_LIMIT)


def _rms(x, g):
    ms = jnp.mean(x * x, axis=-1, keepdims=True)
    return x * lax.rsqrt(ms + RMS_EPS) * g


def _dot(a, b):
    return jnp.dot(a, b, preferred_element_type=F32)


def _s5_discretize_kernel(lre_ref, lim_ref, ldt_ref, bre_ref, bim_ref,
                          lbre_ref, lbim_ref, bbre_ref, bbim_ref):
    lr = jnp.minimum(lre_ref[...], -1e-4)
    li = lim_ref[...]
    dt = jnp.exp(ldt_ref[...])
    mag = jnp.exp(lr * dt)
    lb_re = mag * jnp.cos(li * dt)
    lb_im = mag * jnp.sin(li * dt)
    den = lr * lr + li * li
    num_re = lb_re - 1.0
    f_re = (num_re * lr + lb_im * li) / den
    f_im = (lb_im * lr - num_re * li) / den
    lbre_ref[...] = lb_re
    lbim_ref[...] = lb_im
    br = bre_ref[...]
    bi = bim_ref[...]
    bbre_ref[...] = f_re[:, None, :] * br - f_im[:, None, :] * bi
    bbim_ref[...] = f_re[:, None, :] * bi + f_im[:, None, :] * br


def _s5_discretize(lam_re, lam_im, log_dt, b_re, b_im):
    g, p = lam_re.shape
    h = b_re.shape[-1]
    b_re_t = jnp.swapaxes(b_re, 1, 2)
    b_im_t = jnp.swapaxes(b_im, 1, 2)
    return pl.pallas_call(
        _s5_discretize_kernel,
        out_shape=(jax.ShapeDtypeStruct((g, p), F32), jax.ShapeDtypeStruct((g, p), F32),
                   jax.ShapeDtypeStruct((g, h, p), F32), jax.ShapeDtypeStruct((g, h, p), F32)),
        name="s5_discretize",
    )(lam_re, lam_im, log_dt[:, None], b_re_t, b_im_t)


def _even_inproj_kernel(x_ref, g_ref, w_ref, o_ref, h_ref):
    nb, tl, _ = x_ref.shape
    nc = h_ref.shape[0]
    g = g_ref[...]
    for b in range(nb):
        h = _rms(x_ref[b], g)
        for c in range(nc):
            h_ref[c, pl.ds(b, tl, stride=nb), :] = h[:, c * LANES:(c + 1) * LANES]
    h_all = jnp.concatenate([h_ref[c] for c in range(nc)], axis=-1)
    o_ref[...] = _dot(h_all.astype(BF16), w_ref[...]).astype(o_ref.dtype)


def _even_inproj(x, g, w):
    nb, seq, d = x.shape
    n_out = w.shape[1]
    tl = EVEN_TL
    return pl.pallas_call(
        _even_inproj_kernel,
        out_shape=jax.ShapeDtypeStruct((seq * nb, n_out), BF16),
        grid=(seq // tl,),
        in_specs=[pl.BlockSpec((nb, tl, d), lambda i: (0, i, 0)),
                  pl.BlockSpec((1, d), lambda i: (0, 0)),
                  pl.BlockSpec((d, n_out), lambda i: (0, 0))],
        out_specs=pl.BlockSpec((tl * nb, n_out), lambda i: (i, 0)),
        scratch_shapes=[pltpu.VMEM((d // LANES, tl * nb, LANES), F32)],
        compiler_params=_cparams("parallel"),
        name="even_inproj",
    )(x, g, w)


def _s5_kernel(u_ref, bm_ref, lre_ref, lim_ref, cm_ref, d_ref, gw_ref, gb_ref, o_ref,
               bu_ref, sb_ref, st_ref, *, nb):
    rows, width = u_ref.shape
    n_half = bm_ref.shape[0]
    kin = width // n_half
    ncol = bm_ref.shape[2]
    nre = ncol // 2
    chunk = 512
    steps = rows // nb

    @pl.when(pl.program_id(0) == 0)
    def _():
        st_ref[...] = jnp.zeros_like(st_ref)

    u = u_ref[...]
    for j in range(n_half):
        bu_ref[:, j * ncol:(j + 1) * ncol] = _dot(u[:, j * kin:(j + 1) * kin], bm_ref[j])

    for j in range(n_half):
        for c in range(nre // chunk):
            cr = j * ncol + c * chunk
            ci = cr + nre
            lc = j * nre + c * chunk
            lr = lre_ref[:, lc:lc + chunk]
            li = lim_ref[:, lc:lc + chunk]

            def body(i, carry, cr=cr, ci=ci, lr=lr, li=li):
                sr, si = carry
                r0 = pl.multiple_of(i * (2 * nb), 2 * nb)
                sr1 = lr * sr - li * si + bu_ref[pl.ds(r0, nb), cr:cr + chunk]
                si1 = lr * si + li * sr + bu_ref[pl.ds(r0, nb), ci:ci + chunk]
                sr2 = lr * sr1 - li * si1 + bu_ref[pl.ds(r0 + nb, nb), cr:cr + chunk]
                si2 = lr * si1 + li * sr1 + bu_ref[pl.ds(r0 + nb, nb), ci:ci + chunk]
                sb_ref[pl.ds(r0, 2 * nb), cr:cr + chunk] = jnp.concatenate([sr1, sr2], 0).astype(BF16)
                sb_ref[pl.ds(r0, 2 * nb), ci:ci + chunk] = jnp.concatenate([si1, si2], 0).astype(BF16)
                return sr2, si2

            sr, si = lax.fori_loop(0, steps // 2, body,
                                   (st_ref[:, cr:cr + chunk], st_ref[:, ci:ci + chunk]))
            st_ref[:, cr:cr + chunk] = sr
            st_ref[:, ci:ci + chunk] = si

    ys = [_dot(sb_ref[:, j * ncol:(j + 1) * ncol], cm_ref[j]) for j in range(n_half)]
    y = jnp.concatenate(ys, axis=-1) + d_ref[...] * u.astype(F32)
    y = jax.nn.gelu(y)
    z = _dot(y.astype(BF16), gw_ref[...]) + gb_ref[...]
    o_ref[...] = (y * jax.nn.sigmoid(z)).astype(o_ref.dtype)


def _s5_mixer(proj, nb, bm, lre, lim, cm, d, glu_w, glu_b):
    rows_total = proj.shape[0]
    width = d.shape[1]
    rows = EVEN_TL * nb
    n_half, _, ncol = bm.shape
    kern = functools.partial(_s5_kernel, nb=nb)
    return pl.pallas_call(
        kern,
        out_shape=jax.ShapeDtypeStruct((rows_total, width), BF16),
        grid=(rows_total // rows,),
        in_specs=[pl.BlockSpec((rows, width), lambda i: (i, 0)),
                  pl.BlockSpec(bm.shape, lambda i: (0, 0, 0)),
                  pl.BlockSpec(lre.shape, lambda i: (0, 0)),
                  pl.BlockSpec(lim.shape, lambda i: (0, 0)),
                  pl.BlockSpec(cm.shape, lambda i: (0, 0, 0)),
                  pl.BlockSpec((1, width), lambda i: (0, 0)),
                  pl.BlockSpec(glu_w.shape, lambda i: (0, 0)),
                  pl.BlockSpec((1, width), lambda i: (0, 0))],
        out_specs=pl.BlockSpec((rows, width), lambda i: (i, 0)),
        scratch_shapes=[pltpu.VMEM((rows, n_half * ncol), F32),
                        pltpu.VMEM((rows, n_half * ncol), BF16),
                        pltpu.VMEM((nb, n_half * ncol), F32)],
        compiler_params=_cparams("arbitrary"),
        name="s5_mixer",
    )(proj, bm, lre, lim, cm, d, glu_w, glu_b)


def _lru_kernel(x_ref, gate_ref, cw_ref, cb_ref, wa_ref, ba_ref, wi_ref, bi_ref, lam_ref, o_ref,
                xp_ref, a_ref, b_ref, h_ref, *, nb):
    rows, width = x_ref.shape
    halo = (CONV_WIDTH - 1) * nb
    steps = rows // nb

    @pl.when(pl.program_id(0) == 0)
    def _():
        xp_ref[0:halo, :] = jnp.zeros((halo, width), F32)
        h_ref[...] = jnp.zeros_like(h_ref)

    xp_ref[halo:halo + rows, :] = x_ref[...].astype(F32)
    xf = cb_ref[...] + cw_ref[0:1, :] * xp_ref[0:rows, :]
    for k in range(1, CONV_WIDTH):
        xf = xf + cw_ref[k:k + 1, :] * xp_ref[k * nb:k * nb + rows, :]
    xp_ref[0:halo, :] = xp_ref[rows:rows + halo, :]

    xb = xf.astype(BF16)
    nblk = wa_ref.shape[0]
    kb = width // nblk
    ga = jnp.concatenate([_dot(xb[:, j * kb:(j + 1) * kb], wa_ref[j]) for j in range(nblk)], -1)
    gi = jnp.concatenate([_dot(xb[:, j * kb:(j + 1) * kb], wi_ref[j]) for j in range(nblk)], -1)
    r = jax.nn.sigmoid(ga + ba_ref[...])
    ig = jax.nn.sigmoid(gi + bi_ref[...])
    log_a = RG_C * r * jax.nn.log_sigmoid(lam_ref[...])
    a_ref[...] = jnp.exp(log_a)
    th = jnp.tanh(log_a)
    b_ref[...] = jnp.sqrt(-2.0 * th / (1.0 - th)) * (ig * xf)

    def body(t, h):
        r0 = pl.multiple_of(t * nb, nb)
        h = a_ref[pl.ds(r0, nb), :] * h + b_ref[pl.ds(r0, nb), :]
        b_ref[pl.ds(r0, nb), :] = h
        return h

    h_ref[...] = lax.fori_loop(0, steps, body, h_ref[...])
    o_ref[...] = (b_ref[...] * jax.nn.gelu(gate_ref[...].astype(F32))).astype(o_ref.dtype)


def _lru_mixer(proj, nb, conv_w, conv_b, wa, ba, wi, bi, lam):
    rows_total = proj.shape[0]
    width = conv_w.shape[1]
    rows = EVEN_TL * nb
    halo = (CONV_WIDTH - 1) * nb
    kern = functools.partial(_lru_kernel, nb=nb)
    vec = pl.BlockSpec((1, width), lambda i: (0, 0))
    return pl.pallas_call(
        kern,
        out_shape=jax.ShapeDtypeStruct((rows_total, width), BF16),
        grid=(rows_total // rows,),
        in_specs=[pl.BlockSpec((rows, width), lambda i: (i, 1)),
                  pl.BlockSpec((rows, width), lambda i: (i, 2)),
                  pl.BlockSpec(conv_w.shape, lambda i: (0, 0)),
                  vec,
                  pl.BlockSpec(wa.shape, lambda i: (0, 0, 0)),
                  vec,
                  pl.BlockSpec(wi.shape, lambda i: (0, 0, 0)),
                  vec, vec],
        out_specs=pl.BlockSpec((rows, width), lambda i: (i, 0)),
        scratch_shapes=[pltpu.VMEM((rows + halo, width), F32),
                        pltpu.VMEM((rows, width), F32),
                        pltpu.VMEM((rows, width), F32),
                        pltpu.VMEM((nb, width), F32)],
        compiler_params=_cparams("arbitrary"),
        name="lru_mixer",
    )(proj, proj, conv_w, conv_b, wa, ba, wi, bi, lam)


def _even_outproj_kernel(ya_ref, yb_ref, wa_ref, wb_ref, x_ref, o_ref, y_ref):
    nb, tl, _ = x_ref.shape
    nc = y_ref.shape[0]
    y = _dot(ya_ref[...], wa_ref[...]) + _dot(yb_ref[...], wb_ref[...])
    for c in range(nc):
        y_ref[c] = y[:, c * LANES:(c + 1) * LANES]
    for b in range(nb):
        for c in range(nc):
            o_ref[b, :, c * LANES:(c + 1) * LANES] = (x_ref[b, :, c * LANES:(c + 1) * LANES]
                                                     + y_ref[c, pl.ds(b, tl, stride=nb), :])


def _even_outproj(ya, yb, w_a, w_b, x):
    nb, seq, d = x.shape
    tl = EVEN_TL
    rows = tl * nb
    wd = ya.shape[1]
    return pl.pallas_call(
        _even_outproj_kernel,
        out_shape=jax.ShapeDtypeStruct(x.shape, x.dtype),
        grid=(seq // tl,),
        in_specs=[pl.BlockSpec((rows, wd), lambda i: (i, 0)),
                  pl.BlockSpec((rows, wd), lambda i: (i, 0)),
                  pl.BlockSpec(w_a.shape, lambda i: (0, 0)),
                  pl.BlockSpec(w_b.shape, lambda i: (0, 0)),
                  pl.BlockSpec((nb, tl, d), lambda i: (0, i, 0))],
        out_specs=pl.BlockSpec((nb, tl, d), lambda i: (0, i, 0)),
        scratch_shapes=[pltpu.VMEM((d // LANES, rows, LANES), F32)],
        compiler_params=_cparams("parallel"),
        name="even_outproj",
    )(ya, yb, w_a, w_b, x)


def _block_diag(blocks):
    n, r, c = blocks.shape
    tiled = jnp.tile(blocks.reshape(n * r, c), (1, n))
    row_blk = jnp.arange(n * r, dtype=jnp.int32)[:, None] // r
    col_blk = jnp.arange(n * c, dtype=jnp.int32)[None, :] // c
    return jnp.where(row_blk == col_blk, tiled, jnp.zeros((), blocks.dtype))


def _even_mixer(x, norm_g, w_in, lam_re, lam_im, log_dt, b_re, b_im, c_re, c_im, d, glu_w, glu_b,
                conv_w, conv_b, wa, ba, wi, bi, lam, w_out):
    nb, seq, dm = x.shape
    g, p = lam_re.shape
    hch = b_re.shape[-1]
    width = g * hch
    n_half = 2
    gh = g // n_half

    proj = _even_inproj(x, norm_g[None, :], w_in.astype(BF16))

    lb_re, lb_im, bb_re, bb_im = _s5_discretize(lam_re, lam_im, log_dt, b_re, b_im)
    bb_re = bb_re.reshape(n_half, gh, hch, p)
    bb_im = bb_im.reshape(n_half, gh, hch, p)
    bm = jnp.stack([jnp.concatenate([_block_diag(bb_re[j]), _block_diag(bb_im[j])], axis=1)
                    for j in range(n_half)]).astype(BF16)
    c_re_t = jnp.swapaxes(c_re, 1, 2).reshape(n_half, gh, p, hch)
    c_im_t = jnp.swapaxes(c_im, 1, 2).reshape(n_half, gh, p, hch)
    cm = jnp.stack([jnp.concatenate([_block_diag(c_re_t[j]), -_block_diag(c_im_t[j])], axis=0)
                    for j in range(n_half)]).astype(BF16)
    lre = jnp.broadcast_to(lb_re.reshape(1, g * p), (nb, g * p))
    lim = jnp.broadcast_to(lb_im.reshape(1, g * p), (nb, g * p))
    ya = _s5_mixer(proj, nb, bm, lre, lim, cm, d.reshape(1, width), glu_w.astype(BF16),
                   glu_b[None, :])

    lw = conv_w.shape[1]
    nblk = 2
    per = LRU_BLOCKS // nblk
    wa_bd = jnp.stack([_block_diag(wa[j * per:(j + 1) * per]) for j in range(nblk)]).astype(BF16)
    wi_bd = jnp.stack([_block_diag(wi[j * per:(j + 1) * per]) for j in range(nblk)]).astype(BF16)
    yb = _lru_mixer(proj, nb, conv_w, conv_b[None, :], wa_bd, ba[None, :], wi_bd, bi[None, :],
                    lam[None, :])

    w_out_b = w_out.astype(BF16)
    return _even_outproj(ya, yb, w_out_b[:width], w_out_b[width:], x)


def _rope_table_kernel(pos_ref, inv_ref, cc_ref, ss_ref):
    tr = pos_ref.shape[0]
    half = ROPE_DIM // 2
    per_row = LANES // half
    ang = pos_ref[...].astype(F32) * inv_ref[...]
    cos = jnp.cos(ang)
    sin = jnp.sin(ang)
    lane = lax.broadcasted_iota(jnp.int32, ang.shape, 1)
    sign = jnp.where((lane & (ROPE_DIM - 1)) < half, -1.0, 1.0)

    def spread(a, j):
        g = jnp.where(lane // half == j, a, 0.0)
        out = g
        for k in range(1, per_row):
            out = out + pltpu.roll(g, k * half, 1)
        return out

    for j in range(per_row):
        cc_ref[pl.ds(j, tr, stride=per_row), :] = spread(cos, j)
        ss_ref[pl.ds(j, tr, stride=per_row), :] = spread(sin, j) * sign


def _rope_tables(positions):
    nb, seq = positions.shape
    half = ROPE_DIM // 2
    per_row = LANES // half
    n_tok = nb * seq
    tr = ROW_TILE
    inv_freq = 1.0 / (ROPE_THETA ** (jnp.arange(0, ROPE_DIM, 2, dtype=F32) / ROPE_DIM))
    pos_rep = jnp.repeat(positions.reshape(n_tok // per_row, per_row), half, axis=1)
    inv_rep = jnp.tile(inv_freq, per_row)[None, :]
    return pl.pallas_call(
        _rope_table_kernel,
        out_shape=(jax.ShapeDtypeStruct((n_tok, LANES), F32), jax.ShapeDtypeStruct((n_tok, LANES), F32)),
        grid=(n_tok // per_row // tr,),
        in_specs=[pl.BlockSpec((tr, LANES), lambda i: (i, 0)),
                  pl.BlockSpec((1, LANES), lambda i: (0, 0))],
        out_specs=(pl.BlockSpec((tr * per_row, LANES), lambda i: (i, 0)),
                   pl.BlockSpec((tr * per_row, LANES), lambda i: (i, 0))),
        compiler_params=_cparams("parallel"),
        name="rope_table",
    )(pos_rep, inv_rep)


def _mla_proj_kernel(x_ref, g_ref, win_ref, cqn_ref, ckvn_ref, wq_ref, wkv_ref, qg_ref, kg_ref,
                     cc_ref, ss_ref, qn_ref, qrp_ref, k2_ref, v_ref):
    nh = MLA_HEADS
    h = _rms(x_ref[...], g_ref[...])
    proj = _dot(h.astype(BF16), win_ref[...])
    c_q = _rms(proj[:, :Q_LORA], cqn_ref[...])
    c_kv = _rms(proj[:, Q_LORA:Q_LORA + KV_LORA], ckvn_ref[...])
    o = Q_LORA + KV_LORA
    k_r2 = proj[:, o:o + LANES]
    k_sw2 = proj[:, o + LANES:o + 2 * LANES]
    cc2 = cc_ref[...]
    ss2 = ss_ref[...]
    qg = qg_ref[...]
    kg = kg_ref[...]
    lo = lax.broadcasted_iota(jnp.int32, cc2.shape, 1) < ROPE_DIM
    scale = math.log2(math.e) / math.sqrt(QK_DIM)

    q = _dot(c_q.astype(BF16), wq_ref[...])
    q_a = cc2 * qg[1:2, :]
    q_b = ss2 * qg[2:3, :]
    for j in range(nh // 2):
        ro = nh * NOPE_DIM + j * LANES
        q_rp = q[:, ro:ro + LANES]
        q_swp = q[:, ro + nh * ROPE_DIM:ro + nh * ROPE_DIM + LANES]
        rot = q_rp * q_a + q_swp * q_b
        sq_rp = q_rp * q_rp
        rs_pair = []
        for par in range(2):
            hd = 2 * j + par
            q_n = q[:, hd * NOPE_DIM:(hd + 1) * NOPE_DIM]
            own = lo if par == 0 else jnp.logical_not(lo)
            ms = jnp.sum(q_n * q_n + jnp.where(own, sq_rp, 0.0), axis=-1, keepdims=True) / QK_DIM
            rs = lax.rsqrt(ms + RMS_EPS) * scale
            qn_ref[0, hd] = (q_n * rs * qg[0:1, :]).astype(qn_ref.dtype)
            rs_pair.append(rs)
        qrp_ref[0, j] = (rot * jnp.where(lo, rs_pair[0], rs_pair[1])).astype(qrp_ref.dtype)

    kv = _dot(c_kv.astype(BF16), wkv_ref[...])
    k_rot2 = k_r2 * (cc2 * kg[1:2, :]) + k_sw2 * (ss2 * kg[2:3, :])
    k_r_ss = jnp.sum(jnp.where(lo, k_r2 * k_r2, 0.0), axis=-1, keepdims=True)
    for hd in range(nh):
        ko = hd * (NOPE_DIM + V_DIM)
        k_n = kv[:, ko:ko + NOPE_DIM]
        ms_k = (jnp.sum(k_n * k_n, axis=-1, keepdims=True) + k_r_ss) / QK_DIM
        rs_k = lax.rsqrt(ms_k + RMS_EPS)
        own = lo if hd % 2 == 0 else jnp.logical_not(lo)
        k2_ref[0, hd, :, 0:NOPE_DIM] = (k_n * rs_k * kg[0:1, :]).astype(k2_ref.dtype)
        k2_ref[0, hd, :, NOPE_DIM:] = jnp.where(own, k_rot2 * rs_k, 0.0).astype(k2_ref.dtype)
        v_ref[0, hd] = kv[:, ko + NOPE_DIM:ko + NOPE_DIM + V_DIM].astype(v_ref.dtype)


def _swap_halves(a, axis=-1):
    h = a.shape[axis] // 2
    lo = lax.slice_in_dim(a, 0, h, axis=axis)
    hi = lax.slice_in_dim(a, h, 2 * h, axis=axis)
    return jnp.concatenate([hi, lo], axis=axis)


def _mla_proj(x, norm_g, positions, w_in, cq_norm, ckv_norm, w_uq, w_ukv, q_norm, k_norm):
    nb, seq, dm = x.shape
    nh = MLA_HEADS
    tm = ROW_TILE
    n_tok = nb * seq
    cc2, ss2 = _rope_tables(positions)
    o = Q_LORA + KV_LORA
    w_kr = w_in[:, o:o + ROPE_DIM]
    w_sw = _swap_halves(w_kr)
    w_in_x = jnp.concatenate([w_in[:, :o], w_kr, w_kr, w_sw, w_sw], axis=1).astype(BF16)
    wq = w_uq.reshape(Q_LORA, nh, QK_DIM)
    wq_n = wq[:, :, :NOPE_DIM].reshape(Q_LORA, nh * NOPE_DIM)
    wq_r = wq[:, :, NOPE_DIM:]
    wq_x = jnp.concatenate([wq_n, wq_r.reshape(Q_LORA, nh * ROPE_DIM),
                            _swap_halves(wq_r).reshape(Q_LORA, nh * ROPE_DIM)], axis=1).astype(BF16)

    def gains(gv):
        r = gv[NOPE_DIM:]
        rs = _swap_halves(r)
        return jnp.stack([gv[:NOPE_DIM], jnp.concatenate([r, r]), jnp.concatenate([rs, rs])])

    x2 = x.reshape(n_tok, dm)
    per_b = seq // tm
    full = lambda a: pl.BlockSpec(a.shape, lambda i: (0,) * a.ndim)
    args = (x2, norm_g[None, :], w_in_x, cq_norm[None, :], ckv_norm[None, :], wq_x,
            w_ukv.astype(BF16), gains(q_norm), gains(k_norm), cc2, ss2)
    in_specs = [pl.BlockSpec((tm, dm), lambda i: (i, 0))] + [full(a) for a in args[1:9]] + [
        pl.BlockSpec((tm, LANES), lambda i: (i, 0)), pl.BlockSpec((tm, LANES), lambda i: (i, 0))]

    def head_spec(n_heads, dh):
        return pl.BlockSpec((1, n_heads, tm, dh), lambda i: (i // per_b, 0, i % per_b, 0))

    def head_shape(n_heads, dh):
        return jax.ShapeDtypeStruct((nb, n_heads, seq, dh), BF16)

    return pl.pallas_call(
        _mla_proj_kernel,
        out_shape=(head_shape(nh, NOPE_DIM), head_shape(nh // 2, LANES), head_shape(nh, NOPE_DIM + LANES),
                   head_shape(nh, V_DIM)),
        grid=(n_tok // tm,),
        in_specs=in_specs,
        out_specs=(head_spec(nh, NOPE_DIM), head_spec(nh // 2, LANES), head_spec(nh, NOPE_DIM + LANES),
                   head_spec(nh, V_DIM)),
        compiler_params=_cparams("parallel"),
        name="mla_proj",
    )(*args)


def _attention_kernel(qn_ref, qrp_ref, k2_ref, v_ref, o_ref):
    seq = qn_ref.shape[2]
    t = ATT_TQ
    nt = (((1,), (1,)), ((), ()))
    tri = (lax.broadcasted_iota(jnp.int32, (t, t), 1) <= lax.broadcasted_iota(jnp.int32, (t, t), 0))
    for i in range(seq // t):
        q = jnp.concatenate([qn_ref[0, 0, i * t:(i + 1) * t, :], qrp_ref[0, 0, i * t:(i + 1) * t, :]], axis=-1)
        m = jnp.full((t, 1), NEG_INF, F32)
        l = jnp.zeros((t, 1), F32)
        acc = jnp.zeros((t, V_DIM), F32)
        for j in range(i + 1):
            s = lax.dot_general(q, k2_ref[0, 0, j * t:(j + 1) * t, :], nt, preferred_element_type=F32)
            if j == i:
                s = jnp.where(tri, s, NEG_INF)
            m_new = jnp.maximum(m, jnp.max(s, axis=-1, keepdims=True))
            alpha = jnp.exp2(m - m_new)
            p = jnp.exp2(s - m_new)
            l = alpha * l + jnp.sum(p, axis=-1, keepdims=True)
            acc = alpha * acc + _dot(p.astype(BF16), v_ref[0, 0, j * t:(j + 1) * t, :])
            m = m_new
        o_ref[0, i * t:(i + 1) * t, :] = (acc / l).astype(o_ref.dtype)


def _attention(qn, qrp, k2, v):
    nb, nh, seq, _ = qn.shape

    def spec(dh, per=1):
        return pl.BlockSpec((1, 1, seq, dh), lambda b, h: (b, h // per, 0, 0))

    return pl.pallas_call(
        _attention_kernel,
        out_shape=jax.ShapeDtypeStruct((nb, seq, nh * V_DIM), BF16),
        grid=(nb, nh),
        in_specs=[spec(NOPE_DIM), spec(LANES, 2), spec(NOPE_DIM + LANES), spec(V_DIM)],
        out_specs=pl.BlockSpec((1, seq, V_DIM), lambda b, h: (b, 0, h)),
        compiler_params=_cparams("parallel", "parallel"),
        name="mla_attention",
    )(qn, qrp, k2, v)


def _proj_residual_kernel(a_ref, w_ref, x_ref, o_ref):
    o_ref[...] = x_ref[...] + _dot(a_ref[...], w_ref[...])


def _proj_residual(a, w, x2):
    n_tok, dm = x2.shape
    tm = ROW_TILE
    return pl.pallas_call(
        _proj_residual_kernel,
        out_shape=jax.ShapeDtypeStruct(x2.shape, x2.dtype),
        grid=(n_tok // tm,),
        in_specs=[pl.BlockSpec((tm, a.shape[1]), lambda i: (i, 0)),
                  pl.BlockSpec(w.shape, lambda i: (0, 0)),
                  pl.BlockSpec((tm, dm), lambda i: (i, 0))],
        out_specs=pl.BlockSpec((tm, dm), lambda i: (i, 0)),
        compiler_params=_cparams("parallel"),
        name="proj_residual",
    )(a, w, x2)


def _mla_mixer(x, norm_g, positions, w_in, cq_norm, ckv_norm, w_uq, w_ukv, q_norm, k_norm, w_o):
    nb, seq, dm = x.shape
    qn, qrp, k2, v = _mla_proj(x, norm_g, positions, w_in, cq_norm, ckv_norm, w_uq, w_ukv, q_norm, k_norm)
    o = _attention(qn, qrp, k2, v)
    out = _proj_residual(o.reshape(nb * seq, -1), w_o.astype(BF16), x.reshape(nb * seq, dm))
    return out.reshape(nb, seq, dm)


def _router_kernel(x_ref, g_ref, w_ref, b_ref, tri_ref, meta_ref, gate_ref, cnt_ref, base_ref):
    tm, dm = x_ref.shape
    ne = N_EXPERTS

    @pl.when(pl.program_id(0) == 0)
    def _():
        base_ref[...] = jnp.zeros_like(base_ref)

    hn = _rms(x_ref[...], g_ref[...])
    logits = _dot(hn.astype(BF16), w_ref[...]) + b_ref[...]
    lane = lax.broadcasted_iota(jnp.int32, logits.shape, 1)
    big = jnp.int32(1 << 20)

    def first_argmax(v):
        m = jnp.max(v, axis=-1, keepdims=True)
        idx = jnp.min(jnp.where(v == m, lane, big), axis=-1, keepdims=True)
        return m, idx

    is_g = (lane >= ne) & (lane < ne + N_GROUPS)
    gl = jnp.where(is_g, logits, NEG_INF)
    g_max, g_idx = first_argmax(gl)
    g_sel = g_idx - ne
    p_g = 1.0 / jnp.sum(jnp.where(is_g, jnp.exp(gl - g_max), 0.0), axis=-1, keepdims=True)
    in_grp = (lane >= g_sel * EXPERTS_PER_GROUP) & (lane < (g_sel + 1) * EXPERTS_PER_GROUP)
    el = jnp.where(in_grp, logits, NEG_INF)
    v0, e0 = first_argmax(el)
    v1, e1 = first_argmax(jnp.where(lane == e0, NEG_INF, el))
    t = jnp.exp(v1 - v0)
    gate0 = p_g / (1.0 + t)
    gate1 = p_g * t / (1.0 + t)

    oh0 = (lane == e0).astype(F32)
    oh1 = (lane == e1).astype(F32)
    both = oh0 + oh1
    before = _dot(tri_ref[...], both.astype(BF16)) + base_ref[0:1, :]
    rank0 = jnp.sum(oh0 * before, axis=-1, keepdims=True)
    rank1 = jnp.sum(oh1 * before, axis=-1, keepdims=True)
    base_ref[0:1, :] = base_ref[0:1, :] + jnp.sum(both, axis=0, keepdims=True)
    cnt_ref[...] = jnp.broadcast_to(base_ref[0:1, :], cnt_ref.shape).astype(jnp.int32)

    meta = jnp.where(lane == 0, e0, jnp.where(lane == 1, e1, jnp.where(
        lane == 2, rank0.astype(jnp.int32), jnp.where(lane == 3, rank1.astype(jnp.int32), 0))))
    meta_ref[...] = meta
    gate_ref[...] = jnp.where(lane == 0, gate0, jnp.where(lane == 1, gate1, 0.0))


def _router(x2, norm_g, w_group, b_group, w_expert, b_expert):
    n_tok, dm = x2.shape
    tm = ROW_TILE
    pad = LANES - N_EXPERTS - N_GROUPS
    w = jnp.concatenate([w_expert, w_group, jnp.zeros((dm, pad), F32)], axis=1).astype(BF16)
    b = jnp.concatenate([b_expert, b_group, jnp.zeros((pad,), F32)])[None, :]
    idx = jnp.arange(tm, dtype=jnp.int32)
    tri = (idx[None, :] < idx[:, None]).astype(BF16)
    return pl.pallas_call(
        _router_kernel,
        out_shape=(jax.ShapeDtypeStruct((n_tok, LANES), jnp.int32),
                   jax.ShapeDtypeStruct((n_tok, LANES), F32),
                   jax.ShapeDtypeStruct((SUBLANES, LANES), jnp.int32)),
        grid=(n_tok // tm,),
        in_specs=[pl.BlockSpec((tm, dm), lambda i: (i, 0)),
                  pl.BlockSpec((1, dm), lambda i: (0, 0)),
                  pl.BlockSpec((dm, LANES), lambda i: (0, 0)),
                  pl.BlockSpec((1, LANES), lambda i: (0, 0)),
                  pl.BlockSpec((tm, tm), lambda i: (0, 0))],
        out_specs=(pl.BlockSpec((tm, LANES), lambda i: (i, 0)),
                   pl.BlockSpec((tm, LANES), lambda i: (i, 0)),
                   pl.BlockSpec((SUBLANES, LANES), lambda i: (0, 0))),
        scratch_shapes=[pltpu.VMEM((SUBLANES, LANES), F32)],
        compiler_params=_cparams("arbitrary"),
        name="moe_router",
    )(x2, norm_g[None, :], w, b, tri)


def _dispatch_kernel(dest_ref, dest_prev_ref, x_ref, g_ref, xd_ref, hn_ref, zero_ref, sem, zsem):
    tm = x_ref.shape[0]
    i = pl.program_id(0)
    n_steps = pl.num_programs(0)
    slot = i % 2
    n_assign = n_steps * tm * TOP_K

    def tail_copy():
        return pltpu.make_async_copy(zero_ref, xd_ref.at[pl.ds(n_assign * ROW_CHUNKS, zero_ref.shape[0])], zsem)

    @pl.when(i == 0)
    def _():
        zero_ref[...] = jnp.zeros_like(zero_ref)
        tail_copy().start()

    hn = _rms(x_ref[...], g_ref[...])
    hbuf = hn_ref.at[slot]
    for c in range(ROW_CHUNKS):
        hbuf[pl.ds(c, tm, stride=ROW_CHUNKS), :] = hn[:, c * LANES:(c + 1) * LANES]

    def row_copy(d_ref, sl, t, k):
        src = pl.multiple_of(t * ROW_CHUNKS, ROW_CHUNKS)
        dst = pl.multiple_of(d_ref[0, 0, 2 * t + k], ROW_CHUNKS)
        return pltpu.make_async_copy(hn_ref.at[sl, pl.ds(src, ROW_CHUNKS)], xd_ref.at[pl.ds(dst, ROW_CHUNKS)],
                                     sem.at[sl])

    def issue(j, c):
        for u in range(DMA_UNROLL):
            for k in range(TOP_K):
                row_copy(dest_ref, slot, j * DMA_UNROLL + u, k).start(priority=k)
        return c

    lax.fori_loop(0, tm // DMA_UNROLL, issue, 0)

    def drain_all(d_ref, sl):
        def drain(j, c):
            for u in range(DMA_UNROLL):
                for k in range(TOP_K):
                    row_copy(d_ref, sl, j * DMA_UNROLL + u, k).wait()
            return c
        lax.fori_loop(0, tm // DMA_UNROLL, drain, 0)

    @pl.when(i >= 1)
    def _():
        drain_all(dest_prev_ref, 1 - slot)

    @pl.when(i == n_steps - 1)
    def _():
        drain_all(dest_ref, slot)

    @pl.when(i == 0)
    def _():
        tail_copy().wait()


def _dispatch(x2, norm_g, dest_off):
    tm = GATHER_TM
    n_tok, dm = x2.shape
    dest3 = dest_off.reshape(n_tok // tm, 1, 2 * tm)
    n_rows = n_tok * TOP_K + MOE_BM
    return pl.pallas_call(
        _dispatch_kernel,
        out_shape=jax.ShapeDtypeStruct((n_rows * ROW_CHUNKS, LANES), F32),
        grid=(n_tok // tm,),
        in_specs=[pl.BlockSpec((1, 1, 2 * tm), lambda i: (i, 0, 0), memory_space=pltpu.SMEM),
                  pl.BlockSpec((1, 1, 2 * tm), lambda i: (jnp.maximum(i - 1, 0), 0, 0), memory_space=pltpu.SMEM),
                  pl.BlockSpec((tm, dm), lambda i: (i, 0)),
                  pl.BlockSpec((1, dm), lambda i: (0, 0))],
        out_specs=pl.BlockSpec(memory_space=pl.ANY),
        scratch_shapes=[pltpu.VMEM((2, tm * ROW_CHUNKS, LANES), F32),
                        pltpu.VMEM((MOE_BM * ROW_CHUNKS, LANES), F32),
                        pltpu.SemaphoreType.DMA((2,)), pltpu.SemaphoreType.DMA],
        compiler_params=_cparams("arbitrary"),
        name="moe_dispatch",
    )(dest3, dest3, x2, norm_g[None, :])


def _rows_from_tiles(ref, n, base=0):
    return jnp.concatenate([ref[pl.ds(base + c, n, stride=ROW_CHUNKS), :] for c in range(ROW_CHUNKS)], axis=-1)


def _expert_kernel(be_ref, bs_ref, first_ref, nxt_ref, nu_ref, xd_ref, w1_ref, w3_ref, w2_ref, yd_ref,
                   xbuf, ybuf, w1buf, w3buf, w2buf, w13_s, w2_s, sem_in, sem_out, sem_w, *, layer):
    nu = nu_ref[0]
    n_rows = xbuf.shape[1]
    bm = n_rows // ROW_CHUNKS
    de = w1buf.shape[2]

    def in_copy(blk, sl):
        src = pl.multiple_of(bs_ref[blk] * ROW_CHUNKS, ROW_CHUNKS)
        return pltpu.make_async_copy(xd_ref.at[pl.ds(src, n_rows)], xbuf.at[sl], sem_in.at[sl])

    def out_copy(blk, sl):
        dst = pl.multiple_of(bs_ref[blk] * ROW_CHUNKS, ROW_CHUNKS)
        return pltpu.make_async_copy(ybuf.at[sl], yd_ref.at[pl.ds(dst, n_rows)], sem_out.at[sl])

    def w_copies(e, ws):
        return (pltpu.make_async_copy(w1_ref.at[layer, e], w1buf.at[ws], sem_w.at[ws]),
                pltpu.make_async_copy(w3_ref.at[layer, e], w3buf.at[ws], sem_w.at[ws]),
                pltpu.make_async_copy(w2_ref.at[layer, e], w2buf.at[ws], sem_w.at[ws]))

    in_copy(0, 0).start()
    for cp in w_copies(be_ref[0], 1):
        cp.start(priority=1)

    def block(b, ws):
        slot = b % 2

        @pl.when(b + 1 < nu)
        def _():
            in_copy(b + 1, 1 - slot).start()

        first = first_ref[b] == 1
        ws = jnp.where(first, 1 - ws, ws)

        @pl.when(first)
        def _():
            for cp in w_copies(be_ref[b], ws):
                cp.wait()
            w13_s[:, :de] = w1buf[ws].astype(BF16)
            w13_s[:, de:] = w3buf[ws].astype(BF16)
            w2_s[...] = w2buf[ws].astype(BF16)

            @pl.when(nxt_ref[b] >= 0)
            def _():
                for cp in w_copies(nxt_ref[b], 1 - ws):
                    cp.start(priority=1)

        in_copy(b, slot).wait()
        x = _rows_from_tiles(xbuf.at[slot], bm).astype(BF16)
        h = _dot(x, w13_s[...])
        a = jax.nn.silu(h[:, :de]) * h[:, de:]
        y = _dot(a.astype(BF16), w2_s[...])
        yb = ybuf.at[slot]
        for c in range(ROW_CHUNKS):
            yb[pl.ds(c, bm, stride=ROW_CHUNKS), :] = y[:, c * LANES:(c + 1) * LANES]

        @pl.when(b >= 1)
        def _():
            out_copy(b - 1, 1 - slot).wait()
        out_copy(b, slot).start()
        return ws

    lax.fori_loop(0, nu, block, jnp.int32(0))

    last = nu - 1
    out_copy(last, last % 2).wait()
    tail = yd_ref.shape[0] - n_rows
    fill_slot = 1 - last % 2
    ybuf[fill_slot] = jnp.zeros((n_rows, LANES), F32)
    fill = pltpu.make_async_copy(ybuf.at[fill_slot], yd_ref.at[pl.ds(tail, n_rows)], sem_out.at[fill_slot])
    fill.start()
    fill.wait()


def _experts(xd, blk_expert, blk_start, blk_first, blk_next, n_used, w1, w3, w2, layer):
    bm = MOE_BM
    _, _, dm, de = w1.shape
    any_spec = pl.BlockSpec(memory_space=pl.ANY)
    grid_spec = pltpu.PrefetchScalarGridSpec(
        num_scalar_prefetch=5,
        grid=(1,),
        in_specs=[any_spec, any_spec, any_spec, any_spec],
        out_specs=any_spec,
        scratch_shapes=[pltpu.VMEM((2, bm * ROW_CHUNKS, LANES), F32),
                        pltpu.VMEM((2, bm * ROW_CHUNKS, LANES), F32),
                        pltpu.VMEM((2, dm, de), F32),
                        pltpu.VMEM((2, dm, de), F32),
                        pltpu.VMEM((2, de, dm), F32),
                        pltpu.VMEM((dm, 2 * de), BF16),
                        pltpu.VMEM((de, dm), BF16),
                        pltpu.SemaphoreType.DMA((2,)),
                        pltpu.SemaphoreType.DMA((2,)),
                        pltpu.SemaphoreType.DMA((2,))],
    )
    return pl.pallas_call(
        functools.partial(_expert_kernel, layer=layer),
        out_shape=jax.ShapeDtypeStruct(xd.shape, F32),
        grid_spec=grid_spec,
        compiler_params=_cparams("arbitrary"),
        name="moe_experts",
    )(blk_expert, blk_start, blk_first, blk_next, n_used, xd, w1, w3, w2)


def _combine_kernel(dest_ref, dest_next_ref, yd_ref, gate_ref, x_ref, o_ref, buf_ref, sem):
    tm, dm = x_ref.shape
    i = pl.program_id(0)
    n_steps = pl.num_programs(0)
    slot = i % 2
    k_rows = tm * ROW_CHUNKS

    def row_copy(d_ref, sl, t, k):
        src = pl.multiple_of(d_ref[0, 0, 2 * t + k], ROW_CHUNKS)
        dst = pl.multiple_of(k * k_rows + t * ROW_CHUNKS, ROW_CHUNKS)
        return pltpu.make_async_copy(yd_ref.at[pl.ds(src, ROW_CHUNKS)],
                                     buf_ref.at[sl, pl.ds(dst, ROW_CHUNKS)], sem.at[sl])

    def issue_all(d_ref, sl):
        def issue(j, c):
            for u in range(DMA_UNROLL):
                for k in range(TOP_K):
                    row_copy(d_ref, sl, j * DMA_UNROLL + u, k).start(priority=k)
            return c
        lax.fori_loop(0, tm // DMA_UNROLL, issue, 0)

    @pl.when(i == 0)
    def _():
        issue_all(dest_ref, slot)

    @pl.when(i + 1 < n_steps)
    def _():
        issue_all(dest_next_ref, 1 - slot)

    def drain(j, c):
        for u in range(DMA_UNROLL):
            for k in range(TOP_K):
                row_copy(dest_ref, slot, j * DMA_UNROLL + u, k).wait()
        return c

    lax.fori_loop(0, tm // DMA_UNROLL, drain, 0)

    g0 = gate_ref[:, 0:1]
    g1 = gate_ref[:, 1:2]
    buf = buf_ref.at[slot]
    for c in range(ROW_CHUNKS):
        y0 = buf[pl.ds(c, tm, stride=ROW_CHUNKS), :]
        y1 = buf[pl.ds(k_rows + c, tm, stride=ROW_CHUNKS), :]
        o_ref[:, c * LANES:(c + 1) * LANES] = x_ref[:, c * LANES:(c + 1) * LANES] + (g0 * y0 + g1 * y1)


def _combine(yd, dest_off, gates, x2):
    n_tok, dm = x2.shape
    tm = GATHER_TM
    n_steps = n_tok // tm
    dest3 = dest_off.reshape(n_steps, 1, 2 * tm)
    return pl.pallas_call(
        _combine_kernel,
        out_shape=jax.ShapeDtypeStruct(x2.shape, x2.dtype),
        grid=(n_steps,),
        in_specs=[pl.BlockSpec((1, 1, 2 * tm), lambda i: (i, 0, 0), memory_space=pltpu.SMEM),
                  pl.BlockSpec((1, 1, 2 * tm), lambda i: (jnp.minimum(i + 1, n_steps - 1), 0, 0),
                               memory_space=pltpu.SMEM),
                  pl.BlockSpec(memory_space=pl.ANY),
                  pl.BlockSpec((tm, LANES), lambda i: (i, 0)),
                  pl.BlockSpec((tm, dm), lambda i: (i, 0))],
        out_specs=pl.BlockSpec((tm, dm), lambda i: (i, 0)),
        scratch_shapes=[pltpu.VMEM((2, TOP_K * tm * ROW_CHUNKS, LANES), F32),
                        pltpu.SemaphoreType.DMA((2,))],
        compiler_params=_cparams("arbitrary"),
        name="moe_combine",
    )(dest3, dest3, yd, gates, x2)


def _moe(x, norm_g, w_group, b_group, w_expert, b_expert, w1, w3, w2, layer):
    nb, seq, dm = x.shape
    n_tok = nb * seq
    n_assign = n_tok * TOP_K
    x2 = x.reshape(n_tok, dm)
    meta, gates, counts = _router(x2, norm_g, w_group, b_group, w_expert, b_expert)
    bm = MOE_BM
    n_blocks = n_assign // bm + N_EXPERTS
    counts = counts[0, :N_EXPERTS]
    starts = jnp.cumsum(counts) - counts
    eid = meta[:, 0:TOP_K]
    e_iota = jnp.arange(N_EXPERTS, dtype=jnp.int32)
    seg_start = jnp.sum(jnp.where(eid[:, :, None] == e_iota, starts, 0), axis=-1)
    dest_off = ((seg_start + meta[:, TOP_K:2 * TOP_K]) * ROW_CHUNKS).astype(jnp.int32).reshape(-1)
    nblk = (counts + bm - 1) // bm
    blk_end = jnp.cumsum(nblk)
    b_iota = jnp.arange(n_blocks, dtype=jnp.int32)
    blk_expert = jnp.minimum(jnp.sum((b_iota[:, None] >= blk_end[None, :]).astype(jnp.int32), axis=1),
                             N_EXPERTS - 1)
    j = b_iota - (blk_end - nblk)[blk_expert]
    n_used = blk_end[-1].astype(jnp.int32).reshape(1)
    blk_start = jnp.where(b_iota < n_used[0], starts[blk_expert] + j * bm, 0).astype(jnp.int32)
    blk_first = (j == 0).astype(jnp.int32)
    cand = jnp.where(counts > 0, e_iota, N_EXPERTS)
    later = lax.cummin(cand[::-1])[::-1]
    next_e = jnp.concatenate([later[1:], jnp.full((1,), N_EXPERTS, jnp.int32)])
    blk_next = jnp.where(next_e[blk_expert] < N_EXPERTS, next_e[blk_expert], -1).astype(jnp.int32)

    xd = _dispatch(x2, norm_g, dest_off)
    yd = _experts(xd, blk_expert, blk_start, blk_first, blk_next, n_used, w1, w3, w2, layer)
    out = _combine(yd, dest_off, gates, x2)
    return out.reshape(nb, seq, dm)


def kernel(x, positions, ev_norm, ev_w_in, s5_lambda_re, s5_lambda_im, s5_log_dt, s5_b_re, s5_b_im,
           s5_c_re, s5_c_im, s5_d, s5_glu_w, s5_glu_b, lru_conv_w, lru_conv_b, lru_wa, lru_ba, lru_wi,
           lru_bi, lru_lambda, ev_w_out, od_norm, mla_w_in, mla_cq_norm, mla_ckv_norm, mla_w_uq,
           mla_w_ukv, mla_q_norm, mla_k_norm, mla_w_o, ffn_norm, moe_w_group, moe_b_group,
           moe_w_expert, moe_b_expert, moe_w1, moe_w3, moe_w2):
    depth = ffn_norm.shape[0]
    for layer in range(depth):
        j = layer // 2
        if layer % 2 == 0:
            x = _even_mixer(x, ev_norm[j], ev_w_in[j], s5_lambda_re[j], s5_lambda_im[j], s5_log_dt[j],
                            s5_b_re[j], s5_b_im[j], s5_c_re[j], s5_c_im[j], s5_d[j], s5_glu_w[j],
                            s5_glu_b[j], lru_conv_w[j], lru_conv_b[j], lru_wa[j], lru_ba[j], lru_wi[j],
                            lru_bi[j], lru_lambda[j], ev_w_out[j])
        else:
            x = _mla_mixer(x, od_norm[j], positions, mla_w_in[j], mla_cq_norm[j], mla_ckv_norm[j],
                           mla_w_uq[j], mla_w_ukv[j], mla_q_norm[j], mla_k_norm[j], mla_w_o[j])
        x = _moe(x, ffn_norm[layer], moe_w_group[layer], moe_b_group[layer], moe_w_expert[layer],
                 moe_b_expert[layer], moe_w1, moe_w3, moe_w2, layer)
    return x
```

```python
import functools
import math

import jax
import jax.numpy as jnp
from jax import lax
from jax.experimental import pallas as pl
from jax.experimental.pallas import tpu as pltpu

F32 = jnp.float32
BF16 = jnp.bfloat16

S5_GROUP_CH = 16
S5_STATE = 64
LRU_BLOCKS = 8
CONV_WIDTH = 4
RG_C = 8.0
MLA_HEADS = 8
Q_LORA = 384
KV_LORA = 256
NOPE_DIM = 128
ROPE_DIM = 64
V_DIM = 128
QK_DIM = NOPE_DIM + ROPE_DIM
ROPE_THETA = 10000.0
N_GROUPS = 8
EXPERTS_PER_GROUP = 8
N_EXPERTS = N_GROUPS * EXPERTS_PER_GROUP
TOP_K = 2
RMS_EPS = 1e-6
NEG_INF = -1e30

SUBLANES = 8
LANES = 128
ROW_CHUNKS = 8
VMEM_LIMIT = 48 * 1024 * 1024

EVEN_TL = 128
ROW_TILE = 512
MOE_BM = 512
ATT_TQ = 512
GATHER_TM = 512
COMBINE_TM = 256
DMA_UNROLL = 8


def _cparams(*sem):
    return pltpu.CompilerParams(dimension_semantics=tuple(sem), vmem_limit_bytes=VMEM_LIMIT)


def _rms(x, g):
    ms = jnp.mean(x * x, axis=-1, keepdims=True)
    return x * lax.rsqrt(ms + RMS_EPS) * g


def _dot(a, b):
    return jnp.dot(a, b, preferred_element_type=F32)


def _s5_discretize_kernel(lre_ref, lim_ref, ldt_ref, bre_ref, bim_ref,
                          lbre_ref, lbim_ref, bbre_ref, bbim_ref):
    lr = jnp.minimum(lre_ref[...], -1e-4)
    li = lim_ref[...]
    dt = jnp.exp(ldt_ref[...])
    mag = jnp.exp(lr * dt)
    lb_re = mag * jnp.cos(li * dt)
    lb_im = mag * jnp.sin(li * dt)
    den = lr * lr + li * li
    num_re = lb_re - 1.0
    f_re = (num_re * lr + lb_im * li) / den
    f_im = (lb_im * lr - num_re * li) / den
    lbre_ref[...] = lb_re
    lbim_ref[...] = lb_im
    br = bre_ref[...]
    bi = bim_ref[...]
    bbre_ref[...] = f_re[:, None, :] * br - f_im[:, None, :] * bi
    bbim_ref[...] = f_re[:, None, :] * bi + f_im[:, None, :] * br


def _s5_discretize(lam_re, lam_im, log_dt, b_re, b_im):
    g, p = lam_re.shape
    h = b_re.shape[-1]
    b_re_t = jnp.swapaxes(b_re, 1, 2)
    b_im_t = jnp.swapaxes(b_im, 1, 2)
    return pl.pallas_call(
        _s5_discretize_kernel,
        out_shape=(jax.ShapeDtypeStruct((g, p), F32), jax.ShapeDtypeStruct((g, p), F32),
                   jax.ShapeDtypeStruct((g, h, p), F32), jax.ShapeDtypeStruct((g, h, p), F32)),
        name="s5_discretize",
    )(lam_re, lam_im, log_dt[:, None], b_re_t, b_im_t)


def _even_inproj_kernel(x_ref, g_ref, w_ref, o_ref, h_ref):
    nb, tl, _ = x_ref.shape
    nc = h_ref.shape[0]
    g = g_ref[...]
    for b in range(nb):
        h = _rms(x_ref[b], g)
        for c in range(nc):
            h_ref[c, pl.ds(b, tl, stride=nb), :] = h[:, c * LANES:(c + 1) * LANES]
    h_all = jnp.concatenate([h_ref[c] for c in range(nc)], axis=-1)
    o_ref[...] = _dot(h_all.astype(BF16), w_ref[...]).astype(o_ref.dtype)


def _even_inproj(x, g, w):
    nb, seq, d = x.shape
    n_out = w.shape[1]
    tl = EVEN_TL
    return pl.pallas_call(
        _even_inproj_kernel,
        out_shape=jax.ShapeDtypeStruct((seq * nb, n_out), BF16),
        grid=(seq // tl,),
        in_specs=[pl.BlockSpec((nb, tl, d), lambda i: (0, i, 0)),
                  pl.BlockSpec((1, d), lambda i: (0, 0)),
                  pl.BlockSpec((d, n_out), lambda i: (0, 0))],
        out_specs=pl.BlockSpec((tl * nb, n_out), lambda i: (i, 0)),
        scratch_shapes=[pltpu.VMEM((d // LANES, tl * nb, LANES), F32)],
        compiler_params=_cparams("parallel"),
        name="even_inproj",
    )(x, g, w)


def _s5_kernel(u_ref, bm_ref, lre_ref, lim_ref, cm_ref, d_ref, gw_ref, gb_ref, o_ref,
               bu_ref, sb_ref, st_ref, *, nb):
    rows, width = u_ref.shape
    n_half = bm_ref.shape[0]
    kin = width // n_half
    ncol = bm_ref.shape[2]
    nre = ncol // 2
    chunk = 512
    steps = rows // nb

    @pl.when(pl.program_id(0) == 0)
    def _():
        st_ref[...] = jnp.zeros_like(st_ref)

    u = u_ref[...]
    for j in range(n_half):
        bu_ref[:, j * ncol:(j + 1) * ncol] = _dot(u[:, j * kin:(j + 1) * kin], bm_ref[j])

    for j in range(n_half):
        for c in range(nre // chunk):
            cr = j * ncol + c * chunk
            ci = cr + nre
            lc = j * nre + c * chunk
            lr = lre_ref[:, lc:lc + chunk]
            li = lim_ref[:, lc:lc + chunk]

            def body(i, carry, cr=cr, ci=ci, lr=lr, li=li):
                sr, si = carry
                r0 = pl.multiple_of(i * (2 * nb), 2 * nb)
                sr1 = lr * sr - li * si + bu_ref[pl.ds(r0, nb), cr:cr + chunk]
                si1 = lr * si + li * sr + bu_ref[pl.ds(r0, nb), ci:ci + chunk]
                sr2 = lr * sr1 - li * si1 + bu_ref[pl.ds(r0 + nb, nb), cr:cr + chunk]
                si2 = lr * si1 + li * sr1 + bu_ref[pl.ds(r0 + nb, nb), ci:ci + chunk]
                sb_ref[pl.ds(r0, 2 * nb), cr:cr + chunk] = jnp.concatenate([sr1, sr2], 0).astype(BF16)
                sb_ref[pl.ds(r0, 2 * nb), ci:ci + chunk] = jnp.concatenate([si1, si2], 0).astype(BF16)
                return sr2, si2

            sr, si = lax.fori_loop(0, steps // 2, body,
                                   (st_ref[:, cr:cr + chunk], st_ref[:, ci:ci + chunk]))
            st_ref[:, cr:cr + chunk] = sr
            st_ref[:, ci:ci + chunk] = si

    ys = [_dot(sb_ref[:, j * ncol:(j + 1) * ncol], cm_ref[j]) for j in range(n_half)]
    y = jnp.concatenate(ys, axis=-1) + d_ref[...] * u.astype(F32)
    y = jax.nn.gelu(y)
    z = _dot(y.astype(BF16), gw_ref[...]) + gb_ref[...]
    o_ref[...] = (y * jax.nn.sigmoid(z)).astype(o_ref.dtype)


def _s5_mixer(proj, nb, bm, lre, lim, cm, d, glu_w, glu_b):
    rows_total = proj.shape[0]
    width = d.shape[1]
    rows = EVEN_TL * nb
    n_half, _, ncol = bm.shape
    kern = functools.partial(_s5_kernel, nb=nb)
    return pl.pallas_call(
        kern,
        out_shape=jax.ShapeDtypeStruct((rows_total, width), BF16),
        grid=(rows_total // rows,),
        in_specs=[pl.BlockSpec((rows, width), lambda i: (i, 0)),
                  pl.BlockSpec(bm.shape, lambda i: (0, 0, 0)),
                  pl.BlockSpec(lre.shape, lambda i: (0, 0)),
                  pl.BlockSpec(lim.shape, lambda i: (0, 0)),
                  pl.BlockSpec(cm.shape, lambda i: (0, 0, 0)),
                  pl.BlockSpec((1, width), lambda i: (0, 0)),
                  pl.BlockSpec(glu_w.shape, lambda i: (0, 0)),
                  pl.BlockSpec((1, width), lambda i: (0, 0))],
        out_specs=pl.BlockSpec((rows, width), lambda i: (i, 0)),
        scratch_shapes=[pltpu.VMEM((rows, n_half * ncol), F32),
                        pltpu.VMEM((rows, n_half * ncol), BF16),
                        pltpu.VMEM((nb, n_half * ncol), F32)],
        compiler_params=_cparams("arbitrary"),
        name="s5_mixer",
    )(proj, bm, lre, lim, cm, d, glu_w, glu_b)


def _lru_kernel(x_ref, gate_ref, cw_ref, cb_ref, wa_ref, ba_ref, wi_ref, bi_ref, lam_ref, o_ref,
                xp_ref, a_ref, b_ref, h_ref, *, nb):
    rows, width = x_ref.shape
    halo = (CONV_WIDTH - 1) * nb
    steps = rows // nb

    @pl.when(pl.program_id(0) == 0)
    def _():
        xp_ref[0:halo, :] = jnp.zeros((halo, width), F32)
        h_ref[...] = jnp.zeros_like(h_ref)

    xp_ref[halo:halo + rows, :] = x_ref[...].astype(F32)
    xf = cb_ref[...] + cw_ref[0:1, :] * xp_ref[0:rows, :]
    for k in range(1, CONV_WIDTH):
        xf = xf + cw_ref[k:k + 1, :] * xp_ref[k * nb:k * nb + rows, :]
    xp_ref[0:halo, :] = xp_ref[rows:rows + halo, :]

    xb = xf.astype(BF16)
    nblk = wa_ref.shape[0]
    kb = width // nblk
    ga = jnp.concatenate([_dot(xb[:, j * kb:(j + 1) * kb], wa_ref[j]) for j in range(nblk)], -1)
    gi = jnp.concatenate([_dot(xb[:, j * kb:(j + 1) * kb], wi_ref[j]) for j in range(nblk)], -1)
    r = jax.nn.sigmoid(ga + ba_ref[...])
    ig = jax.nn.sigmoid(gi + bi_ref[...])
    log_a = RG_C * r * jax.nn.log_sigmoid(lam_ref[...])
    a_ref[...] = jnp.exp(log_a)
    th = jnp.tanh(log_a)
    b_ref[...] = jnp.sqrt(-2.0 * th / (1.0 - th)) * (ig * xf)

    def body(t, h):
        r0 = pl.multiple_of(t * nb, nb)
        h = a_ref[pl.ds(r0, nb), :] * h + b_ref[pl.ds(r0, nb), :]
        b_ref[pl.ds(r0, nb), :] = h
        return h

    h_ref[...] = lax.fori_loop(0, steps, body, h_ref[...])
    o_ref[...] = (b_ref[...] * jax.nn.gelu(gate_ref[...].astype(F32))).astype(o_ref.dtype)


def _lru_mixer(proj, nb, conv_w, conv_b, wa, ba, wi, bi, lam):
    rows_total = proj.shape[0]
    width = conv_w.shape[1]
    rows = EVEN_TL * nb
    halo = (CONV_WIDTH - 1) * nb
    kern = functools.partial(_lru_kernel, nb=nb)
    vec = pl.BlockSpec((1, width), lambda i: (0, 0))
    return pl.pallas_call(
        kern,
        out_shape=jax.ShapeDtypeStruct((rows_total, width), BF16),
        grid=(rows_total // rows,),
        in_specs=[pl.BlockSpec((rows, width), lambda i: (i, 1)),
                  pl.BlockSpec((rows, width), lambda i: (i, 2)),
                  pl.BlockSpec(conv_w.shape, lambda i: (0, 0)),
                  vec,
                  pl.BlockSpec(wa.shape, lambda i: (0, 0, 0)),
                  vec,
                  pl.BlockSpec(wi.shape, lambda i: (0, 0, 0)),
                  vec, vec],
        out_specs=pl.BlockSpec((rows, width), lambda i: (i, 0)),
        scratch_shapes=[pltpu.VMEM((rows + halo, width), F32),
                        pltpu.VMEM((rows, width), F32),
                        pltpu.VMEM((rows, width), F32),
                        pltpu.VMEM((nb, width), F32)],
        compiler_params=_cparams("arbitrary"),
        name="lru_mixer",
    )(proj, proj, conv_w, conv_b, wa, ba, wi, bi, lam)


def _even_outproj_kernel(ya_ref, yb_ref, wa_ref, wb_ref, x_ref, o_ref, y_ref):
    nb, tl, _ = x_ref.shape
    nc = y_ref.shape[0]
    y = _dot(ya_ref[...], wa_ref[...]) + _dot(yb_ref[...], wb_ref[...])
    for c in range(nc):
        y_ref[c] = y[:, c * LANES:(c + 1) * LANES]
    for b in range(nb):
        for c in range(nc):
            o_ref[b, :, c * LANES:(c + 1) * LANES] = (x_ref[b, :, c * LANES:(c + 1) * LANES]
                                                     + y_ref[c, pl.ds(b, tl, stride=nb), :])


def _even_outproj(ya, yb, w_a, w_b, x):
    nb, seq, d = x.shape
    tl = EVEN_TL
    rows = tl * nb
    wd = ya.shape[1]
    return pl.pallas_call(
        _even_outproj_kernel,
        out_shape=jax.ShapeDtypeStruct(x.shape, x.dtype),
        grid=(seq // tl,),
        in_specs=[pl.BlockSpec((rows, wd), lambda i: (i, 0)),
                  pl.BlockSpec((rows, wd), lambda i: (i, 0)),
                  pl.BlockSpec(w_a.shape, lambda i: (0, 0)),
                  pl.BlockSpec(w_b.shape, lambda i: (0, 0)),
                  pl.BlockSpec((nb, tl, d), lambda i: (0, i, 0))],
        out_specs=pl.BlockSpec((nb, tl, d), lambda i: (0, i, 0)),
        scratch_shapes=[pltpu.VMEM((d // LANES, rows, LANES), F32)],
        compiler_params=_cparams("parallel"),
        name="even_outproj",
    )(ya, yb, w_a, w_b, x)


def _block_diag(blocks):
    n, r, c = blocks.shape
    tiled = jnp.tile(blocks.reshape(n * r, c), (1, n))
    row_blk = jnp.arange(n * r, dtype=jnp.int32)[:, None] // r
    col_blk = jnp.arange(n * c, dtype=jnp.int32)[None, :] // c
    return jnp.where(row_blk == col_blk, tiled, jnp.zeros((), blocks.dtype))


def _even_mixer(x, norm_g, w_in, lam_re, lam_im, log_dt, b_re, b_im, c_re, c_im, d, glu_w, glu_b,
                conv_w, conv_b, wa, ba, wi, bi, lam, w_out):
    nb, seq, dm = x.shape
    g, p = lam_re.shape
    hch = b_re.shape[-1]
    width = g * hch
    n_half = 2
    gh = g // n_half

    proj = _even_inproj(x, norm_g[None, :], w_in.astype(BF16))

    lb_re, lb_im, bb_re, bb_im = _s5_discretize(lam_re, lam_im, log_dt, b_re, b_im)
    bb_re = bb_re.reshape(n_half, gh, hch, p)
    bb_im = bb_im.reshape(n_half, gh, hch, p)
    bm = jnp.stack([jnp.concatenate([_block_diag(bb_re[j]), _block_diag(bb_im[j])], axis=1)
                    for j in range(n_half)]).astype(BF16)
    c_re_t = jnp.swapaxes(c_re, 1, 2).reshape(n_half, gh, p, hch)
    c_im_t = jnp.swapaxes(c_im, 1, 2).reshape(n_half, gh, p, hch)
    cm = jnp.stack([jnp.concatenate([_block_diag(c_re_t[j]), -_block_diag(c_im_t[j])], axis=0)
                    for j in range(n_half)]).astype(BF16)
    lre = jnp.broadcast_to(lb_re.reshape(1, g * p), (nb, g * p))
    lim = jnp.broadcast_to(lb_im.reshape(1, g * p), (nb, g * p))
    ya = _s5_mixer(proj, nb, bm, lre, lim, cm, d.reshape(1, width), glu_w.astype(BF16),
                   glu_b[None, :])

    lw = conv_w.shape[1]
    nblk = 2
    per = LRU_BLOCKS // nblk
    wa_bd = jnp.stack([_block_diag(wa[j * per:(j + 1) * per]) for j in range(nblk)]).astype(BF16)
    wi_bd = jnp.stack([_block_diag(wi[j * per:(j + 1) * per]) for j in range(nblk)]).astype(BF16)
    yb = _lru_mixer(proj, nb, conv_w, conv_b[None, :], wa_bd, ba[None, :], wi_bd, bi[None, :],
                    lam[None, :])

    w_out_b = w_out.astype(BF16)
    return _even_outproj(ya, yb, w_out_b[:width], w_out_b[width:], x)


def _rope_table_kernel(pos_ref, inv_ref, cc_ref, ss_ref):
    tr = pos_ref.shape[0]
    half = ROPE_DIM // 2
    per_row = LANES // half
    ang = pos_ref[...].astype(F32) * inv_ref[...]
    cos = jnp.cos(ang)
    sin = jnp.sin(ang)
    lane = lax.broadcasted_iota(jnp.int32, ang.shape, 1)
    sign = jnp.where((lane & (ROPE_DIM - 1)) < half, -1.0, 1.0)

    def spread(a, j):
        g = jnp.where(lane // half == j, a, 0.0)
        out = g
        for k in range(1, per_row):
            out = out + pltpu.roll(g, k * half, 1)
        return out

    for j in range(per_row):
        cc_ref[pl.ds(j, tr, stride=per_row), :] = spread(cos, j)
        ss_ref[pl.ds(j, tr, stride=per_row), :] = spread(sin, j) * sign


def _rope_tables(positions):
    nb, seq = positions.shape
    half = ROPE_DIM // 2
    per_row = LANES // half
    n_tok = nb * seq
    tr = ROW_TILE
    inv_freq = 1.0 / (ROPE_THETA ** (jnp.arange(0, ROPE_DIM, 2, dtype=F32) / ROPE_DIM))
    pos_rep = jnp.repeat(positions.reshape(n_tok // per_row, per_row), half, axis=1)
    inv_rep = jnp.tile(inv_freq, per_row)[None, :]
    return pl.pallas_call(
        _rope_table_kernel,
        out_shape=(jax.ShapeDtypeStruct((n_tok, LANES), F32), jax.ShapeDtypeStruct((n_tok, LANES), F32)),
        grid=(n_tok // per_row // tr,),
        in_specs=[pl.BlockSpec((tr, LANES), lambda i: (i, 0)),
                  pl.BlockSpec((1, LANES), lambda i: (0, 0))],
        out_specs=(pl.BlockSpec((tr * per_row, LANES), lambda i: (i, 0)),
                   pl.BlockSpec((tr * per_row, LANES), lambda i: (i, 0))),
        compiler_params=_cparams("parallel"),
        name="rope_table",
    )(pos_rep, inv_rep)


def _mla_proj_kernel(x_ref, g_ref, win_ref, cqn_ref, ckvn_ref, wq_ref, wkv_ref, qg_ref, kg_ref,
                     cc_ref, ss_ref, qn_ref, qrp_ref, k2_ref, v_ref):
    nh = MLA_HEADS
    h = _rms(x_ref[...], g_ref[...])
    proj = _dot(h.astype(BF16), win_ref[...])
    c_q = _rms(proj[:, :Q_LORA], cqn_ref[...])
    c_kv = _rms(proj[:, Q_LORA:Q_LORA + KV_LORA], ckvn_ref[...])
    o = Q_LORA + KV_LORA
    k_r2 = proj[:, o:o + LANES]
    k_sw2 = proj[:, o + LANES:o + 2 * LANES]
    cc2 = cc_ref[...]
    ss2 = ss_ref[...]
    qg = qg_ref[...]
    kg = kg_ref[...]
    lo = lax.broadcasted_iota(jnp.int32, cc2.shape, 1) < ROPE_DIM
    scale = math.log2(math.e) / math.sqrt(QK_DIM)

    q = _dot(c_q.astype(BF16), wq_ref[...])
    q_a = cc2 * qg[1:2, :]
    q_b = ss2 * qg[2:3, :]
    for j in range(nh // 2):
        ro = nh * NOPE_DIM + j * LANES
        q_rp = q[:, ro:ro + LANES]
        q_swp = q[:, ro + nh * ROPE_DIM:ro + nh * ROPE_DIM + LANES]
        rot = q_rp * q_a + q_swp * q_b
        sq_rp = q_rp * q_rp
        rs_pair = []
        for par in range(2):
            hd = 2 * j + par
            q_n = q[:, hd * NOPE_DIM:(hd + 1) * NOPE_DIM]
            own = lo if par == 0 else jnp.logical_not(lo)
            ms = jnp.sum(q_n * q_n + jnp.where(own, sq_rp, 0.0), axis=-1, keepdims=True) / QK_DIM
            rs = lax.rsqrt(ms + RMS_EPS) * scale
            qn_ref[0, hd] = (q_n * rs * qg[0:1, :]).astype(qn_ref.dtype)
            rs_pair.append(rs)
        qrp_ref[0, j] = (rot * jnp.where(lo, rs_pair[0], rs_pair[1])).astype(qrp_ref.dtype)

    kv = _dot(c_kv.astype(BF16), wkv_ref[...])
    k_rot2 = k_r2 * (cc2 * kg[1:2, :]) + k_sw2 * (ss2 * kg[2:3, :])
    k_r_ss = jnp.sum(jnp.where(lo, k_r2 * k_r2, 0.0), axis=-1, keepdims=True)
    for hd in range(nh):
        ko = hd * (NOPE_DIM + V_DIM)
        k_n = kv[:, ko:ko + NOPE_DIM]
        ms_k = (jnp.sum(k_n * k_n, axis=-1, keepdims=True) + k_r_ss) / QK_DIM
        rs_k = lax.rsqrt(ms_k + RMS_EPS)
        own = lo if hd % 2 == 0 else jnp.logical_not(lo)
        k2_ref[0, hd, :, 0:NOPE_DIM] = (k_n * rs_k * kg[0:1, :]).astype(k2_ref.dtype)
        k2_ref[0, hd, :, NOPE_DIM:] = jnp.where(own, k_rot2 * rs_k, 0.0).astype(k2_ref.dtype)
        v_ref[0, hd] = kv[:, ko + NOPE_DIM:ko + NOPE_DIM + V_DIM].astype(v_ref.dtype)


def _swap_halves(a, axis=-1):
    h = a.shape[axis] // 2
    lo = lax.slice_in_dim(a, 0, h, axis=axis)
    hi = lax.slice_in_dim(a, h, 2 * h, axis=axis)
    return jnp.concatenate([hi, lo], axis=axis)


def _mla_proj(x, norm_g, positions, w_in, cq_norm, ckv_norm, w_uq, w_ukv, q_norm, k_norm):
    nb, seq, dm = x.shape
    nh = MLA_HEADS
    tm = ROW_TILE
    n_tok = nb * seq
    cc2, ss2 = _rope_tables(positions)
    o = Q_LORA + KV_LORA
    w_kr = w_in[:, o:o + ROPE_DIM]
    w_sw = _swap_halves(w_kr)
    w_in_x = jnp.concatenate([w_in[:, :o], w_kr, w_kr, w_sw, w_sw], axis=1).astype(BF16)
    wq = w_uq.reshape(Q_LORA, nh, QK_DIM)
    wq_n = wq[:, :, :NOPE_DIM].reshape(Q_LORA, nh * NOPE_DIM)
    wq_r = wq[:, :, NOPE_DIM:]
    wq_x = jnp.concatenate([wq_n, wq_r.reshape(Q_LORA, nh * ROPE_DIM),
                            _swap_halves(wq_r).reshape(Q_LORA, nh * ROPE_DIM)], axis=1).astype(BF16)

    def gains(gv):
        r = gv[NOPE_DIM:]
        rs = _swap_halves(r)
        return jnp.stack([gv[:NOPE_DIM], jnp.concatenate([r, r]), jnp.concatenate([rs, rs])])

    x2 = x.reshape(n_tok, dm)
    per_b = seq // tm
    full = lambda a: pl.BlockSpec(a.shape, lambda i: (0,) * a.ndim)
    args = (x2, norm_g[None, :], w_in_x, cq_norm[None, :], ckv_norm[None, :], wq_x,
            w_ukv.astype(BF16), gains(q_norm), gains(k_norm), cc2, ss2)
    in_specs = [pl.BlockSpec((tm, dm), lambda i: (i, 0))] + [full(a) for a in args[1:9]] + [
        pl.BlockSpec((tm, LANES), lambda i: (i, 0)), pl.BlockSpec((tm, LANES), lambda i: (i, 0))]

    def head_spec(n_heads, dh):
        return pl.BlockSpec((1, n_heads, tm, dh), lambda i: (i // per_b, 0, i % per_b, 0))

    def head_shape(n_heads, dh):
        return jax.ShapeDtypeStruct((nb, n_heads, seq, dh), BF16)

    return pl.pallas_call(
        _mla_proj_kernel,
        out_shape=(head_shape(nh, NOPE_DIM), head_shape(nh // 2, LANES), head_shape(nh, NOPE_DIM + LANES),
                   head_shape(nh, V_DIM)),
        grid=(n_tok // tm,),
        in_specs=in_specs,
        out_specs=(head_spec(nh, NOPE_DIM), head_spec(nh // 2, LANES), head_spec(nh, NOPE_DIM + LANES),
                   head_spec(nh, V_DIM)),
        compiler_params=_cparams("parallel"),
        name="mla_proj",
    )(*args)


def _attention_kernel(qn_ref, qrp_ref, k2_ref, v_ref, o_ref):
    seq = qn_ref.shape[2]
    t = ATT_TQ
    nt = (((1,), (1,)), ((), ()))
    tri = (lax.broadcasted_iota(jnp.int32, (t, t), 1) <= lax.broadcasted_iota(jnp.int32, (t, t), 0))
    for i in range(seq // t):
        q = jnp.concatenate([qn_ref[0, 0, i * t:(i + 1) * t, :], qrp_ref[0, 0, i * t:(i + 1) * t, :]], axis=-1)
        m = jnp.full((t, 1), NEG_INF, F32)
        l = jnp.zeros((t, 1), F32)
        acc = jnp.zeros((t, V_DIM), F32)
        for j in range(i + 1):
            s = lax.dot_general(q, k2_ref[0, 0, j * t:(j + 1) * t, :], nt, preferred_element_type=F32)
            if j == i:
                s = jnp.where(tri, s, NEG_INF)
            m_new = jnp.maximum(m, jnp.max(s, axis=-1, keepdims=True))
            alpha = jnp.exp2(m - m_new)
            p = jnp.exp2(s - m_new)
            l = alpha * l + jnp.sum(p, axis=-1, keepdims=True)
            acc = alpha * acc + _dot(p.astype(BF16), v_ref[0, 0, j * t:(j + 1) * t, :])
            m = m_new
        o_ref[0, i * t:(i + 1) * t, :] = (acc / l).astype(o_ref.dtype)


def _attention(qn, qrp, k2, v):
    nb, nh, seq, _ = qn.shape

    def spec(dh, per=1):
        return pl.BlockSpec((1, 1, seq, dh), lambda b, h: (b, h // per, 0, 0))

    return pl.pallas_call(
        _attention_kernel,
        out_shape=jax.ShapeDtypeStruct((nb, seq, nh * V_DIM), BF16),
        grid=(nb, nh),
        in_specs=[spec(NOPE_DIM), spec(LANES, 2), spec(NOPE_DIM + LANES), spec(V_DIM)],
        out_specs=pl.BlockSpec((1, seq, V_DIM), lambda b, h: (b, 0, h)),
        compiler_params=_cparams("parallel", "parallel"),
        name="mla_attention",
    )(qn, qrp, k2, v)


def _proj_residual_kernel(a_ref, w_ref, x_ref, o_ref):
    o_ref[...] = x_ref[...] + _dot(a_ref[...], w_ref[...])


def _proj_residual(a, w, x2):
    n_tok, dm = x2.shape
    tm = ROW_TILE
    return pl.pallas_call(
        _proj_residual_kernel,
        out_shape=jax.ShapeDtypeStruct(x2.shape, x2.dtype),
        grid=(n_tok // tm,),
        in_specs=[pl.BlockSpec((tm, a.shape[1]), lambda i: (i, 0)),
                  pl.BlockSpec(w.shape, lambda i: (0, 0)),
                  pl.BlockSpec((tm, dm), lambda i: (i, 0))],
        out_specs=pl.BlockSpec((tm, dm), lambda i: (i, 0)),
        compiler_params=_cparams("parallel"),
        name="proj_residual",
    )(a, w, x2)


def _mla_mixer(x, norm_g, positions, w_in, cq_norm, ckv_norm, w_uq, w_ukv, q_norm, k_norm, w_o):
    nb, seq, dm = x.shape
    qn, qrp, k2, v = _mla_proj(x, norm_g, positions, w_in, cq_norm, ckv_norm, w_uq, w_ukv, q_norm, k_norm)
    o = _attention(qn, qrp, k2, v)
    out = _proj_residual(o.reshape(nb * seq, -1), w_o.astype(BF16), x.reshape(nb * seq, dm))
    return out.reshape(nb, seq, dm)


def _router_kernel(x_ref, g_ref, w_ref, b_ref, tri_ref, meta_ref, gate_ref, cnt_ref, base_ref):
    tm, dm = x_ref.shape
    ne = N_EXPERTS

    @pl.when(pl.program_id(0) == 0)
    def _():
        base_ref[...] = jnp.zeros_like(base_ref)

    hn = _rms(x_ref[...], g_ref[...])
    logits = _dot(hn.astype(BF16), w_ref[...]) + b_ref[...]
    lane = lax.broadcasted_iota(jnp.int32, logits.shape, 1)
    big = jnp.int32(1 << 20)

    def first_argmax(v):
        m = jnp.max(v, axis=-1, keepdims=True)
        idx = jnp.min(jnp.where(v == m, lane, big), axis=-1, keepdims=True)
        return m, idx

    is_g = (lane >= ne) & (lane < ne + N_GROUPS)
    gl = jnp.where(is_g, logits, NEG_INF)
    g_max, g_idx = first_argmax(gl)
    g_sel = g_idx - ne
    p_g = 1.0 / jnp.sum(jnp.where(is_g, jnp.exp(gl - g_max), 0.0), axis=-1, keepdims=True)
    in_grp = (lane >= g_sel * EXPERTS_PER_GROUP) & (lane < (g_sel + 1) * EXPERTS_PER_GROUP)
    el = jnp.where(in_grp, logits, NEG_INF)
    v0, e0 = first_argmax(el)
    v1, e1 = first_argmax(jnp.where(lane == e0, NEG_INF, el))
    t = jnp.exp(v1 - v0)
    gate0 = p_g / (1.0 + t)
    gate1 = p_g * t / (1.0 + t)

    oh0 = (lane == e0).astype(F32)
    oh1 = (lane == e1).astype(F32)
    both = oh0 + oh1
    before = _dot(tri_ref[...], both.astype(BF16)) + base_ref[0:1, :]
    rank0 = jnp.sum(oh0 * before, axis=-1, keepdims=True)
    rank1 = jnp.sum(oh1 * before, axis=-1, keepdims=True)
    base_ref[0:1, :] = base_ref[0:1, :] + jnp.sum(both, axis=0, keepdims=True)
    cnt_ref[...] = jnp.broadcast_to(base_ref[0:1, :], cnt_ref.shape).astype(jnp.int32)

    meta = jnp.where(lane == 0, e0, jnp.where(lane == 1, e1, jnp.where(
        lane == 2, rank0.astype(jnp.int32), jnp.where(lane == 3, rank1.astype(jnp.int32), 0))))
    meta_ref[...] = meta
    gate_ref[...] = jnp.where(lane == 0, gate0, jnp.where(lane == 1, gate1, 0.0))


def _router(x2, norm_g, w_group, b_group, w_expert, b_expert):
    n_tok, dm = x2.shape
    tm = ROW_TILE
    pad = LANES - N_EXPERTS - N_GROUPS
    w = jnp.concatenate([w_expert, w_group, jnp.zeros((dm, pad), F32)], axis=1).astype(BF16)
    b = jnp.concatenate([b_expert, b_group, jnp.zeros((pad,), F32)])[None, :]
    idx = jnp.arange(tm, dtype=jnp.int32)
    tri = (idx[None, :] < idx[:, None]).astype(BF16)
    return pl.pallas_call(
        _router_kernel,
        out_shape=(jax.ShapeDtypeStruct((n_tok, LANES), jnp.int32),
                   jax.ShapeDtypeStruct((n_tok, LANES), F32),
                   jax.ShapeDtypeStruct((SUBLANES, LANES), jnp.int32)),
        grid=(n_tok // tm,),
        in_specs=[pl.BlockSpec((tm, dm), lambda i: (i, 0)),
                  pl.BlockSpec((1, dm), lambda i: (0, 0)),
                  pl.BlockSpec((dm, LANES), lambda i: (0, 0)),
                  pl.BlockSpec((1, LANES), lambda i: (0, 0)),
                  pl.BlockSpec((tm, tm), lambda i: (0, 0))],
        out_specs=(pl.BlockSpec((tm, LANES), lambda i: (i, 0)),
                   pl.BlockSpec((tm, LANES), lambda i: (i, 0)),
                   pl.BlockSpec((SUBLANES, LANES), lambda i: (0, 0))),
        scratch_shapes=[pltpu.VMEM((SUBLANES, LANES), F32)],
        compiler_params=_cparams("arbitrary"),
        name="moe_router",
    )(x2, norm_g[None, :], w, b, tri)


def _dispatch_kernel(dest_ref, dest_prev_ref, x_ref, g_ref, xd_ref, hn_ref, zero_ref, sem, zsem):
    tm = x_ref.shape[0]
    i = pl.program_id(0)
    n_steps = pl.num_programs(0)
    slot = i % 2
    n_assign = n_steps * tm * TOP_K

    def tail_copy():
        return pltpu.make_async_copy(zero_ref, xd_ref.at[pl.ds(n_assign * ROW_CHUNKS, zero_ref.shape[0])], zsem)

    @pl.when(i == 0)
    def _():
        zero_ref[...] = jnp.zeros_like(zero_ref)
        tail_copy().start()

    hn = _rms(x_ref[...], g_ref[...])
    hbuf = hn_ref.at[slot]
    for c in range(ROW_CHUNKS):
        hbuf[pl.ds(c, tm, stride=ROW_CHUNKS), :] = hn[:, c * LANES:(c + 1) * LANES]

    def row_copy(d_ref, sl, t, k):
        src = pl.multiple_of(t * ROW_CHUNKS, ROW_CHUNKS)
        dst = pl.multiple_of(d_ref[0, 0, 2 * t + k], ROW_CHUNKS)
        return pltpu.make_async_copy(hn_ref.at[sl, pl.ds(src, ROW_CHUNKS)], xd_ref.at[pl.ds(dst, ROW_CHUNKS)],
                                     sem.at[sl])

    def issue(j, c):
        for u in range(DMA_UNROLL):
            for k in range(TOP_K):
                row_copy(dest_ref, slot, j * DMA_UNROLL + u, k).start(priority=k)
        return c

    lax.fori_loop(0, tm // DMA_UNROLL, issue, 0)

    def drain_all(d_ref, sl):
        def drain(j, c):
            for u in range(DMA_UNROLL):
                for k in range(TOP_K):
                    row_copy(d_ref, sl, j * DMA_UNROLL + u, k).wait()
            return c
        lax.fori_loop(0, tm // DMA_UNROLL, drain, 0)

    @pl.when(i >= 1)
    def _():
        drain_all(dest_prev_ref, 1 - slot)

    @pl.when(i == n_steps - 1)
    def _():
        drain_all(dest_ref, slot)

    @pl.when(i == 0)
    def _():
        tail_copy().wait()


def _dispatch(x2, norm_g, dest_off):
    tm = GATHER_TM
    n_tok, dm = x2.shape
    dest3 = dest_off.reshape(n_tok // tm, 1, 2 * tm)
    n_rows = n_tok * TOP_K + MOE_BM
    return pl.pallas_call(
        _dispatch_kernel,
        out_shape=jax.ShapeDtypeStruct((n_rows * ROW_CHUNKS, LANES), F32),
        grid=(n_tok // tm,),
        in_specs=[pl.BlockSpec((1, 1, 2 * tm), lambda i: (i, 0, 0), memory_space=pltpu.SMEM),
                  pl.BlockSpec((1, 1, 2 * tm), lambda i: (jnp.maximum(i - 1, 0), 0, 0), memory_space=pltpu.SMEM),
                  pl.BlockSpec((tm, dm), lambda i: (i, 0)),
                  pl.BlockSpec((1, dm), lambda i: (0, 0))],
        out_specs=pl.BlockSpec(memory_space=pl.ANY),
        scratch_shapes=[pltpu.VMEM((2, tm * ROW_CHUNKS, LANES), F32),
                        pltpu.VMEM((MOE_BM * ROW_CHUNKS, LANES), F32),
                        pltpu.SemaphoreType.DMA((2,)), pltpu.SemaphoreType.DMA],
        compiler_params=_cparams("arbitrary"),
        name="moe_dispatch",
    )(dest3, dest3, x2, norm_g[None, :])


def _rows_from_tiles(ref, n, base=0):
    return jnp.concatenate([ref[pl.ds(base + c, n, stride=ROW_CHUNKS), :] for c in range(ROW_CHUNKS)], axis=-1)


def _expert_kernel(be_ref, bs_ref, first_ref, nxt_ref, nu_ref, xd_ref, w1_ref, w3_ref, w2_ref, yd_ref,
                   xbuf, ybuf, w1buf, w3buf, w2buf, w13_s, w2_s, sem_in, sem_out, sem_w, *, layer):
    nu = nu_ref[0]
    n_rows = xbuf.shape[1]
    bm = n_rows // ROW_CHUNKS
    de = w1buf.shape[2]

    def in_copy(blk, sl):
        src = pl.multiple_of(bs_ref[blk] * ROW_CHUNKS, ROW_CHUNKS)
        return pltpu.make_async_copy(xd_ref.at[pl.ds(src, n_rows)], xbuf.at[sl], sem_in.at[sl])

    def out_copy(blk, sl):
        dst = pl.multiple_of(bs_ref[blk] * ROW_CHUNKS, ROW_CHUNKS)
        return pltpu.make_async_copy(ybuf.at[sl], yd_ref.at[pl.ds(dst, n_rows)], sem_out.at[sl])

    def w_copies(e, ws):
        return (pltpu.make_async_copy(w1_ref.at[layer, e], w1buf.at[ws], sem_w.at[ws]),
                pltpu.make_async_copy(w3_ref.at[layer, e], w3buf.at[ws], sem_w.at[ws]),
                pltpu.make_async_copy(w2_ref.at[layer, e], w2buf.at[ws], sem_w.at[ws]))

    in_copy(0, 0).start()
    for cp in w_copies(be_ref[0], 1):
        cp.start(priority=1)

    def block(b, ws):
        slot = b % 2

        @pl.when(b + 1 < nu)
        def _():
            in_copy(b + 1, 1 - slot).start()

        first = first_ref[b] == 1
        ws = jnp.where(first, 1 - ws, ws)

        @pl.when(first)
        def _():
            for cp in w_copies(be_ref[b], ws):
                cp.wait()
            w13_s[:, :de] = w1buf[ws].astype(BF16)
            w13_s[:, de:] = w3buf[ws].astype(BF16)
            w2_s[...] = w2buf[ws].astype(BF16)

            @pl.when(nxt_ref[b] >= 0)
            def _():
                for cp in w_copies(nxt_ref[b], 1 - ws):
                    cp.start(priority=1)

        in_copy(b, slot).wait()
        x = _rows_from_tiles(xbuf.at[slot], bm).astype(BF16)
        h = _dot(x, w13_s[...])
        a = jax.nn.silu(h[:, :de]) * h[:, de:]
        y = _dot(a.astype(BF16), w2_s[...])
        yb = ybuf.at[slot]
        for c in range(ROW_CHUNKS):
            yb[pl.ds(c, bm, stride=ROW_CHUNKS), :] = y[:, c * LANES:(c + 1) * LANES]

        @pl.when(b >= 1)
        def _():
            out_copy(b - 1, 1 - slot).wait()
        out_copy(b, slot).start()
        return ws

    lax.fori_loop(0, nu, block, jnp.int32(0))

    last = nu - 1
    out_copy(last, last % 2).wait()
    tail = yd_ref.shape[0] - n_rows
    fill_slot = 1 - last % 2
    ybuf[fill_slot] = jnp.zeros((n_rows, LANES), F32)
    fill = pltpu.make_async_copy(ybuf.at[fill_slot], yd_ref.at[pl.ds(tail, n_rows)], sem_out.at[fill_slot])
    fill.start()
    fill.wait()


def _experts(xd, blk_expert, blk_start, blk_first, blk_next, n_used, w1, w3, w2, layer):
    bm = MOE_BM
    _, _, dm, de = w1.shape
    any_spec = pl.BlockSpec(memory_space=pl.ANY)
    grid_spec = pltpu.PrefetchScalarGridSpec(
        num_scalar_prefetch=5,
        grid=(1,),
        in_specs=[any_spec, any_spec, any_spec, any_spec],
        out_specs=any_spec,
        scratch_shapes=[pltpu.VMEM((2, bm * ROW_CHUNKS, LANES), F32),
                        pltpu.VMEM((2, bm * ROW_CHUNKS, LANES), F32),
                        pltpu.VMEM((2, dm, de), F32),
                        pltpu.VMEM((2, dm, de), F32),
                        pltpu.VMEM((2, de, dm), F32),
                        pltpu.VMEM((dm, 2 * de), BF16),
                        pltpu.VMEM((de, dm), BF16),
                        pltpu.SemaphoreType.DMA((2,)),
                        pltpu.SemaphoreType.DMA((2,)),
                        pltpu.SemaphoreType.DMA((2,))],
    )
    return pl.pallas_call(
        functools.partial(_expert_kernel, layer=layer),
        out_shape=jax.ShapeDtypeStruct(xd.shape, F32),
        grid_spec=grid_spec,
        compiler_params=_cparams("arbitrary"),
        name="moe_experts",
    )(blk_expert, blk_start, blk_first, blk_next, n_used, xd, w1, w3, w2)


def _combine_kernel(dest_ref, dest_next_ref, yd_ref, gate_ref, x_ref, o_ref, buf_ref, sem):
    tm, dm = x_ref.shape
    i = pl.program_id(0)
    n_steps = pl.num_programs(0)
    slot = i % 2
    k_rows = tm * ROW_CHUNKS

    def row_copy(d_ref, sl, t, k):
        src = pl.multiple_of(d_ref[0, 0, 2 * t + k], ROW_CHUNKS)
        dst = pl.multiple_of(k * k_rows + t * ROW_CHUNKS, ROW_CHUNKS)
        return pltpu.make_async_copy(yd_ref.at[pl.ds(src, ROW_CHUNKS)],
                                     buf_ref.at[sl, pl.ds(dst, ROW_CHUNKS)], sem.at[sl])

    def issue_all(d_ref, sl):
        def issue(j, c):
            for u in range(DMA_UNROLL):
                for k in range(TOP_K):
                    row_copy(d_ref, sl, j * DMA_UNROLL + u, k).start(priority=k)
            return c
        lax.fori_loop(0, tm // DMA_UNROLL, issue, 0)

    @pl.when(i == 0)
    def _():
        issue_all(dest_ref, slot)

    @pl.when(i + 1 < n_steps)
    def _():
        issue_all(dest_next_ref, 1 - slot)

    def drain(j, c):
        for u in range(DMA_UNROLL):
            for k in range(TOP_K):
                row_copy(dest_ref, slot, j * DMA_UNROLL + u, k).wait()
        return c

    lax.fori_loop(0, tm // DMA_UNROLL, drain, 0)

    g0 = gate_ref[:, 0:1]
    g1 = gate_ref[:, 1:2]
    buf = buf_ref.at[slot]
    for c in range(ROW_CHUNKS):
        y0 = buf[pl.ds(c, tm, stride=ROW_CHUNKS), :]
        y1 = buf[pl.ds(k_rows + c, tm, stride=ROW_CHUNKS), :]
        o_ref[:, c * LANES:(c + 1) * LANES] = x_ref[:, c * LANES:(c + 1) * LANES] + (g0 * y0 + g1 * y1)


def _combine(yd, dest_off, gates, x2):
    n_tok, dm = x2.shape
    tm = COMBINE_TM
    n_steps = n_tok // tm
    dest3 = dest_off.reshape(n_steps, 1, 2 * tm)
    return pl.pallas_call(
        _combine_kernel,
        out_shape=jax.ShapeDtypeStruct(x2.shape, x2.dtype),
        grid=(n_steps,),
        in_specs=[pl.BlockSpec((1, 1, 2 * tm), lambda i: (i, 0, 0), memory_space=pltpu.SMEM),
                  pl.BlockSpec((1, 1, 2 * tm), lambda i: (jnp.minimum(i + 1, n_steps - 1), 0, 0),
                               memory_space=pltpu.SMEM),
                  pl.BlockSpec(memory_space=pl.ANY),
                  pl.BlockSpec((tm, LANES), lambda i: (i, 0)),
                  pl.BlockSpec((tm, dm), lambda i: (i, 0))],
        out_specs=pl.BlockSpec((tm, dm), lambda i: (i, 0)),
        scratch_shapes=[pltpu.VMEM((2, TOP_K * tm * ROW_CHUNKS, LANES), F32),
                        pltpu.SemaphoreType.DMA((2,))],
        compiler_params=_cparams("arbitrary"),
        name="moe_combine",
    )(dest3, dest3, yd, gates, x2)


def _moe(x, norm_g, w_group, b_group, w_expert, b_expert, w1, w3, w2, layer):
    nb, seq, dm = x.shape
    n_tok = nb * seq
    n_assign = n_tok * TOP_K
    x2 = x.reshape(n_tok, dm)
    meta, gates, counts = _router(x2, norm_g, w_group, b_group, w_expert, b_expert)
    bm = MOE_BM
    n_blocks = n_assign // bm + N_EXPERTS
    counts = counts[0, :N_EXPERTS]
    starts = jnp.cumsum(counts) - counts
    eid = meta[:, 0:TOP_K]
    e_iota = jnp.arange(N_EXPERTS, dtype=jnp.int32)
    seg_start = jnp.sum(jnp.where(eid[:, :, None] == e_iota, starts, 0), axis=-1)
    dest_off = ((seg_start + meta[:, TOP_K:2 * TOP_K]) * ROW_CHUNKS).astype(jnp.int32).reshape(-1)
    nblk = (counts + bm - 1) // bm
    blk_end = jnp.cumsum(nblk)
    b_iota = jnp.arange(n_blocks, dtype=jnp.int32)
    blk_expert = jnp.minimum(jnp.sum((b_iota[:, None] >= blk_end[None, :]).astype(jnp.int32), axis=1),
                             N_EXPERTS - 1)
    j = b_iota - (blk_end - nblk)[blk_expert]
    n_used = blk_end[-1].astype(jnp.int32).reshape(1)
    blk_start = jnp.where(b_iota < n_used[0], starts[blk_expert] + j * bm, 0).astype(jnp.int32)
    blk_first = (j == 0).astype(jnp.int32)
    cand = jnp.where(counts > 0, e_iota, N_EXPERTS)
    later = lax.cummin(cand[::-1])[::-1]
    next_e = jnp.concatenate([later[1:], jnp.full((1,), N_EXPERTS, jnp.int32)])
    blk_next = jnp.where(next_e[blk_expert] < N_EXPERTS, next_e[blk_expert], -1).astype(jnp.int32)

    xd = _dispatch(x2, norm_g, dest_off)
    yd = _experts(xd, blk_expert, blk_start, blk_first, blk_next, n_used, w1, w3, w2, layer)
    out = _combine(yd, dest_off, gates, x2)
    return out.reshape(nb, seq, dm)


def kernel(x, positions, ev_norm, ev_w_in, s5_lambda_re, s5_lambda_im, s5_log_dt, s5_b_re, s5_b_im,
           s5_c_re, s5_c_im, s5_d, s5_glu_w, s5_glu_b, lru_conv_w, lru_conv_b, lru_wa, lru_ba, lru_wi,
           lru_bi, lru_lambda, ev_w_out, od_norm, mla_w_in, mla_cq_norm, mla_ckv_norm, mla_w_uq,
           mla_w_ukv, mla_q_norm, mla_k_norm, mla_w_o, ffn_norm, moe_w_group, moe_b_group,
           moe_w_expert, moe_b_expert, moe_w1, moe_w3, moe_w2):
    depth = ffn_norm.shape[0]
    for layer in range(depth):
        j = layer // 2
        if layer % 2 == 0:
            x = _even_mixer(x, ev_norm[j], ev_w_in[j], s5_lambda_re[j], s5_lambda_im[j], s5_log_dt[j],
                            s5_b_re[j], s5_b_im[j], s5_c_re[j], s5_c_im[j], s5_d[j], s5_glu_w[j],
                            s5_glu_b[j], lru_conv_w[j], lru_conv_b[j], lru_wa[j], lru_ba[j], lru_wi[j],
                            lru_bi[j], lru_lambda[j], ev_w_out[j])
        else:
            x = _mla_mixer(x, od_norm[j], positions, mla_w_in[j], mla_cq_norm[j], mla_ckv_norm[j],
                           mla_w_uq[j], mla_w_ukv[j], mla_q_norm[j], mla_k_norm[j], mla_w_o[j])
        x = _moe(x, ffn_norm[layer], moe_w_group[layer], moe_b_group[layer], moe_w_expert[layer],
                 moe_b_expert[layer], moe_w1, moe_w3, moe_w2, layer)
    return x
```

```python
import functools
import math

import jax
import jax.numpy as jnp
from jax import lax
from jax.experimental import pallas as pl
from jax.experimental.pallas import tpu as pltpu

F32 = jnp.float32
BF16 = jnp.bfloat16

S5_GROUP_CH = 16
S5_STATE = 64
LRU_BLOCKS = 8
CONV_WIDTH = 4
RG_C = 8.0
MLA_HEADS = 8
Q_LORA = 384
KV_LORA = 256
NOPE_DIM = 128
ROPE_DIM = 64
V_DIM = 128
QK_DIM = NOPE_DIM + ROPE_DIM
ROPE_THETA = 10000.0
N_GROUPS = 8
EXPERTS_PER_GROUP = 8
N_EXPERTS = N_GROUPS * EXPERTS_PER_GROUP
TOP_K = 2
RMS_EPS = 1e-6
NEG_INF = -1e30

SUBLANES = 8
LANES = 128
ROW_CHUNKS = 8
VMEM_LIMIT = 48 * 1024 * 1024

EVEN_TL = 128
ROW_TILE = 512
MOE_BM = 512
ATT_TQ = 512
GATHER_TM = 512
DMA_UNROLL = 8


def _cparams(*sem):
    return pltpu.CompilerParams(dimension_semantics=tuple(sem), vmem_limit_bytes=VMEM_LIMIT)


def _rms(x, g):
    ms = jnp.mean(x * x, axis=-1, keepdims=True)
    return x * lax.rsqrt(ms + RMS_EPS) * g


def _dot(a, b):
    return jnp.dot(a, b, preferred_element_type=F32)


def _s5_discretize_kernel(lre_ref, lim_ref, ldt_ref, bre_ref, bim_ref,
                          lbre_ref, lbim_ref, bbre_ref, bbim_ref):
    lr = jnp.minimum(lre_ref[...], -1e-4)
    li = lim_ref[...]
    dt = jnp.exp(ldt_ref[...])
    mag = jnp.exp(lr * dt)
    lb_re = mag * jnp.cos(li * dt)
    lb_im = mag * jnp.sin(li * dt)
    den = lr * lr + li * li
    num_re = lb_re - 1.0
    f_re = (num_re * lr + lb_im * li) / den
    f_im = (lb_im * lr - num_re * li) / den
    lbre_ref[...] = lb_re
    lbim_ref[...] = lb_im
    br = bre_ref[...]
    bi = bim_ref[...]
    bbre_ref[...] = f_re[:, None, :] * br - f_im[:, None, :] * bi
    bbim_ref[...] = f_re[:, None, :] * bi + f_im[:, None, :] * br


def _s5_discretize(lam_re, lam_im, log_dt, b_re, b_im):
    g, p = lam_re.shape
    h = b_re.shape[-1]
    b_re_t = jnp.swapaxes(b_re, 1, 2)
    b_im_t = jnp.swapaxes(b_im, 1, 2)
    return pl.pallas_call(
        _s5_discretize_kernel,
        out_shape=(jax.ShapeDtypeStruct((g, p), F32), jax.ShapeDtypeStruct((g, p), F32),
                   jax.ShapeDtypeStruct((g, h, p), F32), jax.ShapeDtypeStruct((g, h, p), F32)),
        name="s5_discretize",
    )(lam_re, lam_im, log_dt[:, None], b_re_t, b_im_t)


def _even_inproj_kernel(x_ref, g_ref, w_ref, o_ref, h_ref):
    nb, tl, _ = x_ref.shape
    nc = h_ref.shape[0]
    g = g_ref[...]
    for b in range(nb):
        h = _rms(x_ref[b], g)
        for c in range(nc):
            h_ref[c, pl.ds(b, tl, stride=nb), :] = h[:, c * LANES:(c + 1) * LANES]
    h_all = jnp.concatenate([h_ref[c] for c in range(nc)], axis=-1)
    o_ref[...] = _dot(h_all.astype(BF16), w_ref[...]).astype(o_ref.dtype)


def _even_inproj(x, g, w):
    nb, seq, d = x.shape
    n_out = w.shape[1]
    tl = EVEN_TL
    return pl.pallas_call(
        _even_inproj_kernel,
        out_shape=jax.ShapeDtypeStruct((seq * nb, n_out), BF16),
        grid=(seq // tl,),
        in_specs=[pl.BlockSpec((nb, tl, d), lambda i: (0, i, 0)),
                  pl.BlockSpec((1, d), lambda i: (0, 0)),
                  pl.BlockSpec((d, n_out), lambda i: (0, 0))],
        out_specs=pl.BlockSpec((tl * nb, n_out), lambda i: (i, 0)),
        scratch_shapes=[pltpu.VMEM((d // LANES, tl * nb, LANES), F32)],
        compiler_params=_cparams("parallel"),
        name="even_inproj",
    )(x, g, w)


def _s5_kernel(u_ref, bm_ref, lre_ref, lim_ref, cm_ref, d_ref, gw_ref, gb_ref, o_ref,
               bu_ref, sb_ref, st_ref, *, nb):
    rows, width = u_ref.shape
    n_half = bm_ref.shape[0]
    kin = width // n_half
    ncol = bm_ref.shape[2]
    nre = ncol // 2
    chunk = 512
    steps = rows // nb

    @pl.when(pl.program_id(0) == 0)
    def _():
        st_ref[...] = jnp.zeros_like(st_ref)

    u = u_ref[...]
    for j in range(n_half):
        bu_ref[:, j * ncol:(j + 1) * ncol] = _dot(u[:, j * kin:(j + 1) * kin], bm_ref[j])

    for j in range(n_half):
        for c in range(nre // chunk):
            cr = j * ncol + c * chunk
            ci = cr + nre
            lc = j * nre + c * chunk
            lr = lre_ref[:, lc:lc + chunk]
            li = lim_ref[:, lc:lc + chunk]

            def body(i, carry, cr=cr, ci=ci, lr=lr, li=li):
                sr, si = carry
                r0 = pl.multiple_of(i * (2 * nb), 2 * nb)
                sr1 = lr * sr - li * si + bu_ref[pl.ds(r0, nb), cr:cr + chunk]
                si1 = lr * si + li * sr + bu_ref[pl.ds(r0, nb), ci:ci + chunk]
                sr2 = lr * sr1 - li * si1 + bu_ref[pl.ds(r0 + nb, nb), cr:cr + chunk]
                si2 = lr * si1 + li * sr1 + bu_ref[pl.ds(r0 + nb, nb), ci:ci + chunk]
                sb_ref[pl.ds(r0, 2 * nb), cr:cr + chunk] = jnp.concatenate([sr1, sr2], 0).astype(BF16)
                sb_ref[pl.ds(r0, 2 * nb), ci:ci + chunk] = jnp.concatenate([si1, si2], 0).astype(BF16)
                return sr2, si2

            sr, si = lax.fori_loop(0, steps // 2, body,
                                   (st_ref[:, cr:cr + chunk], st_ref[:, ci:ci + chunk]))
            st_ref[:, cr:cr + chunk] = sr
            st_ref[:, ci:ci + chunk] = si

    ys = [_dot(sb_ref[:, j * ncol:(j + 1) * ncol], cm_ref[j]) for j in range(n_half)]
    y = jnp.concatenate(ys, axis=-1) + d_ref[...] * u.astype(F32)
    y = jax.nn.gelu(y)
    z = _dot(y.astype(BF16), gw_ref[...]) + gb_ref[...]
    o_ref[...] = (y * jax.nn.sigmoid(z)).astype(o_ref.dtype)


def _s5_mixer(proj, nb, bm, lre, lim, cm, d, glu_w, glu_b):
    rows_total = proj.shape[0]
    width = d.shape[1]
    rows = EVEN_TL * nb
    n_half, _, ncol = bm.shape
    kern = functools.partial(_s5_kernel, nb=nb)
    return pl.pallas_call(
        kern,
        out_shape=jax.ShapeDtypeStruct((rows_total, width), BF16),
        grid=(rows_total // rows,),
        in_specs=[pl.BlockSpec((rows, width), lambda i: (i, 0)),
                  pl.BlockSpec(bm.shape, lambda i: (0, 0, 0)),
                  pl.BlockSpec(lre.shape, lambda i: (0, 0)),
                  pl.BlockSpec(lim.shape, lambda i: (0, 0)),
                  pl.BlockSpec(cm.shape, lambda i: (0, 0, 0)),
                  pl.BlockSpec((1, width), lambda i: (0, 0)),
                  pl.BlockSpec(glu_w.shape, lambda i: (0, 0)),
                  pl.BlockSpec((1, width), lambda i: (0, 0))],
        out_specs=pl.BlockSpec((rows, width), lambda i: (i, 0)),
        scratch_shapes=[pltpu.VMEM((rows, n_half * ncol), F32),
                        pltpu.VMEM((rows, n_half * ncol), BF16),
                        pltpu.VMEM((nb, n_half * ncol), F32)],
        compiler_params=_cparams("arbitrary"),
        name="s5_mixer",
    )(proj, bm, lre, lim, cm, d, glu_w, glu_b)


def _lru_kernel(x_ref, gate_ref, cw_ref, cb_ref, wa_ref, ba_ref, wi_ref, bi_ref, lam_ref, o_ref,
                xp_ref, a_ref, b_ref, h_ref, *, nb):
    rows, width = x_ref.shape
    halo = (CONV_WIDTH - 1) * nb
    steps = rows // nb

    @pl.when(pl.program_id(0) == 0)
    def _():
        xp_ref[0:halo, :] = jnp.zeros((halo, width), F32)
        h_ref[...] = jnp.zeros_like(h_ref)

    xp_ref[halo:halo + rows, :] = x_ref[...].astype(F32)
    xf = cb_ref[...] + cw_ref[0:1, :] * xp_ref[0:rows, :]
    for k in range(1, CONV_WIDTH):
        xf = xf + cw_ref[k:k + 1, :] * xp_ref[k * nb:k * nb + rows, :]
    xp_ref[0:halo, :] = xp_ref[rows:rows + halo, :]

    xb = xf.astype(BF16)
    nblk = wa_ref.shape[0]
    kb = width // nblk
    ga = jnp.concatenate([_dot(xb[:, j * kb:(j + 1) * kb], wa_ref[j]) for j in range(nblk)], -1)
    gi = jnp.concatenate([_dot(xb[:, j * kb:(j + 1) * kb], wi_ref[j]) for j in range(nblk)], -1)
    r = jax.nn.sigmoid(ga + ba_ref[...])
    ig = jax.nn.sigmoid(gi + bi_ref[...])
    log_a = RG_C * r * jax.nn.log_sigmoid(lam_ref[...])
    a_ref[...] = jnp.exp(log_a)
    th = jnp.tanh(log_a)
    b_ref[...] = jnp.sqrt(-2.0 * th / (1.0 - th)) * (ig * xf)

    def body(t, h):
        r0 = pl.multiple_of(t * nb, nb)
        h = a_ref[pl.ds(r0, nb), :] * h + b_ref[pl.ds(r0, nb), :]
        b_ref[pl.ds(r0, nb), :] = h
        return h

    h_ref[...] = lax.fori_loop(0, steps, body, h_ref[...])
    o_ref[...] = (b_ref[...] * jax.nn.gelu(gate_ref[...].astype(F32))).astype(o_ref.dtype)


def _lru_mixer(proj, nb, conv_w, conv_b, wa, ba, wi, bi, lam):
    rows_total = proj.shape[0]
    width = conv_w.shape[1]
    rows = EVEN_TL * nb
    halo = (CONV_WIDTH - 1) * nb
    kern = functools.partial(_lru_kernel, nb=nb)
    vec = pl.BlockSpec((1, width), lambda i: (0, 0))
    return pl.pallas_call(
        kern,
        out_shape=jax.ShapeDtypeStruct((rows_total, width), BF16),
        grid=(rows_total // rows,),
        in_specs=[pl.BlockSpec((rows, width), lambda i: (i, 1)),
                  pl.BlockSpec((rows, width), lambda i: (i, 2)),
                  pl.BlockSpec(conv_w.shape, lambda i: (0, 0)),
                  vec,
                  pl.BlockSpec(wa.shape, lambda i: (0, 0, 0)),
                  vec,
                  pl.BlockSpec(wi.shape, lambda i: (0, 0, 0)),
                  vec, vec],
        out_specs=pl.BlockSpec((rows, width), lambda i: (i, 0)),
        scratch_shapes=[pltpu.VMEM((rows + halo, width), F32),
                        pltpu.VMEM((rows, width), F32),
                        pltpu.VMEM((rows, width), F32),
                        pltpu.VMEM((nb, width), F32)],
        compiler_params=_cparams("arbitrary"),
        name="lru_mixer",
    )(proj, proj, conv_w, conv_b, wa, ba, wi, bi, lam)


def _even_outproj_kernel(ya_ref, yb_ref, wa_ref, wb_ref, x_ref, o_ref, y_ref):
    nb, tl, _ = x_ref.shape
    nc = y_ref.shape[0]
    y = _dot(ya_ref[...], wa_ref[...]) + _dot(yb_ref[...], wb_ref[...])
    for c in range(nc):
        y_ref[c] = y[:, c * LANES:(c + 1) * LANES]
    for b in range(nb):
        for c in range(nc):
            o_ref[b, :, c * LANES:(c + 1) * LANES] = (x_ref[b, :, c * LANES:(c + 1) * LANES]
                                                     + y_ref[c, pl.ds(b, tl, stride=nb), :])


def _even_outproj(ya, yb, w_a, w_b, x):
    nb, seq, d = x.shape
    tl = EVEN_TL
    rows = tl * nb
    wd = ya.shape[1]
    return pl.pallas_call(
        _even_outproj_kernel,
        out_shape=jax.ShapeDtypeStruct(x.shape, x.dtype),
        grid=(seq // tl,),
        in_specs=[pl.BlockSpec((rows, wd), lambda i: (i, 0)),
                  pl.BlockSpec((rows, wd), lambda i: (i, 0)),
                  pl.BlockSpec(w_a.shape, lambda i: (0, 0)),
                  pl.BlockSpec(w_b.shape, lambda i: (0, 0)),
                  pl.BlockSpec((nb, tl, d), lambda i: (0, i, 0))],
        out_specs=pl.BlockSpec((nb, tl, d), lambda i: (0, i, 0)),
        scratch_shapes=[pltpu.VMEM((d // LANES, rows, LANES), F32)],
        compiler_params=_cparams("parallel"),
        name="even_outproj",
    )(ya, yb, w_a, w_b, x)


def _block_diag(blocks):
    n, r, c = blocks.shape
    tiled = jnp.tile(blocks.reshape(n * r, c), (1, n))
    row_blk = jnp.arange(n * r, dtype=jnp.int32)[:, None] // r
    col_blk = jnp.arange(n * c, dtype=jnp.int32)[None, :] // c
    return jnp.where(row_blk == col_blk, tiled, jnp.zeros((), blocks.dtype))


def _even_mixer(x, norm_g, w_in, lam_re, lam_im, log_dt, b_re, b_im, c_re, c_im, d, glu_w, glu_b,
                conv_w, conv_b, wa, ba, wi, bi, lam, w_out):
    nb, seq, dm = x.shape
    g, p = lam_re.shape
    hch = b_re.shape[-1]
    width = g * hch
    n_half = 2
    gh = g // n_half

    proj = _even_inproj(x, norm_g[None, :], w_in.astype(BF16))

    lb_re, lb_im, bb_re, bb_im = _s5_discretize(lam_re, lam_im, log_dt, b_re, b_im)
    bb_re = bb_re.reshape(n_half, gh, hch, p)
    bb_im = bb_im.reshape(n_half, gh, hch, p)
    bm = jnp.stack([jnp.concatenate([_block_diag(bb_re[j]), _block_diag(bb_im[j])], axis=1)
                    for j in range(n_half)]).astype(BF16)
    c_re_t = jnp.swapaxes(c_re, 1, 2).reshape(n_half, gh, p, hch)
    c_im_t = jnp.swapaxes(c_im, 1, 2).reshape(n_half, gh, p, hch)
    cm = jnp.stack([jnp.concatenate([_block_diag(c_re_t[j]), -_block_diag(c_im_t[j])], axis=0)
                    for j in range(n_half)]).astype(BF16)
    lre = jnp.broadcast_to(lb_re.reshape(1, g * p), (nb, g * p))
    lim = jnp.broadcast_to(lb_im.reshape(1, g * p), (nb, g * p))
    ya = _s5_mixer(proj, nb, bm, lre, lim, cm, d.reshape(1, width), glu_w.astype(BF16),
                   glu_b[None, :])

    lw = conv_w.shape[1]
    nblk = 2
    per = LRU_BLOCKS // nblk
    wa_bd = jnp.stack([_block_diag(wa[j * per:(j + 1) * per]) for j in range(nblk)]).astype(BF16)
    wi_bd = jnp.stack([_block_diag(wi[j * per:(j + 1) * per]) for j in range(nblk)]).astype(BF16)
    yb = _lru_mixer(proj, nb, conv_w, conv_b[None, :], wa_bd, ba[None, :], wi_bd, bi[None, :],
                    lam[None, :])

    w_out_b = w_out.astype(BF16)
    return _even_outproj(ya, yb, w_out_b[:width], w_out_b[width:], x)


def _rope_table_kernel(pos_ref, inv_ref, cc_ref, ss_ref):
    tr = pos_ref.shape[0]
    half = ROPE_DIM // 2
    per_row = LANES // half
    ang = pos_ref[...].astype(F32) * inv_ref[...]
    cos = jnp.cos(ang)
    sin = jnp.sin(ang)
    lane = lax.broadcasted_iota(jnp.int32, ang.shape, 1)
    sign = jnp.where((lane & (ROPE_DIM - 1)) < half, -1.0, 1.0)

    def spread(a, j):
        g = jnp.where(lane // half == j, a, 0.0)
        out = g
        for k in range(1, per_row):
            out = out + pltpu.roll(g, k * half, 1)
        return out

    for j in range(per_row):
        cc_ref[pl.ds(j, tr, stride=per_row), :] = spread(cos, j)
        ss_ref[pl.ds(j, tr, stride=per_row), :] = spread(sin, j) * sign


def _rope_tables(positions):
    nb, seq = positions.shape
    half = ROPE_DIM // 2
    per_row = LANES // half
    n_tok = nb * seq
    tr = ROW_TILE
    inv_freq = 1.0 / (ROPE_THETA ** (jnp.arange(0, ROPE_DIM, 2, dtype=F32) / ROPE_DIM))
    pos_rep = jnp.repeat(positions.reshape(n_tok // per_row, per_row), half, axis=1)
    inv_rep = jnp.tile(inv_freq, per_row)[None, :]
    return pl.pallas_call(
        _rope_table_kernel,
        out_shape=(jax.ShapeDtypeStruct((n_tok, LANES), F32), jax.ShapeDtypeStruct((n_tok, LANES), F32)),
        grid=(n_tok // per_row // tr,),
        in_specs=[pl.BlockSpec((tr, LANES), lambda i: (i, 0)),
                  pl.BlockSpec((1, LANES), lambda i: (0, 0))],
        out_specs=(pl.BlockSpec((tr * per_row, LANES), lambda i: (i, 0)),
                   pl.BlockSpec((tr * per_row, LANES), lambda i: (i, 0))),
        compiler_params=_cparams("parallel"),
        name="rope_table",
    )(pos_rep, inv_rep)


def _mla_proj_kernel(x_ref, g_ref, win_ref, cqn_ref, ckvn_ref, wq_ref, wkv_ref, qg_ref, kg_ref,
                     cc_ref, ss_ref, qn_ref, qrp_ref, k2_ref, v_ref):
    nh = MLA_HEADS
    h = _rms(x_ref[...], g_ref[...])
    proj = _dot(h.astype(BF16), win_ref[...])
    c_q = _rms(proj[:, :Q_LORA], cqn_ref[...])
    c_kv = _rms(proj[:, Q_LORA:Q_LORA + KV_LORA], ckvn_ref[...])
    o = Q_LORA + KV_LORA
    k_r2 = proj[:, o:o + LANES]
    k_sw2 = proj[:, o + LANES:o + 2 * LANES]
    cc2 = cc_ref[...]
    ss2 = ss_ref[...]
    qg = qg_ref[...]
    kg = kg_ref[...]
    lo = lax.broadcasted_iota(jnp.int32, cc2.shape, 1) < ROPE_DIM
    scale = math.log2(math.e) / math.sqrt(QK_DIM)

    q = _dot(c_q.astype(BF16), wq_ref[...])
    q_a = cc2 * qg[1:2, :]
    q_b = ss2 * qg[2:3, :]
    for j in range(nh // 2):
        ro = nh * NOPE_DIM + j * LANES
        q_rp = q[:, ro:ro + LANES]
        q_swp = q[:, ro + nh * ROPE_DIM:ro + nh * ROPE_DIM + LANES]
        rot = q_rp * q_a + q_swp * q_b
        sq_rp = q_rp * q_rp
        rs_pair = []
        for par in range(2):
            hd = 2 * j + par
            q_n = q[:, hd * NOPE_DIM:(hd + 1) * NOPE_DIM]
            own = lo if par == 0 else jnp.logical_not(lo)
            ms = jnp.sum(q_n * q_n + jnp.where(own, sq_rp, 0.0), axis=-1, keepdims=True) / QK_DIM
            rs = lax.rsqrt(ms + RMS_EPS) * scale
            qn_ref[0, hd] = (q_n * rs * qg[0:1, :]).astype(qn_ref.dtype)
            rs_pair.append(rs)
        qrp_ref[0, j] = (rot * jnp.where(lo, rs_pair[0], rs_pair[1])).astype(qrp_ref.dtype)

    kv = _dot(c_kv.astype(BF16), wkv_ref[...])
    k_rot2 = k_r2 * (cc2 * kg[1:2, :]) + k_sw2 * (ss2 * kg[2:3, :])
    k_r_ss = jnp.sum(jnp.where(lo, k_r2 * k_r2, 0.0), axis=-1, keepdims=True)
    for hd in range(nh):
        ko = hd * (NOPE_DIM + V_DIM)
        k_n = kv[:, ko:ko + NOPE_DIM]
        ms_k = (jnp.sum(k_n * k_n, axis=-1, keepdims=True) + k_r_ss) / QK_DIM
        rs_k = lax.rsqrt(ms_k + RMS_EPS)
        own = lo if hd % 2 == 0 else jnp.logical_not(lo)
        k2_ref[0, hd, :, 0:NOPE_DIM] = (k_n * rs_k * kg[0:1, :]).astype(k2_ref.dtype)
        k2_ref[0, hd, :, NOPE_DIM:] = jnp.where(own, k_rot2 * rs_k, 0.0).astype(k2_ref.dtype)
        v_ref[0, hd] = kv[:, ko + NOPE_DIM:ko + NOPE_DIM + V_DIM].astype(v_ref.dtype)


def _swap_halves(a, axis=-1):
    h = a.shape[axis] // 2
    lo = lax.slice_in_dim(a, 0, h, axis=axis)
    hi = lax.slice_in_dim(a, h, 2 * h, axis=axis)
    return jnp.concatenate([hi, lo], axis=axis)


def _mla_proj(x, norm_g, positions, w_in, cq_norm, ckv_norm, w_uq, w_ukv, q_norm, k_norm):
    nb, seq, dm = x.shape
    nh = MLA_HEADS
    tm = ROW_TILE
    n_tok = nb * seq
    cc2, ss2 = _rope_tables(positions)
    o = Q_LORA + KV_LORA
    w_kr = w_in[:, o:o + ROPE_DIM]
    w_sw = _swap_halves(w_kr)
    w_in_x = jnp.concatenate([w_in[:, :o], w_kr, w_kr, w_sw, w_sw], axis=1).astype(BF16)
    wq = w_uq.reshape(Q_LORA, nh, QK_DIM)
    wq_n = wq[:, :, :NOPE_DIM].reshape(Q_LORA, nh * NOPE_DIM)
    wq_r = wq[:, :, NOPE_DIM:]
    wq_x = jnp.concatenate([wq_n, wq_r.reshape(Q_LORA, nh * ROPE_DIM),
                            _swap_halves(wq_r).reshape(Q_LORA, nh * ROPE_DIM)], axis=1).astype(BF16)

    def gains(gv):
        r = gv[NOPE_DIM:]
        rs = _swap_halves(r)
        return jnp.stack([gv[:NOPE_DIM], jnp.concatenate([r, r]), jnp.concatenate([rs, rs])])

    x2 = x.reshape(n_tok, dm)
    per_b = seq // tm
    full = lambda a: pl.BlockSpec(a.shape, lambda i: (0,) * a.ndim)
    args = (x2, norm_g[None, :], w_in_x, cq_norm[None, :], ckv_norm[None, :], wq_x,
            w_ukv.astype(BF16), gains(q_norm), gains(k_norm), cc2, ss2)
    in_specs = [pl.BlockSpec((tm, dm), lambda i: (i, 0))] + [full(a) for a in args[1:9]] + [
        pl.BlockSpec((tm, LANES), lambda i: (i, 0)), pl.BlockSpec((tm, LANES), lambda i: (i, 0))]

    def head_spec(n_heads, dh):
        return pl.BlockSpec((1, n_heads, tm, dh), lambda i: (i // per_b, 0, i % per_b, 0))

    def head_shape(n_heads, dh):
        return jax.ShapeDtypeStruct((nb, n_heads, seq, dh), BF16)

    return pl.pallas_call(
        _mla_proj_kernel,
        out_shape=(head_shape(nh, NOPE_DIM), head_shape(nh // 2, LANES), head_shape(nh, NOPE_DIM + LANES),
                   head_shape(nh, V_DIM)),
        grid=(n_tok // tm,),
        in_specs=in_specs,
        out_specs=(head_spec(nh, NOPE_DIM), head_spec(nh // 2, LANES), head_spec(nh, NOPE_DIM + LANES),
                   head_spec(nh, V_DIM)),
        compiler_params=_cparams("parallel"),
        name="mla_proj",
    )(*args)


def _attention_kernel(qn_ref, qrp_ref, k2_ref, v_ref, o_ref):
    seq = qn_ref.shape[2]
    t = ATT_TQ
    nt = (((1,), (1,)), ((), ()))
    tri = (lax.broadcasted_iota(jnp.int32, (t, t), 1) <= lax.broadcasted_iota(jnp.int32, (t, t), 0))
    for i in range(seq // t):
        q = jnp.concatenate([qn_ref[0, 0, i * t:(i + 1) * t, :], qrp_ref[0, 0, i * t:(i + 1) * t, :]], axis=-1)
        m = jnp.full((t, 1), NEG_INF, F32)
        acc = jnp.zeros((t, 2 * V_DIM), F32)
        for j in range(i + 1):
            s = lax.dot_general(q, k2_ref[0, 0, j * t:(j + 1) * t, :], nt, preferred_element_type=F32)
            if j == i:
                s = jnp.where(tri, s, NEG_INF)
            m_new = jnp.maximum(m, jnp.max(s, axis=-1, keepdims=True))
            alpha = jnp.exp2(m - m_new)
            p = jnp.exp2(s - m_new)
            v1 = jnp.concatenate([v_ref[0, 0, j * t:(j + 1) * t, :], jnp.ones((t, V_DIM), BF16)], axis=-1)
            acc = alpha * acc + _dot(p.astype(BF16), v1)
            m = m_new
        o_ref[0, i * t:(i + 1) * t, :] = (acc[:, :V_DIM] / acc[:, V_DIM:V_DIM + 1]).astype(o_ref.dtype)


def _attention(qn, qrp, k2, v):
    nb, nh, seq, _ = qn.shape

    def spec(dh, per=1):
        return pl.BlockSpec((1, 1, seq, dh), lambda b, h: (b, h // per, 0, 0))

    return pl.pallas_call(
        _attention_kernel,
        out_shape=jax.ShapeDtypeStruct((nb, seq, nh * V_DIM), BF16),
        grid=(nb, nh),
        in_specs=[spec(NOPE_DIM), spec(LANES, 2), spec(NOPE_DIM + LANES), spec(V_DIM)],
        out_specs=pl.BlockSpec((1, seq, V_DIM), lambda b, h: (b, 0, h)),
        compiler_params=_cparams("parallel", "parallel"),
        name="mla_attention",
    )(qn, qrp, k2, v)


def _proj_residual_kernel(a_ref, w_ref, x_ref, o_ref):
    o_ref[...] = x_ref[...] + _dot(a_ref[...], w_ref[...])


def _proj_residual(a, w, x2):
    n_tok, dm = x2.shape
    tm = ROW_TILE
    return pl.pallas_call(
        _proj_residual_kernel,
        out_shape=jax.ShapeDtypeStruct(x2.shape, x2.dtype),
        grid=(n_tok // tm,),
        in_specs=[pl.BlockSpec((tm, a.shape[1]), lambda i: (i, 0)),
                  pl.BlockSpec(w.shape, lambda i: (0, 0)),
                  pl.BlockSpec((tm, dm), lambda i: (i, 0))],
        out_specs=pl.BlockSpec((tm, dm), lambda i: (i, 0)),
        compiler_params=_cparams("parallel"),
        name="proj_residual",
    )(a, w, x2)


def _mla_mixer(x, norm_g, positions, w_in, cq_norm, ckv_norm, w_uq, w_ukv, q_norm, k_norm, w_o):
    nb, seq, dm = x.shape
    qn, qrp, k2, v = _mla_proj(x, norm_g, positions, w_in, cq_norm, ckv_norm, w_uq, w_ukv, q_norm, k_norm)
    o = _attention(qn, qrp, k2, v)
    out = _proj_residual(o.reshape(nb * seq, -1), w_o.astype(BF16), x.reshape(nb * seq, dm))
    return out.reshape(nb, seq, dm)


def _router_kernel(x_ref, g_ref, w_ref, b_ref, tri_ref, meta_ref, gate_ref, cnt_ref, base_ref):
    tm, dm = x_ref.shape
    ne = N_EXPERTS

    @pl.when(pl.program_id(0) == 0)
    def _():
        base_ref[...] = jnp.zeros_like(base_ref)

    hn = _rms(x_ref[...], g_ref[...])
    logits = _dot(hn.astype(BF16), w_ref[...]) + b_ref[...]
    lane = lax.broadcasted_iota(jnp.int32, logits.shape, 1)
    big = jnp.int32(1 << 20)

    def first_argmax(v):
        m = jnp.max(v, axis=-1, keepdims=True)
        idx = jnp.min(jnp.where(v == m, lane, big), axis=-1, keepdims=True)
        return m, idx

    is_g = (lane >= ne) & (lane < ne + N_GROUPS)
    gl = jnp.where(is_g, logits, NEG_INF)
    g_max, g_idx = first_argmax(gl)
    g_sel = g_idx - ne
    p_g = 1.0 / jnp.sum(jnp.where(is_g, jnp.exp(gl - g_max), 0.0), axis=-1, keepdims=True)
    in_grp = (lane >= g_sel * EXPERTS_PER_GROUP) & (lane < (g_sel + 1) * EXPERTS_PER_GROUP)
    el = jnp.where(in_grp, logits, NEG_INF)
    v0, e0 = first_argmax(el)
    v1, e1 = first_argmax(jnp.where(lane == e0, NEG_INF, el))
    t = jnp.exp(v1 - v0)
    gate0 = p_g / (1.0 + t)
    gate1 = p_g * t / (1.0 + t)

    oh0 = (lane == e0).astype(F32)
    oh1 = (lane == e1).astype(F32)
    both = oh0 + oh1
    before = _dot(tri_ref[...], both.astype(BF16)) + base_ref[0:1, :]
    rank0 = jnp.sum(oh0 * before, axis=-1, keepdims=True)
    rank1 = jnp.sum(oh1 * before, axis=-1, keepdims=True)
    base_ref[0:1, :] = base_ref[0:1, :] + jnp.sum(both, axis=0, keepdims=True)
    cnt_ref[...] = jnp.broadcast_to(base_ref[0:1, :], cnt_ref.shape).astype(jnp.int32)

    meta = jnp.where(lane == 0, e0, jnp.where(lane == 1, e1, jnp.where(
        lane == 2, rank0.astype(jnp.int32), jnp.where(lane == 3, rank1.astype(jnp.int32), 0))))
    meta_ref[...] = meta
    gate_ref[...] = jnp.where(lane == 0, gate0, jnp.where(lane == 1, gate1, 0.0))


def _router(x2, norm_g, w_group, b_group, w_expert, b_expert):
    n_tok, dm = x2.shape
    tm = ROW_TILE
    pad = LANES - N_EXPERTS - N_GROUPS
    w = jnp.concatenate([w_expert, w_group, jnp.zeros((dm, pad), F32)], axis=1).astype(BF16)
    b = jnp.concatenate([b_expert, b_group, jnp.zeros((pad,), F32)])[None, :]
    idx = jnp.arange(tm, dtype=jnp.int32)
    tri = (idx[None, :] < idx[:, None]).astype(BF16)
    return pl.pallas_call(
        _router_kernel,
        out_shape=(jax.ShapeDtypeStruct((n_tok, LANES), jnp.int32),
                   jax.ShapeDtypeStruct((n_tok, LANES), F32),
                   jax.ShapeDtypeStruct((SUBLANES, LANES), jnp.int32)),
        grid=(n_tok // tm,),
        in_specs=[pl.BlockSpec((tm, dm), lambda i: (i, 0)),
                  pl.BlockSpec((1, dm), lambda i: (0, 0)),
                  pl.BlockSpec((dm, LANES), lambda i: (0, 0)),
                  pl.BlockSpec((1, LANES), lambda i: (0, 0)),
                  pl.BlockSpec((tm, tm), lambda i: (0, 0))],
        out_specs=(pl.BlockSpec((tm, LANES), lambda i: (i, 0)),
                   pl.BlockSpec((tm, LANES), lambda i: (i, 0)),
                   pl.BlockSpec((SUBLANES, LANES), lambda i: (0, 0))),
        scratch_shapes=[pltpu.VMEM((SUBLANES, LANES), F32)],
        compiler_params=_cparams("arbitrary"),
        name="moe_router",
    )(x2, norm_g[None, :], w, b, tri)


def _dispatch_kernel(dest_ref, dest_prev_ref, x_ref, g_ref, xd_ref, hn_ref, zero_ref, sem, zsem):
    tm = x_ref.shape[0]
    i = pl.program_id(0)
    n_steps = pl.num_programs(0)
    slot = i % 2
    n_assign = n_steps * tm * TOP_K

    def tail_copy():
        return pltpu.make_async_copy(zero_ref, xd_ref.at[pl.ds(n_assign * ROW_CHUNKS, zero_ref.shape[0])], zsem)

    @pl.when(i == 0)
    def _():
        zero_ref[...] = jnp.zeros_like(zero_ref)
        tail_copy().start()

    hn = _rms(x_ref[...], g_ref[...])
    hbuf = hn_ref.at[slot]
    for c in range(ROW_CHUNKS):
        hbuf[pl.ds(c, tm, stride=ROW_CHUNKS), :] = hn[:, c * LANES:(c + 1) * LANES]

    def row_copy(d_ref, sl, t, k):
        src = pl.multiple_of(t * ROW_CHUNKS, ROW_CHUNKS)
        dst = pl.multiple_of(d_ref[0, 0, 2 * t + k], ROW_CHUNKS)
        return pltpu.make_async_copy(hn_ref.at[sl, pl.ds(src, ROW_CHUNKS)], xd_ref.at[pl.ds(dst, ROW_CHUNKS)],
                                     sem.at[sl])

    def issue(j, c):
        for u in range(DMA_UNROLL):
            for k in range(TOP_K):
                row_copy(dest_ref, slot, j * DMA_UNROLL + u, k).start(priority=k)
        return c

    lax.fori_loop(0, tm // DMA_UNROLL, issue, 0)

    def drain_all(d_ref, sl):
        def drain(j, c):
            for u in range(DMA_UNROLL):
                for k in range(TOP_K):
                    row_copy(d_ref, sl, j * DMA_UNROLL + u, k).wait()
            return c
        lax.fori_loop(0, tm // DMA_UNROLL, drain, 0)

    @pl.when(i >= 1)
    def _():
        drain_all(dest_prev_ref, 1 - slot)

    @pl.when(i == n_steps - 1)
    def _():
        drain_all(dest_ref, slot)

    @pl.when(i == 0)
    def _():
        tail_copy().wait()


def _dispatch(x2, norm_g, dest_off):
    tm = GATHER_TM
    n_tok, dm = x2.shape
    dest3 = dest_off.reshape(n_tok // tm, 1, 2 * tm)
    n_rows = n_tok * TOP_K + MOE_BM
    return pl.pallas_call(
        _dispatch_kernel,
        out_shape=jax.ShapeDtypeStruct((n_rows * ROW_CHUNKS, LANES), F32),
        grid=(n_tok // tm,),
        in_specs=[pl.BlockSpec((1, 1, 2 * tm), lambda i: (i, 0, 0), memory_space=pltpu.SMEM),
                  pl.BlockSpec((1, 1, 2 * tm), lambda i: (jnp.maximum(i - 1, 0), 0, 0), memory_space=pltpu.SMEM),
                  pl.BlockSpec((tm, dm), lambda i: (i, 0)),
                  pl.BlockSpec((1, dm), lambda i: (0, 0))],
        out_specs=pl.BlockSpec(memory_space=pl.ANY),
        scratch_shapes=[pltpu.VMEM((2, tm * ROW_CHUNKS, LANES), F32),
                        pltpu.VMEM((MOE_BM * ROW_CHUNKS, LANES), F32),
                        pltpu.SemaphoreType.DMA((2,)), pltpu.SemaphoreType.DMA],
        compiler_params=_cparams("arbitrary"),
        name="moe_dispatch",
    )(dest3, dest3, x2, norm_g[None, :])


def _rows_from_tiles(ref, n, base=0):
    return jnp.concatenate([ref[pl.ds(base + c, n, stride=ROW_CHUNKS), :] for c in range(ROW_CHUNKS)], axis=-1)


def _expert_kernel(be_ref, bs_ref, first_ref, nxt_ref, nu_ref, xd_ref, w1_ref, w3_ref, w2_ref, yd_ref,
                   xbuf, ybuf, w1buf, w3buf, w2buf, w13_s, w2_s, sem_in, sem_out, sem_w, *, layer):
    nu = nu_ref[0]
    n_rows = xbuf.shape[1]
    bm = n_rows // ROW_CHUNKS
    de = w1buf.shape[2]

    def in_copy(blk, sl):
        src = pl.multiple_of(bs_ref[blk] * ROW_CHUNKS, ROW_CHUNKS)
        return pltpu.make_async_copy(xd_ref.at[pl.ds(src, n_rows)], xbuf.at[sl], sem_in.at[sl])

    def out_copy(blk, sl):
        dst = pl.multiple_of(bs_ref[blk] * ROW_CHUNKS, ROW_CHUNKS)
        return pltpu.make_async_copy(ybuf.at[sl], yd_ref.at[pl.ds(dst, n_rows)], sem_out.at[sl])

    def w_copies(e, ws):
        return (pltpu.make_async_copy(w1_ref.at[layer, e], w1buf.at[ws], sem_w.at[ws]),
                pltpu.make_async_copy(w3_ref.at[layer, e], w3buf.at[ws], sem_w.at[ws]),
                pltpu.make_async_copy(w2_ref.at[layer, e], w2buf.at[ws], sem_w.at[ws]))

    in_copy(0, 0).start()
    for cp in w_copies(be_ref[0], 1):
        cp.start(priority=1)

    def block(b, ws):
        slot = b % 2

        @pl.when(b + 1 < nu)
        def _():
            in_copy(b + 1, 1 - slot).start()

        first = first_ref[b] == 1
        ws = jnp.where(first, 1 - ws, ws)

        @pl.when(first)
        def _():
            for cp in w_copies(be_ref[b], ws):
                cp.wait()
            w13_s[:, :de] = w1buf[ws].astype(BF16)
            w13_s[:, de:] = w3buf[ws].astype(BF16)
            w2_s[...] = w2buf[ws].astype(BF16)

            @pl.when(nxt_ref[b] >= 0)
            def _():
                for cp in w_copies(nxt_ref[b], 1 - ws):
                    cp.start(priority=1)

        in_copy(b, slot).wait()
        x = _rows_from_tiles(xbuf.at[slot], bm).astype(BF16)
        h = _dot(x, w13_s[...])
        a = jax.nn.silu(h[:, :de]) * h[:, de:]
        y = _dot(a.astype(BF16), w2_s[...])
        yb = ybuf.at[slot]
        for c in range(ROW_CHUNKS):
            yb[pl.ds(c, bm, stride=ROW_CHUNKS), :] = y[:, c * LANES:(c + 1) * LANES]

        @pl.when(b >= 1)
        def _():
            out_copy(b - 1, 1 - slot).wait()
        out_copy(b, slot).start()
        return ws

    lax.fori_loop(0, nu, block, jnp.int32(0))

    last = nu - 1
    out_copy(last, last % 2).wait()
    tail = yd_ref.shape[0] - n_rows
    fill_slot = 1 - last % 2
    ybuf[fill_slot] = jnp.zeros((n_rows, LANES), F32)
    fill = pltpu.make_async_copy(ybuf.at[fill_slot], yd_ref.at[pl.ds(tail, n_rows)], sem_out.at[fill_slot])
    fill.start()
    fill.wait()


def _experts(xd, blk_expert, blk_start, blk_first, blk_next, n_used, w1, w3, w2, layer):
    bm = MOE_BM
    _, _, dm, de = w1.shape
    any_spec = pl.BlockSpec(memory_space=pl.ANY)
    grid_spec = pltpu.PrefetchScalarGridSpec(
        num_scalar_prefetch=5,
        grid=(1,),
        in_specs=[any_spec, any_spec, any_spec, any_spec],
        out_specs=any_spec,
        scratch_shapes=[pltpu.VMEM((2, bm * ROW_CHUNKS, LANES), F32),
                        pltpu.VMEM((2, bm * ROW_CHUNKS, LANES), F32),
                        pltpu.VMEM((2, dm, de), F32),
                        pltpu.VMEM((2, dm, de), F32),
                        pltpu.VMEM((2, de, dm), F32),
                        pltpu.VMEM((dm, 2 * de), BF16),
                        pltpu.VMEM((de, dm), BF16),
                        pltpu.SemaphoreType.DMA((2,)),
                        pltpu.SemaphoreType.DMA((2,)),
                        pltpu.SemaphoreType.DMA((2,))],
    )
    return pl.pallas_call(
        functools.partial(_expert_kernel, layer=layer),
        out_shape=jax.ShapeDtypeStruct(xd.shape, F32),
        grid_spec=grid_spec,
        compiler_params=_cparams("arbitrary"),
        name="moe_experts",
    )(blk_expert, blk_start, blk_first, blk_next, n_used, xd, w1, w3, w2)


def _combine_kernel(dest_ref, dest_next_ref, yd_ref, gate_ref, x_ref, o_ref, buf_ref, sem):
    tm, dm = x_ref.shape
    i = pl.program_id(0)
    n_steps = pl.num_programs(0)
    slot = i % 2
    k_rows = tm * ROW_CHUNKS

    def row_copy(d_ref, sl, t, k):
        src = pl.multiple_of(d_ref[0, 0, 2 * t + k], ROW_CHUNKS)
        dst = pl.multiple_of(k * k_rows + t * ROW_CHUNKS, ROW_CHUNKS)
        return pltpu.make_async_copy(yd_ref.at[pl.ds(src, ROW_CHUNKS)],
                                     buf_ref.at[sl, pl.ds(dst, ROW_CHUNKS)], sem.at[sl])

    def issue_all(d_ref, sl):
        def issue(j, c):
            for u in range(DMA_UNROLL):
                for k in range(TOP_K):
                    row_copy(d_ref, sl, j * DMA_UNROLL + u, k).start(priority=k)
            return c
        lax.fori_loop(0, tm // DMA_UNROLL, issue, 0)

    @pl.when(i == 0)
    def _():
        issue_all(dest_ref, slot)

    @pl.when(i + 1 < n_steps)
    def _():
        issue_all(dest_next_ref, 1 - slot)

    def drain(j, c):
        for u in range(DMA_UNROLL):
            for k in range(TOP_K):
                row_copy(dest_ref, slot, j * DMA_UNROLL + u, k).wait()
        return c

    lax.fori_loop(0, tm // DMA_UNROLL, drain, 0)

    g0 = gate_ref[:, 0:1]
    g1 = gate_ref[:, 1:2]
    buf = buf_ref.at[slot]
    for c in range(ROW_CHUNKS):
        y0 = buf[pl.ds(c, tm, stride=ROW_CHUNKS), :]
        y1 = buf[pl.ds(k_rows + c, tm, stride=ROW_CHUNKS), :]
        o_ref[:, c * LANES:(c + 1) * LANES] = x_ref[:, c * LANES:(c + 1) * LANES] + (g0 * y0 + g1 * y1)


def _combine(yd, dest_off, gates, x2):
    n_tok, dm = x2.shape
    tm = GATHER_TM
    n_steps = n_tok // tm
    dest3 = dest_off.reshape(n_steps, 1, 2 * tm)
    return pl.pallas_call(
        _combine_kernel,
        out_shape=jax.ShapeDtypeStruct(x2.shape, x2.dtype),
        grid=(n_steps,),
        in_specs=[pl.BlockSpec((1, 1, 2 * tm), lambda i: (i, 0, 0), memory_space=pltpu.SMEM),
                  pl.BlockSpec((1, 1, 2 * tm), lambda i: (jnp.minimum(i + 1, n_steps - 1), 0, 0),
                               memory_space=pltpu.SMEM),
                  pl.BlockSpec(memory_space=pl.ANY),
                  pl.BlockSpec((tm, LANES), lambda i: (i, 0)),
                  pl.BlockSpec((tm, dm), lambda i: (i, 0))],
        out_specs=pl.BlockSpec((tm, dm), lambda i: (i, 0)),
        scratch_shapes=[pltpu.VMEM((2, TOP_K * tm * ROW_CHUNKS, LANES), F32),
                        pltpu.SemaphoreType.DMA((2,))],
        compiler_params=_cparams("arbitrary"),
        name="moe_combine",
    )(dest3, dest3, yd, gates, x2)


def _moe(x, norm_g, w_group, b_group, w_expert, b_expert, w1, w3, w2, layer):
    nb, seq, dm = x.shape
    n_tok = nb * seq
    n_assign = n_tok * TOP_K
    x2 = x.reshape(n_tok, dm)
    meta, gates, counts = _router(x2, norm_g, w_group, b_group, w_expert, b_expert)
    bm = MOE_BM
    n_blocks = n_assign // bm + N_EXPERTS
    counts = counts[0, :N_EXPERTS]
    starts = jnp.cumsum(counts) - counts
    eid = meta[:, 0:TOP_K]
    e_iota = jnp.arange(N_EXPERTS, dtype=jnp.int32)
    seg_start = jnp.sum(jnp.where(eid[:, :, None] == e_iota, starts, 0), axis=-1)
    dest_off = ((seg_start + meta[:, TOP_K:2 * TOP_K]) * ROW_CHUNKS).astype(jnp.int32).reshape(-1)
    nblk = (counts + bm - 1) // bm
    blk_end = jnp.cumsum(nblk)
    b_iota = jnp.arange(n_blocks, dtype=jnp.int32)
    blk_expert = jnp.minimum(jnp.sum((b_iota[:, None] >= blk_end[None, :]).astype(jnp.int32), axis=1),
                             N_EXPERTS - 1)
    j = b_iota - (blk_end - nblk)[blk_expert]
    n_used = blk_end[-1].astype(jnp.int32).reshape(1)
    blk_start = jnp.where(b_iota < n_used[0], starts[blk_expert] + j * bm, 0).astype(jnp.int32)
    blk_first = (j == 0).astype(jnp.int32)
    cand = jnp.where(counts > 0, e_iota, N_EXPERTS)
    later = lax.cummin(cand[::-1])[::-1]
    next_e = jnp.concatenate([later[1:], jnp.full((1,), N_EXPERTS, jnp.int32)])
    blk_next = jnp.where(next_e[blk_expert] < N_EXPERTS, next_e[blk_expert], -1).astype(jnp.int32)

    xd = _dispatch(x2, norm_g, dest_off)
    yd = _experts(xd, blk_expert, blk_start, blk_first, blk_next, n_used, w1, w3, w2, layer)
    out = _combine(yd, dest_off, gates, x2)
    return out.reshape(nb, seq, dm)


def kernel(x, positions, ev_norm, ev_w_in, s5_lambda_re, s5_lambda_im, s5_log_dt, s5_b_re, s5_b_im,
           s5_c_re, s5_c_im, s5_d, s5_glu_w, s5_glu_b, lru_conv_w, lru_conv_b, lru_wa, lru_ba, lru_wi,
           lru_bi, lru_lambda, ev_w_out, od_norm, mla_w_in, mla_cq_norm, mla_ckv_norm, mla_w_uq,
           mla_w_ukv, mla_q_norm, mla_k_norm, mla_w_o, ffn_norm, moe_w_group, moe_b_group,
           moe_w_expert, moe_b_expert, moe_w1, moe_w3, moe_w2):
    depth = ffn_norm.shape[0]
    for layer in range(depth):
        j = layer // 2
        if layer % 2 == 0:
            x = _even_mixer(x, ev_norm[j], ev_w_in[j], s5_lambda_re[j], s5_lambda_im[j], s5_log_dt[j],
                            s5_b_re[j], s5_b_im[j], s5_c_re[j], s5_c_im[j], s5_d[j], s5_glu_w[j],
                            s5_glu_b[j], lru_conv_w[j], lru_conv_b[j], lru_wa[j], lru_ba[j], lru_wi[j],
                            lru_bi[j], lru_lambda[j], ev_w_out[j])
        else:
            x = _mla_mixer(x, od_norm[j], positions, mla_w_in[j], mla_cq_norm[j], mla_ckv_norm[j],
                           mla_w_uq[j], mla_w_ukv[j], mla_q_norm[j], mla_k_norm[j], mla_w_o[j])
        x = _moe(x, ffn_norm[layer], moe_w_group[layer], moe_b_group[layer], moe_w_expert[layer],
                 moe_b_expert[layer], moe_w1, moe_w3, moe_w2, layer)
    return x
```
